```python
import jax, jax.numpy as jnp
from jax import lax
import numpy as np

D_MODEL = 1024
BATCH = 32
SEQ = 256
DEPTH = 1
DEC_BATCH = 8
DEC_SEQ = 1024
PAST_LEN = 512

GRID_W = 64
D_MIX = D_MODEL
D_LRU = D_MIX // 2
LRU_BLOCKS = 8
LRU_BW = D_LRU // LRU_BLOCKS
LRU_C = 8.0
CONV_W = 4
CONV_LEFT = 2
D_HGRN = D_MIX - D_LRU
HG_HEADS = 4
HG_DK = D_HGRN // HG_HEADS
HG_DV = D_HGRN // HG_HEADS
CHUNK = 16
N_EXPERTS = 16
EC_CAPACITY_FACTOR = 2
D_EXPERT = D_MODEL
N_MOD = 6
IN_SIZES = (D_LRU, D_LRU, D_HGRN, D_HGRN, D_HGRN, D_HGRN, D_HGRN)
D_IN = sum(IN_SIZES)
EPS = 1e-6

kernel_name = "hybrid_rglru_hgrn2_ec_moe_diffusion_step"


def rmsnorm(x, g):
    xf = x.astype(jnp.float32)
    y = xf * lax.rsqrt(jnp.mean(xf * xf, axis=-1, keepdims=True) + EPS)
    return (y * g.astype(jnp.float32)).astype(x.dtype)


def adaln(cond, w, b):
    m = jax.nn.silu(cond) @ w + b
    m = m.reshape(cond.shape[0], 1, N_MOD, D_MODEL)
    return [m[:, :, j] for j in range(N_MOD)]


def depthwise_conv(x, w, b, row_len):
    B, T, C = x.shape
    rows = T // row_len
    xr = x.reshape(B, rows, row_len, C)
    xp = jnp.pad(xr, ((0, 0), (0, 0), (CONV_LEFT, CONV_W - 1 - CONV_LEFT), (0, 0)))
    y = sum(xp[:, :, j:j + row_len] * w[j] for j in range(CONV_W)) + b
    return y.reshape(B, T, C)


def _affine_combine(left, right):
    a_l, b_l = left
    a_r, b_r = right
    return a_l * a_r, a_r * b_l + b_r


def rglru(x, w_a, b_a, w_x, b_x, lam, h0, reverse):
    B, T, W = x.shape
    if reverse:
        x = jnp.flip(x, axis=1)
    xb = x.reshape(B, T, LRU_BLOCKS, LRU_BW)
    r = jax.nn.sigmoid(jnp.einsum('btnd,nde->btne', xb, w_a).reshape(B, T, W) + b_a)
    i = jax.nn.sigmoid(jnp.einsum('btnd,nde->btne', xb, w_x).reshape(B, T, W) + b_x)
    log_a = -LRU_C * r * jax.nn.softplus(-lam)
    a = jnp.exp(log_a)
    u = jnp.sqrt(-jnp.expm1(2.0 * log_a)) * (i * x)
    a_cum, u_cum = lax.associative_scan(_affine_combine, (a, u), axis=1)
    h = a_cum * h0[:, None, :].astype(x.dtype) + u_cum
    h_last = h[:, -1]
    if reverse:
        h = jnp.flip(h, axis=1)
    return h, h_last


def gated_chunk(q, k, v, logf, s0):
    B, H, T, K = q.shape
    V = v.shape[-1]
    N = T // CHUNK
    q, k, logf = (z.reshape(B, H, N, CHUNK, K) for z in (q, k, logf))
    v = v.reshape(B, H, N, CHUNK, V)
    b = jnp.cumsum(logf.astype(jnp.float32), axis=3)
    incl = jnp.tril(jnp.ones((CHUNK, CHUNK), dtype=bool))[:, :, None]
    diff = b[:, :, :, :, None, :] - b[:, :, :, None, :, :]
    decay = jnp.exp(jnp.where(incl, diff, -jnp.inf)).astype(q.dtype)
    scores = jnp.einsum('bhntk,bhnsk,bhntsk->bhnts', q, k, decay)
    o_intra = jnp.einsum('bhnts,bhnsv->bhntv', scores, v)
    g = b[:, :, :, -1, :]
    k_dec = k * jnp.exp(g[:, :, :, None, :] - b).astype(k.dtype)
    u = jnp.einsum('bhnsk,bhnsv->bhnkv', k_dec, v)
    g_exp = jnp.exp(g).astype(u.dtype)

    def step(s, xs):
        eg, uc = xs
        return eg[..., None] * s + uc, s

    s_last, s_start = lax.scan(step, s0.astype(u.dtype),
                               (jnp.moveaxis(g_exp, 2, 0), jnp.moveaxis(u, 2, 0)))
    s_start = jnp.moveaxis(s_start, 0, 2)
    o_inter = jnp.einsum('bhntk,bhnkv->bhntv', q * jnp.exp(b).astype(q.dtype), s_start)
    return (o_intra + o_inter).reshape(B, H, T, V), s_last


def hgrn2(q_raw, f_raw_f, f_raw_b, i_raw, g_raw, lb, gain, s0):
    B, T, _ = q_raw.shape

    def heads(z):
        return z.reshape(B, T, HG_HEADS, -1).transpose(0, 2, 1, 3)

    q = heads(jax.nn.silu(q_raw)) * (HG_DK ** -0.5)
    v = heads(i_raw)

    def direction(f_raw, lbd, s_init, reverse):
        sig = jax.nn.sigmoid(f_raw)
        logf = heads(jnp.log(lbd + (1.0 - lbd) * sig))
        k = heads((1.0 - lbd) * jax.nn.sigmoid(-f_raw))
        qq, kk, vv = q, k, v
        if reverse:
            qq, kk, vv, logf = (jnp.flip(z, axis=2) for z in (qq, kk, vv, logf))
        o, s_last = gated_chunk(qq, kk, vv, logf, s_init)
        if reverse:
            o = jnp.flip(o, axis=2)
        return o, s_last

    o_f, s_f = direction(f_raw_f, lb[0], s0[:, 0], False)
    o_b, s_b = direction(f_raw_b, lb[1], s0[:, 1], True)
    o = rmsnorm(o_f + o_b, gain)
    o = o.transpose(0, 2, 1, 3).reshape(B, T, D_HGRN) * jax.nn.silu(g_raw)
    return o, jnp.stack([s_f, s_b], axis=1)


def token_mixer(h, s_lru, s_hgrn, row_len, lb, w_in, conv_w, conv_b, lru_wa, lru_ba,
                lru_wx, lru_bx, lru_lambda, norm_lru, norm_hgrn, w_out):
    proj = h @ w_in
    xr, gr, q_raw, ff, fb, iv, og = jnp.split(proj, np.cumsum(IN_SIZES)[:-1].tolist(), axis=-1)
    xc = depthwise_conv(xr, conv_w, conv_b, row_len)
    h_f, last_f = rglru(xc, lru_wa[0], lru_ba[0], lru_wx[0], lru_bx[0], lru_lambda[0], s_lru[:, 0], False)
    h_b, last_b = rglru(xc, lru_wa[1], lru_ba[1], lru_wx[1], lru_bx[1], lru_lambda[1], s_lru[:, 1], True)
    y_lru = rmsnorm(h_f + h_b, norm_lru) * jax.nn.gelu(gr)
    y_hg, new_hgrn = hgrn2(q_raw, ff, fb, iv, og, lb, norm_hgrn, s_hgrn)
    y = jnp.concatenate([y_lru, y_hg], axis=-1) @ w_out
    return y, jnp.stack([last_f, last_b], axis=1), new_hgrn


def expert_choice_ffn(h, w_router, w_gate, w_up, w_down):
    B, T, D = h.shape
    cap = EC_CAPACITY_FACTOR * T // N_EXPERTS
    aff = jax.nn.softmax((h @ w_router).astype(jnp.float32), axis=-1)
    gate_vals, idx = lax.top_k(jnp.swapaxes(aff, 1, 2), cap)
    bidx = jnp.arange(B)[:, None, None]
    xs = h[bidx, idx]
    hid = jax.nn.silu(jnp.einsum('becd,edf->becf', xs, w_gate)) * jnp.einsum('becd,edf->becf', xs, w_up)
    out = jnp.einsum('becf,efd->becd', hid, w_down) * gate_vals[..., None].astype(h.dtype)
    return jnp.zeros_like(h).at[bidx, idx].add(out)


def setup_inputs(seed: int = 0) -> dict:
    key = jax.random.key(seed)
    ks = jax.random.split(key, 32)
    f32 = jnp.float32
    nrm = lambda k, shape, s: jax.random.normal(k, shape, f32) * s
    u = jax.random.uniform(ks[14], (DEPTH, 2, D_LRU), f32, minval=0.9, maxval=0.999)
    a0 = u ** (1.0 / LRU_C)
    return {
        "x_prompt": nrm(ks[0], (BATCH, SEQ, D_MODEL), 1.0),
        "x_sample": nrm(ks[1], (DEC_BATCH, DEC_SEQ, D_MODEL), 1.0),
        "state_lru": nrm(ks[2], (DEC_BATCH, DEPTH, 2, D_LRU), 0.5),
        "state_hgrn": nrm(ks[3], (DEC_BATCH, DEPTH, 2, HG_HEADS, HG_DK, HG_DV), 0.5),
        "c": nrm(ks[4], (DEC_BATCH, D_MODEL), 1.0),
        "c_ctx": nrm(ks[5], (D_MODEL,), 1.0),
        "w_ada": nrm(ks[6], (DEPTH, D_MODEL, N_MOD * D_MODEL), 0.5 * D_MODEL ** -0.5),
        "b_ada": nrm(ks[7], (DEPTH, N_MOD * D_MODEL), 0.01),
        "norm_mix": 1.0 + nrm(ks[8], (DEPTH, D_MODEL), 0.02),
        "w_in": nrm(ks[9], (DEPTH, D_MODEL, D_IN), D_MODEL ** -0.5),
        "conv_w": nrm(ks[10], (DEPTH, CONV_W, D_LRU), CONV_W ** -0.5),
        "conv_b": nrm(ks[11], (DEPTH, D_LRU), 0.01),
        "lru_wa": nrm(ks[12], (DEPTH, 2, LRU_BLOCKS, LRU_BW, LRU_BW), LRU_BW ** -0.5),
        "lru_ba": nrm(ks[13], (DEPTH, 2, D_LRU), 0.01),
        "lru_wx": nrm(ks[15], (DEPTH, 2, LRU_BLOCKS, LRU_BW, LRU_BW), LRU_BW ** -0.5),
        "lru_bx": nrm(ks[16], (DEPTH, 2, D_LRU), 0.01),
        "lru_lambda": jnp.log(a0) - jnp.log1p(-a0),
        "norm_lru": 1.0 + nrm(ks[17], (DEPTH, D_LRU), 0.02),
        "hgrn_gamma": nrm(ks[18], (2, DEPTH + 1, D_HGRN), 0.5),
        "norm_hgrn": 1.0 + nrm(ks[19], (DEPTH, HG_DV), 0.02),
        "w_out": nrm(ks[20], (DEPTH, D_MIX, D_MODEL), D_MIX ** -0.5),
        "norm_ffn": 1.0 + nrm(ks[21], (DEPTH, D_MODEL), 0.02),
        "w_router": nrm(ks[22], (DEPTH, D_MODEL, N_EXPERTS), D_MODEL ** -0.5),
        "w_gate": nrm(ks[23], (DEPTH, N_EXPERTS, D_MODEL, D_EXPERT), D_MODEL ** -0.5),
        "w_up": nrm(ks[24], (DEPTH, N_EXPERTS, D_MODEL, D_EXPERT), D_MODEL ** -0.5),
        "w_down": nrm(ks[25], (DEPTH, N_EXPERTS, D_EXPERT, D_MODEL), D_EXPERT ** -0.5),
        "norm_final": 1.0 + nrm(ks[26], (D_MODEL,), 0.02),
    }


def reference(x_prompt, x_sample, state_lru, state_hgrn, c, c_ctx, w_ada, b_ada, norm_mix,
              w_in, conv_w, conv_b, lru_wa, lru_ba, lru_wx, lru_bx, lru_lambda, norm_lru,
              hgrn_gamma, norm_hgrn, w_out, norm_ffn, w_router, w_gate, w_up, w_down, norm_final):
    lb_all = jnp.cumsum(jax.nn.softmax(hgrn_gamma.astype(jnp.float32), axis=1), axis=1).astype(x_prompt.dtype)

    def run_layer(x, cond, s_lru, s_hgrn, l, row_len):
        sh1, sc1, g1, sh2, sc2, g2 = adaln(cond, w_ada[l], b_ada[l])
        h = rmsnorm(x, norm_mix[l]) * (1.0 + sc1) + sh1
        y, new_lru, new_hgrn = token_mixer(h, s_lru, s_hgrn, row_len, lb_all[:, l], w_in[l],
                                           conv_w[l], conv_b[l], lru_wa[l], lru_ba[l], lru_wx[l],
                                           lru_bx[l], lru_lambda[l], norm_lru[l], norm_hgrn[l], w_out[l])
        x = x + g1 * y
        h = rmsnorm(x, norm_ffn[l]) * (1.0 + sc2) + sh2
        x = x + g2 * expert_choice_ffn(h, w_router[l], w_gate[l], w_up[l], w_down[l])
        return x, new_lru, new_hgrn

    xp = x_prompt
    Bp, Tp, _ = xp.shape
    zero_lru = jnp.zeros((Bp, 2, D_LRU), xp.dtype)
    zero_hgrn = jnp.zeros((Bp, 2, HG_HEADS, HG_DK, HG_DV), xp.dtype)
    lru_states, hgrn_states = [], []
    for l in range(DEPTH):
        xp, s_l, s_h = run_layer(xp, c_ctx[None, :], zero_lru, zero_hgrn, l, Tp)
        lru_states.append(s_l)
        hgrn_states.append(s_h)
    y_prompt = rmsnorm(xp, norm_final)
    new_state_lru = jnp.stack(lru_states, axis=1)
    new_state_hgrn = jnp.stack(hgrn_states, axis=1)

    xs = x_sample
    for l in range(DEPTH):
        xs, _, _ = run_layer(xs, c, state_lru[:, l], state_hgrn[:, l], l, GRID_W)
    y_sample = rmsnorm(xs, norm_final)

    return (y_prompt, y_sample, new_state_lru, new_state_hgrn)
```

```python
import functools

import jax
import jax.numpy as jnp
from jax import lax
from jax.experimental import pallas as pl
from jax.experimental.pallas import tpu as pltpu

F32 = jnp.float32
BF16 = jnp.bfloat16

D_MODEL = 1024
D_LRU = 512
D_HGRN = 512
HG_HEADS = 4
HG_DK = 128
LRU_BLOCKS = 8
LRU_BW = 64
LRU_C = 8.0
N_EXPERTS = 16
EC_CAPACITY_FACTOR = 2
N_MOD = 6
D_IN = 7 * 512
GRID_W = 64
EPS = 1e-6

LANES = 128
TOK_TILE = 256
PROJ_TILE = 512
LRU_GROUP = 8
PITCH_PAD = 8
HG_CHUNK = 64
FFN_TILE = 512
D_AUG = D_MODEL + LANES
VMEM_LIMIT = 56 * 1024 * 1024


def _cp(sem, vmem=VMEM_LIMIT):
    return pltpu.CompilerParams(dimension_semantics=sem, vmem_limit_bytes=vmem)


def _sigmoid(x):
    return 0.5 * (1.0 + jnp.tanh(0.5 * x))


def _dot(a, b):
    return jnp.dot(a, b, preferred_element_type=F32)


def _dot_nt(a, b):
    return lax.dot_general(a, b, (((1,), (1,)), ((), ())), preferred_element_type=F32)


def _dot_tn(a, b):
    return lax.dot_general(a, b, (((0,), (0,)), ((), ())), preferred_element_type=F32)


def _adaln_kernel(c_ref, w_ref, b_ref, o_ref):
    c = c_ref[...]
    s = (c * _sigmoid(c)).astype(BF16)
    o_ref[...] = _dot(s, w_ref[...].astype(BF16)) + b_ref[...]


def _adaln(cond, w, b):
    n = cond.shape[0]
    tn = 1024
    return pl.pallas_call(
        _adaln_kernel,
        grid=(w.shape[1] // tn,),
        in_specs=[pl.BlockSpec((n, D_MODEL), lambda j: (0, 0)),
                  pl.BlockSpec((D_MODEL, tn), lambda j: (0, j)),
                  pl.BlockSpec((1, tn), lambda j: (0, j))],
        out_specs=pl.BlockSpec((n, tn), lambda j: (0, j)),
        out_shape=jax.ShapeDtypeStruct((n, w.shape[1]), F32),
        compiler_params=_cp(("arbitrary",)),
    )(cond, w, b)


def _proj_kernel(x_ref, mod_ref, g_ref, w_ref, o_ref):
    x = x_ref[...]
    ms = jnp.mean(x * x, axis=-1, keepdims=True)
    y = x * lax.rsqrt(ms + EPS) * g_ref[...]
    h = y * (1.0 + mod_ref[0, 1:2, :]) + mod_ref[0, 0:1, :]
    o_ref[...] = _dot(h.astype(BF16), w_ref[...])


def _in_proj(x, mods, gain, w_bf, T):
    n = x.shape[0]
    per_req = T // PROJ_TILE if T >= PROJ_TILE else None
    if per_req is None:
        reqs = PROJ_TILE // T
        mod_map = lambda i: (0, 0, 0)
        assert mods.shape[0] == 1 and reqs >= 1
    else:
        mod_map = lambda i: (i // per_req, 0, 0)
    return pl.pallas_call(
        _proj_kernel,
        grid=(n // PROJ_TILE,),
        in_specs=[pl.BlockSpec((PROJ_TILE, D_MODEL), lambda i: (i, 0)),
                  pl.BlockSpec((1, N_MOD, D_MODEL), mod_map),
                  pl.BlockSpec((1, D_MODEL), lambda i: (0, 0)),
                  pl.BlockSpec((D_MODEL, D_IN), lambda i: (0, 0))],
        out_specs=pl.BlockSpec((PROJ_TILE, D_IN), lambda i: (i, 0)),
        out_shape=jax.ShapeDtypeStruct((n, D_IN), F32),
        compiler_params=_cp(("arbitrary",)),
    )(x, mods, gain, w_bf)


def _lru_kernel(T, row_len, xr_ref, cw_ref, cb_ref, w_ref, bias_ref, lam_ref, h0f_ref, h0b_ref,
                hs_ref, lf_ref, lb_ref, af, uf, ab, ub):
    pitch = T + PITCH_PAD
    pos = lax.broadcasted_iota(jnp.int32, (T, LANES), 0) & (row_len - 1)
    cw = cw_ref[...]
    cb = cb_ref[...]
    nl = -lam_ref[...]
    sp = jnp.maximum(nl, 0.0) + jnp.log1p(jnp.exp(-jnp.abs(nl)))
    w = w_ref[0]
    bias = bias_ref[0]
    for b in range(LRU_GROUP):
        x = xr_ref[b]
        xm2 = jnp.where(pos >= 2, pltpu.roll(x, 2, 0), 0.0)
        xm1 = jnp.where(pos >= 1, pltpu.roll(x, 1, 0), 0.0)
        xp1 = jnp.where(pos <= row_len - 2, pltpu.roll(x, T - 1, 0), 0.0)
        xc = cw[0:1] * xm2 + cw[1:2] * xm1 + cw[2:3] * x + cw[3:4] * xp1 + cb
        gates = _dot(xc.astype(BF16), w) + bias
        rows = pl.ds(b * pitch, T)
        for d, (a_s, u_s) in enumerate(((af, uf), (ab, ub))):
            r = _sigmoid(gates[:, (2 * d) * LANES:(2 * d + 1) * LANES])
            i = _sigmoid(gates[:, (2 * d + 1) * LANES:(2 * d + 2) * LANES])
            a = jnp.exp(-LRU_C * r * sp[d:d + 1])
            a_s[rows, :] = a
            u_s[rows, :] = jnp.sqrt(1.0 - a * a) * (i * xc)

    def step(t, carry):
        hf, hb = carry
        rf = pl.ds(t, LRU_GROUP, stride=pitch)
        hf = af[rf, :] * hf + uf[rf, :]
        uf[rf, :] = hf
        rb = pl.ds(T - 1 - t, LRU_GROUP, stride=pitch)
        hb = ab[rb, :] * hb + ub[rb, :]
        ub[rb, :] = hb
        return hf, hb

    hf, hb = lax.fori_loop(0, T, step, (h0f_ref[...], h0b_ref[...]), unroll=8)
    lf_ref[...] = hf
    lb_ref[...] = hb
    for b in range(LRU_GROUP):
        rows = pl.ds(b * pitch, T)
        hs_ref[b] = uf[rows, :] + ub[rows, :]


def _lru(proj3, conv_w, conv_b, w_tiles, bias_tiles, lam, h0f, h0b, T, row_len):
    B = proj3.shape[0]
    n_ct = D_LRU // LANES
    rows = LRU_GROUP * (T + PITCH_PAD)
    vec = pl.BlockSpec((LRU_GROUP, LANES), lambda g, c: (g, c))
    return pl.pallas_call(
        functools.partial(_lru_kernel, T, row_len),
        grid=(B // LRU_GROUP, n_ct),
        in_specs=[pl.BlockSpec((LRU_GROUP, T, LANES), lambda g, c: (g, 0, c)),
                  pl.BlockSpec((4, LANES), lambda g, c: (0, c)),
                  pl.BlockSpec((1, LANES), lambda g, c: (0, c)),
                  pl.BlockSpec((1, LANES, 4 * LANES), lambda g, c: (c, 0, 0)),
                  pl.BlockSpec((1, 1, 4 * LANES), lambda g, c: (c, 0, 0)),
                  pl.BlockSpec((2, LANES), lambda g, c: (0, c)),
                  vec, vec],
        out_specs=[pl.BlockSpec((LRU_GROUP, T, LANES), lambda g, c: (g, 0, c)), vec, vec],
        out_shape=[jax.ShapeDtypeStruct((B, T, D_LRU), F32),
                   jax.ShapeDtypeStruct((B, D_LRU), F32),
                   jax.ShapeDtypeStruct((B, D_LRU), F32)],
        scratch_shapes=[pltpu.VMEM((rows, LANES), F32)] * 4,
        compiler_params=_cp(("arbitrary", "arbitrary")),
    )(proj3, conv_w, conv_b, w_tiles, bias_tiles, lam, h0f, h0b)


def _hgrn_kernel(T, has_s0, want_state, *refs):
    q_ref, ff_ref, fb_ref, v_ref, og_ref, gam_ref, gain_ref = refs[:7]
    refs = refs[7:]
    if has_s0:
        s0_ref, refs = refs[0], refs[1:]
    y_ref, refs = refs[0], refs[1:]
    if want_state:
        st_ref, refs = refs[0], refs[1:]
    q_s, v_s, kf_s, kb_s, bf_s, bb_s, of_s, ob_s = refs
    C = HG_CHUNK
    n = T // C

    qr = q_ref[...]
    q_s[...] = qr * _sigmoid(qr) * (HG_DK ** -0.5)
    v_s[...] = v_ref[...].astype(BF16)
    pos = lax.broadcasted_iota(jnp.int32, (T, LANES), 0) & (C - 1)
    for d, (f_ref, k_s, b_s) in enumerate(((ff_ref, kf_s, bf_s), (fb_ref, kb_s, bb_s))):
        g0 = gam_ref[d, 0:1, :]
        g1 = gam_ref[d, 1:2, :]
        m = jnp.maximum(g0, g1)
        e0 = jnp.exp(g0 - m)
        lbd = e0 / (e0 + jnp.exp(g1 - m))
        sig = _sigmoid(f_ref[...])
        k_s[...] = (1.0 - lbd) * (1.0 - sig)
        cs = jnp.log(lbd + (1.0 - lbd) * sig)
        s = 1
        while s < C:
            if d == 0:
                cs = cs + jnp.where(pos >= s, pltpu.roll(cs, s, 0), 0.0)
            else:
                cs = cs + jnp.where(pos <= C - 1 - s, pltpu.roll(cs, T - s, 0), 0.0)
            s *= 2
        b_s[...] = cs

    ri = lax.broadcasted_iota(jnp.int32, (C, C), 0)
    ci = lax.broadcasted_iota(jnp.int32, (C, C), 1)

    def chunk(c, st, k_s, b_s, fwd):
        sl = pl.ds(pl.multiple_of(c * C, C), C)
        q = q_s[sl, :]
        k = k_s[sl, :]
        b = b_s[sl, :]
        v = v_s[sl, :]
        bm = b[C // 2:C // 2 + 1, :]
        g = b[C - 1:C, :] if fwd else b[0:1, :]
        qe = (q * jnp.exp(b)).astype(BF16)
        qi = (q * jnp.exp(b - bm)).astype(BF16)
        e2 = jnp.exp(bm - b)
        ki = (k * e2).astype(BF16)
        kd = (k * (e2 * jnp.exp(g - bm))).astype(BF16)
        s = _dot_nt(qi, ki)
        p = jnp.where((ci <= ri) if fwd else (ci >= ri), s, 0.0).astype(BF16)
        o = _dot(p, v) + _dot_nt(qe, st.astype(BF16))
        st = st * jnp.exp(g) + _dot_tn(v, kd)
        return o, st

    def body(i, carry):
        stf, stb = carry
        o, stf = chunk(i, stf, kf_s, bf_s, True)
        of_s[pl.ds(pl.multiple_of(i * C, C), C), :] = o
        j = n - 1 - i
        o, stb = chunk(j, stb, kb_s, bb_s, False)
        ob_s[pl.ds(pl.multiple_of(j * C, C), C), :] = o
        return stf, stb

    if has_s0:
        init = (s0_ref[0].T, s0_ref[1].T)
    else:
        init = (jnp.zeros((LANES, LANES), F32), jnp.zeros((LANES, LANES), F32))
    stf, stb = lax.fori_loop(0, n, body, init)
    if want_state:
        st_ref[0] = stf.T
        st_ref[1] = stb.T

    o = of_s[...] + ob_s[...]
    ms = jnp.mean(o * o, axis=-1, keepdims=True)
    og = og_ref[...]
    y_ref[...] = o * lax.rsqrt(ms + EPS) * gain_ref[...] * (og * _sigmoid(og))


def _hgrn(proj3, gamma, gain, s0, T, want_state):
    B = proj3.shape[0]
    has_s0 = s0 is not None
    col0 = 2 * D_LRU // LANES

    def col(k):
        return pl.BlockSpec((None, T, LANES), lambda b, h, k=k: (b, 0, col0 + HG_HEADS * k + h))

    st_spec = pl.BlockSpec((None, 2, None, HG_DK, HG_DK), lambda b, h: (b, 0, h, 0, 0))
    in_specs = [col(0), col(1), col(2), col(3), col(4),
                pl.BlockSpec((2, 2, LANES), lambda b, h: (0, 0, h)),
                pl.BlockSpec((1, LANES), lambda b, h: (0, 0))]
    args = [proj3, proj3, proj3, proj3, proj3, gamma, gain]
    if has_s0:
        in_specs.append(st_spec)
        args.append(s0)
    out_specs = [pl.BlockSpec((None, T, LANES), lambda b, h: (b, 0, h))]
    out_shape = [jax.ShapeDtypeStruct((B, T, D_HGRN), F32)]
    if want_state:
        out_specs.append(st_spec)
        out_shape.append(jax.ShapeDtypeStruct((B, 2, HG_HEADS, HG_DK, HG_DK), F32))
    res = pl.pallas_call(
        functools.partial(_hgrn_kernel, T, has_s0, want_state),
        grid=(B, HG_HEADS),
        in_specs=in_specs,
        out_specs=out_specs,
        out_shape=out_shape,
        scratch_shapes=[pltpu.VMEM((T, LANES), F32), pltpu.VMEM((T, LANES), BF16)]
        + [pltpu.VMEM((T, LANES), F32)] * 6,
        compiler_params=_cp(("arbitrary", "arbitrary")),
    )(*args)
    return res if want_state else (res[0], None)


def _mix_out_kernel(hs_ref, gr_ref, yh_ref, x_ref, mod_ref, nl_ref, wo_ref, nf_ref, wr_ref,
                    x1_ref, ha_ref, at_ref):
    hs = hs_ref[...]
    ms = jnp.mean(hs * hs, axis=-1, keepdims=True)
    gr = gr_ref[...]
    gelu = 0.5 * gr * (1.0 + jnp.tanh(0.7978845608028654 * (gr + 0.044715 * (gr * gr * gr))))
    y_lru = hs * lax.rsqrt(ms + EPS) * nl_ref[...] * gelu
    ycat = jnp.concatenate([y_lru.astype(BF16), yh_ref[...].astype(BF16)], axis=-1)
    y = _dot(ycat, wo_ref[...])
    x1 = x_ref[...] + mod_ref[0, 2:3, :] * y
    x1_ref[...] = x1
    ms = jnp.mean(x1 * x1, axis=-1, keepdims=True)
    h2 = x1 * lax.rsqrt(ms + EPS) * nf_ref[...] * (1.0 + mod_ref[0, 4:5, :]) + mod_ref[0, 3:4, :]
    logits = jnp.dot(h2, wr_ref[...], preferred_element_type=F32, precision=lax.Precision.HIGHEST)
    lane = lax.broadcasted_iota(jnp.int32, logits.shape, 1)
    logits = jnp.where(lane < N_EXPERTS, logits, -jnp.inf)
    e = jnp.exp(logits - jnp.max(logits, axis=-1, keepdims=True))
    aff = e / jnp.sum(e, axis=-1, keepdims=True)
    at_ref[...] = aff.T[0:N_EXPERTS, :]
    hi = aff.astype(BF16).astype(F32)
    r1 = aff - hi
    mid = r1.astype(BF16).astype(F32)
    lo = (r1 - mid).astype(BF16).astype(F32)
    split = hi + pltpu.roll(mid, N_EXPERTS, 1) + pltpu.roll(lo, 2 * N_EXPERTS, 1)
    ha_ref[:, 0:D_MODEL] = h2.astype(BF16)
    ha_ref[:, D_MODEL:D_AUG] = split.astype(BF16)


def _mix_out(hsum, proj, yh, x, mods, norm_lru, wo_bf, norm_ffn, wr_pad, T):
    n = x.shape[0]
    tiles = n // TOK_TILE
    q = T // TOK_TILE
    mod_map = (lambda i: (i // q, 0, 0)) if mods.shape[0] > 1 else (lambda i: (0, 0, 0))
    tile = lambda w: pl.BlockSpec((TOK_TILE, w), lambda i: (i, 0))
    const = lambda r, w: pl.BlockSpec((r, w), lambda i: (0, 0))
    return pl.pallas_call(
        _mix_out_kernel,
        grid=(tiles,),
        in_specs=[tile(D_LRU),
                  pl.BlockSpec((TOK_TILE, D_LRU), lambda i: (i, 1)),
                  tile(D_HGRN), tile(D_MODEL),
                  pl.BlockSpec((1, N_MOD, D_MODEL), mod_map),
                  const(1, D_LRU), const(D_MODEL, D_MODEL), const(1, D_MODEL), const(D_MODEL, LANES)],
        out_specs=[tile(D_MODEL), tile(D_AUG),
                   pl.BlockSpec((N_EXPERTS, TOK_TILE), lambda i: (i, 0))],
        out_shape=[jax.ShapeDtypeStruct((n, D_MODEL), F32),
                   jax.ShapeDtypeStruct((n, D_AUG), BF16),
                   jax.ShapeDtypeStruct((tiles * N_EXPERTS, TOK_TILE), F32)],
        compiler_params=_cp(("arbitrary",)),
    )(hsum, proj, yh, x, mods, norm_lru, wo_bf, norm_ffn, wr_pad)


def _select_kernel(q, cap, at_ref, slot_ref):
    R = at_ref.shape[0]
    W = TOK_TILE
    bits = pltpu.bitcast(at_ref[...], jnp.int32)
    ones = jnp.ones((W, LANES), BF16)
    su = (lax.broadcasted_iota(jnp.int32, (W, W), 0)
          < lax.broadcasted_iota(jnp.int32, (W, W), 1)).astype(BF16)
    if q > 1:
        rr = lax.broadcasted_iota(jnp.int32, (R, R), 0)
        cc = lax.broadcasted_iota(jnp.int32, (R, R), 1)
        sh = (N_EXPERTS * q).bit_length() - 1
        same = ((rr >> sh) == (cc >> sh)) & ((rr & (N_EXPERTS - 1)) == (cc & (N_EXPERTS - 1)))
        g_all = same.astype(BF16)
        g_prev = (same & (cc < rr)).astype(BF16)

    def wide(x):
        return jnp.concatenate([x] * (W // LANES), axis=1)

    def tile_count(mask):
        return _dot(mask.astype(BF16), ones)

    def req_count(mask):
        c = tile_count(mask)
        return _dot(g_all, c.astype(BF16)) if q > 1 else c

    def prefix(mask):
        p = _dot(mask.astype(BF16), su)
        if q > 1:
            p = p + wide(_dot(g_prev, tile_count(mask).astype(BF16)))
        return p

    thr = jnp.zeros((R, LANES), jnp.int32)
    for bit in range(30, -1, -1):
        cand = thr | (1 << bit)
        cnt = req_count(bits >= wide(cand))
        thr = jnp.where(cnt >= cap, cand, thr)
    thr_w = wide(thr)
    gt = bits > thr_w
    eq = bits == thr_w
    need = wide(cap - req_count(gt))
    sel = gt | (eq & (prefix(eq) < need))
    slot_ref[...] = jnp.where(sel, prefix(sel), -1.0).astype(jnp.int32)


def _select(at, T):
    q = T // TOK_TILE
    cap = EC_CAPACITY_FACTOR * T // N_EXPERTS
    R = at.shape[0]
    return pl.pallas_call(
        functools.partial(_select_kernel, q, cap),
        grid=(1,),
        in_specs=[pl.BlockSpec((R, TOK_TILE), lambda i: (0, 0))],
        out_specs=pl.BlockSpec((R, TOK_TILE), lambda i: (0, 0)),
        out_shape=jax.ShapeDtypeStruct((R, TOK_TILE), jnp.int32),
        compiler_params=_cp(("arbitrary",)),
    )(at)


def _dispatch_kernel(q, cap, slot_ref, ha_ref, xd_ref):
    ha = ha_ref[...]
    j = lax.broadcasted_iota(jnp.int32, (cap, TOK_TILE), 0)
    first = pl.program_id(0) % q == 0
    for e in range(N_EXPERTS):
        onehot = (slot_ref[e:e + 1, :] == j).astype(BF16)
        rows = _dot(onehot, ha).astype(BF16)

        @pl.when(first)
        def _():
            xd_ref[e] = rows

        if q > 1:
            @pl.when(jnp.logical_not(first))
            def _():
                xd_ref[e] = xd_ref[e] + rows


def _dispatch(slot, ha, T):
    q = T // TOK_TILE
    cap = EC_CAPACITY_FACTOR * T // N_EXPERTS
    tiles = ha.shape[0] // TOK_TILE
    B = tiles // q
    return pl.pallas_call(
        functools.partial(_dispatch_kernel, q, cap),
        grid=(tiles,),
        in_specs=[pl.BlockSpec((N_EXPERTS, TOK_TILE), lambda i: (i, 0)),
                  pl.BlockSpec((TOK_TILE, D_AUG), lambda i: (i, 0))],
        out_specs=pl.BlockSpec((N_EXPERTS, cap, D_AUG), lambda i: (0, i // q, 0)),
        out_shape=jax.ShapeDtypeStruct((N_EXPERTS, B * cap, D_AUG), BF16),
        compiler_params=_cp(("arbitrary",)),
    )(slot, ha)


def _ffn_kernel(xp_ref, xs_ref, wg_ref, wu_ref, wd_ref, yp_ref, ys_ref, wg_s, wu_s, wd_s):
    e = pl.program_id(0)

    @pl.when(pl.program_id(1) == 0)
    def _():
        wg_s[...] = wg_ref[...].astype(BF16)
        wu_s[...] = wu_ref[...].astype(BF16)
        wd_s[...] = wd_ref[...].astype(BF16)

    lane = lax.broadcasted_iota(jnp.int32, (FFN_TILE, LANES), 1)
    pick = (lane == e) | (lane == e + N_EXPERTS) | (lane == e + 2 * N_EXPERTS)
    for x_ref, y_ref in ((xp_ref, yp_ref), (xs_ref, ys_ref)):
        x = x_ref[:, 0:D_MODEL]
        gate = jnp.sum(jnp.where(pick, x_ref[:, D_MODEL:D_AUG].astype(F32), 0.0), axis=-1, keepdims=True)
        hg = _dot(x, wg_s[...])
        hid = (hg * _sigmoid(hg)) * _dot(x, wu_s[...])
        y_ref[...] = (_dot(hid.astype(BF16), wd_s[...]) * gate).astype(BF16)


def _ffn(xd_p, xd_s, w_gate, w_up, w_down):
    rows = xd_p.shape[1]
    assert xd_s.shape[1] == rows and rows % FFN_TILE == 0
    xspec = pl.BlockSpec((None, FFN_TILE, D_AUG), lambda e, m: (e, m, 0))
    wspec = pl.BlockSpec((None, D_MODEL, D_MODEL), lambda e, m: (e, 0, 0))
    yspec = pl.BlockSpec((None, FFN_TILE, D_MODEL), lambda e, m: (e, m, 0))
    yshape = jax.ShapeDtypeStruct((N_EXPERTS, rows, D_MODEL), BF16)
    return pl.pallas_call(
        _ffn_kernel,
        grid=(N_EXPERTS, rows // FFN_TILE),
        in_specs=[xspec, xspec, wspec, wspec, wspec],
        out_specs=[yspec, yspec],
        out_shape=[yshape, yshape],
        scratch_shapes=[pltpu.VMEM((D_MODEL, D_MODEL), BF16)] * 3,
        compiler_params=_cp(("arbitrary", "arbitrary")),
    )(xd_p, xd_s, w_gate, w_up, w_down)


def _combine_kernel(cap, slot_ref, yd_ref, x1_ref, mod_ref, nf_ref, o_ref):
    j = lax.broadcasted_iota(jnp.int32, (cap, TOK_TILE), 0)
    acc = jnp.zeros((TOK_TILE, D_MODEL), F32)
    for e in range(N_EXPERTS):
        onehot = (slot_ref[e:e + 1, :] == j).astype(BF16)
        acc = acc + _dot_tn(onehot, yd_ref[e])
    x2 = x1_ref[...] + mod_ref[0, 5:6, :] * acc
    ms = jnp.mean(x2 * x2, axis=-1, keepdims=True)
    o_ref[...] = x2 * lax.rsqrt(ms + EPS) * nf_ref[...]


def _combine(slot, yd, x1, mods, norm_final, T):
    q = T // TOK_TILE
    cap = EC_CAPACITY_FACTOR * T // N_EXPERTS
    n = x1.shape[0]
    mod_map = (lambda i: (i // q, 0, 0)) if mods.shape[0] > 1 else (lambda i: (0, 0, 0))
    return pl.pallas_call(
        functools.partial(_combine_kernel, cap),
        grid=(n // TOK_TILE,),
        in_specs=[pl.BlockSpec((N_EXPERTS, TOK_TILE), lambda i: (i, 0)),
                  pl.BlockSpec((N_EXPERTS, cap, D_MODEL), lambda i: (0, i // q, 0)),
                  pl.BlockSpec((TOK_TILE, D_MODEL), lambda i: (i, 0)),
                  pl.BlockSpec((1, N_MOD, D_MODEL), mod_map),
                  pl.BlockSpec((1, D_MODEL), lambda i: (0, 0))],
        out_specs=pl.BlockSpec((TOK_TILE, D_MODEL), lambda i: (i, 0)),
        out_shape=jax.ShapeDtypeStruct((n, D_MODEL), F32),
        compiler_params=_cp(("arbitrary",)),
    )(slot, yd, x1, mods, norm_final)


def _gate_tiles(wa, wx):
    eye = jnp.eye(LRU_BLOCKS, dtype=wa.dtype)

    def dense(w):
        full = jnp.einsum('nde,nm->ndme', w, eye).reshape(D_LRU, D_LRU)
        t = full.reshape(D_LRU // LANES, LANES, D_LRU // LANES, LANES)
        return jnp.stack([t[c, :, c, :] for c in range(D_LRU // LANES)])

    return jnp.concatenate([dense(wa[0]), dense(wx[0]), dense(wa[1]), dense(wx[1])], axis=-1).astype(BF16)


def _bias_tiles(ba, bx):
    t = lambda v: v.reshape(D_LRU // LANES, 1, LANES)
    return jnp.concatenate([t(ba[0]), t(bx[0]), t(ba[1]), t(bx[1])], axis=-1)


def _mixer_and_router(x, mods, p, T, row_len, h0f, h0b, s0, want_state):
    B = x.shape[0]
    xf = x.reshape(B * T, D_MODEL)
    proj = _in_proj(xf, mods, p["norm_mix"], p["w_in"], T)
    proj3 = proj.reshape(B, T, D_IN)
    hsum, last_f, last_b = _lru(proj3, p["conv_w"], p["conv_b"], p["gate_w"], p["gate_b"],
                                p["lam"], h0f, h0b, T, row_len)
    yh, st = _hgrn(proj3, p["gamma"], p["norm_hgrn"], s0, T, want_state)
    x1, ha, at = _mix_out(hsum.reshape(B * T, D_LRU), proj, yh.reshape(B * T, D_HGRN), xf, mods,
                          p["norm_lru"], p["w_out"], p["norm_ffn"], p["w_router"], T)
    slot = _select(at, T)
    xd = _dispatch(slot, ha, T)
    return x1, slot, xd, last_f, last_b, st


def kernel(x_prompt, x_sample, state_lru, state_hgrn, c, c_ctx, w_ada, b_ada, norm_mix, w_in, conv_w,
           conv_b, lru_wa, lru_ba, lru_wx, lru_bx, lru_lambda, norm_lru, hgrn_gamma, norm_hgrn, w_out,
           norm_ffn, w_router, w_gate, w_up, w_down, norm_final):
    assert w_ada.shape[0] == 1 and hgrn_gamma.shape[1] == 2, "one trunk layer"
    l = 0
    Bp, Tp, _ = x_prompt.shape
    Bs, Ts, _ = x_sample.shape

    cond = jnp.concatenate([c_ctx[None, :], c, jnp.zeros((16 - 1 - Bs, D_MODEL), F32)], axis=0)
    mods = _adaln(cond, w_ada[l], b_ada[l][None, :]).reshape(16, N_MOD, D_MODEL)
    mods_p, mods_s = mods[0:1], mods[1:1 + Bs]

    p = {
        "norm_mix": norm_mix[l][None, :], "w_in": w_in[l].astype(BF16),
        "conv_w": conv_w[l], "conv_b": conv_b[l][None, :],
        "gate_w": _gate_tiles(lru_wa[l], lru_wx[l]), "gate_b": _bias_tiles(lru_ba[l], lru_bx[l]),
        "lam": lru_lambda[l], "norm_lru": norm_lru[l][None, :],
        "gamma": hgrn_gamma, "norm_hgrn": norm_hgrn[l][None, :],
        "w_out": w_out[l].astype(BF16), "norm_ffn": norm_ffn[l][None, :],
        "w_router": jnp.pad(w_router[l], ((0, 0), (0, LANES - N_EXPERTS))),
    }
    zeros_p = jnp.zeros((Bp, D_LRU), F32)
    x1p, slot_p, xd_p, last_f, last_b, st_p = _mixer_and_router(
        x_prompt, mods_p, p, Tp, Tp, zeros_p, zeros_p, None, True)
    x1s, slot_s, xd_s, _, _, _ = _mixer_and_router(
        x_sample, mods_s, p, Ts, GRID_W, state_lru[:, l, 0], state_lru[:, l, 1], state_hgrn[:, l], False)

    yd_p, yd_s = _ffn(xd_p, xd_s, w_gate[l], w_up[l], w_down[l])
    nf = norm_final[None, :]
    y_prompt = _combine(slot_p, yd_p, x1p, mods_p, nf, Tp).reshape(Bp, Tp, D_MODEL)
    y_sample = _combine(slot_s, yd_s, x1s, mods_s, nf, Ts).reshape(Bs, Ts, D_MODEL)
    new_state_lru = jnp.stack([last_f, last_b], axis=1)[:, None]
    new_state_hgrn = st_p[:, None]
    return (y_prompt, y_sample, new_state_lru, new_state_hgrn)
```

```python
import functools

import jax
import jax.numpy as jnp
from jax import lax
from jax.experimental import pallas as pl
from jax.experimental.pallas import tpu as pltpu

F32 = jnp.float32
BF16 = jnp.bfloat16

D_MODEL = 1024
D_LRU = 512
D_HGRN = 512
HG_HEADS = 4
HG_DK = 128
LRU_BLOCKS = 8
LRU_BW = 64
LRU_C = 8.0
N_EXPERTS = 16
EC_CAPACITY_FACTOR = 2
N_MOD = 6
D_IN = 7 * 512
GRID_W = 64
EPS = 1e-6

LANES = 128
TOK_TILE = 256
PROJ_TILE = 512
LRU_GROUP = 8
PITCH_PAD = 8
HG_CHUNK = 64
FFN_TILE = 512
D_AUG = D_MODEL + LANES
VMEM_LIMIT = 56 * 1024 * 1024


def _cp(sem, vmem=VMEM_LIMIT):
    return pltpu.CompilerParams(dimension_semantics=sem, vmem_limit_bytes=vmem)


def _sigmoid(x):
    return 0.5 * (1.0 + jnp.tanh(0.5 * x))


def _dot(a, b):
    return jnp.dot(a, b, preferred_element_type=F32)


def _dot_nt(a, b):
    return lax.dot_general(a, b, (((1,), (1,)), ((), ())), preferred_element_type=F32)


def _dot_tn(a, b):
    return lax.dot_general(a, b, (((0,), (0,)), ((), ())), preferred_element_type=F32)


def _adaln_kernel(c_ref, w_ref, b_ref, o_ref):
    c = c_ref[...]
    s = (c * _sigmoid(c)).astype(BF16)
    o_ref[...] = _dot(s, w_ref[...].astype(BF16)) + b_ref[...]


def _adaln(cond, w, b):
    n = cond.shape[0]
    tn = 1024
    return pl.pallas_call(
        _adaln_kernel,
        grid=(w.shape[1] // tn,),
        in_specs=[pl.BlockSpec((n, D_MODEL), lambda j: (0, 0)),
                  pl.BlockSpec((D_MODEL, tn), lambda j: (0, j)),
                  pl.BlockSpec((1, tn), lambda j: (0, j))],
        out_specs=pl.BlockSpec((n, tn), lambda j: (0, j)),
        out_shape=jax.ShapeDtypeStruct((n, w.shape[1]), F32),
        compiler_params=_cp(("arbitrary",)),
    )(cond, w, b)


def _proj_kernel(x_ref, mod_ref, g_ref, w_ref, o_ref):
    x = x_ref[...]
    ms = jnp.mean(x * x, axis=-1, keepdims=True)
    y = x * lax.rsqrt(ms + EPS) * g_ref[...]
    h = y * (1.0 + mod_ref[0, 1:2, :]) + mod_ref[0, 0:1, :]
    o_ref[...] = _dot(h.astype(BF16), w_ref[...])


def _in_proj(x, mods, gain, w_bf, T):
    n = x.shape[0]
    per_req = T // PROJ_TILE if T >= PROJ_TILE else None
    if per_req is None:
        reqs = PROJ_TILE // T
        mod_map = lambda i: (0, 0, 0)
        assert mods.shape[0] == 1 and reqs >= 1
    else:
        mod_map = lambda i: (i // per_req, 0, 0)
    return pl.pallas_call(
        _proj_kernel,
        grid=(n // PROJ_TILE,),
        in_specs=[pl.BlockSpec((PROJ_TILE, D_MODEL), lambda i: (i, 0)),
                  pl.BlockSpec((1, N_MOD, D_MODEL), mod_map),
                  pl.BlockSpec((1, D_MODEL), lambda i: (0, 0)),
                  pl.BlockSpec((D_MODEL, D_IN), lambda i: (0, 0))],
        out_specs=pl.BlockSpec((PROJ_TILE, D_IN), lambda i: (i, 0)),
        out_shape=jax.ShapeDtypeStruct((n, D_IN), F32),
        compiler_params=_cp(("arbitrary",)),
    )(x, mods, gain, w_bf)


def _lru_kernel(T, row_len, xr_ref, cw_ref, cb_ref, w_ref, bias_ref, lam_ref, h0f_ref, h0b_ref,
                hs_ref, lf_ref, lb_ref, af, uf, ab, ub):
    pitch = T + PITCH_PAD
    pos = lax.broadcasted_iota(jnp.int32, (T, LANES), 0) & (row_len - 1)
    cw = cw_ref[...]
    cb = cb_ref[...]
    nl = -lam_ref[...]
    sp = jnp.maximum(nl, 0.0) + jnp.log1p(jnp.exp(-jnp.abs(nl)))
    w = w_ref[0]
    bias = bias_ref[0]
    for b in range(LRU_GROUP):
        x = xr_ref[b]
        xm2 = jnp.where(pos >= 2, pltpu.roll(x, 2, 0), 0.0)
        xm1 = jnp.where(pos >= 1, pltpu.roll(x, 1, 0), 0.0)
        xp1 = jnp.where(pos <= row_len - 2, pltpu.roll(x, T - 1, 0), 0.0)
        xc = cw[0:1] * xm2 + cw[1:2] * xm1 + cw[2:3] * x + cw[3:4] * xp1 + cb
        gates = _dot(xc.astype(BF16), w) + bias
        rows = pl.ds(b * pitch, T)
        for d, (a_s, u_s) in enumerate(((af, uf), (ab, ub))):
            r = _sigmoid(gates[:, (2 * d) * LANES:(2 * d + 1) * LANES])
            i = _sigmoid(gates[:, (2 * d + 1) * LANES:(2 * d + 2) * LANES])
            a = jnp.exp(-LRU_C * r * sp[d:d + 1])
            a_s[rows, :] = a
            u_s[rows, :] = jnp.sqrt(1.0 - a * a) * (i * xc)

    def step(t, carry):
        hf, hb = carry
        rf = pl.ds(t, LRU_GROUP, stride=pitch)
        hf = af[rf, :] * hf + uf[rf, :]
        uf[rf, :] = hf
        rb = pl.ds(T - 1 - t, LRU_GROUP, stride=pitch)
        hb = ab[rb, :] * hb + ub[rb, :]
        ub[rb, :] = hb
        return hf, hb

    hf, hb = lax.fori_loop(0, T, step, (h0f_ref[...], h0b_ref[...]), unroll=8)
    lf_ref[...] = hf
    lb_ref[...] = hb
    for b in range(LRU_GROUP):
        rows = pl.ds(b * pitch, T)
        hs_ref[b] = uf[rows, :] + ub[rows, :]


def _lru(proj3, conv_w, conv_b, w_tiles, bias_tiles, lam, h0f, h0b, T, row_len):
    B = proj3.shape[0]
    n_ct = D_LRU // LANES
    rows = LRU_GROUP * (T + PITCH_PAD)
    vec = pl.BlockSpec((LRU_GROUP, LANES), lambda g, c: (g, c))
    return pl.pallas_call(
        functools.partial(_lru_kernel, T, row_len),
        grid=(B // LRU_GROUP, n_ct),
        in_specs=[pl.BlockSpec((LRU_GROUP, T, LANES), lambda g, c: (g, 0, c)),
                  pl.BlockSpec((4, LANES), lambda g, c: (0, c)),
                  pl.BlockSpec((1, LANES), lambda g, c: (0, c)),
                  pl.BlockSpec((1, LANES, 4 * LANES), lambda g, c: (c, 0, 0)),
                  pl.BlockSpec((1, 1, 4 * LANES), lambda g, c: (c, 0, 0)),
                  pl.BlockSpec((2, LANES), lambda g, c: (0, c)),
                  vec, vec],
        out_specs=[pl.BlockSpec((LRU_GROUP, T, LANES), lambda g, c: (g, 0, c)), vec, vec],
        out_shape=[jax.ShapeDtypeStruct((B, T, D_LRU), F32),
                   jax.ShapeDtypeStruct((B, D_LRU), F32),
                   jax.ShapeDtypeStruct((B, D_LRU), F32)],
        scratch_shapes=[pltpu.VMEM((rows, LANES), F32)] * 4,
        compiler_params=_cp(("arbitrary", "arbitrary")),
    )(proj3, conv_w, conv_b, w_tiles, bias_tiles, lam, h0f, h0b)


def _hgrn_kernel(T, has_s0, want_state, *refs):
    q_ref, ff_ref, fb_ref, v_ref, og_ref, gam_ref, gain_ref = refs[:7]
    refs = refs[7:]
    if has_s0:
        s0_ref, refs = refs[0], refs[1:]
    y_ref, refs = refs[0], refs[1:]
    if want_state:
        st_ref, refs = refs[0], refs[1:]
    q_s, v_s, kf_s, kb_s, bf_s, bb_s, of_s, ob_s = refs
    C = HG_CHUNK
    n = T // C

    qr = q_ref[...]
    q_s[...] = qr * _sigmoid(qr) * (HG_DK ** -0.5)
    v_s[...] = v_ref[...].astype(BF16)
    pos = lax.broadcasted_iota(jnp.int32, (T, LANES), 0) & (C - 1)
    for d, (f_ref, k_s, b_s) in enumerate(((ff_ref, kf_s, bf_s), (fb_ref, kb_s, bb_s))):
        g0 = gam_ref[d, 0:1, :]
        g1 = gam_ref[d, 1:2, :]
        m = jnp.maximum(g0, g1)
        e0 = jnp.exp(g0 - m)
        lbd = e0 / (e0 + jnp.exp(g1 - m))
        sig = _sigmoid(f_ref[...])
        k_s[...] = (1.0 - lbd) * (1.0 - sig)
        cs = jnp.log(lbd + (1.0 - lbd) * sig)
        s = 1
        while s < C:
            if d == 0:
                cs = cs + jnp.where(pos >= s, pltpu.roll(cs, s, 0), 0.0)
            else:
                cs = cs + jnp.where(pos <= C - 1 - s, pltpu.roll(cs, T - s, 0), 0.0)
            s *= 2
        b_s[...] = cs

    ri = lax.broadcasted_iota(jnp.int32, (C, C), 0)
    ci = lax.broadcasted_iota(jnp.int32, (C, C), 1)

    def chunk(c, st, k_s, b_s, fwd):
        sl = pl.ds(pl.multiple_of(c * C, C), C)
        q = q_s[sl, :]
        k = k_s[sl, :]
        b = b_s[sl, :]
        v = v_s[sl, :]
        bm = b[C // 2:C // 2 + 1, :]
        g = b[C - 1:C, :] if fwd else b[0:1, :]
        qe = (q * jnp.exp(b)).astype(BF16)
        qi = (q * jnp.exp(b - bm)).astype(BF16)
        e2 = jnp.exp(bm - b)
        ki = (k * e2).astype(BF16)
        kd = (k * (e2 * jnp.exp(g - bm))).astype(BF16)
        s = _dot_nt(qi, ki)
        p = jnp.where((ci <= ri) if fwd else (ci >= ri), s, 0.0).astype(BF16)
        o = _dot(p, v) + _dot_nt(qe, st.astype(BF16))
        st = st * jnp.exp(g) + _dot_tn(v, kd)
        return o, st

    def body(i, carry):
        stf, stb = carry
        o, stf = chunk(i, stf, kf_s, bf_s, True)
        of_s[pl.ds(pl.multiple_of(i * C, C), C), :] = o
        j = n - 1 - i
        o, stb = chunk(j, stb, kb_s, bb_s, False)
        ob_s[pl.ds(pl.multiple_of(j * C, C), C), :] = o
        return stf, stb

    if has_s0:
        init = (s0_ref[0].T, s0_ref[1].T)
    else:
        init = (jnp.zeros((LANES, LANES), F32), jnp.zeros((LANES, LANES), F32))
    stf, stb = lax.fori_loop(0, n, body, init)
    if want_state:
        st_ref[0] = stf.T
        st_ref[1] = stb.T

    o = of_s[...] + ob_s[...]
    ms = jnp.mean(o * o, axis=-1, keepdims=True)
    og = og_ref[...]
    y_ref[...] = o * lax.rsqrt(ms + EPS) * gain_ref[...] * (og * _sigmoid(og))


def _hgrn(proj3, gamma, gain, s0, T, want_state):
    B = proj3.shape[0]
    has_s0 = s0 is not None
    col0 = 2 * D_LRU // LANES

    def col(k):
        return pl.BlockSpec((None, T, LANES), lambda b, h, k=k: (b, 0, col0 + HG_HEADS * k + h))

    st_spec = pl.BlockSpec((None, 2, None, HG_DK, HG_DK), lambda b, h: (b, 0, h, 0, 0))
    in_specs = [col(0), col(1), col(2), col(3), col(4),
                pl.BlockSpec((2, 2, LANES), lambda b, h: (0, 0, h)),
                pl.BlockSpec((1, LANES), lambda b, h: (0, 0))]
    args = [proj3, proj3, proj3, proj3, proj3, gamma, gain]
    if has_s0:
        in_specs.append(st_spec)
        args.append(s0)
    out_specs = [pl.BlockSpec((None, T, LANES), lambda b, h: (b, 0, h))]
    out_shape = [jax.ShapeDtypeStruct((B, T, D_HGRN), F32)]
    if want_state:
        out_specs.append(st_spec)
        out_shape.append(jax.ShapeDtypeStruct((B, 2, HG_HEADS, HG_DK, HG_DK), F32))
    res = pl.pallas_call(
        functools.partial(_hgrn_kernel, T, has_s0, want_state),
        grid=(B, HG_HEADS),
        in_specs=in_specs,
        out_specs=out_specs,
        out_shape=out_shape,
        scratch_shapes=[pltpu.VMEM((T, LANES), F32), pltpu.VMEM((T, LANES), BF16)]
        + [pltpu.VMEM((T, LANES), F32)] * 6,
        compiler_params=_cp(("arbitrary", "arbitrary")),
    )(*args)
    return res if want_state else (res[0], None)


def _mix_out_kernel(hs_ref, gr_ref, yh_ref, x_ref, mod_ref, nl_ref, wo_ref, nf_ref, wr_ref,
                    x1_ref, ha_ref, at_ref):
    hs = hs_ref[...]
    ms = jnp.mean(hs * hs, axis=-1, keepdims=True)
    gr = gr_ref[...]
    gelu = 0.5 * gr * (1.0 + jnp.tanh(0.7978845608028654 * (gr + 0.044715 * (gr * gr * gr))))
    y_lru = hs * lax.rsqrt(ms + EPS) * nl_ref[...] * gelu
    ycat = jnp.concatenate([y_lru.astype(BF16), yh_ref[...].astype(BF16)], axis=-1)
    y = _dot(ycat, wo_ref[...])
    x1 = x_ref[...] + mod_ref[0, 2:3, :] * y
    x1_ref[...] = x1
    ms = jnp.mean(x1 * x1, axis=-1, keepdims=True)
    h2 = x1 * lax.rsqrt(ms + EPS) * nf_ref[...] * (1.0 + mod_ref[0, 4:5, :]) + mod_ref[0, 3:4, :]
    h2_hi = h2.astype(BF16)
    h2_mid = (h2 - h2_hi.astype(F32)).astype(BF16)
    wr = wr_ref[...]
    p_hi = _dot(h2_hi, wr)
    logits = p_hi[:, 0:LANES] + (p_hi[:, LANES:2 * LANES] + _dot(h2_mid, wr)[:, 0:LANES])
    lane = lax.broadcasted_iota(jnp.int32, logits.shape, 1)
    logits = jnp.where(lane < N_EXPERTS, logits, -jnp.inf)
    e = jnp.exp(logits - jnp.max(logits, axis=-1, keepdims=True))
    aff = e / jnp.sum(e, axis=-1, keepdims=True)
    at_ref[...] = aff.T[0:N_EXPERTS, :]
    hi = aff.astype(BF16).astype(F32)
    r1 = aff - hi
    mid = r1.astype(BF16).astype(F32)
    lo = (r1 - mid).astype(BF16).astype(F32)
    split = hi + pltpu.roll(mid, N_EXPERTS, 1) + pltpu.roll(lo, 2 * N_EXPERTS, 1)
    ha_ref[:, 0:D_MODEL] = h2_hi
    ha_ref[:, D_MODEL:D_AUG] = split.astype(BF16)


def _mix_out(hsum, proj, yh, x, mods, norm_lru, wo_bf, norm_ffn, wr_pad, T):
    n = x.shape[0]
    tiles = n // TOK_TILE
    q = T // TOK_TILE
    mod_map = (lambda i: (i // q, 0, 0)) if mods.shape[0] > 1 else (lambda i: (0, 0, 0))
    tile = lambda w: pl.BlockSpec((TOK_TILE, w), lambda i: (i, 0))
    const = lambda r, w: pl.BlockSpec((r, w), lambda i: (0, 0))
    return pl.pallas_call(
        _mix_out_kernel,
        grid=(tiles,),
        in_specs=[tile(D_LRU),
                  pl.BlockSpec((TOK_TILE, D_LRU), lambda i: (i, 1)),
                  tile(D_HGRN), tile(D_MODEL),
                  pl.BlockSpec((1, N_MOD, D_MODEL), mod_map),
                  const(1, D_LRU), const(D_MODEL, D_MODEL), const(1, D_MODEL), const(D_MODEL, 2 * LANES)],
        out_specs=[tile(D_MODEL), tile(D_AUG),
                   pl.BlockSpec((N_EXPERTS, TOK_TILE), lambda i: (i, 0))],
        out_shape=[jax.ShapeDtypeStruct((n, D_MODEL), F32),
                   jax.ShapeDtypeStruct((n, D_AUG), BF16),
                   jax.ShapeDtypeStruct((tiles * N_EXPERTS, TOK_TILE), F32)],
        compiler_params=_cp(("arbitrary",)),
    )(hsum, proj, yh, x, mods, norm_lru, wo_bf, norm_ffn, wr_pad)


def _select_kernel(q, cap, at_ref, slot_ref):
    R = at_ref.shape[0]
    W = TOK_TILE
    bits = pltpu.bitcast(at_ref[...], jnp.int32)
    ones = jnp.ones((W, LANES), BF16)
    su = (lax.broadcasted_iota(jnp.int32, (W, W), 0)
          < lax.broadcasted_iota(jnp.int32, (W, W), 1)).astype(BF16)
    if q > 1:
        rr = lax.broadcasted_iota(jnp.int32, (R, R), 0)
        cc = lax.broadcasted_iota(jnp.int32, (R, R), 1)
        sh = (N_EXPERTS * q).bit_length() - 1
        same = ((rr >> sh) == (cc >> sh)) & ((rr & (N_EXPERTS - 1)) == (cc & (N_EXPERTS - 1)))
        g_all = same.astype(BF16)
        g_prev = (same & (cc < rr)).astype(BF16)

    def wide(x):
        return jnp.concatenate([x] * (W // LANES), axis=1)

    def tile_count(mask):
        return _dot(mask.astype(BF16), ones)

    def req_count(mask):
        c = tile_count(mask)
        return _dot(g_all, c.astype(BF16)) if q > 1 else c

    def prefix(mask):
        p = _dot(mask.astype(BF16), su)
        if q > 1:
            p = p + wide(_dot(g_prev, tile_count(mask).astype(BF16)))
        return p

    thr = jnp.zeros((R, LANES), jnp.int32)
    for bit in range(30, -1, -1):
        cand = thr | (1 << bit)
        cnt = req_count(bits >= wide(cand))
        thr = jnp.where(cnt >= cap, cand, thr)
    thr_w = wide(thr)
    gt = bits > thr_w
    eq = bits == thr_w
    need = wide(cap - req_count(gt))
    sel = gt | (eq & (prefix(eq) < need))
    slot_ref[...] = jnp.where(sel, prefix(sel), -1.0).astype(jnp.int32)


def _select(at, T):
    q = T // TOK_TILE
    cap = EC_CAPACITY_FACTOR * T // N_EXPERTS
    R = at.shape[0]
    return pl.pallas_call(
        functools.partial(_select_kernel, q, cap),
        grid=(1,),
        in_specs=[pl.BlockSpec((R, TOK_TILE), lambda i: (0, 0))],
        out_specs=pl.BlockSpec((R, TOK_TILE), lambda i: (0, 0)),
        out_shape=jax.ShapeDtypeStruct((R, TOK_TILE), jnp.int32),
        compiler_params=_cp(("arbitrary",)),
    )(at)


def _onehot(q, cap, slot_ref):
    j = lax.broadcasted_iota(jnp.int32, (cap, TOK_TILE), 0)
    blocks = []
    for e in range(N_EXPERTS):
        r = [(slot_ref[t * N_EXPERTS + e:t * N_EXPERTS + e + 1, :] == j).astype(BF16) for t in range(q)]
        blocks.append(r[0] if q == 1 else jnp.concatenate(r, axis=1))
    return jnp.concatenate(blocks, axis=0)


def _dispatch_kernel(q, cap, slot_ref, ha_ref, xd_ref):
    rows = _dot(_onehot(q, cap, slot_ref), ha_ref[...])
    for e in range(N_EXPERTS):
        xd_ref[e] = rows[e * cap:(e + 1) * cap].astype(BF16)


def _dispatch(slot, ha, T):
    q = T // TOK_TILE
    cap = EC_CAPACITY_FACTOR * T // N_EXPERTS
    B = ha.shape[0] // T
    return pl.pallas_call(
        functools.partial(_dispatch_kernel, q, cap),
        grid=(B,),
        in_specs=[pl.BlockSpec((q * N_EXPERTS, TOK_TILE), lambda b: (b, 0)),
                  pl.BlockSpec((T, D_AUG), lambda b: (b, 0))],
        out_specs=pl.BlockSpec((N_EXPERTS, cap, D_AUG), lambda b: (0, b, 0)),
        out_shape=jax.ShapeDtypeStruct((N_EXPERTS, B * cap, D_AUG), BF16),
        compiler_params=_cp(("arbitrary",)),
    )(slot, ha)


def _ffn_kernel(xp_ref, xs_ref, wg_ref, wu_ref, wd_ref, yp_ref, ys_ref, wg_s, wu_s, wd_s):
    e = pl.program_id(0)

    @pl.when(pl.program_id(1) == 0)
    def _():
        wg_s[...] = wg_ref[...].astype(BF16)
        wu_s[...] = wu_ref[...].astype(BF16)
        wd_s[...] = wd_ref[...].astype(BF16)

    lane = lax.broadcasted_iota(jnp.int32, (FFN_TILE, LANES), 1)
    pick = (lane == e) | (lane == e + N_EXPERTS) | (lane == e + 2 * N_EXPERTS)
    for x_ref, y_ref in ((xp_ref, yp_ref), (xs_ref, ys_ref)):
        x = x_ref[:, 0:D_MODEL]
        gate = jnp.sum(jnp.where(pick, x_ref[:, D_MODEL:D_AUG].astype(F32), 0.0), axis=-1, keepdims=True)
        hg = _dot(x, wg_s[...])
        hid = (hg * _sigmoid(hg)) * _dot(x, wu_s[...])
        y_ref[...] = (_dot(hid.astype(BF16), wd_s[...]) * gate).astype(BF16)


def _ffn(xd_p, xd_s, w_gate, w_up, w_down):
    rows = xd_p.shape[1]
    assert xd_s.shape[1] == rows and rows % FFN_TILE == 0
    xspec = pl.BlockSpec((None, FFN_TILE, D_AUG), lambda e, m: (e, m, 0))
    wspec = pl.BlockSpec((None, D_MODEL, D_MODEL), lambda e, m: (e, 0, 0))
    yspec = pl.BlockSpec((None, FFN_TILE, D_MODEL), lambda e, m: (e, m, 0))
    yshape = jax.ShapeDtypeStruct((N_EXPERTS, rows, D_MODEL), BF16)
    return pl.pallas_call(
        _ffn_kernel,
        grid=(N_EXPERTS, rows // FFN_TILE),
        in_specs=[xspec, xspec, wspec, wspec, wspec],
        out_specs=[yspec, yspec],
        out_shape=[yshape, yshape],
        scratch_shapes=[pltpu.VMEM((D_MODEL, D_MODEL), BF16)] * 3,
        compiler_params=_cp(("arbitrary", "arbitrary")),
    )(xd_p, xd_s, w_gate, w_up, w_down)


def _combine_kernel(q, cap, slot_ref, yd_ref, x1_ref, mod_ref, nf_ref, o_ref):
    yd = jnp.concatenate([yd_ref[e] for e in range(N_EXPERTS)], axis=0)
    acc = _dot_tn(_onehot(q, cap, slot_ref), yd)
    x2 = x1_ref[...] + mod_ref[0, 5:6, :] * acc
    ms = jnp.mean(x2 * x2, axis=-1, keepdims=True)
    o_ref[...] = x2 * lax.rsqrt(ms + EPS) * nf_ref[...]


def _combine(slot, yd, x1, mods, norm_final, T):
    q = T // TOK_TILE
    cap = EC_CAPACITY_FACTOR * T // N_EXPERTS
    n = x1.shape[0]
    mod_map = (lambda b: (b, 0, 0)) if mods.shape[0] > 1 else (lambda b: (0, 0, 0))
    return pl.pallas_call(
        functools.partial(_combine_kernel, q, cap),
        grid=(n // T,),
        in_specs=[pl.BlockSpec((q * N_EXPERTS, TOK_TILE), lambda b: (b, 0)),
                  pl.BlockSpec((N_EXPERTS, cap, D_MODEL), lambda b: (0, b, 0)),
                  pl.BlockSpec((T, D_MODEL), lambda b: (b, 0)),
                  pl.BlockSpec((1, N_MOD, D_MODEL), mod_map),
                  pl.BlockSpec((1, D_MODEL), lambda b: (0, 0))],
        out_specs=pl.BlockSpec((T, D_MODEL), lambda b: (b, 0)),
        out_shape=jax.ShapeDtypeStruct((n, D_MODEL), F32),
        compiler_params=_cp(("arbitrary",)),
    )(slot, yd, x1, mods, norm_final)


def _gate_tiles(wa, wx):
    eye = jnp.eye(LRU_BLOCKS, dtype=wa.dtype)

    def dense(w):
        full = jnp.einsum('nde,nm->ndme', w, eye).reshape(D_LRU, D_LRU)
        t = full.reshape(D_LRU // LANES, LANES, D_LRU // LANES, LANES)
        return jnp.stack([t[c, :, c, :] for c in range(D_LRU // LANES)])

    return jnp.concatenate([dense(wa[0]), dense(wx[0]), dense(wa[1]), dense(wx[1])], axis=-1).astype(BF16)


def _router_split(w):
    wp = jnp.pad(w, ((0, 0), (0, LANES - N_EXPERTS)))
    hi = wp.astype(BF16)
    mid = (wp - hi.astype(F32)).astype(BF16)
    return jnp.concatenate([hi, mid], axis=1)


def _bias_tiles(ba, bx):
    t = lambda v: v.reshape(D_LRU // LANES, 1, LANES)
    return jnp.concatenate([t(ba[0]), t(bx[0]), t(ba[1]), t(bx[1])], axis=-1)


def _mixer_and_router(x, mods, p, T, row_len, h0f, h0b, s0, want_state):
    B = x.shape[0]
    xf = x.reshape(B * T, D_MODEL)
    proj = _in_proj(xf, mods, p["norm_mix"], p["w_in"], T)
    proj3 = proj.reshape(B, T, D_IN)
    hsum, last_f, last_b = _lru(proj3, p["conv_w"], p["conv_b"], p["gate_w"], p["gate_b"],
                                p["lam"], h0f, h0b, T, row_len)
    yh, st = _hgrn(proj3, p["gamma"], p["norm_hgrn"], s0, T, want_state)
    x1, ha, at = _mix_out(hsum.reshape(B * T, D_LRU), proj, yh.reshape(B * T, D_HGRN), xf, mods,
                          p["norm_lru"], p["w_out"], p["norm_ffn"], p["w_router"], T)
    slot = _select(at, T)
    xd = _dispatch(slot, ha, T)
    return x1, slot, xd, last_f, last_b, st


def kernel(x_prompt, x_sample, state_lru, state_hgrn, c, c_ctx, w_ada, b_ada, norm_mix, w_in, conv_w,
           conv_b, lru_wa, lru_ba, lru_wx, lru_bx, lru_lambda, norm_lru, hgrn_gamma, norm_hgrn, w_out,
           norm_ffn, w_router, w_gate, w_up, w_down, norm_final):
    assert w_ada.shape[0] == 1 and hgrn_gamma.shape[1] == 2, "one trunk layer"
    l = 0
    Bp, Tp, _ = x_prompt.shape
    Bs, Ts, _ = x_sample.shape

    cond = jnp.concatenate([c_ctx[None, :], c, jnp.zeros((16 - 1 - Bs, D_MODEL), F32)], axis=0)
    mods = _adaln(cond, w_ada[l], b_ada[l][None, :]).reshape(16, N_MOD, D_MODEL)
    mods_p, mods_s = mods[0:1], mods[1:1 + Bs]

    p = {
        "norm_mix": norm_mix[l][None, :], "w_in": w_in[l].astype(BF16),
        "conv_w": conv_w[l], "conv_b": conv_b[l][None, :],
        "gate_w": _gate_tiles(lru_wa[l], lru_wx[l]), "gate_b": _bias_tiles(lru_ba[l], lru_bx[l]),
        "lam": lru_lambda[l], "norm_lru": norm_lru[l][None, :],
        "gamma": hgrn_gamma, "norm_hgrn": norm_hgrn[l][None, :],
        "w_out": w_out[l].astype(BF16), "norm_ffn": norm_ffn[l][None, :],
        "w_router": _router_split(w_router[l]),
    }
    zeros_p = jnp.zeros((Bp, D_LRU), F32)
    x1p, slot_p, xd_p, last_f, last_b, st_p = _mixer_and_router(
        x_prompt, mods_p, p, Tp, Tp, zeros_p, zeros_p, None, True)
    x1s, slot_s, xd_s, _, _, _ = _mixer_and_router(
        x_sample, mods_s, p, Ts, GRID_W, state_lru[:, l, 0], state_lru[:, l, 1], state_hgrn[:, l], False)

    yd_p, yd_s = _ffn(xd_p, xd_s, w_gate[l], w_up[l], w_down[l])
    nf = norm_final[None, :]
    y_prompt = _combine(slot_p, yd_p, x1p, mods_p, nf, Tp).reshape(Bp, Tp, D_MODEL)
    y_sample = _combine(slot_s, yd_s, x1s, mods_s, nf, Ts).reshape(Bs, Ts, D_MODEL)
    new_state_lru = jnp.stack([last_f, last_b], axis=1)[:, None]
    new_state_hgrn = st_p[:, None]
    return (y_prompt, y_sample, new_state_lru, new_state_hgrn)
```

```python
import functools

import jax
import jax.numpy as jnp
from jax import lax
from jax.experimental import pallas as pl
from jax.experimental.pallas import tpu as pltpu

F32 = jnp.float32
BF16 = jnp.bfloat16

D_MODEL = 1024
D_LRU = 512
D_HGRN = 512
HG_HEADS = 4
HG_DK = 128
LRU_BLOCKS = 8
LRU_BW = 64
LRU_C = 8.0
N_EXPERTS = 16
EC_CAPACITY_FACTOR = 2
N_MOD = 6
D_IN = 7 * 512
GRID_W = 64
EPS = 1e-6

LANES = 128
TOK_TILE = 256
PROJ_TILE = 512
LRU_GROUP = 8
PITCH_PAD = 8
HG_CHUNK = 64
HG_ROWS = 256
FFN_TILE = 512
D_AUG = D_MODEL + LANES
VMEM_LIMIT = 56 * 1024 * 1024


def _cp(sem, vmem=VMEM_LIMIT):
    return pltpu.CompilerParams(dimension_semantics=sem, vmem_limit_bytes=vmem)


def _sigmoid(x):
    return 0.5 * (1.0 + jnp.tanh(0.5 * x))


def _dot(a, b):
    return jnp.dot(a, b, preferred_element_type=F32)


def _dot_nt(a, b):
    return lax.dot_general(a, b, (((1,), (1,)), ((), ())), preferred_element_type=F32)


def _dot_tn(a, b):
    return lax.dot_general(a, b, (((0,), (0,)), ((), ())), preferred_element_type=F32)


def _adaln_kernel(c_ref, w_ref, b_ref, o_ref):
    c = c_ref[...]
    s = (c * _sigmoid(c)).astype(BF16)
    o_ref[...] = _dot(s, w_ref[...].astype(BF16)) + b_ref[...]


def _adaln(cond, w, b):
    n = cond.shape[0]
    tn = 1024
    return pl.pallas_call(
        _adaln_kernel,
        grid=(w.shape[1] // tn,),
        in_specs=[pl.BlockSpec((n, D_MODEL), lambda j: (0, 0)),
                  pl.BlockSpec((D_MODEL, tn), lambda j: (0, j)),
                  pl.BlockSpec((1, tn), lambda j: (0, j))],
        out_specs=pl.BlockSpec((n, tn), lambda j: (0, j)),
        out_shape=jax.ShapeDtypeStruct((n, w.shape[1]), F32),
        compiler_params=_cp(("arbitrary",)),
    )(cond, w, b)


def _proj_kernel(x_ref, mod_ref, g_ref, w_ref, o_ref):
    x = x_ref[...]
    ms = jnp.mean(x * x, axis=-1, keepdims=True)
    y = x * lax.rsqrt(ms + EPS) * g_ref[...]
    h = y * (1.0 + mod_ref[0, 1:2, :]) + mod_ref[0, 0:1, :]
    o_ref[...] = _dot(h.astype(BF16), w_ref[...])


def _in_proj(x, mods, gain, w_bf, T):
    n = x.shape[0]
    per_req = T // PROJ_TILE if T >= PROJ_TILE else None
    if per_req is None:
        reqs = PROJ_TILE // T
        mod_map = lambda i: (0, 0, 0)
        assert mods.shape[0] == 1 and reqs >= 1
    else:
        mod_map = lambda i: (i // per_req, 0, 0)
    return pl.pallas_call(
        _proj_kernel,
        grid=(n // PROJ_TILE,),
        in_specs=[pl.BlockSpec((PROJ_TILE, D_MODEL), lambda i: (i, 0)),
                  pl.BlockSpec((1, N_MOD, D_MODEL), mod_map),
                  pl.BlockSpec((1, D_MODEL), lambda i: (0, 0)),
                  pl.BlockSpec((D_MODEL, D_IN), lambda i: (0, 0))],
        out_specs=pl.BlockSpec((PROJ_TILE, D_IN), lambda i: (i, 0)),
        out_shape=jax.ShapeDtypeStruct((n, D_IN), F32),
        compiler_params=_cp(("arbitrary",)),
    )(x, mods, gain, w_bf)


def _lru_kernel(T, row_len, xr_ref, cw_ref, cb_ref, w_ref, bias_ref, lam_ref, h0f_ref, h0b_ref,
                hs_ref, lf_ref, lb_ref, af, uf, ab, ub):
    pitch = T + PITCH_PAD
    pos = lax.broadcasted_iota(jnp.int32, (T, LANES), 0) & (row_len - 1)
    cw = cw_ref[...]
    cb = cb_ref[...]
    nl = -lam_ref[...]
    sp = jnp.maximum(nl, 0.0) + jnp.log1p(jnp.exp(-jnp.abs(nl)))
    w = w_ref[0]
    bias = bias_ref[0]
    for b in range(LRU_GROUP):
        x = xr_ref[b]
        xm2 = jnp.where(pos >= 2, pltpu.roll(x, 2, 0), 0.0)
        xm1 = jnp.where(pos >= 1, pltpu.roll(x, 1, 0), 0.0)
        xp1 = jnp.where(pos <= row_len - 2, pltpu.roll(x, T - 1, 0), 0.0)
        xc = cw[0:1] * xm2 + cw[1:2] * xm1 + cw[2:3] * x + cw[3:4] * xp1 + cb
        gates = _dot(xc.astype(BF16), w) + bias
        rows = pl.ds(b * pitch, T)
        for d, (a_s, u_s) in enumerate(((af, uf), (ab, ub))):
            r = _sigmoid(gates[:, (2 * d) * LANES:(2 * d + 1) * LANES])
            i = _sigmoid(gates[:, (2 * d + 1) * LANES:(2 * d + 2) * LANES])
            a = jnp.exp(-LRU_C * r * sp[d:d + 1])
            a_s[rows, :] = a
            u_s[rows, :] = jnp.sqrt(1.0 - a * a) * (i * xc)

    def step(t, carry):
        hf, hb = carry
        rf = pl.ds(t, LRU_GROUP, stride=pitch)
        hf = af[rf, :] * hf + uf[rf, :]
        uf[rf, :] = hf
        rb = pl.ds(T - 1 - t, LRU_GROUP, stride=pitch)
        hb = ab[rb, :] * hb + ub[rb, :]
        ub[rb, :] = hb
        return hf, hb

    hf, hb = lax.fori_loop(0, T, step, (h0f_ref[...], h0b_ref[...]), unroll=8)
    lf_ref[...] = hf
    lb_ref[...] = hb
    for b in range(LRU_GROUP):
        rows = pl.ds(b * pitch, T)
        hs_ref[b] = uf[rows, :] + ub[rows, :]


def _lru(proj3, conv_w, conv_b, w_tiles, bias_tiles, lam, h0f, h0b, T, row_len):
    B = proj3.shape[0]
    n_ct = D_LRU // LANES
    rows = LRU_GROUP * (T + PITCH_PAD)
    vec = pl.BlockSpec((LRU_GROUP, LANES), lambda g, c: (g, c))
    return pl.pallas_call(
        functools.partial(_lru_kernel, T, row_len),
        grid=(B // LRU_GROUP, n_ct),
        in_specs=[pl.BlockSpec((LRU_GROUP, T, LANES), lambda g, c: (g, 0, c)),
                  pl.BlockSpec((4, LANES), lambda g, c: (0, c)),
                  pl.BlockSpec((1, LANES), lambda g, c: (0, c)),
                  pl.BlockSpec((1, LANES, 4 * LANES), lambda g, c: (c, 0, 0)),
                  pl.BlockSpec((1, 1, 4 * LANES), lambda g, c: (c, 0, 0)),
                  pl.BlockSpec((2, LANES), lambda g, c: (0, c)),
                  vec, vec],
        out_specs=[pl.BlockSpec((LRU_GROUP, T, LANES), lambda g, c: (g, 0, c)), vec, vec],
        out_shape=[jax.ShapeDtypeStruct((B, T, D_LRU), F32),
                   jax.ShapeDtypeStruct((B, D_LRU), F32),
                   jax.ShapeDtypeStruct((B, D_LRU), F32)],
        scratch_shapes=[pltpu.VMEM((rows, LANES), F32)] * 4,
        compiler_params=_cp(("arbitrary", "arbitrary")),
    )(proj3, conv_w, conv_b, w_tiles, bias_tiles, lam, h0f, h0b)


def _hgrn_kernel(T, has_s0, want_state, *refs):
    q_ref, ff_ref, fb_ref, v_ref, og_ref, gam_ref, gain_ref = refs[:7]
    refs = refs[7:]
    if has_s0:
        s0_ref, refs = refs[0], refs[1:]
    y_ref, refs = refs[0], refs[1:]
    if want_state:
        st_ref, refs = refs[0], refs[1:]
    q_s, v_s, kf_s, kb_s, bf_s, bb_s, of_s, ob_s, st_s = refs
    C = HG_CHUNK
    n = T // C
    RB = HG_ROWS
    heads = [slice(h * LANES, (h + 1) * LANES) for h in range(HG_HEADS)]

    ti = lax.broadcasted_iota(jnp.int32, (RB, RB), 0)
    si = lax.broadcasted_iota(jnp.int32, (RB, RB), 1)
    sh = C.bit_length() - 1
    same = (ti >> sh) == (si >> sh)
    tri = ((same & (si <= ti)).astype(BF16), (same & (si >= ti)).astype(BF16))

    for r in range(T // RB):
        rows = slice(r * RB, (r + 1) * RB)
        qr = q_ref[rows, :]
        q_s[rows, :] = qr * _sigmoid(qr) * (HG_DK ** -0.5)
        v_s[rows, :] = v_ref[rows, :].astype(BF16)
        for d, (f_ref, k_s, b_s) in enumerate(((ff_ref, kf_s, bf_s), (fb_ref, kb_s, bb_s))):
            g0 = gam_ref[d, 0:1, :]
            g1 = gam_ref[d, 1:2, :]
            m = jnp.maximum(g0, g1)
            e0 = jnp.exp(g0 - m)
            lbd = e0 / (e0 + jnp.exp(g1 - m))
            sig = _sigmoid(f_ref[rows, :])
            k_s[rows, :] = (1.0 - lbd) * (1.0 - sig)
            lf = jnp.log(lbd + (1.0 - lbd) * sig)
            hi = lf.astype(BF16)
            r1 = lf - hi.astype(F32)
            mid = r1.astype(BF16)
            lo = (r1 - mid.astype(F32)).astype(BF16)
            cs = _dot(tri[d], jnp.concatenate([hi, mid, lo], axis=1))
            b_s[rows, :] = cs[:, 0:D_HGRN] + (cs[:, D_HGRN:2 * D_HGRN] + cs[:, 2 * D_HGRN:3 * D_HGRN])

    for d in range(2):
        for h in range(HG_HEADS):
            st_s[d * HG_HEADS + h] = s0_ref[d, h].T if has_s0 else jnp.zeros((LANES, LANES), F32)

    ri = lax.broadcasted_iota(jnp.int32, (C, C), 0)
    ci = lax.broadcasted_iota(jnp.int32, (C, C), 1)

    def body(i, carry):
        for d, (k_s, b_s, o_s) in enumerate(((kf_s, bf_s, of_s), (kb_s, bb_s, ob_s))):
            c = i if d == 0 else n - 1 - i
            sl = pl.ds(pl.multiple_of(c * C, C), C)
            for h in range(HG_HEADS):
                q = q_s[sl, heads[h]]
                k = k_s[sl, heads[h]]
                b = b_s[sl, heads[h]]
                v = v_s[sl, heads[h]]
                bm = b[C // 2:C // 2 + 1, :]
                g = b[C - 1:C, :] if d == 0 else b[0:1, :]
                qi = q * jnp.exp(b - bm)
                qe = (qi * jnp.exp(bm)).astype(BF16)
                ki = k * jnp.exp(bm - b)
                kd = (ki * jnp.exp(g - bm)).astype(BF16)
                s = _dot_nt(qi.astype(BF16), ki.astype(BF16))
                p = jnp.where((ci <= ri) if d == 0 else (ci >= ri), s, 0.0).astype(BF16)
                st = st_s[d * HG_HEADS + h]
                o_s[sl, heads[h]] = _dot(p, v) + _dot_nt(qe, st.astype(BF16))
                st_s[d * HG_HEADS + h] = st * jnp.exp(g) + _dot_tn(v, kd)
        return carry

    lax.fori_loop(0, n, body, 0)
    if want_state:
        for d in range(2):
            for h in range(HG_HEADS):
                st_ref[d, h] = st_s[d * HG_HEADS + h].T

    for r in range(T // RB):
        rows = slice(r * RB, (r + 1) * RB)
        for h in range(HG_HEADS):
            o = of_s[rows, heads[h]] + ob_s[rows, heads[h]]
            ms = jnp.mean(o * o, axis=-1, keepdims=True)
            og = og_ref[rows, heads[h]]
            y_ref[rows, heads[h]] = o * lax.rsqrt(ms + EPS) * gain_ref[...] * (og * _sigmoid(og))


def _hgrn(proj3, gamma, gain, s0, T, want_state):
    B = proj3.shape[0]
    has_s0 = s0 is not None
    col0 = 2 * D_LRU // D_HGRN

    def col(k):
        return pl.BlockSpec((None, T, D_HGRN), lambda b, k=k: (b, 0, col0 + k))

    st_spec = pl.BlockSpec((None, 2, HG_HEADS, HG_DK, HG_DK), lambda b: (b, 0, 0, 0, 0))
    in_specs = [col(0), col(1), col(2), col(3), col(4),
                pl.BlockSpec((2, 2, D_HGRN), lambda b: (0, 0, 0)),
                pl.BlockSpec((1, LANES), lambda b: (0, 0))]
    args = [proj3, proj3, proj3, proj3, proj3, gamma, gain]
    if has_s0:
        in_specs.append(st_spec)
        args.append(s0)
    out_specs = [pl.BlockSpec((None, T, D_HGRN), lambda b: (b, 0, 0))]
    out_shape = [jax.ShapeDtypeStruct((B, T, D_HGRN), F32)]
    if want_state:
        out_specs.append(st_spec)
        out_shape.append(jax.ShapeDtypeStruct((B, 2, HG_HEADS, HG_DK, HG_DK), F32))
    res = pl.pallas_call(
        functools.partial(_hgrn_kernel, T, has_s0, want_state),
        grid=(B,),
        in_specs=in_specs,
        out_specs=out_specs,
        out_shape=out_shape,
        scratch_shapes=[pltpu.VMEM((T, D_HGRN), F32), pltpu.VMEM((T, D_HGRN), BF16)]
        + [pltpu.VMEM((T, D_HGRN), F32)] * 6
        + [pltpu.VMEM((2 * HG_HEADS, HG_DK, HG_DK), F32)],
        compiler_params=_cp(("arbitrary",)),
    )(*args)
    return res if want_state else (res[0], None)


def _mix_out_kernel(hs_ref, gr_ref, yh_ref, x_ref, mod_ref, nl_ref, wo_ref, nf_ref, wr_ref,
                    x1_ref, ha_ref, at_ref):
    hs = hs_ref[...]
    ms = jnp.mean(hs * hs, axis=-1, keepdims=True)
    gr = gr_ref[...]
    gelu = 0.5 * gr * (1.0 + jnp.tanh(0.7978845608028654 * (gr + 0.044715 * (gr * gr * gr))))
    y_lru = hs * lax.rsqrt(ms + EPS) * nl_ref[...] * gelu
    ycat = jnp.concatenate([y_lru.astype(BF16), yh_ref[...].astype(BF16)], axis=-1)
    y = _dot(ycat, wo_ref[...])
    x1 = x_ref[...] + mod_ref[0, 2:3, :] * y
    x1_ref[...] = x1
    ms = jnp.mean(x1 * x1, axis=-1, keepdims=True)
    h2 = x1 * lax.rsqrt(ms + EPS) * nf_ref[...] * (1.0 + mod_ref[0, 4:5, :]) + mod_ref[0, 3:4, :]
    h2_hi = h2.astype(BF16)
    h2_mid = (h2 - h2_hi.astype(F32)).astype(BF16)
    wr = wr_ref[...]
    p_hi = _dot(h2_hi, wr)
    logits = p_hi[:, 0:LANES] + (p_hi[:, LANES:2 * LANES] + _dot(h2_mid, wr)[:, 0:LANES])
    lane = lax.broadcasted_iota(jnp.int32, logits.shape, 1)
    logits = jnp.where(lane < N_EXPERTS, logits, -jnp.inf)
    e = jnp.exp(logits - jnp.max(logits, axis=-1, keepdims=True))
    aff = e / jnp.sum(e, axis=-1, keepdims=True)
    at_ref[...] = aff.T[0:N_EXPERTS, :]
    hi = aff.astype(BF16).astype(F32)
    r1 = aff - hi
    mid = r1.astype(BF16).astype(F32)
    lo = (r1 - mid).astype(BF16).astype(F32)
    split = hi + pltpu.roll(mid, N_EXPERTS, 1) + pltpu.roll(lo, 2 * N_EXPERTS, 1)
    ha_ref[:, 0:D_MODEL] = h2_hi
    ha_ref[:, D_MODEL:D_AUG] = split.astype(BF16)


def _mix_out(hsum, proj, yh, x, mods, norm_lru, wo_bf, norm_ffn, wr_pad, T):
    n = x.shape[0]
    tiles = n // TOK_TILE
    q = T // TOK_TILE
    mod_map = (lambda i: (i // q, 0, 0)) if mods.shape[0] > 1 else (lambda i: (0, 0, 0))
    tile = lambda w: pl.BlockSpec((TOK_TILE, w), lambda i: (i, 0))
    const = lambda r, w: pl.BlockSpec((r, w), lambda i: (0, 0))
    return pl.pallas_call(
        _mix_out_kernel,
        grid=(tiles,),
        in_specs=[tile(D_LRU),
                  pl.BlockSpec((TOK_TILE, D_LRU), lambda i: (i, 1)),
                  tile(D_HGRN), tile(D_MODEL),
                  pl.BlockSpec((1, N_MOD, D_MODEL), mod_map),
                  const(1, D_LRU), const(D_MODEL, D_MODEL), const(1, D_MODEL), const(D_MODEL, 2 * LANES)],
        out_specs=[tile(D_MODEL), tile(D_AUG),
                   pl.BlockSpec((N_EXPERTS, TOK_TILE), lambda i: (i, 0))],
        out_shape=[jax.ShapeDtypeStruct((n, D_MODEL), F32),
                   jax.ShapeDtypeStruct((n, D_AUG), BF16),
                   jax.ShapeDtypeStruct((tiles * N_EXPERTS, TOK_TILE), F32)],
        compiler_params=_cp(("arbitrary",)),
    )(hsum, proj, yh, x, mods, norm_lru, wo_bf, norm_ffn, wr_pad)


def _select_kernel(q, cap, at_ref, slot_ref):
    R = at_ref.shape[0]
    W = TOK_TILE
    bits = pltpu.bitcast(at_ref[...], jnp.int32)
    ones = jnp.ones((W, LANES), BF16)
    su = (lax.broadcasted_iota(jnp.int32, (W, W), 0)
          < lax.broadcasted_iota(jnp.int32, (W, W), 1)).astype(BF16)
    if q > 1:
        rr = lax.broadcasted_iota(jnp.int32, (R, R), 0)
        cc = lax.broadcasted_iota(jnp.int32, (R, R), 1)
        sh = (N_EXPERTS * q).bit_length() - 1
        same = ((rr >> sh) == (cc >> sh)) & ((rr & (N_EXPERTS - 1)) == (cc & (N_EXPERTS - 1)))
        g_all = same.astype(BF16)
        g_prev = (same & (cc < rr)).astype(BF16)

    def wide(x):
        return jnp.concatenate([x] * (W // LANES), axis=1)

    def tile_count(mask):
        return _dot(mask.astype(BF16), ones)

    def req_count(mask):
        c = tile_count(mask)
        return _dot(g_all, c.astype(BF16)) if q > 1 else c

    def prefix(mask):
        p = _dot(mask.astype(BF16), su)
        if q > 1:
            p = p + wide(_dot(g_prev, tile_count(mask).astype(BF16)))
        return p

    thr = jnp.zeros((R, LANES), jnp.int32)
    for bit in range(30, -1, -1):
        cand = thr | (1 << bit)
        cnt = req_count(bits >= wide(cand))
        thr = jnp.where(cnt >= cap, cand, thr)
    thr_w = wide(thr)
    gt = bits > thr_w
    eq = bits == thr_w
    need = wide(cap - req_count(gt))
    sel = gt | (eq & (prefix(eq) < need))
    slot_ref[...] = jnp.where(sel, prefix(sel), -1.0).astype(jnp.int32)


def _select(at, T):
    q = T // TOK_TILE
    cap = EC_CAPACITY_FACTOR * T // N_EXPERTS
    R = at.shape[0]
    return pl.pallas_call(
        functools.partial(_select_kernel, q, cap),
        grid=(1,),
        in_specs=[pl.BlockSpec((R, TOK_TILE), lambda i: (0, 0))],
        out_specs=pl.BlockSpec((R, TOK_TILE), lambda i: (0, 0)),
        out_shape=jax.ShapeDtypeStruct((R, TOK_TILE), jnp.int32),
        compiler_params=_cp(("arbitrary",)),
    )(at)


def _onehot(q, cap, slot_ref):
    j = lax.broadcasted_iota(jnp.int32, (cap, TOK_TILE), 0)
    blocks = []
    for e in range(N_EXPERTS):
        r = [(slot_ref[t * N_EXPERTS + e:t * N_EXPERTS + e + 1, :] == j).astype(BF16) for t in range(q)]
        blocks.append(r[0] if q == 1 else jnp.concatenate(r, axis=1))
    return jnp.concatenate(blocks, axis=0)


def _dispatch_kernel(q, cap, slot_ref, ha_ref, xd_ref):
    rows = _dot(_onehot(q, cap, slot_ref), ha_ref[...])
    for e in range(N_EXPERTS):
        xd_ref[e] = rows[e * cap:(e + 1) * cap].astype(BF16)


def _dispatch(slot, ha, T):
    q = T // TOK_TILE
    cap = EC_CAPACITY_FACTOR * T // N_EXPERTS
    B = ha.shape[0] // T
    return pl.pallas_call(
        functools.partial(_dispatch_kernel, q, cap),
        grid=(B,),
        in_specs=[pl.BlockSpec((q * N_EXPERTS, TOK_TILE), lambda b: (b, 0)),
                  pl.BlockSpec((T, D_AUG), lambda b: (b, 0))],
        out_specs=pl.BlockSpec((N_EXPERTS, cap, D_AUG), lambda b: (0, b, 0)),
        out_shape=jax.ShapeDtypeStruct((N_EXPERTS, B * cap, D_AUG), BF16),
        compiler_params=_cp(("arbitrary",)),
    )(slot, ha)


def _ffn_kernel(xp_ref, xs_ref, wg_ref, wu_ref, wd_ref, yp_ref, ys_ref, wg_s, wu_s, wd_s):
    e = pl.program_id(0)

    @pl.when(pl.program_id(1) == 0)
    def _():
        wg_s[...] = wg_ref[...].astype(BF16)
        wu_s[...] = wu_ref[...].astype(BF16)
        wd_s[...] = wd_ref[...].astype(BF16)

    lane = lax.broadcasted_iota(jnp.int32, (FFN_TILE, LANES), 1)
    pick = (lane == e) | (lane == e + N_EXPERTS) | (lane == e + 2 * N_EXPERTS)
    for x_ref, y_ref in ((xp_ref, yp_ref), (xs_ref, ys_ref)):
        x = x_ref[:, 0:D_MODEL]
        gate = jnp.sum(jnp.where(pick, x_ref[:, D_MODEL:D_AUG].astype(F32), 0.0), axis=-1, keepdims=True)
        hg = _dot(x, wg_s[...])
        hid = (hg * _sigmoid(hg)) * _dot(x, wu_s[...])
        y_ref[...] = (_dot(hid.astype(BF16), wd_s[...]) * gate).astype(BF16)


def _ffn(xd_p, xd_s, w_gate, w_up, w_down):
    rows = xd_p.shape[1]
    assert xd_s.shape[1] == rows and rows % FFN_TILE == 0
    xspec = pl.BlockSpec((None, FFN_TILE, D_AUG), lambda e, m: (e, m, 0))
    wspec = pl.BlockSpec((None, D_MODEL, D_MODEL), lambda e, m: (e, 0, 0))
    yspec = pl.BlockSpec((None, FFN_TILE, D_MODEL), lambda e, m: (e, m, 0))
    yshape = jax.ShapeDtypeStruct((N_EXPERTS, rows, D_MODEL), BF16)
    return pl.pallas_call(
        _ffn_kernel,
        grid=(N_EXPERTS, rows // FFN_TILE),
        in_specs=[xspec, xspec, wspec, wspec, wspec],
        out_specs=[yspec, yspec],
        out_shape=[yshape, yshape],
        scratch_shapes=[pltpu.VMEM((D_MODEL, D_MODEL), BF16)] * 3,
        compiler_params=_cp(("arbitrary", "arbitrary")),
    )(xd_p, xd_s, w_gate, w_up, w_down)


def _combine_kernel(q, cap, slot_ref, yd_ref, x1_ref, mod_ref, nf_ref, o_ref):
    yd = jnp.concatenate([yd_ref[e] for e in range(N_EXPERTS)], axis=0)
    acc = _dot_tn(_onehot(q, cap, slot_ref), yd)
    x2 = x1_ref[...] + mod_ref[0, 5:6, :] * acc
    ms = jnp.mean(x2 * x2, axis=-1, keepdims=True)
    o_ref[...] = x2 * lax.rsqrt(ms + EPS) * nf_ref[...]


def _combine(slot, yd, x1, mods, norm_final, T):
    q = T // TOK_TILE
    cap = EC_CAPACITY_FACTOR * T // N_EXPERTS
    n = x1.shape[0]
    mod_map = (lambda b: (b, 0, 0)) if mods.shape[0] > 1 else (lambda b: (0, 0, 0))
    return pl.pallas_call(
        functools.partial(_combine_kernel, q, cap),
        grid=(n // T,),
        in_specs=[pl.BlockSpec((q * N_EXPERTS, TOK_TILE), lambda b: (b, 0)),
                  pl.BlockSpec((N_EXPERTS, cap, D_MODEL), lambda b: (0, b, 0)),
                  pl.BlockSpec((T, D_MODEL), lambda b: (b, 0)),
                  pl.BlockSpec((1, N_MOD, D_MODEL), mod_map),
                  pl.BlockSpec((1, D_MODEL), lambda b: (0, 0))],
        out_specs=pl.BlockSpec((T, D_MODEL), lambda b: (b, 0)),
        out_shape=jax.ShapeDtypeStruct((n, D_MODEL), F32),
        compiler_params=_cp(("arbitrary",)),
    )(slot, yd, x1, mods, norm_final)


def _gate_tiles(wa, wx):
    eye = jnp.eye(LRU_BLOCKS, dtype=wa.dtype)

    def dense(w):
        full = jnp.einsum('nde,nm->ndme', w, eye).reshape(D_LRU, D_LRU)
        t = full.reshape(D_LRU // LANES, LANES, D_LRU // LANES, LANES)
        return jnp.stack([t[c, :, c, :] for c in range(D_LRU // LANES)])

    return jnp.concatenate([dense(wa[0]), dense(wx[0]), dense(wa[1]), dense(wx[1])], axis=-1).astype(BF16)


def _router_split(w):
    wp = jnp.pad(w, ((0, 0), (0, LANES - N_EXPERTS)))
    hi = wp.astype(BF16)
    mid = (wp - hi.astype(F32)).astype(BF16)
    return jnp.concatenate([hi, mid], axis=1)


def _bias_tiles(ba, bx):
    t = lambda v: v.reshape(D_LRU // LANES, 1, LANES)
    return jnp.concatenate([t(ba[0]), t(bx[0]), t(ba[1]), t(bx[1])], axis=-1)


def _mixer_and_router(x, mods, p, T, row_len, h0f, h0b, s0, want_state):
    B = x.shape[0]
    xf = x.reshape(B * T, D_MODEL)
    proj = _in_proj(xf, mods, p["norm_mix"], p["w_in"], T)
    proj3 = proj.reshape(B, T, D_IN)
    hsum, last_f, last_b = _lru(proj3, p["conv_w"], p["conv_b"], p["gate_w"], p["gate_b"],
                                p["lam"], h0f, h0b, T, row_len)
    yh, st = _hgrn(proj3, p["gamma"], p["norm_hgrn"], s0, T, want_state)
    x1, ha, at = _mix_out(hsum.reshape(B * T, D_LRU), proj, yh.reshape(B * T, D_HGRN), xf, mods,
                          p["norm_lru"], p["w_out"], p["norm_ffn"], p["w_router"], T)
    slot = _select(at, T)
    xd = _dispatch(slot, ha, T)
    return x1, slot, xd, last_f, last_b, st


def kernel(x_prompt, x_sample, state_lru, state_hgrn, c, c_ctx, w_ada, b_ada, norm_mix, w_in, conv_w,
           conv_b, lru_wa, lru_ba, lru_wx, lru_bx, lru_lambda, norm_lru, hgrn_gamma, norm_hgrn, w_out,
           norm_ffn, w_router, w_gate, w_up, w_down, norm_final):
    assert w_ada.shape[0] == 1 and hgrn_gamma.shape[1] == 2, "one trunk layer"
    l = 0
    Bp, Tp, _ = x_prompt.shape
    Bs, Ts, _ = x_sample.shape

    cond = jnp.concatenate([c_ctx[None, :], c, jnp.zeros((16 - 1 - Bs, D_MODEL), F32)], axis=0)
    mods = _adaln(cond, w_ada[l], b_ada[l][None, :]).reshape(16, N_MOD, D_MODEL)
    mods_p, mods_s = mods[0:1], mods[1:1 + Bs]

    p = {
        "norm_mix": norm_mix[l][None, :], "w_in": w_in[l].astype(BF16),
        "conv_w": conv_w[l], "conv_b": conv_b[l][None, :],
        "gate_w": _gate_tiles(lru_wa[l], lru_wx[l]), "gate_b": _bias_tiles(lru_ba[l], lru_bx[l]),
        "lam": lru_lambda[l], "norm_lru": norm_lru[l][None, :],
        "gamma": hgrn_gamma, "norm_hgrn": norm_hgrn[l][None, :],
        "w_out": w_out[l].astype(BF16), "norm_ffn": norm_ffn[l][None, :],
        "w_router": _router_split(w_router[l]),
    }
    zeros_p = jnp.zeros((Bp, D_LRU), F32)
    x1p, slot_p, xd_p, last_f, last_b, st_p = _mixer_and_router(
        x_prompt, mods_p, p, Tp, Tp, zeros_p, zeros_p, None, True)
    x1s, slot_s, xd_s, _, _, _ = _mixer_and_router(
        x_sample, mods_s, p, Ts, GRID_W, state_lru[:, l, 0], state_lru[:, l, 1], state_hgrn[:, l], False)

    yd_p, yd_s = _ffn(xd_p, xd_s, w_gate[l], w_up[l], w_down[l])
    nf = norm_final[None, :]
    y_prompt = _combine(slot_p, yd_p, x1p, mods_p, nf, Tp).reshape(Bp, Tp, D_MODEL)
    y_sample = _combine(slot_s, yd_s, x1s, mods_s, nf, Ts).reshape(Bs, Ts, D_MODEL)
    new_state_lru = jnp.stack([last_f, last_b], axis=1)[:, None]
    new_state_hgrn = st_p[:, None]
    return (y_prompt, y_sample, new_state_lru, new_state_hgrn)
```

```python
import functools

import jax
import jax.numpy as jnp
from jax import lax
from jax.experimental import pallas as pl
from jax.experimental.pallas import tpu as pltpu

F32 = jnp.float32
BF16 = jnp.bfloat16

D_MODEL = 1024
D_LRU = 512
D_HGRN = 512
HG_HEADS = 4
HG_DK = 128
LRU_BLOCKS = 8
LRU_BW = 64
LRU_C = 8.0
N_EXPERTS = 16
EC_CAPACITY_FACTOR = 2
N_MOD = 6
D_IN = 7 * 512
GRID_W = 64
EPS = 1e-6
LOG2E = 1.4426950408889634
TINY = 1e-37

LANES = 128
TOK_TILE = 256
PROJ_TILE = 512
MIX_TILE = 512
LRU_GROUP = 8
PITCH_PAD = 8
HG_CHUNK = 64
HG_ROWS = 256
FFN_TILE = 512
SEL_BISECT = 20
D_AUG = D_MODEL + LANES
VMEM_LIMIT = 56 * 1024 * 1024


def _cp(sem, vmem=VMEM_LIMIT):
    return pltpu.CompilerParams(dimension_semantics=sem, vmem_limit_bytes=vmem)


def _sigmoid(x):
    return 0.5 * (1.0 + jnp.tanh(0.5 * x))


def _dot(a, b):
    return jnp.dot(a, b, preferred_element_type=F32)


def _dot_nt(a, b):
    return lax.dot_general(a, b, (((1,), (1,)), ((), ())), preferred_element_type=F32)


def _dot_tn(a, b):
    return lax.dot_general(a, b, (((0,), (0,)), ((), ())), preferred_element_type=F32)


def _adaln_kernel(c_ref, w_ref, b_ref, o_ref):
    c = c_ref[...]
    s = (c * _sigmoid(c)).astype(BF16)
    o_ref[...] = _dot(s, w_ref[...].astype(BF16)) + b_ref[...]


def _adaln(cond, w, b):
    n = cond.shape[0]
    tn = 1024
    return pl.pallas_call(
        _adaln_kernel,
        grid=(w.shape[1] // tn,),
        in_specs=[pl.BlockSpec((n, D_MODEL), lambda j: (0, 0)),
                  pl.BlockSpec((D_MODEL, tn), lambda j: (0, j)),
                  pl.BlockSpec((1, tn), lambda j: (0, j))],
        out_specs=pl.BlockSpec((n, tn), lambda j: (0, j)),
        out_shape=jax.ShapeDtypeStruct((n, w.shape[1]), F32),
        compiler_params=_cp(("arbitrary",)),
    )(cond, w, b)


def _proj_kernel(x_ref, mod_ref, g_ref, w_ref, o_ref):
    x = x_ref[...]
    ms = jnp.mean(x * x, axis=-1, keepdims=True)
    y = x * lax.rsqrt(ms + EPS) * g_ref[...]
    h = y * (1.0 + mod_ref[0, 1:2, :]) + mod_ref[0, 0:1, :]
    o_ref[...] = _dot(h.astype(BF16), w_ref[...])


def _in_proj(x, mods, gain, w_bf, T):
    n = x.shape[0]
    per_req = T // PROJ_TILE if T >= PROJ_TILE else None
    if per_req is None:
        reqs = PROJ_TILE // T
        mod_map = lambda i: (0, 0, 0)
        assert mods.shape[0] == 1 and reqs >= 1
    else:
        mod_map = lambda i: (i // per_req, 0, 0)
    return pl.pallas_call(
        _proj_kernel,
        grid=(n // PROJ_TILE,),
        in_specs=[pl.BlockSpec((PROJ_TILE, D_MODEL), lambda i: (i, 0)),
                  pl.BlockSpec((1, N_MOD, D_MODEL), mod_map),
                  pl.BlockSpec((1, D_MODEL), lambda i: (0, 0)),
                  pl.BlockSpec((D_MODEL, D_IN), lambda i: (0, 0))],
        out_specs=pl.BlockSpec((PROJ_TILE, D_IN), lambda i: (i, 0)),
        out_shape=jax.ShapeDtypeStruct((n, D_IN), F32),
        compiler_params=_cp(("arbitrary",)),
    )(x, mods, gain, w_bf)


def _lru_kernel(T, row_len, xr_ref, cw_ref, cb_ref, w_ref, bias_ref, lam_ref, h0f_ref, h0b_ref,
                hs_ref, lf_ref, lb_ref, xpad, af, uf, ab, ub, hf_s, hb_s):
    pitch = T + PITCH_PAD
    pos = lax.broadcasted_iota(jnp.int32, (T, LANES), 0) & (row_len - 1)
    cw = cw_ref[...]
    w0 = jnp.where(pos >= 2, cw[0:1], 0.0)
    w1 = jnp.where(pos >= 1, cw[1:2], 0.0)
    w2 = jnp.broadcast_to(cw[2:3], (T, LANES))
    w3 = jnp.where(pos <= row_len - 2, cw[3:4], 0.0)
    cb = cb_ref[...]
    nl = -lam_ref[...]
    sp = jnp.maximum(nl, 0.0) + jnp.log1p(jnp.exp(-jnp.abs(nl)))
    c2 = (-0.5 * LRU_C * LOG2E) * sp
    w = w_ref[0]
    bias = bias_ref[0]
    for s in range(2):
        xpad[s, 0:8, :] = jnp.zeros((8, LANES), F32)
        xpad[s, T + 8:T + 16, :] = jnp.zeros((8, LANES), F32)
    for b in range(LRU_GROUP):
        x = xr_ref[b]
        xp = xpad.at[b % 2]
        xp[8:T + 8, :] = x
        xc = w0 * xp[6:T + 6, :] + w1 * xp[7:T + 7, :] + w2 * x + w3 * xp[9:T + 9, :] + cb
        xh = 0.5 * xc
        z = _dot(xc.astype(BF16), w) + bias
        rows = pl.ds(b * pitch, T)
        for d, (a_s, u_s) in enumerate(((af, uf), (ab, ub))):
            tr = jnp.tanh(z[:, (2 * d) * LANES:(2 * d + 1) * LANES])
            ti = jnp.tanh(z[:, (2 * d + 1) * LANES:(2 * d + 2) * LANES])
            a = jnp.exp2(c2[d:d + 1] + c2[d:d + 1] * tr)
            om = 1.0 - a * a
            a_s[rows, :] = a
            u_s[rows, :] = (om * lax.rsqrt(jnp.maximum(om, TINY))) * (xh + ti * xh)

    def step(t, carry):
        hf, hb = carry
        rf = pl.ds(t, LRU_GROUP, stride=pitch)
        hf = af[rf, :] * hf + uf[rf, :]
        hf_s[rf, :] = hf
        rb = pl.ds(T - 1 - t, LRU_GROUP, stride=pitch)
        hb = ab[rb, :] * hb + ub[rb, :]
        hb_s[rb, :] = hb
        return hf, hb

    hf, hb = lax.fori_loop(0, T, step, (h0f_ref[...], h0b_ref[...]), unroll=8)
    lf_ref[...] = hf
    lb_ref[...] = hb
    for b in range(LRU_GROUP):
        rows = pl.ds(b * pitch, T)
        hs_ref[b] = hf_s[rows, :] + hb_s[rows, :]


def _lru(proj3, conv_w, conv_b, w_tiles, bias_tiles, lam, h0f, h0b, T, row_len):
    B = proj3.shape[0]
    n_ct = D_LRU // LANES
    rows = LRU_GROUP * (T + PITCH_PAD)
    vec = pl.BlockSpec((LRU_GROUP, LANES), lambda g, c: (g, c))
    return pl.pallas_call(
        functools.partial(_lru_kernel, T, row_len),
        grid=(B // LRU_GROUP, n_ct),
        in_specs=[pl.BlockSpec((LRU_GROUP, T, LANES), lambda g, c: (g, 0, c)),
                  pl.BlockSpec((4, LANES), lambda g, c: (0, c)),
                  pl.BlockSpec((1, LANES), lambda g, c: (0, c)),
                  pl.BlockSpec((1, LANES, 4 * LANES), lambda g, c: (c, 0, 0)),
                  pl.BlockSpec((1, 1, 4 * LANES), lambda g, c: (c, 0, 0)),
                  pl.BlockSpec((2, LANES), lambda g, c: (0, c)),
                  vec, vec],
        out_specs=[pl.BlockSpec((LRU_GROUP, T, LANES), lambda g, c: (g, 0, c)), vec, vec],
        out_shape=[jax.ShapeDtypeStruct((B, T, D_LRU), F32),
                   jax.ShapeDtypeStruct((B, D_LRU), F32),
                   jax.ShapeDtypeStruct((B, D_LRU), F32)],
        scratch_shapes=[pltpu.VMEM((2, T + 16, LANES), F32)] + [pltpu.VMEM((rows, LANES), F32)] * 6,
        compiler_params=_cp(("arbitrary", "arbitrary")),
    )(proj3, conv_w, conv_b, w_tiles, bias_tiles, lam, h0f, h0b)


def _hgrn_kernel(T, has_s0, want_state, *refs):
    q_ref, ff_ref, fb_ref, v_ref, og_ref, gam_ref, gain_ref = refs[:7]
    refs = refs[7:]
    if has_s0:
        s0_ref, refs = refs[0], refs[1:]
    y_ref, refs = refs[0], refs[1:]
    if want_state:
        st_ref, refs = refs[0], refs[1:]
    q_s, v_s, kf_s, kb_s, bf_s, bb_s, of_s, ob_s, st_s = refs
    C = HG_CHUNK
    n = T // C
    RB = HG_ROWS
    heads = [slice(h * LANES, (h + 1) * LANES) for h in range(HG_HEADS)]

    ti = lax.broadcasted_iota(jnp.int32, (RB, RB), 0)
    si = lax.broadcasted_iota(jnp.int32, (RB, RB), 1)
    sh = C.bit_length() - 1
    same = (ti >> sh) == (si >> sh)
    tri = ((same & (si <= ti)).astype(BF16), (same & (si >= ti)).astype(BF16))

    for r in range(T // RB):
        rows = slice(r * RB, (r + 1) * RB)
        qr = q_ref[rows, :]
        q_s[rows, :] = qr * _sigmoid(qr) * (HG_DK ** -0.5)
        v_s[rows, :] = v_ref[rows, :].astype(BF16)
        for d, (f_ref, k_s, b_s) in enumerate(((ff_ref, kf_s, bf_s), (fb_ref, kb_s, bb_s))):
            g0 = gam_ref[d, 0:1, :]
            g1 = gam_ref[d, 1:2, :]
            m = jnp.maximum(g0, g1)
            e0 = jnp.exp(g0 - m)
            lbd = e0 / (e0 + jnp.exp(g1 - m))
            sig = _sigmoid(f_ref[rows, :])
            k_s[rows, :] = (1.0 - lbd) * (1.0 - sig)
            lf = jnp.log(lbd + (1.0 - lbd) * sig)
            hi = lf.astype(BF16)
            r1 = lf - hi.astype(F32)
            mid = r1.astype(BF16)
            lo = (r1 - mid.astype(F32)).astype(BF16)
            cs = _dot(tri[d], jnp.concatenate([hi, mid, lo], axis=1))
            b_s[rows, :] = cs[:, 0:D_HGRN] + (cs[:, D_HGRN:2 * D_HGRN] + cs[:, 2 * D_HGRN:3 * D_HGRN])

    for d in range(2):
        for h in range(HG_HEADS):
            st_s[d * HG_HEADS + h] = s0_ref[d, h].T if has_s0 else jnp.zeros((LANES, LANES), F32)

    ri = lax.broadcasted_iota(jnp.int32, (C, C), 0)
    ci = lax.broadcasted_iota(jnp.int32, (C, C), 1)

    def body(i, carry):
        for d, (k_s, b_s, o_s) in enumerate(((kf_s, bf_s, of_s), (kb_s, bb_s, ob_s))):
            c = i if d == 0 else n - 1 - i
            sl = pl.ds(pl.multiple_of(c * C, C), C)
            for h in range(HG_HEADS):
                q = q_s[sl, heads[h]]
                k = k_s[sl, heads[h]]
                b = b_s[sl, heads[h]]
                v = v_s[sl, heads[h]]
                bm = b[C // 2:C // 2 + 1, :]
                g = b[C - 1:C, :] if d == 0 else b[0:1, :]
                qi = q * jnp.exp(b - bm)
                qe = (qi * jnp.exp(bm)).astype(BF16)
                ki = k * jnp.exp(bm - b)
                kd = (ki * jnp.exp(g - bm)).astype(BF16)
                s = _dot_nt(qi.astype(BF16), ki.astype(BF16))
                p = jnp.where((ci <= ri) if d == 0 else (ci >= ri), s, 0.0).astype(BF16)
                st = st_s[d * HG_HEADS + h]
                o_s[sl, heads[h]] = _dot(p, v) + _dot_nt(qe, st.astype(BF16))
                st_s[d * HG_HEADS + h] = st * jnp.exp(g) + _dot_tn(v, kd)
        return carry

    lax.fori_loop(0, n, body, 0)
    if want_state:
        for d in range(2):
            for h in range(HG_HEADS):
                st_ref[d, h] = st_s[d * HG_HEADS + h].T

    for r in range(T // RB):
        rows = slice(r * RB, (r + 1) * RB)
        for h in range(HG_HEADS):
            o = of_s[rows, heads[h]] + ob_s[rows, heads[h]]
            ms = jnp.mean(o * o, axis=-1, keepdims=True)
            og = og_ref[rows, heads[h]]
            y_ref[rows, heads[h]] = o * lax.rsqrt(ms + EPS) * gain_ref[...] * (og * _sigmoid(og))


def _hgrn(proj3, gamma, gain, s0, T, want_state):
    B = proj3.shape[0]
    has_s0 = s0 is not None
    col0 = 2 * D_LRU // D_HGRN

    def col(k):
        return pl.BlockSpec((None, T, D_HGRN), lambda b, k=k: (b, 0, col0 + k))

    st_spec = pl.BlockSpec((None, 2, HG_HEADS, HG_DK, HG_DK), lambda b: (b, 0, 0, 0, 0))
    in_specs = [col(0), col(1), col(2), col(3), col(4),
                pl.BlockSpec((2, 2, D_HGRN), lambda b: (0, 0, 0)),
                pl.BlockSpec((1, LANES), lambda b: (0, 0))]
    args = [proj3, proj3, proj3, proj3, proj3, gamma, gain]
    if has_s0:
        in_specs.append(st_spec)
        args.append(s0)
    out_specs = [pl.BlockSpec((None, T, D_HGRN), lambda b: (b, 0, 0))]
    out_shape = [jax.ShapeDtypeStruct((B, T, D_HGRN), F32)]
    if want_state:
        out_specs.append(st_spec)
        out_shape.append(jax.ShapeDtypeStruct((B, 2, HG_HEADS, HG_DK, HG_DK), F32))
    res = pl.pallas_call(
        functools.partial(_hgrn_kernel, T, has_s0, want_state),
        grid=(B,),
        in_specs=in_specs,
        out_specs=out_specs,
        out_shape=out_shape,
        scratch_shapes=[pltpu.VMEM((T, D_HGRN), F32), pltpu.VMEM((T, D_HGRN), BF16)]
        + [pltpu.VMEM((T, D_HGRN), F32)] * 6
        + [pltpu.VMEM((2 * HG_HEADS, HG_DK, HG_DK), F32)],
        compiler_params=_cp(("arbitrary",)),
    )(*args)
    return res if want_state else (res[0], None)


def _mix_out_kernel(hs_ref, gr_ref, yh_ref, x_ref, mod_ref, nl_ref, wo_ref, nf_ref, wr_ref,
                    x1_ref, ha_ref, at_ref):
    wr = wr_ref[...]
    g1 = mod_ref[0, 2:3, :]
    sh2 = mod_ref[0, 3:4, :]
    gain2 = nf_ref[...] * (1.0 + mod_ref[0, 4:5, :])
    lane = lax.broadcasted_iota(jnp.int32, (TOK_TILE, LANES), 1)
    for s in range(MIX_TILE // TOK_TILE):
        rows = slice(s * TOK_TILE, (s + 1) * TOK_TILE)
        hs = hs_ref[rows, :]
        ms = jnp.mean(hs * hs, axis=-1, keepdims=True)
        gr = gr_ref[rows, :]
        gelu = 0.5 * gr * (1.0 + jnp.tanh(0.7978845608028654 * (gr + 0.044715 * (gr * gr * gr))))
        y_lru = hs * lax.rsqrt(ms + EPS) * nl_ref[...] * gelu
        ycat = jnp.concatenate([y_lru.astype(BF16), yh_ref[rows, :].astype(BF16)], axis=-1)
        x1 = x_ref[rows, :] + g1 * _dot(ycat, wo_ref[...])
        x1_ref[rows, :] = x1
        ms = jnp.mean(x1 * x1, axis=-1, keepdims=True)
        h2 = x1 * lax.rsqrt(ms + EPS) * gain2 + sh2
        h2_hi = h2.astype(BF16)
        h2_mid = (h2 - h2_hi.astype(F32)).astype(BF16)
        p_hi = _dot(h2_hi, wr)
        logits = p_hi[:, 0:LANES] + (p_hi[:, LANES:2 * LANES] + _dot(h2_mid, wr)[:, 0:LANES])
        logits = jnp.where(lane < N_EXPERTS, logits, -jnp.inf)
        e = jnp.exp(logits - jnp.max(logits, axis=-1, keepdims=True))
        aff = e / jnp.sum(e, axis=-1, keepdims=True)
        w = at_ref.shape[1]
        r, c = (s * TOK_TILE) // w, (s * TOK_TILE) % w
        at_ref[r * N_EXPERTS:(r + 1) * N_EXPERTS, c:c + TOK_TILE] = aff.T[0:N_EXPERTS, :]
        hi = aff.astype(BF16).astype(F32)
        r1 = aff - hi
        mid = r1.astype(BF16).astype(F32)
        lo = (r1 - mid).astype(BF16).astype(F32)
        split = hi + pltpu.roll(mid, N_EXPERTS, 1) + pltpu.roll(lo, 2 * N_EXPERTS, 1)
        ha_ref[rows, 0:D_MODEL] = h2_hi
        ha_ref[rows, D_MODEL:D_AUG] = split.astype(BF16)


def _mix_out(hsum, proj, yh, x, mods, norm_lru, wo_bf, norm_ffn, wr_pad, T):
    n = x.shape[0]
    tiles = n // MIX_TILE
    assert T % TOK_TILE == 0 and (MIX_TILE % T == 0 or T % MIX_TILE == 0)
    q = max(T // MIX_TILE, 1)
    at_rows = N_EXPERTS * max(MIX_TILE // T, 1)
    at_spec = pl.BlockSpec((at_rows, min(T, MIX_TILE)), lambda i: (i // q, i % q))
    mod_map = (lambda i: (i // q, 0, 0)) if mods.shape[0] > 1 else (lambda i: (0, 0, 0))
    tile = lambda w: pl.BlockSpec((MIX_TILE, w), lambda i: (i, 0))
    const = lambda r, w: pl.BlockSpec((r, w), lambda i: (0, 0))
    return pl.pallas_call(
        _mix_out_kernel,
        grid=(tiles,),
        in_specs=[tile(D_LRU),
                  pl.BlockSpec((MIX_TILE, D_LRU), lambda i: (i, 1)),
                  tile(D_HGRN), tile(D_MODEL),
                  pl.BlockSpec((1, N_MOD, D_MODEL), mod_map),
                  const(1, D_LRU), const(D_MODEL, D_MODEL), const(1, D_MODEL), const(D_MODEL, 2 * LANES)],
        out_specs=[tile(D_MODEL), tile(D_AUG), at_spec],
        out_shape=[jax.ShapeDtypeStruct((n, D_MODEL), F32),
                   jax.ShapeDtypeStruct((n, D_AUG), BF16),
                   jax.ShapeDtypeStruct((n // T * N_EXPERTS, T), F32)],
        compiler_params=_cp(("arbitrary",)),
    )(hsum, proj, yh, x, mods, norm_lru, wo_bf, norm_ffn, wr_pad)


def _select_kernel(cap, at_ref, slot_ref):
    a = at_ref[...]
    R, T = a.shape
    ones = jnp.ones((T, LANES), BF16)
    su = (lax.broadcasted_iota(jnp.int32, (T, T), 0)
          < lax.broadcasted_iota(jnp.int32, (T, T), 1)).astype(BF16)

    def wide(x):
        return jnp.concatenate([x] * (T // LANES), axis=1)

    def count(mask):
        return _dot(mask.astype(BF16), ones)

    lo = jnp.zeros((R, LANES), F32)
    hi = jnp.full((R, LANES), 2.0, F32)
    for _ in range(SEL_BISECT):
        mid = 0.5 * (lo + hi)
        ge = count(a >= wide(mid)) >= cap
        lo = jnp.where(ge, mid, lo)
        hi = jnp.where(ge, hi, mid)

    def cond(st):
        return (st[0] < T) & (jnp.min(st[2]) < cap)

    def body(st):
        it, cur, n, thr = st
        m = jnp.max(jnp.where(a < wide(cur), a, -1.0), axis=1, keepdims=True)
        m = jnp.broadcast_to(m, (R, LANES))
        c = count(a >= wide(m))
        act = n < cap
        return it + 1, jnp.where(act, m, cur), jnp.where(act, c, n), jnp.where(act, m, thr)

    _, _, _, thr = lax.while_loop(cond, body, (jnp.int32(0), hi, count(a >= wide(hi)), lo))
    thr_w = wide(thr)
    gt = a > thr_w
    eq = a == thr_w
    need = wide(cap - count(gt))
    sel = gt | (eq & (_dot(eq.astype(BF16), su) < need))
    slot_ref[...] = jnp.where(sel, _dot(sel.astype(BF16), su), -1.0).astype(jnp.int32)


def _select(at, T):
    cap = EC_CAPACITY_FACTOR * T // N_EXPERTS
    R = at.shape[0]
    return pl.pallas_call(
        functools.partial(_select_kernel, cap),
        grid=(1,),
        in_specs=[pl.BlockSpec((R, T), lambda i: (0, 0))],
        out_specs=pl.BlockSpec((R, T), lambda i: (0, 0)),
        out_shape=jax.ShapeDtypeStruct((R, T), jnp.int32),
        compiler_params=_cp(("arbitrary",)),
    )(at)


def _onehot(cap, slot_ref):
    j = lax.broadcasted_iota(jnp.int32, (cap, slot_ref.shape[1]), 0)
    return jnp.concatenate([(slot_ref[e:e + 1, :] == j).astype(BF16) for e in range(N_EXPERTS)], axis=0)


def _dispatch_kernel(cap, slot_ref, ha_ref, xd_ref):
    rows = _dot(_onehot(cap, slot_ref), ha_ref[...])
    for e in range(N_EXPERTS):
        xd_ref[e] = rows[e * cap:(e + 1) * cap].astype(BF16)


def _dispatch(slot, ha, T):
    cap = EC_CAPACITY_FACTOR * T // N_EXPERTS
    B = ha.shape[0] // T
    return pl.pallas_call(
        functools.partial(_dispatch_kernel, cap),
        grid=(B,),
        in_specs=[pl.BlockSpec((N_EXPERTS, T), lambda b: (b, 0)),
                  pl.BlockSpec((T, D_AUG), lambda b: (b, 0))],
        out_specs=pl.BlockSpec((N_EXPERTS, cap, D_AUG), lambda b: (0, b, 0)),
        out_shape=jax.ShapeDtypeStruct((N_EXPERTS, B * cap, D_AUG), BF16),
        compiler_params=_cp(("arbitrary",)),
    )(slot, ha)


def _ffn_kernel(xp_ref, xs_ref, wg_ref, wu_ref, wd_ref, yp_ref, ys_ref, wg_s, wu_s, wd_s):
    e = pl.program_id(0)

    @pl.when(pl.program_id(1) == 0)
    def _():
        wg_s[...] = wg_ref[...].astype(BF16)
        wu_s[...] = wu_ref[...].astype(BF16)
        wd_s[...] = wd_ref[...].astype(BF16)

    lane = lax.broadcasted_iota(jnp.int32, (FFN_TILE, LANES), 1)
    pick = (lane == e) | (lane == e + N_EXPERTS) | (lane == e + 2 * N_EXPERTS)
    for x_ref, y_ref in ((xp_ref, yp_ref), (xs_ref, ys_ref)):
        x = x_ref[:, 0:D_MODEL]
        gate = jnp.sum(jnp.where(pick, x_ref[:, D_MODEL:D_AUG].astype(F32), 0.0), axis=-1, keepdims=True)
        hg = _dot(x, wg_s[...])
        hid = (hg * _sigmoid(hg)) * _dot(x, wu_s[...])
        y_ref[...] = (_dot(hid.astype(BF16), wd_s[...]) * gate).astype(BF16)


def _ffn(xd_p, xd_s, w_gate, w_up, w_down):
    rows = xd_p.shape[1]
    assert xd_s.shape[1] == rows and rows % FFN_TILE == 0
    xspec = pl.BlockSpec((None, FFN_TILE, D_AUG), lambda e, m: (e, m, 0))
    wspec = pl.BlockSpec((None, D_MODEL, D_MODEL), lambda e, m: (e, 0, 0))
    yspec = pl.BlockSpec((None, FFN_TILE, D_MODEL), lambda e, m: (e, m, 0))
    yshape = jax.ShapeDtypeStruct((N_EXPERTS, rows, D_MODEL), BF16)
    return pl.pallas_call(
        _ffn_kernel,
        grid=(N_EXPERTS, rows // FFN_TILE),
        in_specs=[xspec, xspec, wspec, wspec, wspec],
        out_specs=[yspec, yspec],
        out_shape=[yshape, yshape],
        scratch_shapes=[pltpu.VMEM((D_MODEL, D_MODEL), BF16)] * 3,
        compiler_params=_cp(("arbitrary", "arbitrary")),
    )(xd_p, xd_s, w_gate, w_up, w_down)


def _combine_kernel(cap, slot_ref, yd_ref, x1_ref, mod_ref, nf_ref, o_ref):
    yd = jnp.concatenate([yd_ref[e] for e in range(N_EXPERTS)], axis=0)
    acc = _dot_tn(_onehot(cap, slot_ref), yd)
    x2 = x1_ref[...] + mod_ref[0, 5:6, :] * acc
    ms = jnp.mean(x2 * x2, axis=-1, keepdims=True)
    o_ref[...] = x2 * lax.rsqrt(ms + EPS) * nf_ref[...]


def _combine(slot, yd, x1, mods, norm_final, T):
    cap = EC_CAPACITY_FACTOR * T // N_EXPERTS
    n = x1.shape[0]
    mod_map = (lambda b: (b, 0, 0)) if mods.shape[0] > 1 else (lambda b: (0, 0, 0))
    return pl.pallas_call(
        functools.partial(_combine_kernel, cap),
        grid=(n // T,),
        in_specs=[pl.BlockSpec((N_EXPERTS, T), lambda b: (b, 0)),
                  pl.BlockSpec((N_EXPERTS, cap, D_MODEL), lambda b: (0, b, 0)),
                  pl.BlockSpec((T, D_MODEL), lambda b: (b, 0)),
                  pl.BlockSpec((1, N_MOD, D_MODEL), mod_map),
                  pl.BlockSpec((1, D_MODEL), lambda b: (0, 0))],
        out_specs=pl.BlockSpec((T, D_MODEL), lambda b: (b, 0)),
        out_shape=jax.ShapeDtypeStruct((n, D_MODEL), F32),
        compiler_params=_cp(("arbitrary",)),
    )(slot, yd, x1, mods, norm_final)


def _gate_tiles(wa, wx):
    eye = jnp.eye(LRU_BLOCKS, dtype=wa.dtype)

    def dense(w):
        full = jnp.einsum('nde,nm->ndme', w, eye).reshape(D_LRU, D_LRU)
        t = full.reshape(D_LRU // LANES, LANES, D_LRU // LANES, LANES)
        return jnp.stack([t[c, :, c, :] for c in range(D_LRU // LANES)])

    tiles = jnp.concatenate([dense(wa[0]), dense(wx[0]), dense(wa[1]), dense(wx[1])], axis=-1)
    return (0.5 * tiles).astype(BF16)


def _router_split(w):
    wp = jnp.pad(w, ((0, 0), (0, LANES - N_EXPERTS)))
    hi = wp.astype(BF16)
    mid = (wp - hi.astype(F32)).astype(BF16)
    return jnp.concatenate([hi, mid], axis=1)


def _bias_tiles(ba, bx):
    t = lambda v: v.reshape(D_LRU // LANES, 1, LANES)
    return 0.5 * jnp.concatenate([t(ba[0]), t(bx[0]), t(ba[1]), t(bx[1])], axis=-1)


def _mixer_and_router(x, mods, p, T, row_len, h0f, h0b, s0, want_state):
    B = x.shape[0]
    xf = x.reshape(B * T, D_MODEL)
    proj = _in_proj(xf, mods, p["norm_mix"], p["w_in"], T)
    proj3 = proj.reshape(B, T, D_IN)
    hsum, last_f, last_b = _lru(proj3, p["conv_w"], p["conv_b"], p["gate_w"], p["gate_b"],
                                p["lam"], h0f, h0b, T, row_len)
    yh, st = _hgrn(proj3, p["gamma"], p["norm_hgrn"], s0, T, want_state)
    x1, ha, at = _mix_out(hsum.reshape(B * T, D_LRU), proj, yh.reshape(B * T, D_HGRN), xf, mods,
                          p["norm_lru"], p["w_out"], p["norm_ffn"], p["w_router"], T)
    slot = _select(at, T)
    xd = _dispatch(slot, ha, T)
    return x1, slot, xd, last_f, last_b, st


def kernel(x_prompt, x_sample, state_lru, state_hgrn, c, c_ctx, w_ada, b_ada, norm_mix, w_in, conv_w,
           conv_b, lru_wa, lru_ba, lru_wx, lru_bx, lru_lambda, norm_lru, hgrn_gamma, norm_hgrn, w_out,
           norm_ffn, w_router, w_gate, w_up, w_down, norm_final):
    assert w_ada.shape[0] == 1 and hgrn_gamma.shape[1] == 2, "one trunk layer"
    l = 0
    Bp, Tp, _ = x_prompt.shape
    Bs, Ts, _ = x_sample.shape

    cond = jnp.concatenate([c_ctx[None, :], c, jnp.zeros((16 - 1 - Bs, D_MODEL), F32)], axis=0)
    mods = _adaln(cond, w_ada[l], b_ada[l][None, :]).reshape(16, N_MOD, D_MODEL)
    mods_p, mods_s = mods[0:1], mods[1:1 + Bs]

    p = {
        "norm_mix": norm_mix[l][None, :], "w_in": w_in[l].astype(BF16),
        "conv_w": conv_w[l], "conv_b": conv_b[l][None, :],
        "gate_w": _gate_tiles(lru_wa[l], lru_wx[l]), "gate_b": _bias_tiles(lru_ba[l], lru_bx[l]),
        "lam": lru_lambda[l], "norm_lru": norm_lru[l][None, :],
        "gamma": hgrn_gamma, "norm_hgrn": norm_hgrn[l][None, :],
        "w_out": w_out[l].astype(BF16), "norm_ffn": norm_ffn[l][None, :],
        "w_router": _router_split(w_router[l]),
    }
    zeros_p = jnp.zeros((Bp, D_LRU), F32)
    x1p, slot_p, xd_p, last_f, last_b, st_p = _mixer_and_router(
        x_prompt, mods_p, p, Tp, Tp, zeros_p, zeros_p, None, True)
    x1s, slot_s, xd_s, _, _, _ = _mixer_and_router(
        x_sample, mods_s, p, Ts, GRID_W, state_lru[:, l, 0], state_lru[:, l, 1], state_hgrn[:, l], False)

    yd_p, yd_s = _ffn(xd_p, xd_s, w_gate[l], w_up[l], w_down[l])
    nf = norm_final[None, :]
    y_prompt = _combine(slot_p, yd_p, x1p, mods_p, nf, Tp).reshape(Bp, Tp, D_MODEL)
    y_sample = _combine(slot_s, yd_s, x1s, mods_s, nf, Ts).reshape(Bs, Ts, D_MODEL)
    new_state_lru = jnp.stack([last_f, last_b], axis=1)[:, None]
    new_state_hgrn = st_p[:, None]
    return (y_prompt, y_sample, new_state_lru, new_state_hgrn)
```

```python
import functools

import jax
import jax.numpy as jnp
from jax import lax
from jax.experimental import pallas as pl
from jax.experimental.pallas import tpu as pltpu

F32 = jnp.float32
BF16 = jnp.bfloat16

D_MODEL = 1024
D_LRU = 512
D_HGRN = 512
HG_HEADS = 4
HG_DK = 128
LRU_BLOCKS = 8
LRU_BW = 64
LRU_C = 8.0
N_EXPERTS = 16
EC_CAPACITY_FACTOR = 2
N_MOD = 6
D_IN = 7 * 512
GRID_W = 64
EPS = 1e-6
LOG2E = 1.4426950408889634
TINY = 1e-37

LANES = 128
TOK_TILE = 256
PROJ_TILE = 512
MIX_TILE = 1024
LRU_GROUP = 8
PITCH_PAD = 8
HG_CHUNK = 64
HG_ROWS = 256
FFN_TILE = 512
SEL_BISECT = 20
D_AUG = D_MODEL + LANES
VMEM_LIMIT = 56 * 1024 * 1024


def _cp(sem, vmem=VMEM_LIMIT):
    return pltpu.CompilerParams(dimension_semantics=sem, vmem_limit_bytes=vmem)


def _sigmoid(x):
    return 0.5 * (1.0 + jnp.tanh(0.5 * x))


def _dot(a, b):
    return jnp.dot(a, b, preferred_element_type=F32)


def _dot_nt(a, b):
    return lax.dot_general(a, b, (((1,), (1,)), ((), ())), preferred_element_type=F32)


def _dot_tn(a, b):
    return lax.dot_general(a, b, (((0,), (0,)), ((), ())), preferred_element_type=F32)


def _adaln_kernel(c_ref, w_ref, b_ref, o_ref):
    c = c_ref[...]
    s = (c * _sigmoid(c)).astype(BF16)
    o_ref[...] = _dot(s, w_ref[...].astype(BF16)) + b_ref[...]


def _adaln(cond, w, b):
    n = cond.shape[0]
    tn = 1024
    return pl.pallas_call(
        _adaln_kernel,
        grid=(w.shape[1] // tn,),
        in_specs=[pl.BlockSpec((n, D_MODEL), lambda j: (0, 0)),
                  pl.BlockSpec((D_MODEL, tn), lambda j: (0, j)),
                  pl.BlockSpec((1, tn), lambda j: (0, j))],
        out_specs=pl.BlockSpec((n, tn), lambda j: (0, j)),
        out_shape=jax.ShapeDtypeStruct((n, w.shape[1]), F32),
        compiler_params=_cp(("arbitrary",)),
    )(cond, w, b)


def _proj_kernel(x_ref, mod_ref, g_ref, w_ref, o_ref):
    x = x_ref[...]
    ms = jnp.mean(x * x, axis=-1, keepdims=True)
    y = x * lax.rsqrt(ms + EPS) * g_ref[...]
    h = y * (1.0 + mod_ref[0, 1:2, :]) + mod_ref[0, 0:1, :]
    o_ref[...] = _dot(h.astype(BF16), w_ref[...].astype(BF16))


def _in_proj(x, mods, gain, w_bf, T):
    n = x.shape[0]
    per_req = T // PROJ_TILE if T >= PROJ_TILE else None
    if per_req is None:
        reqs = PROJ_TILE // T
        mod_map = lambda i: (0, 0, 0)
        assert mods.shape[0] == 1 and reqs >= 1
    else:
        mod_map = lambda i: (i // per_req, 0, 0)
    return pl.pallas_call(
        _proj_kernel,
        grid=(n // PROJ_TILE,),
        in_specs=[pl.BlockSpec((PROJ_TILE, D_MODEL), lambda i: (i, 0)),
                  pl.BlockSpec((1, N_MOD, D_MODEL), mod_map),
                  pl.BlockSpec((1, D_MODEL), lambda i: (0, 0)),
                  pl.BlockSpec((D_MODEL, D_IN), lambda i: (0, 0), pipeline_mode=pl.Buffered(1))],
        out_specs=pl.BlockSpec((PROJ_TILE, D_IN), lambda i: (i, 0)),
        out_shape=jax.ShapeDtypeStruct((n, D_IN), F32),
        compiler_params=_cp(("arbitrary",)),
    )(x, mods, gain, w_bf)


def _lru_kernel(T, row_len, xr_ref, cw_ref, cb_ref, w_ref, lam_ref, h0f_ref, h0b_ref,
                hs_ref, lf_ref, lb_ref, xpad, af, uf, ab, ub, hf_s, hb_s):
    pitch = T + PITCH_PAD
    pos = lax.broadcasted_iota(jnp.int32, (T, LANES), 0) & (row_len - 1)
    cw = cw_ref[...]
    w0 = jnp.where(pos >= 2, cw[0:1], 0.0)
    w1 = jnp.where(pos >= 1, cw[1:2], 0.0)
    w2 = jnp.broadcast_to(cw[2:3], (T, LANES))
    w3 = jnp.where(pos <= row_len - 2, cw[3:4], 0.0)
    cb = cb_ref[...]
    nl = -lam_ref[...]
    sp = jnp.maximum(nl, 0.0) + jnp.log1p(jnp.exp(-jnp.abs(nl)))
    c2 = (-0.5 * LRU_C * LOG2E) * sp
    w = w_ref[0]
    ones = (lax.broadcasted_iota(jnp.int32, (T, LANES), 1) < 2).astype(BF16)
    for s in range(2):
        xpad[s, 0:8, :] = jnp.zeros((8, LANES), F32)
        xpad[s, T + 8:T + 16, :] = jnp.zeros((8, LANES), F32)
    for b in range(LRU_GROUP):
        x = xr_ref[b]
        xp = xpad.at[b % 2]
        xp[8:T + 8, :] = x
        xc = w0 * xp[6:T + 6, :] + w1 * xp[7:T + 7, :] + w2 * x + w3 * xp[9:T + 9, :] + cb
        xh = 0.5 * xc
        z = _dot(jnp.concatenate([xc.astype(BF16), ones], axis=1), w)
        rows = pl.ds(b * pitch, T)
        for d, (a_s, u_s) in enumerate(((af, uf), (ab, ub))):
            tr = jnp.tanh(z[:, (2 * d) * LANES:(2 * d + 1) * LANES])
            ti = jnp.tanh(z[:, (2 * d + 1) * LANES:(2 * d + 2) * LANES])
            a = jnp.exp2(c2[d:d + 1] + c2[d:d + 1] * tr)
            om = 1.0 - a * a
            a_s[rows, :] = a
            u_s[rows, :] = (om * lax.rsqrt(jnp.maximum(om, TINY))) * (xh + ti * xh)

    def step(t, carry):
        hf, hb = carry
        rf = pl.ds(t, LRU_GROUP, stride=pitch)
        hf = af[rf, :] * hf + uf[rf, :]
        hf_s[rf, :] = hf
        rb = pl.ds(T - 1 - t, LRU_GROUP, stride=pitch)
        hb = ab[rb, :] * hb + ub[rb, :]
        hb_s[rb, :] = hb
        return hf, hb

    hf, hb = lax.fori_loop(0, T, step, (h0f_ref[...], h0b_ref[...]), unroll=8)
    lf_ref[...] = hf
    lb_ref[...] = hb
    for b in range(LRU_GROUP):
        rows = pl.ds(b * pitch, T)
        hs_ref[b] = hf_s[rows, :] + hb_s[rows, :]


def _lru(proj3, conv_w, conv_b, w_tiles, lam, h0f, h0b, T, row_len):
    B = proj3.shape[0]
    n_ct = D_LRU // LANES
    rows = LRU_GROUP * (T + PITCH_PAD)
    vec = pl.BlockSpec((LRU_GROUP, LANES), lambda g, c: (g, c))
    return pl.pallas_call(
        functools.partial(_lru_kernel, T, row_len),
        grid=(B // LRU_GROUP, n_ct),
        in_specs=[pl.BlockSpec((LRU_GROUP, T, LANES), lambda g, c: (g, 0, c)),
                  pl.BlockSpec((4, LANES), lambda g, c: (0, c)),
                  pl.BlockSpec((1, LANES), lambda g, c: (0, c)),
                  pl.BlockSpec((1, 2 * LANES, 4 * LANES), lambda g, c: (c, 0, 0)),
                  pl.BlockSpec((2, LANES), lambda g, c: (0, c)),
                  vec, vec],
        out_specs=[pl.BlockSpec((LRU_GROUP, T, LANES), lambda g, c: (g, 0, c)), vec, vec],
        out_shape=[jax.ShapeDtypeStruct((B, T, D_LRU), F32),
                   jax.ShapeDtypeStruct((B, D_LRU), F32),
                   jax.ShapeDtypeStruct((B, D_LRU), F32)],
        scratch_shapes=[pltpu.VMEM((2, T + 16, LANES), F32)] + [pltpu.VMEM((rows, LANES), F32)] * 6,
        compiler_params=_cp(("arbitrary", "arbitrary")),
    )(proj3, conv_w, conv_b, w_tiles, lam, h0f, h0b)


def _hgrn_kernel(T, has_s0, want_state, *refs):
    q_ref, ff_ref, fb_ref, v_ref, og_ref, gam_ref, gain_ref = refs[:7]
    refs = refs[7:]
    if has_s0:
        s0_ref, refs = refs[0], refs[1:]
    y_ref, refs = refs[0], refs[1:]
    if want_state:
        st_ref, refs = refs[0], refs[1:]
    q_s, v_s, kf_s, kb_s, bf_s, bb_s, of_s, ob_s, st_s = refs
    C = HG_CHUNK
    n = T // C
    RB = HG_ROWS
    heads = [slice(h * LANES, (h + 1) * LANES) for h in range(HG_HEADS)]

    ti = lax.broadcasted_iota(jnp.int32, (RB, RB), 0)
    si = lax.broadcasted_iota(jnp.int32, (RB, RB), 1)
    sh = C.bit_length() - 1
    same = (ti >> sh) == (si >> sh)
    tri = ((same & (si <= ti)).astype(BF16), (same & (si >= ti)).astype(BF16))

    for r in range(T // RB):
        rows = slice(r * RB, (r + 1) * RB)
        qr = q_ref[rows, :]
        qh = (0.5 * HG_DK ** -0.5) * qr
        q_s[rows, :] = qh + qh * jnp.tanh(0.5 * qr)
        v_s[rows, :] = v_ref[rows, :].astype(BF16)
        for d, (f_ref, k_s, b_s) in enumerate(((ff_ref, kf_s, bf_s), (fb_ref, kb_s, bb_s))):
            g0 = gam_ref[d, 0:1, :]
            g1 = gam_ref[d, 1:2, :]
            m = jnp.maximum(g0, g1)
            e0 = jnp.exp(g0 - m)
            lbd = e0 / (e0 + jnp.exp(g1 - m))
            ck = 0.5 * (1.0 - lbd)
            pt = ck * jnp.tanh(0.5 * f_ref[rows, :])
            k_s[rows, :] = ck - pt
            lf = jnp.log((lbd + ck) + pt)
            hi = lf.astype(BF16)
            r1 = lf - hi.astype(F32)
            mid = r1.astype(BF16)
            lo = (r1 - mid.astype(F32)).astype(BF16)
            cs = _dot(tri[d], jnp.concatenate([hi, mid, lo], axis=1))
            b_s[rows, :] = cs[:, 0:D_HGRN] + (cs[:, D_HGRN:2 * D_HGRN] + cs[:, 2 * D_HGRN:3 * D_HGRN])

    for d in range(2):
        for h in range(HG_HEADS):
            st_s[d * HG_HEADS + h] = s0_ref[d, h].T if has_s0 else jnp.zeros((LANES, LANES), F32)

    ri = lax.broadcasted_iota(jnp.int32, (C, C), 0)
    ci = lax.broadcasted_iota(jnp.int32, (C, C), 1)

    def body(i, carry):
        for d, (k_s, b_s, o_s) in enumerate(((kf_s, bf_s, of_s), (kb_s, bb_s, ob_s))):
            c = i if d == 0 else n - 1 - i
            sl = pl.ds(pl.multiple_of(c * C, C), C)
            for h in range(HG_HEADS):
                q = q_s[sl, heads[h]]
                k = k_s[sl, heads[h]]
                b = b_s[sl, heads[h]]
                v = v_s[sl, heads[h]]
                bm = b[C // 2:C // 2 + 1, :]
                g = b[C - 1:C, :] if d == 0 else b[0:1, :]
                qi = q * jnp.exp(b - bm)
                qe = (qi * jnp.exp(bm)).astype(BF16)
                ki = k * jnp.exp(bm - b)
                kd = (ki * jnp.exp(g - bm)).astype(BF16)
                s = _dot_nt(qi.astype(BF16), ki.astype(BF16))
                p = jnp.where((ci <= ri) if d == 0 else (ci >= ri), s, 0.0).astype(BF16)
                st = st_s[d * HG_HEADS + h]
                o_s[sl, heads[h]] = _dot(p, v) + _dot_nt(qe, st.astype(BF16))
                st_s[d * HG_HEADS + h] = st * jnp.exp(g) + _dot_tn(v, kd)
        return carry

    lax.fori_loop(0, n, body, 0, unroll=4)
    if want_state:
        for d in range(2):
            for h in range(HG_HEADS):
                st_ref[d, h] = st_s[d * HG_HEADS + h].T

    for r in range(T // RB):
        rows = slice(r * RB, (r + 1) * RB)
        for h in range(HG_HEADS):
            o = of_s[rows, heads[h]] + ob_s[rows, heads[h]]
            ms = jnp.mean(o * o, axis=-1, keepdims=True)
            oh = 0.5 * og_ref[rows, heads[h]]
            y_ref[rows, heads[h]] = o * lax.rsqrt(ms + EPS) * gain_ref[...] * (oh + oh * jnp.tanh(oh))


def _hgrn(proj3, gamma, gain, s0, T, want_state):
    B = proj3.shape[0]
    has_s0 = s0 is not None
    col0 = 2 * D_LRU // D_HGRN

    def col(k):
        return pl.BlockSpec((None, T, D_HGRN), lambda b, k=k: (b, 0, col0 + k))

    st_spec = pl.BlockSpec((None, 2, HG_HEADS, HG_DK, HG_DK), lambda b: (b, 0, 0, 0, 0))
    in_specs = [col(0), col(1), col(2), col(3), col(4),
                pl.BlockSpec((2, 2, D_HGRN), lambda b: (0, 0, 0)),
                pl.BlockSpec((1, LANES), lambda b: (0, 0))]
    args = [proj3, proj3, proj3, proj3, proj3, gamma, gain]
    if has_s0:
        in_specs.append(st_spec)
        args.append(s0)
    out_specs = [pl.BlockSpec((None, T, D_HGRN), lambda b: (b, 0, 0))]
    out_shape = [jax.ShapeDtypeStruct((B, T, D_HGRN), F32)]
    if want_state:
        out_specs.append(st_spec)
        out_shape.append(jax.ShapeDtypeStruct((B, 2, HG_HEADS, HG_DK, HG_DK), F32))
    res = pl.pallas_call(
        functools.partial(_hgrn_kernel, T, has_s0, want_state),
        grid=(B,),
        in_specs=in_specs,
        out_specs=out_specs,
        out_shape=out_shape,
        scratch_shapes=[pltpu.VMEM((T, D_HGRN), F32), pltpu.VMEM((T, D_HGRN), BF16)]
        + [pltpu.VMEM((T, D_HGRN), F32)] * 6
        + [pltpu.VMEM((2 * HG_HEADS, HG_DK, HG_DK), F32)],
        compiler_params=_cp(("arbitrary",)),
    )(*args)
    return res if want_state else (res[0], None)


def _mix_out_kernel(hs_ref, gr_ref, yh_ref, x_ref, mod_ref, nl_ref, wo_ref, nf_ref, wr_ref,
                    x1_ref, ha_ref, at_ref):
    wr = wr_ref[...]
    wo = wo_ref[...].astype(BF16)
    g1 = mod_ref[0, 2:3, :]
    sh2 = mod_ref[0, 3:4, :]
    gain2 = nf_ref[...] * (1.0 + mod_ref[0, 4:5, :])
    lane = lax.broadcasted_iota(jnp.int32, (TOK_TILE, LANES), 1)
    for s in range(MIX_TILE // TOK_TILE):
        rows = slice(s * TOK_TILE, (s + 1) * TOK_TILE)
        hs = hs_ref[rows, :]
        ms = jnp.mean(hs * hs, axis=-1, keepdims=True)
        gr = gr_ref[rows, :]
        gelu = 0.5 * gr * (1.0 + jnp.tanh(0.7978845608028654 * (gr + 0.044715 * (gr * gr * gr))))
        y_lru = hs * lax.rsqrt(ms + EPS) * nl_ref[...] * gelu
        ycat = jnp.concatenate([y_lru.astype(BF16), yh_ref[rows, :].astype(BF16)], axis=-1)
        x1 = x_ref[rows, :] + g1 * _dot(ycat, wo)
        x1_ref[rows, :] = x1
        ms = jnp.mean(x1 * x1, axis=-1, keepdims=True)
        h2 = x1 * lax.rsqrt(ms + EPS) * gain2 + sh2
        h2_hi = h2.astype(BF16)
        h2_mid = (h2 - h2_hi.astype(F32)).astype(BF16)
        p_hi = _dot(h2_hi, wr)
        logits = p_hi[:, 0:LANES] + (p_hi[:, LANES:2 * LANES] + _dot(h2_mid, wr)[:, 0:LANES])
        logits = jnp.where(lane < N_EXPERTS, logits, -jnp.inf)
        e = jnp.exp(logits - jnp.max(logits, axis=-1, keepdims=True))
        aff = e / jnp.sum(e, axis=-1, keepdims=True)
        w = at_ref.shape[1]
        r, c = (s * TOK_TILE) // w, (s * TOK_TILE) % w
        at_ref[r * N_EXPERTS:(r + 1) * N_EXPERTS, c:c + TOK_TILE] = aff.T[0:N_EXPERTS, :]
        hi = aff.astype(BF16).astype(F32)
        r1 = aff - hi
        mid = r1.astype(BF16).astype(F32)
        lo = (r1 - mid).astype(BF16).astype(F32)
        split = hi + pltpu.roll(mid, N_EXPERTS, 1) + pltpu.roll(lo, 2 * N_EXPERTS, 1)
        ha_ref[rows, 0:D_MODEL] = h2_hi
        ha_ref[rows, D_MODEL:D_AUG] = split.astype(BF16)


def _mix_out(hsum, proj, yh, x, mods, norm_lru, wo_bf, norm_ffn, wr_pad, T):
    n = x.shape[0]
    tiles = n // MIX_TILE
    assert T % TOK_TILE == 0 and (MIX_TILE % T == 0 or T % MIX_TILE == 0)
    q = max(T // MIX_TILE, 1)
    at_rows = N_EXPERTS * max(MIX_TILE // T, 1)
    at_spec = pl.BlockSpec((at_rows, min(T, MIX_TILE)), lambda i: (i // q, i % q))
    mod_map = (lambda i: (i // q, 0, 0)) if mods.shape[0] > 1 else (lambda i: (0, 0, 0))
    tile = lambda w: pl.BlockSpec((MIX_TILE, w), lambda i: (i, 0))
    const = lambda r, w: pl.BlockSpec((r, w), lambda i: (0, 0))
    return pl.pallas_call(
        _mix_out_kernel,
        grid=(tiles,),
        in_specs=[tile(D_LRU),
                  pl.BlockSpec((MIX_TILE, D_LRU), lambda i: (i, 1)),
                  tile(D_HGRN), tile(D_MODEL),
                  pl.BlockSpec((1, N_MOD, D_MODEL), mod_map),
                  const(1, D_LRU), const(D_MODEL, D_MODEL), const(1, D_MODEL), const(D_MODEL, 2 * LANES)],
        out_specs=[tile(D_MODEL), tile(D_AUG), at_spec],
        out_shape=[jax.ShapeDtypeStruct((n, D_MODEL), F32),
                   jax.ShapeDtypeStruct((n, D_AUG), BF16),
                   jax.ShapeDtypeStruct((n // T * N_EXPERTS, T), F32)],
        compiler_params=_cp(("arbitrary",)),
    )(hsum, proj, yh, x, mods, norm_lru, wo_bf, norm_ffn, wr_pad)


def _select_kernel(cap, at_ref, slot_ref):
    a = at_ref[...]
    R, T = a.shape
    ones = jnp.ones((T, LANES), BF16)
    su = (lax.broadcasted_iota(jnp.int32, (T, T), 0)
          < lax.broadcasted_iota(jnp.int32, (T, T), 1)).astype(BF16)

    def wide(x):
        return jnp.concatenate([x] * (T // LANES), axis=1)

    def count(mask):
        return _dot(mask.astype(BF16), ones)

    lo = jnp.zeros((R, LANES), F32)
    hi = jnp.full((R, LANES), 2.0, F32)
    for _ in range(SEL_BISECT):
        mid = 0.5 * (lo + hi)
        ge = count(a >= wide(mid)) >= cap
        lo = jnp.where(ge, mid, lo)
        hi = jnp.where(ge, hi, mid)

    def cond(st):
        return (st[0] < T) & (jnp.min(st[2]) < cap)

    def body(st):
        it, cur, n, thr = st
        m = jnp.max(jnp.where(a < wide(cur), a, -1.0), axis=1, keepdims=True)
        m = jnp.broadcast_to(m, (R, LANES))
        c = count(a >= wide(m))
        act = n < cap
        return it + 1, jnp.where(act, m, cur), jnp.where(act, c, n), jnp.where(act, m, thr)

    _, _, _, thr = lax.while_loop(cond, body, (jnp.int32(0), hi, count(a >= wide(hi)), lo))
    thr_w = wide(thr)
    gt = a > thr_w
    eq = a == thr_w
    need = wide(cap - count(gt))
    sel = gt | (eq & (_dot(eq.astype(BF16), su) < need))
    slot_ref[...] = jnp.where(sel, _dot(sel.astype(BF16), su), -1.0).astype(jnp.int32)


def _select(at, T):
    cap = EC_CAPACITY_FACTOR * T // N_EXPERTS
    R = at.shape[0]
    return pl.pallas_call(
        functools.partial(_select_kernel, cap),
        grid=(1,),
        in_specs=[pl.BlockSpec((R, T), lambda i: (0, 0))],
        out_specs=pl.BlockSpec((R, T), lambda i: (0, 0)),
        out_shape=jax.ShapeDtypeStruct((R, T), jnp.int32),
        compiler_params=_cp(("arbitrary",)),
    )(at)


def _onehot(cap, slot_ref):
    j = lax.broadcasted_iota(jnp.int32, (cap, slot_ref.shape[1]), 0)
    return jnp.concatenate([(slot_ref[e:e + 1, :] == j).astype(BF16) for e in range(N_EXPERTS)], axis=0)


def _dispatch_kernel(cap, slot_ref, ha_ref, xd_ref):
    rows = _dot(_onehot(cap, slot_ref), ha_ref[...])
    for e in range(N_EXPERTS):
        xd_ref[e] = rows[e * cap:(e + 1) * cap].astype(BF16)


def _dispatch(slot, ha, T):
    cap = EC_CAPACITY_FACTOR * T // N_EXPERTS
    B = ha.shape[0] // T
    return pl.pallas_call(
        functools.partial(_dispatch_kernel, cap),
        grid=(B,),
        in_specs=[pl.BlockSpec((N_EXPERTS, T), lambda b: (b, 0)),
                  pl.BlockSpec((T, D_AUG), lambda b: (b, 0))],
        out_specs=pl.BlockSpec((N_EXPERTS, cap, D_AUG), lambda b: (0, b, 0)),
        out_shape=jax.ShapeDtypeStruct((N_EXPERTS, B * cap, D_AUG), BF16),
        compiler_params=_cp(("arbitrary",)),
    )(slot, ha)


def _ffn_kernel(xp_ref, xs_ref, wg_ref, wu_ref, wd_ref, yp_ref, ys_ref):
    e = pl.program_id(0)
    lane = lax.broadcasted_iota(jnp.int32, (FFN_TILE, LANES), 1)
    pick = (lane == e) | (lane == e + N_EXPERTS) | (lane == e + 2 * N_EXPERTS)
    wg = wg_ref[...].astype(BF16)
    wu = wu_ref[...].astype(BF16)
    wd = wd_ref[...].astype(BF16)
    for x_ref, y_ref in ((xp_ref, yp_ref), (xs_ref, ys_ref)):
        x = x_ref[:, 0:D_MODEL]
        gate = jnp.sum(jnp.where(pick, x_ref[:, D_MODEL:D_AUG].astype(F32), 0.0), axis=-1, keepdims=True)
        hg = _dot(x, wg)
        hid = (hg * _sigmoid(hg)) * _dot(x, wu)
        y_ref[...] = (_dot(hid.astype(BF16), wd) * gate).astype(BF16)


def _ffn(xd_p, xd_s, w_gate, w_up, w_down):
    rows = xd_p.shape[1]
    assert xd_s.shape[1] == rows and rows % FFN_TILE == 0
    xspec = pl.BlockSpec((None, FFN_TILE, D_AUG), lambda e, m: (e, m, 0))
    wspec = pl.BlockSpec((None, D_MODEL, D_MODEL), lambda e, m: (e, 0, 0))
    yspec = pl.BlockSpec((None, FFN_TILE, D_MODEL), lambda e, m: (e, m, 0))
    yshape = jax.ShapeDtypeStruct((N_EXPERTS, rows, D_MODEL), BF16)
    return pl.pallas_call(
        _ffn_kernel,
        grid=(N_EXPERTS, rows // FFN_TILE),
        in_specs=[xspec, xspec, wspec, wspec, wspec],
        out_specs=[yspec, yspec],
        out_shape=[yshape, yshape],
        compiler_params=_cp(("arbitrary", "arbitrary")),
    )(xd_p, xd_s, w_gate, w_up, w_down)


def _combine_kernel(cap, slot_ref, yd_ref, x1_ref, mod_ref, nf_ref, o_ref):
    yd = jnp.concatenate([yd_ref[e] for e in range(N_EXPERTS)], axis=0)
    acc = _dot_tn(_onehot(cap, slot_ref), yd)
    x2 = x1_ref[...] + mod_ref[0, 5:6, :] * acc
    ms = jnp.mean(x2 * x2, axis=-1, keepdims=True)
    o_ref[...] = x2 * lax.rsqrt(ms + EPS) * nf_ref[...]


def _combine(slot, yd, x1, mods, norm_final, T):
    cap = EC_CAPACITY_FACTOR * T // N_EXPERTS
    n = x1.shape[0]
    mod_map = (lambda b: (b, 0, 0)) if mods.shape[0] > 1 else (lambda b: (0, 0, 0))
    return pl.pallas_call(
        functools.partial(_combine_kernel, cap),
        grid=(n // T,),
        in_specs=[pl.BlockSpec((N_EXPERTS, T), lambda b: (b, 0)),
                  pl.BlockSpec((N_EXPERTS, cap, D_MODEL), lambda b: (0, b, 0)),
                  pl.BlockSpec((T, D_MODEL), lambda b: (b, 0)),
                  pl.BlockSpec((1, N_MOD, D_MODEL), mod_map),
                  pl.BlockSpec((1, D_MODEL), lambda b: (0, 0))],
        out_specs=pl.BlockSpec((T, D_MODEL), lambda b: (b, 0)),
        out_shape=jax.ShapeDtypeStruct((n, D_MODEL), F32),
        compiler_params=_cp(("arbitrary",)),
    )(slot, yd, x1, mods, norm_final)


def _gate_tiles(wa, wx, ba, bx):
    n_ct = D_LRU // LANES
    eye = jnp.eye(LRU_BLOCKS, dtype=wa.dtype)

    def dense(w):
        full = jnp.einsum('nde,nm->ndme', w, eye).reshape(D_LRU, D_LRU)
        t = full.reshape(n_ct, LANES, n_ct, LANES)
        return jnp.stack([t[c, :, c, :] for c in range(n_ct)])

    tiles = 0.5 * jnp.concatenate([dense(wa[0]), dense(wx[0]), dense(wa[1]), dense(wx[1])], axis=-1)
    t = lambda v: v.reshape(n_ct, 1, LANES)
    bias = 0.5 * jnp.concatenate([t(ba[0]), t(bx[0]), t(ba[1]), t(bx[1])], axis=-1)
    hi = bias.astype(BF16)
    lo = (bias - hi.astype(F32)).astype(BF16)
    pad = jnp.zeros((n_ct, LANES - 2, 4 * LANES), BF16)
    return jnp.concatenate([tiles.astype(BF16), hi, lo, pad], axis=1)


def _router_split(w):
    wp = jnp.pad(w, ((0, 0), (0, LANES - N_EXPERTS)))
    hi = wp.astype(BF16)
    mid = (wp - hi.astype(F32)).astype(BF16)
    return jnp.concatenate([hi, mid], axis=1)


def _mixer_and_router(x, mods, p, T, row_len, h0f, h0b, s0, want_state):
    B = x.shape[0]
    xf = x.reshape(B * T, D_MODEL)
    proj = _in_proj(xf, mods, p["norm_mix"], p["w_in"], T)
    proj3 = proj.reshape(B, T, D_IN)
    hsum, last_f, last_b = _lru(proj3, p["conv_w"], p["conv_b"], p["gate_w"], p["lam"], h0f, h0b, T, row_len)
    yh, st = _hgrn(proj3, p["gamma"], p["norm_hgrn"], s0, T, want_state)
    x1, ha, at = _mix_out(hsum.reshape(B * T, D_LRU), proj, yh.reshape(B * T, D_HGRN), xf, mods,
                          p["norm_lru"], p["w_out"], p["norm_ffn"], p["w_router"], T)
    slot = _select(at, T)
    xd = _dispatch(slot, ha, T)
    return x1, slot, xd, last_f, last_b, st


def kernel(x_prompt, x_sample, state_lru, state_hgrn, c, c_ctx, w_ada, b_ada, norm_mix, w_in, conv_w,
           conv_b, lru_wa, lru_ba, lru_wx, lru_bx, lru_lambda, norm_lru, hgrn_gamma, norm_hgrn, w_out,
           norm_ffn, w_router, w_gate, w_up, w_down, norm_final):
    assert w_ada.shape[0] == 1 and hgrn_gamma.shape[1] == 2, "one trunk layer"
    l = 0
    Bp, Tp, _ = x_prompt.shape
    Bs, Ts, _ = x_sample.shape

    cond = jnp.concatenate([c_ctx[None, :], c, jnp.zeros((16 - 1 - Bs, D_MODEL), F32)], axis=0)
    mods = _adaln(cond, w_ada[l], b_ada[l][None, :]).reshape(16, N_MOD, D_MODEL)
    mods_p, mods_s = mods[0:1], mods[1:1 + Bs]

    p = {
        "norm_mix": norm_mix[l][None, :], "w_in": w_in[l],
        "conv_w": conv_w[l], "conv_b": conv_b[l][None, :],
        "gate_w": _gate_tiles(lru_wa[l], lru_wx[l], lru_ba[l], lru_bx[l]),
        "lam": lru_lambda[l], "norm_lru": norm_lru[l][None, :],
        "gamma": hgrn_gamma, "norm_hgrn": norm_hgrn[l][None, :],
        "w_out": w_out[l], "norm_ffn": norm_ffn[l][None, :],
        "w_router": _router_split(w_router[l]),
    }
    zeros_p = jnp.zeros((Bp, D_LRU), F32)
    x1p, slot_p, xd_p, last_f, last_b, st_p = _mixer_and_router(
        x_prompt, mods_p, p, Tp, Tp, zeros_p, zeros_p, None, True)
    x1s, slot_s, xd_s, _, _, _ = _mixer_and_router(
        x_sample, mods_s, p, Ts, GRID_W, state_lru[:, l, 0], state_lru[:, l, 1], state_hgrn[:, l], False)

    yd_p, yd_s = _ffn(xd_p, xd_s, w_gate[l], w_up[l], w_down[l])
    nf = norm_final[None, :]
    y_prompt = _combine(slot_p, yd_p, x1p, mods_p, nf, Tp).reshape(Bp, Tp, D_MODEL)
    y_sample = _combine(slot_s, yd_s, x1s, mods_s, nf, Ts).reshape(Bs, Ts, D_MODEL)
    new_state_lru = jnp.stack([last_f, last_b], axis=1)[:, None]
    new_state_hgrn = st_p[:, None]
    return (y_prompt, y_sample, new_state_lru, new_state_hgrn)
```

```python
import functools

import jax
import jax.numpy as jnp
from jax import lax
from jax.experimental import pallas as pl
from jax.experimental.pallas import tpu as pltpu

F32 = jnp.float32
BF16 = jnp.bfloat16

D_MODEL = 1024
D_LRU = 512
D_HGRN = 512
HG_HEADS = 4
HG_DK = 128
LRU_BLOCKS = 8
LRU_BW = 64
LRU_C = 8.0
N_EXPERTS = 16
EC_CAPACITY_FACTOR = 2
N_MOD = 6
D_IN = 7 * 512
N_HG_IN = 5
GRID_W = 64
EPS = 1e-6
LOG2E = 1.4426950408889634
TINY = 1e-37

LANES = 128
TOK_TILE = 256
PROJ_TILE = 512
MIX_TILE = 1024
LRU_GROUP = 8
PITCH_PAD = 8
HG_CHUNK = 64
HG_ROWS = 256
FFN_TILE = 512
ROUTE_TOKENS = 1024
SEL_BISECT = 20
D_AUG = D_MODEL + LANES
VMEM_LIMIT = 56 * 1024 * 1024


def _cp(sem, vmem=VMEM_LIMIT):
    return pltpu.CompilerParams(dimension_semantics=sem, vmem_limit_bytes=vmem)


def _sigmoid(x):
    return 0.5 * (1.0 + jnp.tanh(0.5 * x))


def _dot(a, b):
    return jnp.dot(a, b, preferred_element_type=F32)


def _dot_nt(a, b):
    return lax.dot_general(a, b, (((1,), (1,)), ((), ())), preferred_element_type=F32)


def _dot_tn(a, b):
    return lax.dot_general(a, b, (((0,), (0,)), ((), ())), preferred_element_type=F32)


def _adaln_kernel(c_ref, w_ref, b_ref, o_ref):
    c = c_ref[...]
    s = (c * _sigmoid(c)).astype(BF16)
    o_ref[...] = _dot(s, w_ref[...].astype(BF16)) + b_ref[...]


def _adaln(cond, w, b):
    n = cond.shape[0]
    tn = 1024
    return pl.pallas_call(
        _adaln_kernel,
        grid=(w.shape[1] // tn,),
        in_specs=[pl.BlockSpec((n, D_MODEL), lambda j: (0, 0)),
                  pl.BlockSpec((D_MODEL, tn), lambda j: (0, j)),
                  pl.BlockSpec((1, tn), lambda j: (0, j))],
        out_specs=pl.BlockSpec((n, tn), lambda j: (0, j)),
        out_shape=jax.ShapeDtypeStruct((n, w.shape[1]), F32),
        compiler_params=_cp(("arbitrary",)),
    )(cond, w, b)


def _proj_kernel(x_ref, mod_ref, g_ref, w_ref, xr_ref, gr_ref, hg_ref):
    x = x_ref[...]
    ms = jnp.mean(x * x, axis=-1, keepdims=True)
    y = x * lax.rsqrt(ms + EPS) * g_ref[...]
    h = y * (1.0 + mod_ref[0, 1:2, :]) + mod_ref[0, 0:1, :]
    res = _dot(h.astype(BF16), w_ref[...].astype(BF16))
    for c in range(D_LRU // LANES):
        xr_ref[c] = res[:, c * LANES:(c + 1) * LANES]
    gr_ref[...] = res[:, D_LRU:2 * D_LRU]
    for k in range(N_HG_IN):
        hg_ref[k] = res[:, 2 * D_LRU + k * D_HGRN:2 * D_LRU + (k + 1) * D_HGRN]


def _in_proj(x, mods, gain, w, T):
    n = x.shape[0]
    per_req = T // PROJ_TILE if T >= PROJ_TILE else None
    if per_req is None:
        reqs = PROJ_TILE // T
        mod_map = lambda i: (0, 0, 0)
        assert mods.shape[0] == 1 and reqs >= 1
    else:
        mod_map = lambda i: (i // per_req, 0, 0)
    n_ct = D_LRU // LANES
    return pl.pallas_call(
        _proj_kernel,
        grid=(n // PROJ_TILE,),
        in_specs=[pl.BlockSpec((PROJ_TILE, D_MODEL), lambda i: (i, 0)),
                  pl.BlockSpec((1, N_MOD, D_MODEL), mod_map),
                  pl.BlockSpec((1, D_MODEL), lambda i: (0, 0)),
                  pl.BlockSpec((D_MODEL, D_IN), lambda i: (0, 0), pipeline_mode=pl.Buffered(1))],
        out_specs=[pl.BlockSpec((n_ct, PROJ_TILE, LANES), lambda i: (0, i, 0)),
                   pl.BlockSpec((PROJ_TILE, D_LRU), lambda i: (i, 0)),
                   pl.BlockSpec((N_HG_IN, PROJ_TILE, D_HGRN), lambda i: (0, i, 0))],
        out_shape=[jax.ShapeDtypeStruct((n_ct, n, LANES), F32),
                   jax.ShapeDtypeStruct((n, D_LRU), F32),
                   jax.ShapeDtypeStruct((N_HG_IN, n, D_HGRN), F32)],
        compiler_params=_cp(("arbitrary",)),
    )(x, mods, gain, w)


def _lru_kernel(T, row_len, xr_ref, cw_ref, cb_ref, w_ref, lam_ref, h0f_ref, h0b_ref,
                hs_ref, lf_ref, lb_ref, xpad, af, uf, ab, ub, hf_s, hb_s):
    pitch = T + PITCH_PAD
    pos = lax.broadcasted_iota(jnp.int32, (T, LANES), 0) & (row_len - 1)
    cw = cw_ref[...]
    w0 = jnp.where(pos >= 2, cw[0:1], 0.0)
    w1 = jnp.where(pos >= 1, cw[1:2], 0.0)
    w2 = jnp.broadcast_to(cw[2:3], (T, LANES))
    w3 = jnp.where(pos <= row_len - 2, cw[3:4], 0.0)
    cb = cb_ref[...]
    nl = -lam_ref[...]
    sp = jnp.maximum(nl, 0.0) + jnp.log1p(jnp.exp(-jnp.abs(nl)))
    c2 = (-0.5 * LRU_C * LOG2E) * sp
    w = w_ref[0]
    ones = (lax.broadcasted_iota(jnp.int32, (T, LANES), 1) < 2).astype(BF16)
    for s in range(2):
        xpad[s, 0:8, :] = jnp.zeros((8, LANES), F32)
        xpad[s, T + 8:T + 16, :] = jnp.zeros((8, LANES), F32)
    for b in range(LRU_GROUP):
        x = xr_ref[b]
        xp = xpad.at[b % 2]
        xp[8:T + 8, :] = x
        xc = w0 * xp[6:T + 6, :] + w1 * xp[7:T + 7, :] + w2 * x + w3 * xp[9:T + 9, :] + cb
        xh = 0.5 * xc
        z = _dot(jnp.concatenate([xc.astype(BF16), ones], axis=1), w)
        rows = pl.ds(b * pitch, T)
        for d, (a_s, u_s) in enumerate(((af, uf), (ab, ub))):
            tr = jnp.tanh(z[:, (2 * d) * LANES:(2 * d + 1) * LANES])
            ti = jnp.tanh(z[:, (2 * d + 1) * LANES:(2 * d + 2) * LANES])
            a = jnp.exp2(c2[d:d + 1] + c2[d:d + 1] * tr)
            om = 1.0 - a * a
            a_s[rows, :] = a
            u_s[rows, :] = (om * lax.rsqrt(jnp.maximum(om, TINY))) * (xh + ti * xh)

    def step(t, carry):
        hf, hb = carry
        rf = pl.ds(t, LRU_GROUP, stride=pitch)
        hf = af[rf, :] * hf + uf[rf, :]
        hf_s[rf, :] = hf
        rb = pl.ds(T - 1 - t, LRU_GROUP, stride=pitch)
        hb = ab[rb, :] * hb + ub[rb, :]
        hb_s[rb, :] = hb
        return hf, hb

    hf, hb = lax.fori_loop(0, T, step, (h0f_ref[...], h0b_ref[...]), unroll=8)
    lf_ref[...] = hf
    lb_ref[...] = hb
    for b in range(LRU_GROUP):
        rows = pl.ds(b * pitch, T)
        hs_ref[b] = hf_s[rows, :] + hb_s[rows, :]


def _lru(xr4, conv_w, conv_b, w_tiles, lam, h0f, h0b, T, row_len):
    n_ct, B = xr4.shape[:2]
    rows = LRU_GROUP * (T + PITCH_PAD)
    vec = pl.BlockSpec((LRU_GROUP, LANES), lambda g, c: (g, c))
    slab = pl.BlockSpec((None, LRU_GROUP, T, LANES), lambda g, c: (c, g, 0, 0))
    return pl.pallas_call(
        functools.partial(_lru_kernel, T, row_len),
        grid=(B // LRU_GROUP, n_ct),
        in_specs=[slab,
                  pl.BlockSpec((4, LANES), lambda g, c: (0, c)),
                  pl.BlockSpec((1, LANES), lambda g, c: (0, c)),
                  pl.BlockSpec((1, 2 * LANES, 4 * LANES), lambda g, c: (c, 0, 0)),
                  pl.BlockSpec((2, LANES), lambda g, c: (0, c)),
                  vec, vec],
        out_specs=[slab, vec, vec],
        out_shape=[jax.ShapeDtypeStruct((n_ct, B, T, LANES), F32),
                   jax.ShapeDtypeStruct((B, D_LRU), F32),
                   jax.ShapeDtypeStruct((B, D_LRU), F32)],
        scratch_shapes=[pltpu.VMEM((2, T + 16, LANES), F32)] + [pltpu.VMEM((rows, LANES), F32)] * 6,
        compiler_params=_cp(("arbitrary", "arbitrary")),
    )(xr4, conv_w, conv_b, w_tiles, lam, h0f, h0b)


def _hgrn_kernel(T, has_s0, want_state, *refs):
    q_ref, ff_ref, fb_ref, v_ref, og_ref, gam_ref, gain_ref = refs[:7]
    refs = refs[7:]
    if has_s0:
        s0_ref, refs = refs[0], refs[1:]
    y_ref, refs = refs[0], refs[1:]
    if want_state:
        st_ref, refs = refs[0], refs[1:]
    q_s, v_s, kf_s, kb_s, bf_s, bb_s, of_s, ob_s, st_s = refs
    C = HG_CHUNK
    n = T // C
    RB = HG_ROWS
    heads = [slice(h * LANES, (h + 1) * LANES) for h in range(HG_HEADS)]

    ti = lax.broadcasted_iota(jnp.int32, (RB, RB), 0)
    si = lax.broadcasted_iota(jnp.int32, (RB, RB), 1)
    sh = C.bit_length() - 1
    same = (ti >> sh) == (si >> sh)
    tri = ((same & (si <= ti)).astype(BF16), (same & (si >= ti)).astype(BF16))

    for r in range(T // RB):
        rows = slice(r * RB, (r + 1) * RB)
        qr = q_ref[rows, :]
        qh = (0.5 * HG_DK ** -0.5) * qr
        q_s[rows, :] = qh + qh * jnp.tanh(0.5 * qr)
        v_s[rows, :] = v_ref[rows, :].astype(BF16)
        for d, (f_ref, k_s, b_s) in enumerate(((ff_ref, kf_s, bf_s), (fb_ref, kb_s, bb_s))):
            g0 = gam_ref[d, 0:1, :]
            g1 = gam_ref[d, 1:2, :]
            m = jnp.maximum(g0, g1)
            e0 = jnp.exp(g0 - m)
            lbd = e0 / (e0 + jnp.exp(g1 - m))
            ck = 0.5 * (1.0 - lbd)
            pt = ck * jnp.tanh(0.5 * f_ref[rows, :])
            k_s[rows, :] = ck - pt
            lf = jnp.log((lbd + ck) + pt)
            hi = lf.astype(BF16)
            r1 = lf - hi.astype(F32)
            mid = r1.astype(BF16)
            lo = (r1 - mid.astype(F32)).astype(BF16)
            cs = _dot(tri[d], jnp.concatenate([hi, mid, lo], axis=1))
            b_s[rows, :] = cs[:, 0:D_HGRN] + (cs[:, D_HGRN:2 * D_HGRN] + cs[:, 2 * D_HGRN:3 * D_HGRN])

    for d in range(2):
        for h in range(HG_HEADS):
            st_s[d * HG_HEADS + h] = s0_ref[d, h].T if has_s0 else jnp.zeros((LANES, LANES), F32)

    ri = lax.broadcasted_iota(jnp.int32, (C, C), 0)
    ci = lax.broadcasted_iota(jnp.int32, (C, C), 1)

    def body(i, carry):
        for d, (k_s, b_s, o_s) in enumerate(((kf_s, bf_s, of_s), (kb_s, bb_s, ob_s))):
            c = i if d == 0 else n - 1 - i
            sl = pl.ds(pl.multiple_of(c * C, C), C)
            for h in range(HG_HEADS):
                q = q_s[sl, heads[h]]
                k = k_s[sl, heads[h]]
                b = b_s[sl, heads[h]]
                v = v_s[sl, heads[h]]
                bm = b[C // 2:C // 2 + 1, :]
                g = b[C - 1:C, :] if d == 0 else b[0:1, :]
                qi = q * jnp.exp(b - bm)
                qe = (qi * jnp.exp(bm)).astype(BF16)
                ki = k * jnp.exp(bm - b)
                kd = (ki * jnp.exp(g - bm)).astype(BF16)
                s = _dot_nt(qi.astype(BF16), ki.astype(BF16))
                p = jnp.where((ci <= ri) if d == 0 else (ci >= ri), s, 0.0).astype(BF16)
                st = st_s[d * HG_HEADS + h]
                o_s[sl, heads[h]] = _dot(p, v) + _dot_nt(qe, st.astype(BF16))
                st_s[d * HG_HEADS + h] = st * jnp.exp(g) + _dot_tn(v, kd)
        return carry

    lax.fori_loop(0, n, body, 0, unroll=4)
    if want_state:
        for d in range(2):
            for h in range(HG_HEADS):
                st_ref[d, h] = st_s[d * HG_HEADS + h].T

    for r in range(T // RB):
        rows = slice(r * RB, (r + 1) * RB)
        for h in range(HG_HEADS):
            o = of_s[rows, heads[h]] + ob_s[rows, heads[h]]
            ms = jnp.mean(o * o, axis=-1, keepdims=True)
            oh = 0.5 * og_ref[rows, heads[h]]
            y = o * lax.rsqrt(ms + EPS) * gain_ref[...] * (oh + oh * jnp.tanh(oh))
            y_ref[rows, heads[h]] = y.astype(BF16)


def _hgrn(hg5, gamma, gain, s0, T, want_state):
    B = hg5.shape[1]
    has_s0 = s0 is not None

    def col(k):
        return pl.BlockSpec((None, None, T, D_HGRN), lambda b, k=k: (k, b, 0, 0))

    st_spec = pl.BlockSpec((None, 2, HG_HEADS, HG_DK, HG_DK), lambda b: (b, 0, 0, 0, 0))
    in_specs = [col(0), col(1), col(2), col(3), col(4),
                pl.BlockSpec((2, 2, D_HGRN), lambda b: (0, 0, 0)),
                pl.BlockSpec((1, LANES), lambda b: (0, 0))]
    args = [hg5, hg5, hg5, hg5, hg5, gamma, gain]
    if has_s0:
        in_specs.append(st_spec)
        args.append(s0)
    out_specs = [pl.BlockSpec((None, T, D_HGRN), lambda b: (b, 0, 0))]
    out_shape = [jax.ShapeDtypeStruct((B, T, D_HGRN), BF16)]
    if want_state:
        out_specs.append(st_spec)
        out_shape.append(jax.ShapeDtypeStruct((B, 2, HG_HEADS, HG_DK, HG_DK), F32))
    res = pl.pallas_call(
        functools.partial(_hgrn_kernel, T, has_s0, want_state),
        grid=(B,),
        in_specs=in_specs,
        out_specs=out_specs,
        out_shape=out_shape,
        scratch_shapes=[pltpu.VMEM((T, D_HGRN), F32), pltpu.VMEM((T, D_HGRN), BF16)]
        + [pltpu.VMEM((T, D_HGRN), F32)] * 6
        + [pltpu.VMEM((2 * HG_HEADS, HG_DK, HG_DK), F32)],
        compiler_params=_cp(("arbitrary",)),
    )(*args)
    return res if want_state else (res[0], None)


def _mix_out_kernel(hs_ref, gr_ref, yh_ref, x_ref, mod_ref, nl_ref, wo_ref, nf_ref, wr_ref,
                    x1_ref, ha_ref, at_ref):
    wr = wr_ref[...]
    wo = wo_ref[...].astype(BF16)
    g1 = mod_ref[0, 2:3, :]
    sh2 = mod_ref[0, 3:4, :]
    gain2 = nf_ref[...] * (1.0 + mod_ref[0, 4:5, :])
    lane = lax.broadcasted_iota(jnp.int32, (TOK_TILE, LANES), 1)
    for s in range(MIX_TILE // TOK_TILE):
        rows = slice(s * TOK_TILE, (s + 1) * TOK_TILE)
        hs = jnp.concatenate([hs_ref[c, rows, :] for c in range(D_LRU // LANES)], axis=1)
        ms = jnp.mean(hs * hs, axis=-1, keepdims=True)
        gr = gr_ref[rows, :]
        gelu = 0.5 * gr * (1.0 + jnp.tanh(0.7978845608028654 * (gr + 0.044715 * (gr * gr * gr))))
        y_lru = hs * lax.rsqrt(ms + EPS) * nl_ref[...] * gelu
        ycat = jnp.concatenate([y_lru.astype(BF16), yh_ref[rows, :]], axis=-1)
        x1 = x_ref[rows, :] + g1 * _dot(ycat, wo)
        x1_ref[rows, :] = x1
        ms = jnp.mean(x1 * x1, axis=-1, keepdims=True)
        h2 = x1 * lax.rsqrt(ms + EPS) * gain2 + sh2
        h2_hi = h2.astype(BF16)
        h2_mid = (h2 - h2_hi.astype(F32)).astype(BF16)
        p_hi = _dot(h2_hi, wr)
        logits = p_hi[:, 0:LANES] + (p_hi[:, LANES:2 * LANES] + _dot(h2_mid, wr)[:, 0:LANES])
        logits = jnp.where(lane < N_EXPERTS, logits, -jnp.inf)
        e = jnp.exp(logits - jnp.max(logits, axis=-1, keepdims=True))
        aff = e / jnp.sum(e, axis=-1, keepdims=True)
        w = at_ref.shape[1]
        r, c = (s * TOK_TILE) // w, (s * TOK_TILE) % w
        at_ref[r * N_EXPERTS:(r + 1) * N_EXPERTS, c:c + TOK_TILE] = aff.T[0:N_EXPERTS, :]
        hi = aff.astype(BF16).astype(F32)
        r1 = aff - hi
        mid = r1.astype(BF16).astype(F32)
        lo = (r1 - mid).astype(BF16).astype(F32)
        split = hi + pltpu.roll(mid, N_EXPERTS, 1) + pltpu.roll(lo, 2 * N_EXPERTS, 1)
        ha_ref[rows, 0:D_MODEL] = h2_hi
        ha_ref[rows, D_MODEL:D_AUG] = split.astype(BF16)


def _mix_out(hs4, gr, yh, x, mods, norm_lru, w_out, norm_ffn, wr_pad, T):
    n = x.shape[0]
    tiles = n // MIX_TILE
    assert T % TOK_TILE == 0 and (MIX_TILE % T == 0 or T % MIX_TILE == 0)
    q = max(T // MIX_TILE, 1)
    at_rows = N_EXPERTS * max(MIX_TILE // T, 1)
    at_spec = pl.BlockSpec((at_rows, min(T, MIX_TILE)), lambda i: (i // q, i % q))
    mod_map = (lambda i: (i // q, 0, 0)) if mods.shape[0] > 1 else (lambda i: (0, 0, 0))
    tile = lambda w: pl.BlockSpec((MIX_TILE, w), lambda i: (i, 0))
    const = lambda r, w: pl.BlockSpec((r, w), lambda i: (0, 0))
    return pl.pallas_call(
        _mix_out_kernel,
        grid=(tiles,),
        in_specs=[pl.BlockSpec((D_LRU // LANES, MIX_TILE, LANES), lambda i: (0, i, 0)),
                  tile(D_LRU), tile(D_HGRN), tile(D_MODEL),
                  pl.BlockSpec((1, N_MOD, D_MODEL), mod_map),
                  const(1, D_LRU), const(D_MODEL, D_MODEL), const(1, D_MODEL), const(D_MODEL, 2 * LANES)],
        out_specs=[tile(D_MODEL), tile(D_AUG), at_spec],
        out_shape=[jax.ShapeDtypeStruct((n, D_MODEL), F32),
                   jax.ShapeDtypeStruct((n, D_AUG), BF16),
                   jax.ShapeDtypeStruct((n // T * N_EXPERTS, T), F32)],
        compiler_params=_cp(("arbitrary",)),
    )(hs4, gr, yh, x, mods, norm_lru, w_out, norm_ffn, wr_pad)


def _select_kernel(cap, at_ref, slot_ref):
    a = at_ref[...]
    R, T = a.shape
    ones = jnp.ones((T, LANES), BF16)
    su = (lax.broadcasted_iota(jnp.int32, (T, T), 0)
          < lax.broadcasted_iota(jnp.int32, (T, T), 1)).astype(BF16)

    def wide(x):
        return jnp.concatenate([x] * (T // LANES), axis=1)

    def count(mask):
        return _dot(mask.astype(BF16), ones)

    lo = jnp.zeros((R, LANES), F32)
    hi = jnp.full((R, LANES), 2.0, F32)
    for _ in range(SEL_BISECT):
        mid = 0.5 * (lo + hi)
        ge = count(a >= wide(mid)) >= cap
        lo = jnp.where(ge, mid, lo)
        hi = jnp.where(ge, hi, mid)

    def cond(st):
        return (st[0] < T) & (jnp.min(st[2]) < cap)

    def body(st):
        it, cur, n, thr = st
        m = jnp.max(jnp.where(a < wide(cur), a, -1.0), axis=1, keepdims=True)
        m = jnp.broadcast_to(m, (R, LANES))
        c = count(a >= wide(m))
        act = n < cap
        return it + 1, jnp.where(act, m, cur), jnp.where(act, c, n), jnp.where(act, m, thr)

    _, _, _, thr = lax.while_loop(cond, body, (jnp.int32(0), hi, count(a >= wide(hi)), lo))
    thr_w = wide(thr)
    gt = a > thr_w
    eq = a == thr_w
    need = wide(cap - count(gt))
    sel = gt | (eq & (_dot(eq.astype(BF16), su) < need))
    slot_ref[...] = jnp.where(sel, _dot(sel.astype(BF16), su), -1.0).astype(jnp.int32)


def _select(at, T):
    cap = EC_CAPACITY_FACTOR * T // N_EXPERTS
    R = at.shape[0]
    return pl.pallas_call(
        functools.partial(_select_kernel, cap),
        grid=(1,),
        in_specs=[pl.BlockSpec((R, T), lambda i: (0, 0))],
        out_specs=pl.BlockSpec((R, T), lambda i: (0, 0)),
        out_shape=jax.ShapeDtypeStruct((R, T), jnp.int32),
        compiler_params=_cp(("arbitrary",)),
    )(at)


def _onehot(cap, slot_ref, g):
    j = lax.broadcasted_iota(jnp.int32, (cap, slot_ref.shape[1]), 0)
    r0 = g * N_EXPERTS
    return jnp.concatenate([(slot_ref[r0 + e:r0 + e + 1, :] == j).astype(BF16) for e in range(N_EXPERTS)], axis=0)


def _dispatch_kernel(T, cap, slot_ref, ha_ref, xd_ref):
    for g in range(ha_ref.shape[0] // T):
        rows = _dot(_onehot(cap, slot_ref, g), ha_ref[g * T:(g + 1) * T, :])
        for e in range(N_EXPERTS):
            xd_ref[e, g * cap:(g + 1) * cap, :] = rows[e * cap:(e + 1) * cap].astype(BF16)


def _dispatch(slot, ha, T):
    cap = EC_CAPACITY_FACTOR * T // N_EXPERTS
    B = ha.shape[0] // T
    G = max(ROUTE_TOKENS // T, 1)
    return pl.pallas_call(
        functools.partial(_dispatch_kernel, T, cap),
        grid=(B // G,),
        in_specs=[pl.BlockSpec((G * N_EXPERTS, T), lambda b: (b, 0)),
                  pl.BlockSpec((G * T, D_AUG), lambda b: (b, 0))],
        out_specs=pl.BlockSpec((N_EXPERTS, G * cap, D_AUG), lambda b: (0, b, 0)),
        out_shape=jax.ShapeDtypeStruct((N_EXPERTS, B * cap, D_AUG), BF16),
        compiler_params=_cp(("arbitrary",)),
    )(slot, ha)


def _ffn_kernel(xp_ref, xs_ref, wg_ref, wu_ref, wd_ref, yp_ref, ys_ref):
    e = pl.program_id(0)
    lane = lax.broadcasted_iota(jnp.int32, (FFN_TILE, LANES), 1)
    pick = (lane == e) | (lane == e + N_EXPERTS) | (lane == e + 2 * N_EXPERTS)
    wg = wg_ref[...].astype(BF16)
    wu = wu_ref[...].astype(BF16)
    wd = wd_ref[...].astype(BF16)
    for x_ref, y_ref in ((xp_ref, yp_ref), (xs_ref, ys_ref)):
        x = x_ref[:, 0:D_MODEL]
        gate = jnp.sum(jnp.where(pick, x_ref[:, D_MODEL:D_AUG].astype(F32), 0.0), axis=-1, keepdims=True)
        hg = _dot(x, wg)
        hid = (hg * _sigmoid(hg)) * _dot(x, wu)
        y_ref[...] = (_dot(hid.astype(BF16), wd) * gate).astype(BF16)


def _ffn(xd_p, xd_s, w_gate, w_up, w_down):
    rows = xd_p.shape[1]
    assert xd_s.shape[1] == rows and rows % FFN_TILE == 0
    xspec = pl.BlockSpec((None, FFN_TILE, D_AUG), lambda e, m: (e, m, 0))
    wspec = pl.BlockSpec((None, D_MODEL, D_MODEL), lambda e, m: (e, 0, 0))
    yspec = pl.BlockSpec((None, FFN_TILE, D_MODEL), lambda e, m: (e, m, 0))
    yshape = jax.ShapeDtypeStruct((N_EXPERTS, rows, D_MODEL), BF16)
    return pl.pallas_call(
        _ffn_kernel,
        grid=(N_EXPERTS, rows // FFN_TILE),
        in_specs=[xspec, xspec, wspec, wspec, wspec],
        out_specs=[yspec, yspec],
        out_shape=[yshape, yshape],
        compiler_params=_cp(("arbitrary", "arbitrary")),
    )(xd_p, xd_s, w_gate, w_up, w_down)


def _combine_kernel(T, cap, slot_ref, yd_ref, x1_ref, mod_ref, nf_ref, o_ref):
    for g in range(x1_ref.shape[0] // T):
        rows = slice(g * T, (g + 1) * T)
        yd = jnp.concatenate([yd_ref[e, g * cap:(g + 1) * cap, :] for e in range(N_EXPERTS)], axis=0)
        acc = _dot_tn(_onehot(cap, slot_ref, g), yd)
        x2 = x1_ref[rows, :] + mod_ref[0, 5:6, :] * acc
        ms = jnp.mean(x2 * x2, axis=-1, keepdims=True)
        o_ref[rows, :] = x2 * lax.rsqrt(ms + EPS) * nf_ref[...]


def _combine(slot, yd, x1, mods, norm_final, T):
    cap = EC_CAPACITY_FACTOR * T // N_EXPERTS
    n = x1.shape[0]
    G = max(ROUTE_TOKENS // T, 1)
    assert G == 1 or mods.shape[0] == 1
    mod_map = (lambda b: (b, 0, 0)) if mods.shape[0] > 1 else (lambda b: (0, 0, 0))
    return pl.pallas_call(
        functools.partial(_combine_kernel, T, cap),
        grid=(n // (G * T),),
        in_specs=[pl.BlockSpec((G * N_EXPERTS, T), lambda b: (b, 0)),
                  pl.BlockSpec((N_EXPERTS, G * cap, D_MODEL), lambda b: (0, b, 0)),
                  pl.BlockSpec((G * T, D_MODEL), lambda b: (b, 0)),
                  pl.BlockSpec((1, N_MOD, D_MODEL), mod_map),
                  pl.BlockSpec((1, D_MODEL), lambda b: (0, 0))],
        out_specs=pl.BlockSpec((G * T, D_MODEL), lambda b: (b, 0)),
        out_shape=jax.ShapeDtypeStruct((n, D_MODEL), F32),
        compiler_params=_cp(("arbitrary",)),
    )(slot, yd, x1, mods, norm_final)


def _gate_tiles(wa, wx, ba, bx):
    n_ct = D_LRU // LANES
    eye = jnp.eye(LRU_BLOCKS, dtype=wa.dtype)

    def dense(w):
        full = jnp.einsum('nde,nm->ndme', w, eye).reshape(D_LRU, D_LRU)
        t = full.reshape(n_ct, LANES, n_ct, LANES)
        return jnp.stack([t[c, :, c, :] for c in range(n_ct)])

    tiles = 0.5 * jnp.concatenate([dense(wa[0]), dense(wx[0]), dense(wa[1]), dense(wx[1])], axis=-1)
    t = lambda v: v.reshape(n_ct, 1, LANES)
    bias = 0.5 * jnp.concatenate([t(ba[0]), t(bx[0]), t(ba[1]), t(bx[1])], axis=-1)
    hi = bias.astype(BF16)
    lo = (bias - hi.astype(F32)).astype(BF16)
    pad = jnp.zeros((n_ct, LANES - 2, 4 * LANES), BF16)
    return jnp.concatenate([tiles.astype(BF16), hi, lo, pad], axis=1)


def _router_split(w):
    wp = jnp.pad(w, ((0, 0), (0, LANES - N_EXPERTS)))
    hi = wp.astype(BF16)
    mid = (wp - hi.astype(F32)).astype(BF16)
    return jnp.concatenate([hi, mid], axis=1)


def _mixer_and_router(x, mods, p, T, row_len, h0f, h0b, s0, want_state):
    B = x.shape[0]
    xf = x.reshape(B * T, D_MODEL)
    xr4, gr, hg5 = _in_proj(xf, mods, p["norm_mix"], p["w_in"], T)
    n_ct = D_LRU // LANES
    hs4, last_f, last_b = _lru(xr4.reshape(n_ct, B, T, LANES), p["conv_w"], p["conv_b"], p["gate_w"], p["lam"],
                               h0f, h0b, T, row_len)
    yh, st = _hgrn(hg5.reshape(N_HG_IN, B, T, D_HGRN), p["gamma"], p["norm_hgrn"], s0, T, want_state)
    x1, ha, at = _mix_out(hs4.reshape(n_ct, B * T, LANES), gr, yh.reshape(B * T, D_HGRN), xf, mods,
                          p["norm_lru"], p["w_out"], p["norm_ffn"], p["w_router"], T)
    slot = _select(at, T)
    xd = _dispatch(slot, ha, T)
    return x1, slot, xd, last_f, last_b, st


def kernel(x_prompt, x_sample, state_lru, state_hgrn, c, c_ctx, w_ada, b_ada, norm_mix, w_in, conv_w,
           conv_b, lru_wa, lru_ba, lru_wx, lru_bx, lru_lambda, norm_lru, hgrn_gamma, norm_hgrn, w_out,
           norm_ffn, w_router, w_gate, w_up, w_down, norm_final):
    assert w_ada.shape[0] == 1 and hgrn_gamma.shape[1] == 2, "one trunk layer"
    l = 0
    Bp, Tp, _ = x_prompt.shape
    Bs, Ts, _ = x_sample.shape

    cond = jnp.concatenate([c_ctx[None, :], c, jnp.zeros((16 - 1 - Bs, D_MODEL), F32)], axis=0)
    mods = _adaln(cond, w_ada[l], b_ada[l][None, :]).reshape(16, N_MOD, D_MODEL)
    mods_p, mods_s = mods[0:1], mods[1:1 + Bs]

    p = {
        "norm_mix": norm_mix[l][None, :], "w_in": w_in[l],
        "conv_w": conv_w[l], "conv_b": conv_b[l][None, :],
        "gate_w": _gate_tiles(lru_wa[l], lru_wx[l], lru_ba[l], lru_bx[l]),
        "lam": lru_lambda[l], "norm_lru": norm_lru[l][None, :],
        "gamma": hgrn_gamma, "norm_hgrn": norm_hgrn[l][None, :],
        "w_out": w_out[l], "norm_ffn": norm_ffn[l][None, :],
        "w_router": _router_split(w_router[l]),
    }
    zeros_p = jnp.zeros((Bp, D_LRU), F32)
    x1p, slot_p, xd_p, last_f, last_b, st_p = _mixer_and_router(
        x_prompt, mods_p, p, Tp, Tp, zeros_p, zeros_p, None, True)
    x1s, slot_s, xd_s, _, _, _ = _mixer_and_router(
        x_sample, mods_s, p, Ts, GRID_W, state_lru[:, l, 0], state_lru[:, l, 1], state_hgrn[:, l], False)

    yd_p, yd_s = _ffn(xd_p, xd_s, w_gate[l], w_up[l], w_down[l])
    nf = norm_final[None, :]
    y_prompt = _combine(slot_p, yd_p, x1p, mods_p, nf, Tp).reshape(Bp, Tp, D_MODEL)
    y_sample = _combine(slot_s, yd_s, x1s, mods_s, nf, Ts).reshape(Bs, Ts, D_MODEL)
    new_state_lru = jnp.stack([last_f, last_b], axis=1)[:, None]
    new_state_hgrn = st_p[:, None]
    return (y_prompt, y_sample, new_state_lru, new_state_hgrn)
```

```python
import functools

import jax
import jax.numpy as jnp
from jax import lax
from jax.experimental import pallas as pl
from jax.experimental.pallas import tpu as pltpu

F32 = jnp.float32
BF16 = jnp.bfloat16

D_MODEL = 1024
D_LRU = 512
D_HGRN = 512
HG_HEADS = 4
HG_DK = 128
LRU_BLOCKS = 8
LRU_BW = 64
LRU_C = 8.0
N_EXPERTS = 16
EC_CAPACITY_FACTOR = 2
N_MOD = 6
D_IN = 7 * 512
N_HG_IN = 5
GRID_W = 64
EPS = 1e-6
LOG2E = 1.4426950408889634
TINY = 1e-37

LANES = 128
TOK_TILE = 256
PROJ_TILE = 512
MIX_TILE = 1024
LRU_GROUP = 8
PITCH_PAD = 8
HG_CHUNK = 64
HG_ROWS = 256
FFN_TILE = 512
ROUTE_TOKENS = 1024
SEL_BISECT = 20
D_AUG = D_MODEL + LANES
VMEM_LIMIT = 56 * 1024 * 1024


def _cp(sem, vmem=VMEM_LIMIT):
    return pltpu.CompilerParams(dimension_semantics=sem, vmem_limit_bytes=vmem)


def _sigmoid(x):
    return 0.5 * (1.0 + jnp.tanh(0.5 * x))


def _dot(a, b):
    return jnp.dot(a, b, preferred_element_type=F32)


def _dot_nt(a, b):
    return lax.dot_general(a, b, (((1,), (1,)), ((), ())), preferred_element_type=F32)


def _dot_tn(a, b):
    return lax.dot_general(a, b, (((0,), (0,)), ((), ())), preferred_element_type=F32)


def _adaln_kernel(c_ref, w_ref, b_ref, o_ref):
    c = c_ref[...]
    s = (c * _sigmoid(c)).astype(BF16)
    o_ref[...] = _dot(s, w_ref[...].astype(BF16)) + b_ref[...]


def _adaln(cond, w, b):
    n = cond.shape[0]
    tn = 1024
    return pl.pallas_call(
        _adaln_kernel,
        grid=(w.shape[1] // tn,),
        in_specs=[pl.BlockSpec((n, D_MODEL), lambda j: (0, 0)),
                  pl.BlockSpec((D_MODEL, tn), lambda j: (0, j)),
                  pl.BlockSpec((1, tn), lambda j: (0, j))],
        out_specs=pl.BlockSpec((n, tn), lambda j: (0, j)),
        out_shape=jax.ShapeDtypeStruct((n, w.shape[1]), F32),
        compiler_params=_cp(("arbitrary",)),
    )(cond, w, b)


def _proj_kernel(x_ref, mod_ref, g_ref, w_ref, xr_ref, gr_ref, hg_ref):
    x = x_ref[...]
    ms = jnp.mean(x * x, axis=-1, keepdims=True)
    y = x * lax.rsqrt(ms + EPS) * g_ref[...]
    h = y * (1.0 + mod_ref[0, 1:2, :]) + mod_ref[0, 0:1, :]
    h = h.astype(BF16)
    W = 2 * LANES
    for j in range(D_IN // W):
        res = _dot(h, w_ref[:, j * W:(j + 1) * W].astype(BF16))
        k, half = divmod(j * W, D_HGRN)
        if k == 0:
            for c in range(W // LANES):
                xr_ref[half // LANES + c] = res[:, c * LANES:(c + 1) * LANES]
        elif k == 1:
            gr_ref[:, half:half + W] = res
        else:
            hg_ref[k - 2, :, half:half + W] = res
        yield


def _run(parts):
    steps = parts[0]["steps"]
    assert all(p["steps"] == steps for p in parts)
    n_in = [len(p["args"]) for p in parts]
    n_out = [len(p["out_shape"]) for p in parts]
    n_sc = [len(p["scratch"]) for p in parts]

    def body(*refs):
        ins, outs, scr = refs[:sum(n_in)], refs[sum(n_in):sum(n_in) + sum(n_out)], refs[sum(n_in) + sum(n_out):]
        i = o = s = 0
        gens = []
        for p, a, b, c in zip(parts, n_in, n_out, n_sc):
            gens.append(p["kernel"](*ins[i:i + a], *outs[o:o + b], *scr[s:s + c]))
            i, o, s = i + a, o + b, s + c
        while gens:
            for g in list(gens):
                if next(g, StopIteration) is StopIteration:
                    gens.remove(g)

    res = pl.pallas_call(
        body,
        grid=(steps,),
        in_specs=[s for p in parts for s in p["in_specs"]],
        out_specs=[s for p in parts for s in p["out_specs"]],
        out_shape=[s for p in parts for s in p["out_shape"]],
        scratch_shapes=[s for p in parts for s in p["scratch"]],
        compiler_params=_cp(("arbitrary",)),
    )(*[a for p in parts for a in p["args"]])
    out, o = [], 0
    for b in n_out:
        out.append(res[o:o + b])
        o += b
    return out


def _proj_part(x, mods, gain, w, T):
    n = x.shape[0]
    per_req = T // PROJ_TILE if T >= PROJ_TILE else None
    if per_req is None:
        reqs = PROJ_TILE // T
        mod_map = lambda i: (0, 0, 0)
        assert mods.shape[0] == 1 and reqs >= 1
    else:
        mod_map = lambda i: (i // per_req, 0, 0)
    n_ct = D_LRU // LANES
    return dict(
        kernel=_proj_kernel,
        steps=n // PROJ_TILE,
        args=[x, mods, gain, w],
        in_specs=[pl.BlockSpec((PROJ_TILE, D_MODEL), lambda i: (i, 0)),
                  pl.BlockSpec((1, N_MOD, D_MODEL), mod_map),
                  pl.BlockSpec((1, D_MODEL), lambda i: (0, 0)),
                  pl.BlockSpec((D_MODEL, D_IN), lambda i: (0, 0), pipeline_mode=pl.Buffered(1))],
        out_specs=[pl.BlockSpec((n_ct, PROJ_TILE, LANES), lambda i: (0, i, 0)),
                   pl.BlockSpec((PROJ_TILE, D_LRU), lambda i: (i, 0)),
                   pl.BlockSpec((N_HG_IN, PROJ_TILE, D_HGRN), lambda i: (0, i, 0))],
        out_shape=[jax.ShapeDtypeStruct((n_ct, n, LANES), F32),
                   jax.ShapeDtypeStruct((n, D_LRU), F32),
                   jax.ShapeDtypeStruct((N_HG_IN, n, D_HGRN), F32)],
        scratch=[],
    )


def _lru_kernel(T, row_len, unroll, xr_ref, cw_ref, cb_ref, w_ref, lam_ref, h0f_ref, h0b_ref,
                hs_ref, lf_ref, lb_ref, xpad, af, uf, ab, ub, hf_s, hb_s):
    pitch = T + PITCH_PAD
    pos = lax.broadcasted_iota(jnp.int32, (T, LANES), 0) & (row_len - 1)
    cw = cw_ref[...]
    w0 = jnp.where(pos >= 2, cw[0:1], 0.0)
    w1 = jnp.where(pos >= 1, cw[1:2], 0.0)
    w2 = jnp.broadcast_to(cw[2:3], (T, LANES))
    w3 = jnp.where(pos <= row_len - 2, cw[3:4], 0.0)
    cb = cb_ref[...]
    nl = -lam_ref[...]
    sp = jnp.maximum(nl, 0.0) + jnp.log1p(jnp.exp(-jnp.abs(nl)))
    c2 = (-0.5 * LRU_C * LOG2E) * sp
    w = w_ref[0]
    ones = (lax.broadcasted_iota(jnp.int32, (T, LANES), 1) < 2).astype(BF16)
    for s in range(2):
        xpad[s, 0:8, :] = jnp.zeros((8, LANES), F32)
        xpad[s, T + 8:T + 16, :] = jnp.zeros((8, LANES), F32)
    for b in range(LRU_GROUP):
        x = xr_ref[b]
        xp = xpad.at[b % 2]
        xp[8:T + 8, :] = x
        xc = w0 * xp[6:T + 6, :] + w1 * xp[7:T + 7, :] + w2 * x + w3 * xp[9:T + 9, :] + cb
        xh = 0.5 * xc
        z = _dot(jnp.concatenate([xc.astype(BF16), ones], axis=1), w)
        rows = pl.ds(b * pitch, T)
        for d, (a_s, u_s) in enumerate(((af, uf), (ab, ub))):
            tr = jnp.tanh(z[:, (2 * d) * LANES:(2 * d + 1) * LANES])
            ti = jnp.tanh(z[:, (2 * d + 1) * LANES:(2 * d + 2) * LANES])
            a = jnp.exp2(c2[d:d + 1] + c2[d:d + 1] * tr)
            om = 1.0 - a * a
            a_s[rows, :] = a
            u_s[rows, :] = (om * lax.rsqrt(jnp.maximum(om, TINY))) * (xh + ti * xh)
        yield

    def step(t, carry):
        hf, hb = carry
        rf = pl.ds(t, LRU_GROUP, stride=pitch)
        hf = af[rf, :] * hf + uf[rf, :]
        hf_s[rf, :] = hf
        rb = pl.ds(T - 1 - t, LRU_GROUP, stride=pitch)
        hb = ab[rb, :] * hb + ub[rb, :]
        hb_s[rb, :] = hb
        return hf, hb

    carry = (h0f_ref[...], h0b_ref[...])
    if unroll is True:
        for t in range(T):
            carry = step(t, carry)
            if t % (T // 8) == T // 8 - 1 and t != T - 1:
                yield
    else:
        carry = lax.fori_loop(0, T, step, carry, unroll=unroll)
    hf, hb = carry
    lf_ref[...] = hf
    lb_ref[...] = hb
    for b in range(LRU_GROUP):
        rows = pl.ds(b * pitch, T)
        hs_ref[b] = hf_s[rows, :] + hb_s[rows, :]


def _lru_part(xr4, conv_w, conv_b, w_tiles, lam, h0f, h0b, T, row_len, unroll):
    n_ct, B = xr4.shape[:2]
    rows = LRU_GROUP * (T + PITCH_PAD)
    vec = pl.BlockSpec((LRU_GROUP, LANES), lambda i: (i // n_ct, i % n_ct))
    slab = pl.BlockSpec((None, LRU_GROUP, T, LANES), lambda i: (i % n_ct, i // n_ct, 0, 0))
    return dict(
        kernel=functools.partial(_lru_kernel, T, row_len, unroll),
        steps=(B // LRU_GROUP) * n_ct,
        args=[xr4, conv_w, conv_b, w_tiles, lam, h0f, h0b],
        in_specs=[slab,
                  pl.BlockSpec((4, LANES), lambda i: (0, i % n_ct)),
                  pl.BlockSpec((1, LANES), lambda i: (0, i % n_ct)),
                  pl.BlockSpec((1, 2 * LANES, 4 * LANES), lambda i: (i % n_ct, 0, 0)),
                  pl.BlockSpec((2, LANES), lambda i: (0, i % n_ct)),
                  vec, vec],
        out_specs=[slab, vec, vec],
        out_shape=[jax.ShapeDtypeStruct((n_ct, B, T, LANES), F32),
                   jax.ShapeDtypeStruct((B, D_LRU), F32),
                   jax.ShapeDtypeStruct((B, D_LRU), F32)],
        scratch=[pltpu.VMEM((2, T + 16, LANES), F32)] + [pltpu.VMEM((rows, LANES), F32)] * 6,
    )


def _hgrn_kernel(T, has_s0, want_state, *refs):
    q_ref, ff_ref, fb_ref, v_ref, og_ref, gam_ref, gain_ref = refs[:7]
    refs = refs[7:]
    if has_s0:
        s0_ref, refs = refs[0], refs[1:]
    y_ref, refs = refs[0], refs[1:]
    if want_state:
        st_ref, refs = refs[0], refs[1:]
    q_s, v_s, kf_s, kb_s, bf_s, bb_s, of_s, ob_s, st_s = refs
    C = HG_CHUNK
    n = T // C
    RB = HG_ROWS
    heads = [slice(h * LANES, (h + 1) * LANES) for h in range(HG_HEADS)]

    ti = lax.broadcasted_iota(jnp.int32, (RB, RB), 0)
    si = lax.broadcasted_iota(jnp.int32, (RB, RB), 1)
    sh = C.bit_length() - 1
    same = (ti >> sh) == (si >> sh)
    tri = ((same & (si <= ti)).astype(BF16), (same & (si >= ti)).astype(BF16))

    for r in range(T // RB):
        rows = slice(r * RB, (r + 1) * RB)
        qr = q_ref[rows, :]
        qh = (0.5 * HG_DK ** -0.5) * qr
        q_s[rows, :] = qh + qh * jnp.tanh(0.5 * qr)
        v_s[rows, :] = v_ref[rows, :].astype(BF16)
        for d, (f_ref, k_s, b_s) in enumerate(((ff_ref, kf_s, bf_s), (fb_ref, kb_s, bb_s))):
            g0 = gam_ref[d, 0:1, :]
            g1 = gam_ref[d, 1:2, :]
            m = jnp.maximum(g0, g1)
            e0 = jnp.exp(g0 - m)
            lbd = e0 / (e0 + jnp.exp(g1 - m))
            ck = 0.5 * (1.0 - lbd)
            pt = ck * jnp.tanh(0.5 * f_ref[rows, :])
            k_s[rows, :] = ck - pt
            lf = jnp.log((lbd + ck) + pt)
            hi = lf.astype(BF16)
            r1 = lf - hi.astype(F32)
            mid = r1.astype(BF16)
            lo = (r1 - mid.astype(F32)).astype(BF16)
            cs = _dot(tri[d], jnp.concatenate([hi, mid, lo], axis=1))
            b_s[rows, :] = cs[:, 0:D_HGRN] + (cs[:, D_HGRN:2 * D_HGRN] + cs[:, 2 * D_HGRN:3 * D_HGRN])

    for d in range(2):
        for h in range(HG_HEADS):
            st_s[d * HG_HEADS + h] = s0_ref[d, h].T if has_s0 else jnp.zeros((LANES, LANES), F32)

    ri = lax.broadcasted_iota(jnp.int32, (C, C), 0)
    ci = lax.broadcasted_iota(jnp.int32, (C, C), 1)

    def body(i, carry):
        for d, (k_s, b_s, o_s) in enumerate(((kf_s, bf_s, of_s), (kb_s, bb_s, ob_s))):
            c = i if d == 0 else n - 1 - i
            sl = pl.ds(pl.multiple_of(c * C, C), C)
            for h in range(HG_HEADS):
                q = q_s[sl, heads[h]]
                k = k_s[sl, heads[h]]
                b = b_s[sl, heads[h]]
                v = v_s[sl, heads[h]]
                bm = b[C // 2:C // 2 + 1, :]
                g = b[C - 1:C, :] if d == 0 else b[0:1, :]
                qi = q * jnp.exp(b - bm)
                qe = (qi * jnp.exp(bm)).astype(BF16)
                ki = k * jnp.exp(bm - b)
                kd = (ki * jnp.exp(g - bm)).astype(BF16)
                s = _dot_nt(qi.astype(BF16), ki.astype(BF16))
                p = jnp.where((ci <= ri) if d == 0 else (ci >= ri), s, 0.0).astype(BF16)
                st = st_s[d * HG_HEADS + h]
                o_s[sl, heads[h]] = _dot(p, v) + _dot_nt(qe, st.astype(BF16))
                st_s[d * HG_HEADS + h] = st * jnp.exp(g) + _dot_tn(v, kd)
        return carry

    lax.fori_loop(0, n, body, 0, unroll=4)
    if want_state:
        for d in range(2):
            for h in range(HG_HEADS):
                st_ref[d, h] = st_s[d * HG_HEADS + h].T

    for r in range(T // RB):
        rows = slice(r * RB, (r + 1) * RB)
        for h in range(HG_HEADS):
            o = of_s[rows, heads[h]] + ob_s[rows, heads[h]]
            ms = jnp.mean(o * o, axis=-1, keepdims=True)
            oh = 0.5 * og_ref[rows, heads[h]]
            y = o * lax.rsqrt(ms + EPS) * gain_ref[...] * (oh + oh * jnp.tanh(oh))
            y_ref[rows, heads[h]] = y.astype(BF16)


def _hgrn(hg5, gamma, gain, s0, T, want_state):
    B = hg5.shape[1]
    has_s0 = s0 is not None

    def col(k):
        return pl.BlockSpec((None, None, T, D_HGRN), lambda b, k=k: (k, b, 0, 0))

    st_spec = pl.BlockSpec((None, 2, HG_HEADS, HG_DK, HG_DK), lambda b: (b, 0, 0, 0, 0))
    in_specs = [col(0), col(1), col(2), col(3), col(4),
                pl.BlockSpec((2, 2, D_HGRN), lambda b: (0, 0, 0)),
                pl.BlockSpec((1, LANES), lambda b: (0, 0))]
    args = [hg5, hg5, hg5, hg5, hg5, gamma, gain]
    if has_s0:
        in_specs.append(st_spec)
        args.append(s0)
    out_specs = [pl.BlockSpec((None, T, D_HGRN), lambda b: (b, 0, 0))]
    out_shape = [jax.ShapeDtypeStruct((B, T, D_HGRN), BF16)]
    if want_state:
        out_specs.append(st_spec)
        out_shape.append(jax.ShapeDtypeStruct((B, 2, HG_HEADS, HG_DK, HG_DK), F32))
    res = pl.pallas_call(
        functools.partial(_hgrn_kernel, T, has_s0, want_state),
        grid=(B,),
        in_specs=in_specs,
        out_specs=out_specs,
        out_shape=out_shape,
        scratch_shapes=[pltpu.VMEM((T, D_HGRN), F32), pltpu.VMEM((T, D_HGRN), BF16)]
        + [pltpu.VMEM((T, D_HGRN), F32)] * 6
        + [pltpu.VMEM((2 * HG_HEADS, HG_DK, HG_DK), F32)],
        compiler_params=_cp(("arbitrary",)),
    )(*args)
    return res if want_state else (res[0], None)


def _mix_out_kernel(hs_ref, gr_ref, yh_ref, x_ref, mod_ref, nl_ref, wo_ref, nf_ref, wr_ref,
                    x1_ref, ha_ref, at_ref):
    wr = wr_ref[...]
    wo = wo_ref[...].astype(BF16)
    g1 = mod_ref[0, 2:3, :]
    sh2 = mod_ref[0, 3:4, :]
    gain2 = nf_ref[...] * (1.0 + mod_ref[0, 4:5, :])
    lane = lax.broadcasted_iota(jnp.int32, (TOK_TILE, LANES), 1)
    for s in range(MIX_TILE // TOK_TILE):
        rows = slice(s * TOK_TILE, (s + 1) * TOK_TILE)
        hs = jnp.concatenate([hs_ref[c, rows, :] for c in range(D_LRU // LANES)], axis=1)
        ms = jnp.mean(hs * hs, axis=-1, keepdims=True)
        gr = gr_ref[rows, :]
        gelu = 0.5 * gr * (1.0 + jnp.tanh(0.7978845608028654 * (gr + 0.044715 * (gr * gr * gr))))
        y_lru = hs * lax.rsqrt(ms + EPS) * nl_ref[...] * gelu
        ycat = jnp.concatenate([y_lru.astype(BF16), yh_ref[rows, :]], axis=-1)
        x1 = x_ref[rows, :] + g1 * _dot(ycat, wo)
        x1_ref[rows, :] = x1
        ms = jnp.mean(x1 * x1, axis=-1, keepdims=True)
        h2 = x1 * lax.rsqrt(ms + EPS) * gain2 + sh2
        h2_hi = h2.astype(BF16)
        h2_mid = (h2 - h2_hi.astype(F32)).astype(BF16)
        p_hi = _dot(h2_hi, wr)
        logits = p_hi[:, 0:LANES] + (p_hi[:, LANES:2 * LANES] + _dot(h2_mid, wr)[:, 0:LANES])
        logits = jnp.where(lane < N_EXPERTS, logits, -jnp.inf)
        e = jnp.exp(logits - jnp.max(logits, axis=-1, keepdims=True))
        aff = e / jnp.sum(e, axis=-1, keepdims=True)
        w = at_ref.shape[1]
        r, c = (s * TOK_TILE) // w, (s * TOK_TILE) % w
        at_ref[r * N_EXPERTS:(r + 1) * N_EXPERTS, c:c + TOK_TILE] = aff.T[0:N_EXPERTS, :]
        hi = aff.astype(BF16).astype(F32)
        r1 = aff - hi
        mid = r1.astype(BF16).astype(F32)
        lo = (r1 - mid).astype(BF16).astype(F32)
        split = hi + pltpu.roll(mid, N_EXPERTS, 1) + pltpu.roll(lo, 2 * N_EXPERTS, 1)
        ha_ref[rows, 0:D_MODEL] = h2_hi
        ha_ref[rows, D_MODEL:D_AUG] = split.astype(BF16)


def _mix_out(hs4, gr, yh, x, mods, norm_lru, w_out, norm_ffn, wr_pad, T):
    n = x.shape[0]
    tiles = n // MIX_TILE
    assert T % TOK_TILE == 0 and (MIX_TILE % T == 0 or T % MIX_TILE == 0)
    q = max(T // MIX_TILE, 1)
    at_rows = N_EXPERTS * max(MIX_TILE // T, 1)
    at_spec = pl.BlockSpec((at_rows, min(T, MIX_TILE)), lambda i: (i // q, i % q))
    mod_map = (lambda i: (i // q, 0, 0)) if mods.shape[0] > 1 else (lambda i: (0, 0, 0))
    tile = lambda w: pl.BlockSpec((MIX_TILE, w), lambda i: (i, 0))
    const = lambda r, w: pl.BlockSpec((r, w), lambda i: (0, 0))
    return pl.pallas_call(
        _mix_out_kernel,
        grid=(tiles,),
        in_specs=[pl.BlockSpec((D_LRU // LANES, MIX_TILE, LANES), lambda i: (0, i, 0)),
                  tile(D_LRU), tile(D_HGRN), tile(D_MODEL),
                  pl.BlockSpec((1, N_MOD, D_MODEL), mod_map),
                  const(1, D_LRU), const(D_MODEL, D_MODEL), const(1, D_MODEL), const(D_MODEL, 2 * LANES)],
        out_specs=[tile(D_MODEL), tile(D_AUG), at_spec],
        out_shape=[jax.ShapeDtypeStruct((n, D_MODEL), F32),
                   jax.ShapeDtypeStruct((n, D_AUG), BF16),
                   jax.ShapeDtypeStruct((n // T * N_EXPERTS, T), F32)],
        compiler_params=_cp(("arbitrary",)),
    )(hs4, gr, yh, x, mods, norm_lru, w_out, norm_ffn, wr_pad)


def _select_kernel(cap, at_ref, slot_ref):
    a = at_ref[...]
    R, T = a.shape
    ones = jnp.ones((T, LANES), BF16)
    su = (lax.broadcasted_iota(jnp.int32, (T, T), 0)
          < lax.broadcasted_iota(jnp.int32, (T, T), 1)).astype(BF16)

    def wide(x):
        return jnp.concatenate([x] * (T // LANES), axis=1)

    def count(mask):
        return _dot(mask.astype(BF16), ones)

    lo = jnp.zeros((R, LANES), F32)
    hi = jnp.full((R, LANES), 2.0, F32)
    for _ in range(SEL_BISECT):
        mid = 0.5 * (lo + hi)
        ge = count(a >= wide(mid)) >= cap
        lo = jnp.where(ge, mid, lo)
        hi = jnp.where(ge, hi, mid)

    def cond(st):
        return (st[0] < T) & (jnp.min(st[2]) < cap)

    def body(st):
        it, cur, n, thr = st
        m = jnp.max(jnp.where(a < wide(cur), a, -1.0), axis=1, keepdims=True)
        m = jnp.broadcast_to(m, (R, LANES))
        c = count(a >= wide(m))
        act = n < cap
        return it + 1, jnp.where(act, m, cur), jnp.where(act, c, n), jnp.where(act, m, thr)

    _, _, _, thr = lax.while_loop(cond, body, (jnp.int32(0), hi, count(a >= wide(hi)), lo))
    thr_w = wide(thr)
    gt = a > thr_w
    eq = a == thr_w
    need = wide(cap - count(gt))
    sel = gt | (eq & (_dot(eq.astype(BF16), su) < need))
    slot_ref[...] = jnp.where(sel, _dot(sel.astype(BF16), su), -1.0).astype(jnp.int32)


def _select(at, T):
    cap = EC_CAPACITY_FACTOR * T // N_EXPERTS
    R = at.shape[0]
    return pl.pallas_call(
        functools.partial(_select_kernel, cap),
        grid=(1,),
        in_specs=[pl.BlockSpec((R, T), lambda i: (0, 0))],
        out_specs=pl.BlockSpec((R, T), lambda i: (0, 0)),
        out_shape=jax.ShapeDtypeStruct((R, T), jnp.int32),
        compiler_params=_cp(("arbitrary",)),
    )(at)


def _onehot(cap, slot_ref, g):
    j = lax.broadcasted_iota(jnp.int32, (cap, slot_ref.shape[1]), 0)
    r0 = g * N_EXPERTS
    return jnp.concatenate([(slot_ref[r0 + e:r0 + e + 1, :] == j).astype(BF16) for e in range(N_EXPERTS)], axis=0)


def _dispatch_kernel(T, cap, slot_ref, ha_ref, xd_ref):
    for g in range(ha_ref.shape[0] // T):
        rows = _dot(_onehot(cap, slot_ref, g), ha_ref[g * T:(g + 1) * T, :])
        for e in range(N_EXPERTS):
            xd_ref[e, g * cap:(g + 1) * cap, :] = rows[e * cap:(e + 1) * cap].astype(BF16)


def _dispatch(slot, ha, T):
    cap = EC_CAPACITY_FACTOR * T // N_EXPERTS
    B = ha.shape[0] // T
    G = max(ROUTE_TOKENS // T, 1)
    return pl.pallas_call(
        functools.partial(_dispatch_kernel, T, cap),
        grid=(B // G,),
        in_specs=[pl.BlockSpec((G * N_EXPERTS, T), lambda b: (b, 0)),
                  pl.BlockSpec((G * T, D_AUG), lambda b: (b, 0))],
        out_specs=pl.BlockSpec((N_EXPERTS, G * cap, D_AUG), lambda b: (0, b, 0)),
        out_shape=jax.ShapeDtypeStruct((N_EXPERTS, B * cap, D_AUG), BF16),
        compiler_params=_cp(("arbitrary",)),
    )(slot, ha)


def _ffn_kernel(xp_ref, xs_ref, wg_ref, wu_ref, wd_ref, yp_ref, ys_ref):
    e = pl.program_id(0)
    lane = lax.broadcasted_iota(jnp.int32, (FFN_TILE, LANES), 1)
    pick = (lane == e) | (lane == e + N_EXPERTS) | (lane == e + 2 * N_EXPERTS)
    wg = wg_ref[...].astype(BF16)
    wu = wu_ref[...].astype(BF16)
    wd = wd_ref[...].astype(BF16)
    for x_ref, y_ref in ((xp_ref, yp_ref), (xs_ref, ys_ref)):
        x = x_ref[:, 0:D_MODEL]
        gate = jnp.sum(jnp.where(pick, x_ref[:, D_MODEL:D_AUG].astype(F32), 0.0), axis=-1, keepdims=True)
        hg = _dot(x, wg)
        hid = (hg * _sigmoid(hg)) * _dot(x, wu)
        y_ref[...] = (_dot(hid.astype(BF16), wd) * gate).astype(BF16)


def _ffn(xd_p, xd_s, w_gate, w_up, w_down):
    rows = xd_p.shape[1]
    assert xd_s.shape[1] == rows and rows % FFN_TILE == 0
    xspec = pl.BlockSpec((None, FFN_TILE, D_AUG), lambda e, m: (e, m, 0))
    wspec = pl.BlockSpec((None, D_MODEL, D_MODEL), lambda e, m: (e, 0, 0))
    yspec = pl.BlockSpec((None, FFN_TILE, D_MODEL), lambda e, m: (e, m, 0))
    yshape = jax.ShapeDtypeStruct((N_EXPERTS, rows, D_MODEL), BF16)
    return pl.pallas_call(
        _ffn_kernel,
        grid=(N_EXPERTS, rows // FFN_TILE),
        in_specs=[xspec, xspec, wspec, wspec, wspec],
        out_specs=[yspec, yspec],
        out_shape=[yshape, yshape],
        compiler_params=_cp(("arbitrary", "arbitrary")),
    )(xd_p, xd_s, w_gate, w_up, w_down)


def _combine_kernel(T, cap, slot_ref, yd_ref, x1_ref, mod_ref, nf_ref, o_ref):
    for g in range(x1_ref.shape[0] // T):
        rows = slice(g * T, (g + 1) * T)
        yd = jnp.concatenate([yd_ref[e, g * cap:(g + 1) * cap, :] for e in range(N_EXPERTS)], axis=0)
        acc = _dot_tn(_onehot(cap, slot_ref, g), yd)
        x2 = x1_ref[rows, :] + mod_ref[0, 5:6, :] * acc
        ms = jnp.mean(x2 * x2, axis=-1, keepdims=True)
        o_ref[rows, :] = x2 * lax.rsqrt(ms + EPS) * nf_ref[...]


def _combine(slot, yd, x1, mods, norm_final, T):
    cap = EC_CAPACITY_FACTOR * T // N_EXPERTS
    n = x1.shape[0]
    G = max(ROUTE_TOKENS // T, 1)
    assert G == 1 or mods.shape[0] == 1
    mod_map = (lambda b: (b, 0, 0)) if mods.shape[0] > 1 else (lambda b: (0, 0, 0))
    return pl.pallas_call(
        functools.partial(_combine_kernel, T, cap),
        grid=(n // (G * T),),
        in_specs=[pl.BlockSpec((G * N_EXPERTS, T), lambda b: (b, 0)),
                  pl.BlockSpec((N_EXPERTS, G * cap, D_MODEL), lambda b: (0, b, 0)),
                  pl.BlockSpec((G * T, D_MODEL), lambda b: (b, 0)),
                  pl.BlockSpec((1, N_MOD, D_MODEL), mod_map),
                  pl.BlockSpec((1, D_MODEL), lambda b: (0, 0))],
        out_specs=pl.BlockSpec((G * T, D_MODEL), lambda b: (b, 0)),
        out_shape=jax.ShapeDtypeStruct((n, D_MODEL), F32),
        compiler_params=_cp(("arbitrary",)),
    )(slot, yd, x1, mods, norm_final)


def _gate_tiles(wa, wx, ba, bx):
    n_ct = D_LRU // LANES
    eye = jnp.eye(LRU_BLOCKS, dtype=wa.dtype)

    def dense(w):
        full = jnp.einsum('nde,nm->ndme', w, eye).reshape(D_LRU, D_LRU)
        t = full.reshape(n_ct, LANES, n_ct, LANES)
        return jnp.stack([t[c, :, c, :] for c in range(n_ct)])

    tiles = 0.5 * jnp.concatenate([dense(wa[0]), dense(wx[0]), dense(wa[1]), dense(wx[1])], axis=-1)
    t = lambda v: v.reshape(n_ct, 1, LANES)
    bias = 0.5 * jnp.concatenate([t(ba[0]), t(bx[0]), t(ba[1]), t(bx[1])], axis=-1)
    hi = bias.astype(BF16)
    lo = (bias - hi.astype(F32)).astype(BF16)
    pad = jnp.zeros((n_ct, LANES - 2, 4 * LANES), BF16)
    return jnp.concatenate([tiles.astype(BF16), hi, lo, pad], axis=1)


def _router_split(w):
    wp = jnp.pad(w, ((0, 0), (0, LANES - N_EXPERTS)))
    hi = wp.astype(BF16)
    mid = (wp - hi.astype(F32)).astype(BF16)
    return jnp.concatenate([hi, mid], axis=1)


def _proj(x, mods, p, T):
    return _proj_part(x.reshape(-1, D_MODEL), mods, p["norm_mix"], p["w_in"], T)


def _lru(xr4, B, T, row_len, h0f, h0b, p, unroll):
    n_ct = D_LRU // LANES
    return _lru_part(xr4.reshape(n_ct, B, T, LANES), p["conv_w"], p["conv_b"], p["gate_w"], p["lam"],
                     h0f, h0b, T, row_len, unroll)


def _mix_and_route(x, mods, p, T, hs4, gr, hg5, s0, want_state):
    B = x.shape[0]
    n_ct = D_LRU // LANES
    yh, st = _hgrn(hg5.reshape(N_HG_IN, B, T, D_HGRN), p["gamma"], p["norm_hgrn"], s0, T, want_state)
    x1, ha, at = _mix_out(hs4.reshape(n_ct, B * T, LANES), gr, yh.reshape(B * T, D_HGRN),
                          x.reshape(B * T, D_MODEL), mods,
                          p["norm_lru"], p["w_out"], p["norm_ffn"], p["w_router"], T)
    slot = _select(at, T)
    return x1, slot, _dispatch(slot, ha, T), st


def kernel(x_prompt, x_sample, state_lru, state_hgrn, c, c_ctx, w_ada, b_ada, norm_mix, w_in, conv_w,
           conv_b, lru_wa, lru_ba, lru_wx, lru_bx, lru_lambda, norm_lru, hgrn_gamma, norm_hgrn, w_out,
           norm_ffn, w_router, w_gate, w_up, w_down, norm_final):
    assert w_ada.shape[0] == 1 and hgrn_gamma.shape[1] == 2, "one trunk layer"
    l = 0
    Bp, Tp, _ = x_prompt.shape
    Bs, Ts, _ = x_sample.shape

    cond = jnp.concatenate([c_ctx[None, :], c, jnp.zeros((16 - 1 - Bs, D_MODEL), F32)], axis=0)
    mods = _adaln(cond, w_ada[l], b_ada[l][None, :]).reshape(16, N_MOD, D_MODEL)
    mods_p, mods_s = mods[0:1], mods[1:1 + Bs]

    p = {
        "norm_mix": norm_mix[l][None, :], "w_in": w_in[l],
        "conv_w": conv_w[l], "conv_b": conv_b[l][None, :],
        "gate_w": _gate_tiles(lru_wa[l], lru_wx[l], lru_ba[l], lru_bx[l]),
        "lam": lru_lambda[l], "norm_lru": norm_lru[l][None, :],
        "gamma": hgrn_gamma, "norm_hgrn": norm_hgrn[l][None, :],
        "w_out": w_out[l], "norm_ffn": norm_ffn[l][None, :],
        "w_router": _router_split(w_router[l]),
    }
    zeros_p = jnp.zeros((Bp, D_LRU), F32)
    ((xr4_p, gr_p, hg5_p),) = _run([_proj(x_prompt, mods_p, p, Tp)])
    (xr4_s, gr_s, hg5_s), (hs4_p, last_f, last_b) = _run(
        [_proj(x_sample, mods_s, p, Ts), _lru(xr4_p, Bp, Tp, Tp, zeros_p, zeros_p, p, True)])
    x1p, slot_p, xd_p, st_p = _mix_and_route(x_prompt, mods_p, p, Tp, hs4_p, gr_p, hg5_p, None, True)
    ((hs4_s, _, _),) = _run([_lru(xr4_s, Bs, Ts, GRID_W, state_lru[:, l, 0], state_lru[:, l, 1], p, 8)])
    x1s, slot_s, xd_s, _ = _mix_and_route(x_sample, mods_s, p, Ts, hs4_s, gr_s, hg5_s, state_hgrn[:, l], False)

    yd_p, yd_s = _ffn(xd_p, xd_s, w_gate[l], w_up[l], w_down[l])
    nf = norm_final[None, :]
    y_prompt = _combine(slot_p, yd_p, x1p, mods_p, nf, Tp).reshape(Bp, Tp, D_MODEL)
    y_sample = _combine(slot_s, yd_s, x1s, mods_s, nf, Ts).reshape(Bs, Ts, D_MODEL)
    new_state_lru = jnp.stack([last_f, last_b], axis=1)[:, None]
    new_state_hgrn = st_p[:, None]
    return (y_prompt, y_sample, new_state_lru, new_state_hgrn)
```

```python
import functools

import jax
import jax.numpy as jnp
from jax import lax
from jax.experimental import pallas as pl
from jax.experimental.pallas import tpu as pltpu

F32 = jnp.float32
BF16 = jnp.bfloat16

D_MODEL = 1024
D_LRU = 512
D_HGRN = 512
HG_HEADS = 4
HG_DK = 128
LRU_BLOCKS = 8
LRU_BW = 64
LRU_C = 8.0
N_EXPERTS = 16
EC_CAPACITY_FACTOR = 2
N_MOD = 6
D_IN = 7 * 512
N_HG_IN = 5
GRID_W = 64
EPS = 1e-6
LOG2E = 1.4426950408889634
TINY = 1e-37

LANES = 128
TOK_TILE = 256
PROJ_TILE = 512
MIX_TILE = 1024
LRU_GROUP = 8
PITCH_PAD = 8
HG_CHUNK = 64
HG_ROWS = 256
FFN_TILE = 512
ROUTE_TOKENS = 1024
COMBINE_TOKENS = 512
SEL_BISECT = 20
D_AUG = D_MODEL + LANES
VMEM_LIMIT = 56 * 1024 * 1024


def _cp(sem, vmem=VMEM_LIMIT):
    return pltpu.CompilerParams(dimension_semantics=sem, vmem_limit_bytes=vmem)


def _sigmoid(x):
    return 0.5 * (1.0 + jnp.tanh(0.5 * x))


def _dot(a, b):
    return jnp.dot(a, b, preferred_element_type=F32)


def _dot_nt(a, b):
    return lax.dot_general(a, b, (((1,), (1,)), ((), ())), preferred_element_type=F32)


def _dot_tn(a, b):
    return lax.dot_general(a, b, (((0,), (0,)), ((), ())), preferred_element_type=F32)


def _adaln_kernel(c_ref, w_ref, b_ref, o_ref):
    c = c_ref[...]
    s = (c * _sigmoid(c)).astype(BF16)
    o_ref[...] = _dot(s, w_ref[...].astype(BF16)) + b_ref[...]


def _adaln(cond, w, b):
    n = cond.shape[0]
    tn = 1024
    return pl.pallas_call(
        _adaln_kernel,
        grid=(w.shape[1] // tn,),
        in_specs=[pl.BlockSpec((n, D_MODEL), lambda j: (0, 0)),
                  pl.BlockSpec((D_MODEL, tn), lambda j: (0, j)),
                  pl.BlockSpec((1, tn), lambda j: (0, j))],
        out_specs=pl.BlockSpec((n, tn), lambda j: (0, j)),
        out_shape=jax.ShapeDtypeStruct((n, w.shape[1]), F32),
        compiler_params=_cp(("arbitrary",)),
    )(cond, w, b)


def _proj_kernel(x_ref, mod_ref, g_ref, w_ref, xr_ref, gr_ref, hg_ref):
    x = x_ref[...]
    ms = jnp.mean(x * x, axis=-1, keepdims=True)
    y = x * lax.rsqrt(ms + EPS) * g_ref[...]
    h = y * (1.0 + mod_ref[0, 1:2, :]) + mod_ref[0, 0:1, :]
    h = h.astype(BF16)
    W = 2 * LANES
    for j in range(D_IN // W):
        res = _dot(h, w_ref[:, j * W:(j + 1) * W].astype(BF16))
        k, half = divmod(j * W, D_HGRN)
        if k == 0:
            for c in range(W // LANES):
                xr_ref[half // LANES + c] = res[:, c * LANES:(c + 1) * LANES]
        elif k == 1:
            gr_ref[:, half:half + W] = res
        else:
            hg_ref[k - 2, :, half:half + W] = res
        yield


def _run(parts):
    steps = parts[0]["steps"]
    assert all(p["steps"] == steps for p in parts)
    n_in = [len(p["args"]) for p in parts]
    n_out = [len(p["out_shape"]) for p in parts]
    n_sc = [len(p["scratch"]) for p in parts]

    def body(*refs):
        ins, outs, scr = refs[:sum(n_in)], refs[sum(n_in):sum(n_in) + sum(n_out)], refs[sum(n_in) + sum(n_out):]
        i = o = s = 0
        gens = []
        for p, a, b, c in zip(parts, n_in, n_out, n_sc):
            gens.append(p["kernel"](*ins[i:i + a], *outs[o:o + b], *scr[s:s + c]))
            i, o, s = i + a, o + b, s + c
        while gens:
            for g in list(gens):
                if next(g, StopIteration) is StopIteration:
                    gens.remove(g)

    res = pl.pallas_call(
        body,
        grid=(steps,),
        in_specs=[s for p in parts for s in p["in_specs"]],
        out_specs=[s for p in parts for s in p["out_specs"]],
        out_shape=[s for p in parts for s in p["out_shape"]],
        scratch_shapes=[s for p in parts for s in p["scratch"]],
        compiler_params=_cp(("arbitrary",)),
    )(*[a for p in parts for a in p["args"]])
    out, o = [], 0
    for b in n_out:
        out.append(res[o:o + b])
        o += b
    return out


def _proj_part(x, mods, gain, w, T):
    n = x.shape[0]
    per_req = T // PROJ_TILE if T >= PROJ_TILE else None
    if per_req is None:
        reqs = PROJ_TILE // T
        mod_map = lambda i: (0, 0, 0)
        assert mods.shape[0] == 1 and reqs >= 1
    else:
        mod_map = lambda i: (i // per_req, 0, 0)
    n_ct = D_LRU // LANES
    return dict(
        kernel=_proj_kernel,
        steps=n // PROJ_TILE,
        args=[x, mods, gain, w],
        in_specs=[pl.BlockSpec((PROJ_TILE, D_MODEL), lambda i: (i, 0)),
                  pl.BlockSpec((1, N_MOD, D_MODEL), mod_map),
                  pl.BlockSpec((1, D_MODEL), lambda i: (0, 0)),
                  pl.BlockSpec((D_MODEL, D_IN), lambda i: (0, 0), pipeline_mode=pl.Buffered(1))],
        out_specs=[pl.BlockSpec((n_ct, PROJ_TILE, LANES), lambda i: (0, i, 0)),
                   pl.BlockSpec((PROJ_TILE, D_LRU), lambda i: (i, 0)),
                   pl.BlockSpec((N_HG_IN, PROJ_TILE, D_HGRN), lambda i: (0, i, 0))],
        out_shape=[jax.ShapeDtypeStruct((n_ct, n, LANES), F32),
                   jax.ShapeDtypeStruct((n, D_LRU), F32),
                   jax.ShapeDtypeStruct((N_HG_IN, n, D_HGRN), F32)],
        scratch=[],
    )


def _lru_kernel(T, row_len, unroll, xr_ref, cw_ref, cb_ref, w_ref, lam_ref, h0f_ref, h0b_ref,
                hs_ref, lf_ref, lb_ref, xpad, af, uf, ab, ub, hf_s, hb_s):
    pitch = T + PITCH_PAD
    pos = lax.broadcasted_iota(jnp.int32, (T, LANES), 0) & (row_len - 1)
    cw = cw_ref[...]
    w0 = jnp.where(pos >= 2, cw[0:1], 0.0)
    w1 = jnp.where(pos >= 1, cw[1:2], 0.0)
    w2 = jnp.broadcast_to(cw[2:3], (T, LANES))
    w3 = jnp.where(pos <= row_len - 2, cw[3:4], 0.0)
    cb = cb_ref[...]
    nl = -lam_ref[...]
    sp = jnp.maximum(nl, 0.0) + jnp.log1p(jnp.exp(-jnp.abs(nl)))
    c2 = (-0.5 * LRU_C * LOG2E) * sp
    w = w_ref[0]
    ones = (lax.broadcasted_iota(jnp.int32, (T, LANES), 1) < 2).astype(BF16)
    for s in range(2):
        xpad[s, 0:8, :] = jnp.zeros((8, LANES), F32)
        xpad[s, T + 8:T + 16, :] = jnp.zeros((8, LANES), F32)
    for b in range(LRU_GROUP):
        x = xr_ref[b]
        xp = xpad.at[b % 2]
        xp[8:T + 8, :] = x
        xc = w0 * xp[6:T + 6, :] + w1 * xp[7:T + 7, :] + w2 * x + w3 * xp[9:T + 9, :] + cb
        xh = 0.5 * xc
        z = _dot(jnp.concatenate([xc.astype(BF16), ones], axis=1), w)
        rows = pl.ds(b * pitch, T)
        for d, (a_s, u_s) in enumerate(((af, uf), (ab, ub))):
            tr = jnp.tanh(z[:, (2 * d) * LANES:(2 * d + 1) * LANES])
            ti = jnp.tanh(z[:, (2 * d + 1) * LANES:(2 * d + 2) * LANES])
            a = jnp.exp2(c2[d:d + 1] + c2[d:d + 1] * tr)
            om = 1.0 - a * a
            a_s[rows, :] = a
            u_s[rows, :] = (om * lax.rsqrt(jnp.maximum(om, TINY))) * (xh + ti * xh)
        yield

    def step(t, carry):
        hf, hb = carry
        rf = pl.ds(t, LRU_GROUP, stride=pitch)
        hf = af[rf, :] * hf + uf[rf, :]
        hf_s[rf, :] = hf
        rb = pl.ds(T - 1 - t, LRU_GROUP, stride=pitch)
        hb = ab[rb, :] * hb + ub[rb, :]
        hb_s[rb, :] = hb
        return hf, hb

    carry = (h0f_ref[...], h0b_ref[...])
    if unroll is True:
        for t in range(T):
            carry = step(t, carry)
            if t % (T // 8) == T // 8 - 1 and t != T - 1:
                yield
    else:
        carry = lax.fori_loop(0, T, step, carry, unroll=unroll)
    hf, hb = carry
    lf_ref[...] = hf
    lb_ref[...] = hb
    for b in range(LRU_GROUP):
        rows = pl.ds(b * pitch, T)
        hs_ref[b] = hf_s[rows, :] + hb_s[rows, :]


def _lru_part(xr4, conv_w, conv_b, w_tiles, lam, h0f, h0b, T, row_len, unroll):
    n_ct, B = xr4.shape[:2]
    rows = LRU_GROUP * (T + PITCH_PAD)
    vec = pl.BlockSpec((LRU_GROUP, LANES), lambda i: (i // n_ct, i % n_ct))
    slab = pl.BlockSpec((None, LRU_GROUP, T, LANES), lambda i: (i % n_ct, i // n_ct, 0, 0))
    return dict(
        kernel=functools.partial(_lru_kernel, T, row_len, unroll),
        steps=(B // LRU_GROUP) * n_ct,
        args=[xr4, conv_w, conv_b, w_tiles, lam, h0f, h0b],
        in_specs=[slab,
                  pl.BlockSpec((4, LANES), lambda i: (0, i % n_ct)),
                  pl.BlockSpec((1, LANES), lambda i: (0, i % n_ct)),
                  pl.BlockSpec((1, 2 * LANES, 4 * LANES), lambda i: (i % n_ct, 0, 0)),
                  pl.BlockSpec((2, LANES), lambda i: (0, i % n_ct)),
                  vec, vec],
        out_specs=[slab, vec, vec],
        out_shape=[jax.ShapeDtypeStruct((n_ct, B, T, LANES), F32),
                   jax.ShapeDtypeStruct((B, D_LRU), F32),
                   jax.ShapeDtypeStruct((B, D_LRU), F32)],
        scratch=[pltpu.VMEM((2, T + 16, LANES), F32)] + [pltpu.VMEM((rows, LANES), F32)] * 6,
    )


def _hgrn_kernel(T, has_s0, want_state, *refs):
    q_ref, ff_ref, fb_ref, v_ref, og_ref, gam_ref, gain_ref = refs[:7]
    refs = refs[7:]
    if has_s0:
        s0_ref, refs = refs[0], refs[1:]
    y_ref, refs = refs[0], refs[1:]
    if want_state:
        st_ref, refs = refs[0], refs[1:]
    q_s, v_s, kf_s, kb_s, bf_s, bb_s, of_s, ob_s, st_s = refs
    C = HG_CHUNK
    n = T // C
    RB = HG_ROWS
    heads = [slice(h * LANES, (h + 1) * LANES) for h in range(HG_HEADS)]

    ti = lax.broadcasted_iota(jnp.int32, (RB, RB), 0)
    si = lax.broadcasted_iota(jnp.int32, (RB, RB), 1)
    sh = C.bit_length() - 1
    same = (ti >> sh) == (si >> sh)
    tri = ((same & (si <= ti)).astype(BF16), (same & (si >= ti)).astype(BF16))

    for r in range(T // RB):
        rows = slice(r * RB, (r + 1) * RB)
        qr = q_ref[rows, :]
        qh = (0.5 * HG_DK ** -0.5) * qr
        q_s[rows, :] = qh + qh * jnp.tanh(0.5 * qr)
        v_s[rows, :] = v_ref[rows, :].astype(BF16)
        for d, (f_ref, k_s, b_s) in enumerate(((ff_ref, kf_s, bf_s), (fb_ref, kb_s, bb_s))):
            g0 = gam_ref[d, 0:1, :]
            g1 = gam_ref[d, 1:2, :]
            m = jnp.maximum(g0, g1)
            e0 = jnp.exp(g0 - m)
            lbd = e0 / (e0 + jnp.exp(g1 - m))
            ck = 0.5 * (1.0 - lbd)
            pt = ck * jnp.tanh(0.5 * f_ref[rows, :])
            k_s[rows, :] = ck - pt
            lf = jnp.log((lbd + ck) + pt)
            hi = lf.astype(BF16)
            r1 = lf - hi.astype(F32)
            mid = r1.astype(BF16)
            lo = (r1 - mid.astype(F32)).astype(BF16)
            cs = _dot(tri[d], jnp.concatenate([hi, mid, lo], axis=1))
            b_s[rows, :] = cs[:, 0:D_HGRN] + (cs[:, D_HGRN:2 * D_HGRN] + cs[:, 2 * D_HGRN:3 * D_HGRN])

    for d in range(2):
        for h in range(HG_HEADS):
            st_s[d * HG_HEADS + h] = s0_ref[d, h].T if has_s0 else jnp.zeros((LANES, LANES), F32)

    ri = lax.broadcasted_iota(jnp.int32, (C, C), 0)
    ci = lax.broadcasted_iota(jnp.int32, (C, C), 1)

    def body(i, carry):
        for d, (k_s, b_s, o_s) in enumerate(((kf_s, bf_s, of_s), (kb_s, bb_s, ob_s))):
            c = i if d == 0 else n - 1 - i
            sl = pl.ds(pl.multiple_of(c * C, C), C)
            for h in range(HG_HEADS):
                q = q_s[sl, heads[h]]
                k = k_s[sl, heads[h]]
                b = b_s[sl, heads[h]]
                v = v_s[sl, heads[h]]
                bm = b[C // 2:C // 2 + 1, :]
                g = b[C - 1:C, :] if d == 0 else b[0:1, :]
                qi = q * jnp.exp(b - bm)
                qe = (qi * jnp.exp(bm)).astype(BF16)
                ki = k * jnp.exp(bm - b)
                kd = (ki * jnp.exp(g - bm)).astype(BF16)
                s = _dot_nt(qi.astype(BF16), ki.astype(BF16))
                p = jnp.where((ci <= ri) if d == 0 else (ci >= ri), s, 0.0).astype(BF16)
                st = st_s[d * HG_HEADS + h]
                o_s[sl, heads[h]] = _dot(p, v) + _dot_nt(qe, st.astype(BF16))
                st_s[d * HG_HEADS + h] = st * jnp.exp(g) + _dot_tn(v, kd)
        return carry

    lax.fori_loop(0, n, body, 0, unroll=4)
    if want_state:
        for d in range(2):
            for h in range(HG_HEADS):
                st_ref[d, h] = st_s[d * HG_HEADS + h].T

    for r in range(T // RB):
        rows = slice(r * RB, (r + 1) * RB)
        for h in range(HG_HEADS):
            o = of_s[rows, heads[h]] + ob_s[rows, heads[h]]
            ms = jnp.mean(o * o, axis=-1, keepdims=True)
            oh = 0.5 * og_ref[rows, heads[h]]
            y = o * lax.rsqrt(ms + EPS) * gain_ref[...] * (oh + oh * jnp.tanh(oh))
            y_ref[rows, heads[h]] = y.astype(BF16)


def _hgrn(hg5, gamma, gain, s0, T, want_state):
    B = hg5.shape[1]
    has_s0 = s0 is not None

    def col(k):
        return pl.BlockSpec((None, None, T, D_HGRN), lambda b, k=k: (k, b, 0, 0))

    st_spec = pl.BlockSpec((None, 2, HG_HEADS, HG_DK, HG_DK), lambda b: (b, 0, 0, 0, 0))
    in_specs = [col(0), col(1), col(2), col(3), col(4),
                pl.BlockSpec((2, 2, D_HGRN), lambda b: (0, 0, 0)),
                pl.BlockSpec((1, LANES), lambda b: (0, 0))]
    args = [hg5, hg5, hg5, hg5, hg5, gamma, gain]
    if has_s0:
        in_specs.append(st_spec)
        args.append(s0)
    out_specs = [pl.BlockSpec((None, T, D_HGRN), lambda b: (b, 0, 0))]
    out_shape = [jax.ShapeDtypeStruct((B, T, D_HGRN), BF16)]
    if want_state:
        out_specs.append(st_spec)
        out_shape.append(jax.ShapeDtypeStruct((B, 2, HG_HEADS, HG_DK, HG_DK), F32))
    res = pl.pallas_call(
        functools.partial(_hgrn_kernel, T, has_s0, want_state),
        grid=(B,),
        in_specs=in_specs,
        out_specs=out_specs,
        out_shape=out_shape,
        scratch_shapes=[pltpu.VMEM((T, D_HGRN), F32), pltpu.VMEM((T, D_HGRN), BF16)]
        + [pltpu.VMEM((T, D_HGRN), F32)] * 6
        + [pltpu.VMEM((2 * HG_HEADS, HG_DK, HG_DK), F32)],
        compiler_params=_cp(("arbitrary",)),
    )(*args)
    return res if want_state else (res[0], None)


def _mix_out_kernel(hs_ref, gr_ref, yh_ref, x_ref, mod_ref, nl_ref, wo_ref, nf_ref, wr_ref,
                    x1_ref, ha_ref, at_ref):
    wr = wr_ref[...]
    wo = wo_ref[...].astype(BF16)
    g1 = mod_ref[0, 2:3, :]
    sh2 = mod_ref[0, 3:4, :]
    gain2 = nf_ref[...] * (1.0 + mod_ref[0, 4:5, :])
    lane = lax.broadcasted_iota(jnp.int32, (TOK_TILE, LANES), 1)
    for s in range(x_ref.shape[0] // TOK_TILE):
        rows = slice(s * TOK_TILE, (s + 1) * TOK_TILE)
        hs = jnp.concatenate([hs_ref[c, rows, :] for c in range(D_LRU // LANES)], axis=1)
        ms = jnp.mean(hs * hs, axis=-1, keepdims=True)
        gr = gr_ref[rows, :]
        gelu = 0.5 * gr * (1.0 + jnp.tanh(0.7978845608028654 * (gr + 0.044715 * (gr * gr * gr))))
        y_lru = hs * lax.rsqrt(ms + EPS) * nl_ref[...] * gelu
        ycat = jnp.concatenate([y_lru.astype(BF16), yh_ref[rows, :]], axis=-1)
        x1 = x_ref[rows, :] + g1 * _dot(ycat, wo)
        x1_ref[rows, :] = x1
        yield
        ms = jnp.mean(x1 * x1, axis=-1, keepdims=True)
        h2 = x1 * lax.rsqrt(ms + EPS) * gain2 + sh2
        h2_hi = h2.astype(BF16)
        h2_mid = (h2 - h2_hi.astype(F32)).astype(BF16)
        p_hi = _dot(h2_hi, wr)
        logits = p_hi[:, 0:LANES] + (p_hi[:, LANES:2 * LANES] + _dot(h2_mid, wr)[:, 0:LANES])
        logits = jnp.where(lane < N_EXPERTS, logits, -jnp.inf)
        e = jnp.exp(logits - jnp.max(logits, axis=-1, keepdims=True))
        aff = e / jnp.sum(e, axis=-1, keepdims=True)
        w = at_ref.shape[1]
        r, c = (s * TOK_TILE) // w, (s * TOK_TILE) % w
        at_ref[r * N_EXPERTS:(r + 1) * N_EXPERTS, c:c + TOK_TILE] = aff.T[0:N_EXPERTS, :]
        hi = aff.astype(BF16).astype(F32)
        r1 = aff - hi
        mid = r1.astype(BF16).astype(F32)
        lo = (r1 - mid).astype(BF16).astype(F32)
        split = hi + pltpu.roll(mid, N_EXPERTS, 1) + pltpu.roll(lo, 2 * N_EXPERTS, 1)
        ha_ref[rows, 0:D_MODEL] = h2_hi
        ha_ref[rows, D_MODEL:D_AUG] = split.astype(BF16)
        yield


def _mix_out_part(hs4, gr, yh, x, mods, norm_lru, w_out, norm_ffn, wr_pad, T, tile_rows):
    n = x.shape[0]
    assert T % TOK_TILE == 0 and (tile_rows % T == 0 or T % tile_rows == 0)
    q = max(T // tile_rows, 1)
    at_rows = N_EXPERTS * max(tile_rows // T, 1)
    at_spec = pl.BlockSpec((at_rows, min(T, tile_rows)), lambda i: (i // q, i % q))
    mod_map = (lambda i: (i // q, 0, 0)) if mods.shape[0] > 1 else (lambda i: (0, 0, 0))
    tile = lambda w: pl.BlockSpec((tile_rows, w), lambda i: (i, 0))
    const = lambda r, w: pl.BlockSpec((r, w), lambda i: (0, 0))
    return dict(
        kernel=_mix_out_kernel,
        steps=n // tile_rows,
        args=[hs4, gr, yh, x, mods, norm_lru, w_out, norm_ffn, wr_pad],
        in_specs=[pl.BlockSpec((D_LRU // LANES, tile_rows, LANES), lambda i: (0, i, 0)),
                  tile(D_LRU), tile(D_HGRN), tile(D_MODEL),
                  pl.BlockSpec((1, N_MOD, D_MODEL), mod_map),
                  const(1, D_LRU), const(D_MODEL, D_MODEL), const(1, D_MODEL), const(D_MODEL, 2 * LANES)],
        out_specs=[tile(D_MODEL), tile(D_AUG), at_spec],
        out_shape=[jax.ShapeDtypeStruct((n, D_MODEL), F32),
                   jax.ShapeDtypeStruct((n, D_AUG), BF16),
                   jax.ShapeDtypeStruct((n // T * N_EXPERTS, T), F32)],
        scratch=[],
    )


def _select_kernel(cap, at_ref, slot_ref):
    a = at_ref[...]
    R, T = a.shape
    ones = jnp.ones((T, LANES), BF16)
    su = (lax.broadcasted_iota(jnp.int32, (T, T), 0)
          < lax.broadcasted_iota(jnp.int32, (T, T), 1)).astype(BF16)

    def wide(x):
        return jnp.concatenate([x] * (T // LANES), axis=1)

    def count(mask):
        return _dot(mask.astype(BF16), ones)

    lo = jnp.zeros((R, LANES), F32)
    hi = jnp.full((R, LANES), 2.0, F32)
    for _ in range(SEL_BISECT):
        mid = 0.5 * (lo + hi)
        ge = count(a >= wide(mid)) >= cap
        lo = jnp.where(ge, mid, lo)
        hi = jnp.where(ge, hi, mid)

    def cond(st):
        return (st[0] < T) & (jnp.min(st[2]) < cap)

    def body(st):
        it, cur, n, thr = st
        m = jnp.max(jnp.where(a < wide(cur), a, -1.0), axis=1, keepdims=True)
        m = jnp.broadcast_to(m, (R, LANES))
        c = count(a >= wide(m))
        act = n < cap
        return it + 1, jnp.where(act, m, cur), jnp.where(act, c, n), jnp.where(act, m, thr)

    _, _, _, thr = lax.while_loop(cond, body, (jnp.int32(0), hi, count(a >= wide(hi)), lo))
    thr_w = wide(thr)
    gt = a > thr_w
    eq = a == thr_w
    need = wide(cap - count(gt))
    sel = gt | (eq & (_dot(eq.astype(BF16), su) < need))
    slot_ref[...] = jnp.where(sel, _dot(sel.astype(BF16), su), -1.0).astype(jnp.int32)


def _select(at, T):
    cap = EC_CAPACITY_FACTOR * T // N_EXPERTS
    R = at.shape[0]
    return pl.pallas_call(
        functools.partial(_select_kernel, cap),
        grid=(1,),
        in_specs=[pl.BlockSpec((R, T), lambda i: (0, 0))],
        out_specs=pl.BlockSpec((R, T), lambda i: (0, 0)),
        out_shape=jax.ShapeDtypeStruct((R, T), jnp.int32),
        compiler_params=_cp(("arbitrary",)),
    )(at)


def _onehot(cap, slot_ref, g):
    j = lax.broadcasted_iota(jnp.int32, (cap, slot_ref.shape[1]), 0)
    r0 = g * N_EXPERTS
    return jnp.concatenate([(slot_ref[r0 + e:r0 + e + 1, :] == j).astype(BF16) for e in range(N_EXPERTS)], axis=0)


def _dispatch_kernel(T, cap, slot_ref, ha_ref, xd_ref):
    for g in range(ha_ref.shape[0] // T):
        rows = _dot(_onehot(cap, slot_ref, g), ha_ref[g * T:(g + 1) * T, :])
        for e in range(N_EXPERTS):
            xd_ref[e, g * cap:(g + 1) * cap, :] = rows[e * cap:(e + 1) * cap].astype(BF16)
        yield


def _dispatch_part(slot, ha, T, G):
    cap = EC_CAPACITY_FACTOR * T // N_EXPERTS
    B = ha.shape[0] // T
    return dict(
        kernel=functools.partial(_dispatch_kernel, T, cap),
        steps=B // G,
        args=[slot, ha],
        in_specs=[pl.BlockSpec((G * N_EXPERTS, T), lambda b: (b, 0)),
                  pl.BlockSpec((G * T, D_AUG), lambda b: (b, 0))],
        out_specs=[pl.BlockSpec((N_EXPERTS, G * cap, D_AUG), lambda b: (0, b, 0))],
        out_shape=[jax.ShapeDtypeStruct((N_EXPERTS, B * cap, D_AUG), BF16)],
        scratch=[],
    )


def _ffn_kernel(xp_ref, xs_ref, wg_ref, wu_ref, wd_ref, yp_ref, ys_ref):
    e = pl.program_id(0)
    lane = lax.broadcasted_iota(jnp.int32, (FFN_TILE, LANES), 1)
    pick = (lane == e) | (lane == e + N_EXPERTS) | (lane == e + 2 * N_EXPERTS)
    wg = wg_ref[...].astype(BF16)
    wu = wu_ref[...].astype(BF16)
    wd = wd_ref[...].astype(BF16)
    for x_ref, y_ref in ((xp_ref, yp_ref), (xs_ref, ys_ref)):
        x = x_ref[:, 0:D_MODEL]
        gate = jnp.sum(jnp.where(pick, x_ref[:, D_MODEL:D_AUG].astype(F32), 0.0), axis=-1, keepdims=True)
        hg = _dot(x, wg)
        hid = (hg * _sigmoid(hg)) * _dot(x, wu)
        y_ref[...] = (_dot(hid.astype(BF16), wd) * gate).astype(BF16)


def _ffn(xd_p, xd_s, w_gate, w_up, w_down):
    rows = xd_p.shape[1]
    assert xd_s.shape[1] == rows and rows % FFN_TILE == 0
    xspec = pl.BlockSpec((None, FFN_TILE, D_AUG), lambda e, m: (e, m, 0))
    wspec = pl.BlockSpec((None, D_MODEL, D_MODEL), lambda e, m: (e, 0, 0))
    yspec = pl.BlockSpec((None, FFN_TILE, D_MODEL), lambda e, m: (e, m, 0))
    yshape = jax.ShapeDtypeStruct((N_EXPERTS, rows, D_MODEL), BF16)
    return pl.pallas_call(
        _ffn_kernel,
        grid=(N_EXPERTS, rows // FFN_TILE),
        in_specs=[xspec, xspec, wspec, wspec, wspec],
        out_specs=[yspec, yspec],
        out_shape=[yshape, yshape],
        compiler_params=_cp(("arbitrary", "arbitrary")),
    )(xd_p, xd_s, w_gate, w_up, w_down)


def _combine_kernel(cap, slot_ref, yd_ref, x1_ref, mod_ref, nf_ref, o_ref):
    W = slot_ref.shape[1]
    for g in range(x1_ref.shape[0] // W):
        rows = slice(g * W, (g + 1) * W)
        yd = jnp.concatenate([yd_ref[e, g * cap:(g + 1) * cap, :] for e in range(N_EXPERTS)], axis=0)
        acc = _dot_tn(_onehot(cap, slot_ref, g), yd)
        x2 = x1_ref[rows, :] + mod_ref[0, 5:6, :] * acc
        ms = jnp.mean(x2 * x2, axis=-1, keepdims=True)
        o_ref[rows, :] = x2 * lax.rsqrt(ms + EPS) * nf_ref[...]
        yield


def _combine_part(slot, yd, x1, mods, norm_final, T, tok):
    cap = EC_CAPACITY_FACTOR * T // N_EXPERTS
    n = x1.shape[0]
    G, h = (tok // T, 1) if tok >= T else (1, T // tok)
    assert G == 1 or mods.shape[0] == 1
    mod_map = (lambda i: (i // h, 0, 0)) if mods.shape[0] > 1 else (lambda i: (0, 0, 0))
    return dict(
        kernel=functools.partial(_combine_kernel, cap),
        steps=n // tok,
        args=[slot, yd, x1, mods, norm_final],
        in_specs=[pl.BlockSpec((G * N_EXPERTS, min(T, tok)), lambda i: (i // h, i % h)),
                  pl.BlockSpec((N_EXPERTS, G * cap, D_MODEL), lambda i: (0, i // h, 0)),
                  pl.BlockSpec((tok, D_MODEL), lambda i: (i, 0)),
                  pl.BlockSpec((1, N_MOD, D_MODEL), mod_map),
                  pl.BlockSpec((1, D_MODEL), lambda i: (0, 0))],
        out_specs=[pl.BlockSpec((tok, D_MODEL), lambda i: (i, 0))],
        out_shape=[jax.ShapeDtypeStruct((n, D_MODEL), F32)],
        scratch=[],
    )


def _gate_tiles(wa, wx, ba, bx):
    n_ct = D_LRU // LANES
    eye = jnp.eye(LRU_BLOCKS, dtype=wa.dtype)

    def dense(w):
        full = jnp.einsum('nde,nm->ndme', w, eye).reshape(D_LRU, D_LRU)
        t = full.reshape(n_ct, LANES, n_ct, LANES)
        return jnp.stack([t[c, :, c, :] for c in range(n_ct)])

    tiles = 0.5 * jnp.concatenate([dense(wa[0]), dense(wx[0]), dense(wa[1]), dense(wx[1])], axis=-1)
    t = lambda v: v.reshape(n_ct, 1, LANES)
    bias = 0.5 * jnp.concatenate([t(ba[0]), t(bx[0]), t(ba[1]), t(bx[1])], axis=-1)
    hi = bias.astype(BF16)
    lo = (bias - hi.astype(F32)).astype(BF16)
    pad = jnp.zeros((n_ct, LANES - 2, 4 * LANES), BF16)
    return jnp.concatenate([tiles.astype(BF16), hi, lo, pad], axis=1)


def _router_split(w):
    wp = jnp.pad(w, ((0, 0), (0, LANES - N_EXPERTS)))
    hi = wp.astype(BF16)
    mid = (wp - hi.astype(F32)).astype(BF16)
    return jnp.concatenate([hi, mid], axis=1)


def _proj(x, mods, p, T):
    return _proj_part(x.reshape(-1, D_MODEL), mods, p["norm_mix"], p["w_in"], T)


def _lru(xr4, B, T, row_len, h0f, h0b, p, unroll):
    n_ct = D_LRU // LANES
    return _lru_part(xr4.reshape(n_ct, B, T, LANES), p["conv_w"], p["conv_b"], p["gate_w"], p["lam"],
                     h0f, h0b, T, row_len, unroll)


def _mix(x, mods, p, T, hs4, gr, yh, tile_rows):
    B = x.shape[0]
    return _mix_out_part(hs4.reshape(D_LRU // LANES, B * T, LANES), gr, yh.reshape(B * T, D_HGRN),
                         x.reshape(B * T, D_MODEL), mods,
                         p["norm_lru"], p["w_out"], p["norm_ffn"], p["w_router"], T, tile_rows)


def kernel(x_prompt, x_sample, state_lru, state_hgrn, c, c_ctx, w_ada, b_ada, norm_mix, w_in, conv_w,
           conv_b, lru_wa, lru_ba, lru_wx, lru_bx, lru_lambda, norm_lru, hgrn_gamma, norm_hgrn, w_out,
           norm_ffn, w_router, w_gate, w_up, w_down, norm_final):
    assert w_ada.shape[0] == 1 and hgrn_gamma.shape[1] == 2, "one trunk layer"
    l = 0
    Bp, Tp, _ = x_prompt.shape
    Bs, Ts, _ = x_sample.shape

    cond = jnp.concatenate([c_ctx[None, :], c, jnp.zeros((16 - 1 - Bs, D_MODEL), F32)], axis=0)
    mods = _adaln(cond, w_ada[l], b_ada[l][None, :]).reshape(16, N_MOD, D_MODEL)
    mods_p, mods_s = mods[0:1], mods[1:1 + Bs]

    p = {
        "norm_mix": norm_mix[l][None, :], "w_in": w_in[l],
        "conv_w": conv_w[l], "conv_b": conv_b[l][None, :],
        "gate_w": _gate_tiles(lru_wa[l], lru_wx[l], lru_ba[l], lru_bx[l]),
        "lam": lru_lambda[l], "norm_lru": norm_lru[l][None, :],
        "gamma": hgrn_gamma, "norm_hgrn": norm_hgrn[l][None, :],
        "w_out": w_out[l], "norm_ffn": norm_ffn[l][None, :],
        "w_router": _router_split(w_router[l]),
    }
    zeros_p = jnp.zeros((Bp, D_LRU), F32)
    ((xr4_p, gr_p, hg5_p),) = _run([_proj(x_prompt, mods_p, p, Tp)])
    (xr4_s, gr_s, hg5_s), (hs4_p, last_f, last_b) = _run(
        [_proj(x_sample, mods_s, p, Ts), _lru(xr4_p, Bp, Tp, Tp, zeros_p, zeros_p, p, True)])
    yh_p, st_p = _hgrn(hg5_p.reshape(N_HG_IN, Bp, Tp, D_HGRN), p["gamma"], p["norm_hgrn"], None, Tp, True)
    ((x1p, ha_p, at_p),) = _run([_mix(x_prompt, mods_p, p, Tp, hs4_p, gr_p, yh_p, MIX_TILE)])
    slot_p = _select(at_p, Tp)
    ((hs4_s, _, _),) = _run([_lru(xr4_s, Bs, Ts, GRID_W, state_lru[:, l, 0], state_lru[:, l, 1], p, 8)])
    yh_s, _ = _hgrn(hg5_s.reshape(N_HG_IN, Bs, Ts, D_HGRN), p["gamma"], p["norm_hgrn"], state_hgrn[:, l], Ts, False)
    fused_tile = MIX_TILE // 2
    (x1s, ha_s, at_s), (xd_p,) = _run(
        [_mix(x_sample, mods_s, p, Ts, hs4_s, gr_s, yh_s, fused_tile),
         _dispatch_part(slot_p, ha_p, Tp, fused_tile // Tp)])
    slot_s = _select(at_s, Ts)
    ((xd_s,),) = _run([_dispatch_part(slot_s, ha_s, Ts, max(ROUTE_TOKENS // Ts, 1))])

    yd_p, yd_s = _ffn(xd_p, xd_s, w_gate[l], w_up[l], w_down[l])
    nf = norm_final[None, :]
    (y_sample,), (y_prompt,) = _run(
        [_combine_part(slot_s, yd_s, x1s, mods_s, nf, Ts, COMBINE_TOKENS),
         _combine_part(slot_p, yd_p, x1p, mods_p, nf, Tp, COMBINE_TOKENS)])
    y_prompt = y_prompt.reshape(Bp, Tp, D_MODEL)
    y_sample = y_sample.reshape(Bs, Ts, D_MODEL)
    new_state_lru = jnp.stack([last_f, last_b], axis=1)[:, None]
    new_state_hgrn = st_p[:, None]
    return (y_prompt, y_sample, new_state_lru, new_state_hgrn)
```

```python
import functools

import jax
import jax.numpy as jnp
from jax import lax
from jax.experimental import pallas as pl
from jax.experimental.pallas import tpu as pltpu

F32 = jnp.float32
BF16 = jnp.bfloat16

D_MODEL = 1024
D_LRU = 512
D_HGRN = 512
HG_HEADS = 4
HG_DK = 128
LRU_BLOCKS = 8
LRU_BW = 64
LRU_C = 8.0
N_EXPERTS = 16
EC_CAPACITY_FACTOR = 2
N_MOD = 6
D_IN = 7 * 512
N_HG_IN = 5
GRID_W = 64
EPS = 1e-6
LOG2E = 1.4426950408889634
TINY = 1e-37

LANES = 128
TOK_TILE = 256
PROJ_TILE = 512
MIX_TILE = 1024
LRU_GROUP = 8
PITCH_PAD = 8
HG_CHUNK = 64
HG_UNROLL = 16
HG_ROWS = 256
FFN_TILE = 512
ROUTE_TOKENS = 1024
COMBINE_TOKENS = 512
SEL_BISECT = 20
D_AUG = D_MODEL + LANES
VMEM_LIMIT = 56 * 1024 * 1024


def _cp(sem, vmem=VMEM_LIMIT):
    return pltpu.CompilerParams(dimension_semantics=sem, vmem_limit_bytes=vmem)


def _sigmoid(x):
    return 0.5 * (1.0 + jnp.tanh(0.5 * x))


def _dot(a, b):
    return jnp.dot(a, b, preferred_element_type=F32)


def _dot_nt(a, b):
    return lax.dot_general(a, b, (((1,), (1,)), ((), ())), preferred_element_type=F32)


def _dot_tn(a, b):
    return lax.dot_general(a, b, (((0,), (0,)), ((), ())), preferred_element_type=F32)


def _adaln_kernel(c_ref, w_ref, b_ref, o_ref):
    c = c_ref[...]
    s = (c * _sigmoid(c)).astype(BF16)
    o_ref[...] = _dot(s, w_ref[...].astype(BF16)) + b_ref[...]


def _adaln(cond, w, b):
    n = cond.shape[0]
    tn = 1024
    return pl.pallas_call(
        _adaln_kernel,
        grid=(w.shape[2] // tn,),
        in_specs=[pl.BlockSpec((n, D_MODEL), lambda j: (0, 0)),
                  pl.BlockSpec((None, D_MODEL, tn), lambda j: (0, 0, j)),
                  pl.BlockSpec((1, tn), lambda j: (0, j))],
        out_specs=pl.BlockSpec((n, tn), lambda j: (0, j)),
        out_shape=jax.ShapeDtypeStruct((n, w.shape[2]), F32),
        compiler_params=_cp(("arbitrary",)),
    )(cond, w, b)


def _proj_kernel(x_ref, mod_ref, g_ref, w_ref, xr_ref, gr_ref, hg_ref):
    x = x_ref[...]
    ms = jnp.mean(x * x, axis=-1, keepdims=True)
    y = x * lax.rsqrt(ms + EPS) * g_ref[...]
    h = y * (1.0 + mod_ref[0, 1:2, :]) + mod_ref[0, 0:1, :]
    h = h.astype(BF16)
    W = 2 * LANES
    for j in range(D_IN // W):
        res = _dot(h, w_ref[:, j * W:(j + 1) * W].astype(BF16))
        k, half = divmod(j * W, D_HGRN)
        if k == 0:
            for c in range(W // LANES):
                xr_ref[half // LANES + c] = res[:, c * LANES:(c + 1) * LANES]
        elif k == 1:
            gr_ref[:, half:half + W] = res
        else:
            hg_ref[k - 2, :, half:half + W] = res
        yield


def _run(parts):
    steps = parts[0]["steps"]
    assert all(p["steps"] == steps for p in parts)
    n_in = [len(p["args"]) for p in parts]
    n_out = [len(p["out_shape"]) for p in parts]
    n_sc = [len(p["scratch"]) for p in parts]

    def body(*refs):
        ins, outs, scr = refs[:sum(n_in)], refs[sum(n_in):sum(n_in) + sum(n_out)], refs[sum(n_in) + sum(n_out):]
        i = o = s = 0
        gens = []
        for p, a, b, c in zip(parts, n_in, n_out, n_sc):
            gens.append(p["kernel"](*ins[i:i + a], *outs[o:o + b], *scr[s:s + c]))
            i, o, s = i + a, o + b, s + c
        while gens:
            for g in list(gens):
                if next(g, StopIteration) is StopIteration:
                    gens.remove(g)

    res = pl.pallas_call(
        body,
        grid=(steps,),
        in_specs=[s for p in parts for s in p["in_specs"]],
        out_specs=[s for p in parts for s in p["out_specs"]],
        out_shape=[s for p in parts for s in p["out_shape"]],
        scratch_shapes=[s for p in parts for s in p["scratch"]],
        compiler_params=_cp(("arbitrary",)),
    )(*[a for p in parts for a in p["args"]])
    out, o = [], 0
    for b in n_out:
        out.append(res[o:o + b])
        o += b
    return out


def _proj_part(x, mods, gain, w, T):
    n = x.shape[0]
    per_req = T // PROJ_TILE if T >= PROJ_TILE else None
    if per_req is None:
        reqs = PROJ_TILE // T
        mod_map = lambda i: (0, 0, 0)
        assert mods.shape[0] == 1 and reqs >= 1
    else:
        mod_map = lambda i: (i // per_req, 0, 0)
    n_ct = D_LRU // LANES
    return dict(
        kernel=_proj_kernel,
        steps=n // PROJ_TILE,
        args=[x, mods, gain, w],
        in_specs=[pl.BlockSpec((PROJ_TILE, D_MODEL), lambda i: (i, 0)),
                  pl.BlockSpec((1, N_MOD, D_MODEL), mod_map),
                  pl.BlockSpec((1, D_MODEL), lambda i: (0, 0)),
                  pl.BlockSpec((None, D_MODEL, D_IN), lambda i: (0, 0, 0), pipeline_mode=pl.Buffered(1))],
        out_specs=[pl.BlockSpec((n_ct, PROJ_TILE, LANES), lambda i: (0, i, 0)),
                   pl.BlockSpec((PROJ_TILE, D_LRU), lambda i: (i, 0)),
                   pl.BlockSpec((N_HG_IN, PROJ_TILE, D_HGRN), lambda i: (0, i, 0))],
        out_shape=[jax.ShapeDtypeStruct((n_ct, n, LANES), F32),
                   jax.ShapeDtypeStruct((n, D_LRU), F32),
                   jax.ShapeDtypeStruct((N_HG_IN, n, D_HGRN), F32)],
        scratch=[],
    )


def _lru_kernel(T, row_len, unroll, xr_ref, cw_ref, cb_ref, w_ref, lam_ref, h0f_ref, h0b_ref,
                hs_ref, lf_ref, lb_ref, xpad, af, uf, ab, ub, hf_s, hb_s):
    pitch = T + PITCH_PAD
    pos = lax.broadcasted_iota(jnp.int32, (T, LANES), 0) & (row_len - 1)
    cw = cw_ref[...]
    w0 = jnp.where(pos >= 2, cw[0:1], 0.0)
    w1 = jnp.where(pos >= 1, cw[1:2], 0.0)
    w2 = jnp.broadcast_to(cw[2:3], (T, LANES))
    w3 = jnp.where(pos <= row_len - 2, cw[3:4], 0.0)
    cb = cb_ref[...]
    nl = -lam_ref[...]
    sp = jnp.maximum(nl, 0.0) + jnp.log1p(jnp.exp(-jnp.abs(nl)))
    c2 = (-0.5 * LRU_C * LOG2E) * sp
    w = w_ref[0]
    ones = (lax.broadcasted_iota(jnp.int32, (T, LANES), 1) < 2).astype(BF16)
    for s in range(2):
        xpad[s, 0:8, :] = jnp.zeros((8, LANES), F32)
        xpad[s, T + 8:T + 16, :] = jnp.zeros((8, LANES), F32)
    for b in range(LRU_GROUP):
        x = xr_ref[b]
        xp = xpad.at[b % 2]
        xp[8:T + 8, :] = x
        xc = w0 * xp[6:T + 6, :] + w1 * xp[7:T + 7, :] + w2 * x + w3 * xp[9:T + 9, :] + cb
        xh = 0.5 * xc
        z = _dot(jnp.concatenate([xc.astype(BF16), ones], axis=1), w)
        rows = pl.ds(b * pitch, T)
        for d, (a_s, u_s) in enumerate(((af, uf), (ab, ub))):
            tr = jnp.tanh(z[:, (2 * d) * LANES:(2 * d + 1) * LANES])
            ti = jnp.tanh(z[:, (2 * d + 1) * LANES:(2 * d + 2) * LANES])
            a = jnp.exp2(c2[d:d + 1] + c2[d:d + 1] * tr)
            om = 1.0 - a * a
            a_s[rows, :] = a
            u_s[rows, :] = (om * lax.rsqrt(jnp.maximum(om, TINY))) * (xh + ti * xh)
        yield

    def step(t, carry):
        hf, hb = carry
        rf = pl.ds(t, LRU_GROUP, stride=pitch)
        hf = af[rf, :] * hf + uf[rf, :]
        hf_s[rf, :] = hf
        rb = pl.ds(T - 1 - t, LRU_GROUP, stride=pitch)
        hb = ab[rb, :] * hb + ub[rb, :]
        hb_s[rb, :] = hb
        return hf, hb

    carry = (h0f_ref[...], h0b_ref[...])
    if unroll is True:
        for t in range(T):
            carry = step(t, carry)
            if t % (T // 8) == T // 8 - 1 and t != T - 1:
                yield
    else:
        carry = lax.fori_loop(0, T, step, carry, unroll=unroll)
    hf, hb = carry
    lf_ref[...] = hf
    lb_ref[...] = hb
    for b in range(LRU_GROUP):
        rows = pl.ds(b * pitch, T)
        hs_ref[b] = hf_s[rows, :] + hb_s[rows, :]


def _lru_part(xr4, conv_w, conv_b, w_tiles, lam, h0f, h0b, T, row_len, unroll):
    n_ct, B = xr4.shape[:2]
    rows = LRU_GROUP * (T + PITCH_PAD)
    vec = pl.BlockSpec((LRU_GROUP, LANES), lambda i: (i // n_ct, i % n_ct))
    slab = pl.BlockSpec((None, LRU_GROUP, T, LANES), lambda i: (i % n_ct, i // n_ct, 0, 0))
    return dict(
        kernel=functools.partial(_lru_kernel, T, row_len, unroll),
        steps=(B // LRU_GROUP) * n_ct,
        args=[xr4, conv_w, conv_b, w_tiles, lam, h0f, h0b],
        in_specs=[slab,
                  pl.BlockSpec((4, LANES), lambda i: (0, i % n_ct)),
                  pl.BlockSpec((1, LANES), lambda i: (0, i % n_ct)),
                  pl.BlockSpec((1, 2 * LANES, 4 * LANES), lambda i: (i % n_ct, 0, 0)),
                  pl.BlockSpec((2, LANES), lambda i: (0, i % n_ct)),
                  vec, vec],
        out_specs=[slab, vec, vec],
        out_shape=[jax.ShapeDtypeStruct((n_ct, B, T, LANES), F32),
                   jax.ShapeDtypeStruct((B, D_LRU), F32),
                   jax.ShapeDtypeStruct((B, D_LRU), F32)],
        scratch=[pltpu.VMEM((2, T + 16, LANES), F32)] + [pltpu.VMEM((rows, LANES), F32)] * 6,
    )


def _hgrn_kernel(T, has_s0, want_state, *refs):
    q_ref, ff_ref, fb_ref, v_ref, og_ref, gam_ref, gain_ref = refs[:7]
    refs = refs[7:]
    if has_s0:
        s0_ref, refs = refs[0], refs[1:]
    y_ref, refs = refs[0], refs[1:]
    if want_state:
        st_ref, refs = refs[0], refs[1:]
    q_s, v_s, kf_s, kb_s, bf_s, bb_s, of_s, ob_s, st_s = refs
    C = HG_CHUNK
    n = T // C
    RB = HG_ROWS
    heads = [slice(h * LANES, (h + 1) * LANES) for h in range(HG_HEADS)]

    ti = lax.broadcasted_iota(jnp.int32, (RB, RB), 0)
    si = lax.broadcasted_iota(jnp.int32, (RB, RB), 1)
    sh = C.bit_length() - 1
    same = (ti >> sh) == (si >> sh)
    tri = ((same & (si <= ti)).astype(BF16), (same & (si >= ti)).astype(BF16))

    for r in range(T // RB):
        rows = slice(r * RB, (r + 1) * RB)
        qr = q_ref[rows, :]
        qh = (0.5 * HG_DK ** -0.5) * qr
        q_s[rows, :] = qh + qh * jnp.tanh(0.5 * qr)
        v_s[rows, :] = v_ref[rows, :].astype(BF16)
        for d, (f_ref, k_s, b_s) in enumerate(((ff_ref, kf_s, bf_s), (fb_ref, kb_s, bb_s))):
            g0 = gam_ref[d, 0:1, :]
            g1 = gam_ref[d, 1:2, :]
            m = jnp.maximum(g0, g1)
            e0 = jnp.exp(g0 - m)
            lbd = e0 / (e0 + jnp.exp(g1 - m))
            ck = 0.5 * (1.0 - lbd)
            pt = ck * jnp.tanh(0.5 * f_ref[rows, :])
            k_s[rows, :] = ck - pt
            lf = jnp.log((lbd + ck) + pt)
            hi = lf.astype(BF16)
            r1 = lf - hi.astype(F32)
            mid = r1.astype(BF16)
            lo = (r1 - mid.astype(F32)).astype(BF16)
            cs = _dot(tri[d], jnp.concatenate([hi, mid, lo], axis=1))
            b_s[rows, :] = cs[:, 0:D_HGRN] + (cs[:, D_HGRN:2 * D_HGRN] + cs[:, 2 * D_HGRN:3 * D_HGRN])

    for d in range(2):
        for h in range(HG_HEADS):
            st_s[d * HG_HEADS + h] = s0_ref[d, h].T if has_s0 else jnp.zeros((LANES, LANES), F32)

    ri =lax.broadcasted_iota(jnp.int32, (C, C), 0)
    ci = lax.broadcasted_iota(jnp.int32, (C, C), 1)

    def body(i, carry):
        for d, (k_s, b_s, o_s) in enumerate(((kf_s, bf_s, of_s), (kb_s, bb_s, ob_s))):
            c = i if d == 0 else n - 1 - i
            sl = pl.ds(pl.multiple_of(c * C, C), C)
            for h in range(HG_HEADS):
                q = q_s[sl, heads[h]]
                k = k_s[sl, heads[h]]
                b = b_s[sl, heads[h]]
                v = v_s[sl, heads[h]]
                bm = b[C // 2:C // 2 + 1, :]
                g = b[C - 1:C, :] if d == 0 else b[0:1, :]
                qi = q * jnp.exp(b - bm)
                qe = (qi * jnp.exp(bm)).astype(BF16)
                ki = k * jnp.exp(bm - b)
                kd = (ki * jnp.exp(g - bm)).astype(BF16)
                s = _dot_nt(qi.astype(BF16), ki.astype(BF16))
                p = jnp.where((ci <= ri) if d == 0 else (ci >= ri), s, 0.0).astype(BF16)
                st = st_s[d * HG_HEADS + h]
                o_s[sl, heads[h]] = _dot(p, v) + _dot_nt(qe, st.astype(BF16))
                st_s[d * HG_HEADS + h] = st * jnp.exp(g) + _dot_tn(v, kd)
        return carry

    lax.fori_loop(0, n, body, 0, unroll=HG_UNROLL)
    if want_state:
        for d in range(2):
            for h in range(HG_HEADS):
                st_ref[d, h] = st_s[d * HG_HEADS + h].T

    for r in range(T // RB):
        rows = slice(r * RB, (r + 1) * RB)
        for h in range(HG_HEADS):
            o = of_s[rows, heads[h]] + ob_s[rows, heads[h]]
            ms = jnp.mean(o * o, axis=-1, keepdims=True)
            oh = 0.5 * og_ref[rows, heads[h]]
            y = o * lax.rsqrt(ms + EPS) * gain_ref[...] * (oh + oh * jnp.tanh(oh))
            y_ref[rows, heads[h]] = y.astype(BF16)


def _hgrn(hg5, gamma, gain, s0, T, want_state):
    B = hg5.shape[1]
    has_s0 = s0 is not None

    def col(k):
        return pl.BlockSpec((None, None, T, D_HGRN), lambda b, k=k: (k, b, 0, 0))

    st_spec = pl.BlockSpec((None, None, 2, HG_HEADS, HG_DK, HG_DK), lambda b: (b, 0, 0, 0, 0, 0))
    in_specs = [col(0), col(1), col(2), col(3), col(4),
                pl.BlockSpec((2, 2, D_HGRN), lambda b: (0, 0, 0)),
                pl.BlockSpec((1, LANES), lambda b: (0, 0))]
    args = [hg5, hg5, hg5, hg5, hg5, gamma, gain]
    if has_s0:
        in_specs.append(st_spec)
        args.append(s0)
    out_specs = [pl.BlockSpec((None, T, D_HGRN), lambda b: (b, 0, 0))]
    out_shape = [jax.ShapeDtypeStruct((B, T, D_HGRN), BF16)]
    if want_state:
        out_specs.append(st_spec)
        out_shape.append(jax.ShapeDtypeStruct((B, 1, 2, HG_HEADS, HG_DK, HG_DK), F32))
    res = pl.pallas_call(
        functools.partial(_hgrn_kernel, T, has_s0, want_state),
        grid=(B,),
        in_specs=in_specs,
        out_specs=out_specs,
        out_shape=out_shape,
        scratch_shapes=[pltpu.VMEM((T, D_HGRN), F32), pltpu.VMEM((T, D_HGRN), BF16)]
        + [pltpu.VMEM((T, D_HGRN), F32)] * 6
        + [pltpu.VMEM((2 * HG_HEADS, HG_DK, HG_DK), F32)],
        compiler_params=_cp(("arbitrary",)),
    )(*args)
    return res if want_state else (res[0], None)


def _mix_out_kernel(hs_ref, gr_ref, yh_ref, x_ref, mod_ref, nl_ref, wo_ref, nf_ref, wr_ref,
                    x1_ref, ha_ref, at_ref):
    wr = wr_ref[...]
    wo = wo_ref[...].astype(BF16)
    g1 = mod_ref[0, 2:3, :]
    sh2 = mod_ref[0, 3:4, :]
    gain2 = nf_ref[...] * (1.0 + mod_ref[0, 4:5, :])
    lane = lax.broadcasted_iota(jnp.int32, (TOK_TILE, LANES), 1)
    for s in range(x_ref.shape[0] // TOK_TILE):
        rows = slice(s * TOK_TILE, (s + 1) * TOK_TILE)
        hs = jnp.concatenate([hs_ref[c, rows, :] for c in range(D_LRU // LANES)], axis=1)
        ms = jnp.mean(hs * hs, axis=-1, keepdims=True)
        gr = gr_ref[rows, :]
        gelu = 0.5 * gr * (1.0 + jnp.tanh(0.7978845608028654 * (gr + 0.044715 * (gr * gr * gr))))
        y_lru = hs * lax.rsqrt(ms + EPS) * nl_ref[...] * gelu
        ycat = jnp.concatenate([y_lru.astype(BF16), yh_ref[rows, :]], axis=-1)
        x1 = x_ref[rows, :] + g1 * _dot(ycat, wo)
        x1_ref[rows, :] = x1
        yield
        ms = jnp.mean(x1 * x1, axis=-1, keepdims=True)
        h2 = x1 * lax.rsqrt(ms + EPS) * gain2 + sh2
        h2_hi = h2.astype(BF16)
        h2_mid = (h2 - h2_hi.astype(F32)).astype(BF16)
        p_hi = _dot(h2_hi, wr)
        logits = p_hi[:, 0:LANES] + (p_hi[:, LANES:2 * LANES] + _dot(h2_mid, wr)[:, 0:LANES])
        logits = jnp.where(lane < N_EXPERTS, logits, -jnp.inf)
        e = jnp.exp(logits - jnp.max(logits, axis=-1, keepdims=True))
        aff = e / jnp.sum(e, axis=-1, keepdims=True)
        w = at_ref.shape[1]
        r, c = (s * TOK_TILE) // w, (s * TOK_TILE) % w
        at_ref[r * N_EXPERTS:(r + 1) * N_EXPERTS, c:c + TOK_TILE] = aff.T[0:N_EXPERTS, :]
        hi = aff.astype(BF16).astype(F32)
        r1 = aff - hi
        mid = r1.astype(BF16).astype(F32)
        lo = (r1 - mid).astype(BF16).astype(F32)
        split = hi + pltpu.roll(mid, N_EXPERTS, 1) + pltpu.roll(lo, 2 * N_EXPERTS, 1)
        ha_ref[rows, 0:D_MODEL] = h2_hi
        ha_ref[rows, D_MODEL:D_AUG] = split.astype(BF16)
        yield


def _mix_out_part(hs4, gr, yh, x, mods, norm_lru, w_out, norm_ffn, wr_pad, T, tile_rows):
    n = x.shape[0]
    assert T % TOK_TILE == 0 and (tile_rows % T == 0 or T % tile_rows == 0)
    q = max(T // tile_rows, 1)
    at_rows = N_EXPERTS * max(tile_rows // T, 1)
    at_spec = pl.BlockSpec((at_rows, min(T, tile_rows)), lambda i: (i // q, i % q))
    mod_map = (lambda i: (i // q, 0, 0)) if mods.shape[0] > 1 else (lambda i: (0, 0, 0))
    tile = lambda w: pl.BlockSpec((tile_rows, w), lambda i: (i, 0))
    const = lambda r, w: pl.BlockSpec((r, w), lambda i: (0, 0))
    return dict(
        kernel=_mix_out_kernel,
        steps=n // tile_rows,
        args=[hs4, gr, yh, x, mods, norm_lru, w_out, norm_ffn, wr_pad],
        in_specs=[pl.BlockSpec((D_LRU // LANES, tile_rows, LANES), lambda i: (0, i, 0)),
                  tile(D_LRU), tile(D_HGRN), tile(D_MODEL),
                  pl.BlockSpec((1, N_MOD, D_MODEL), mod_map),
                  const(1, D_LRU), pl.BlockSpec((None, D_MODEL, D_MODEL), lambda i: (0, 0, 0)),
                  const(1, D_MODEL), const(D_MODEL, 2 * LANES)],
        out_specs=[tile(D_MODEL), tile(D_AUG), at_spec],
        out_shape=[jax.ShapeDtypeStruct((n, D_MODEL), F32),
                   jax.ShapeDtypeStruct((n, D_AUG), BF16),
                   jax.ShapeDtypeStruct((n // T * N_EXPERTS, T), F32)],
        scratch=[],
    )


def _select_kernel(cap, at_ref, slot_ref):
    a = at_ref[...]
    R, T = a.shape
    ones = jnp.ones((T, LANES), BF16)
    su = (lax.broadcasted_iota(jnp.int32, (T, T), 0)
          < lax.broadcasted_iota(jnp.int32, (T, T), 1)).astype(BF16)

    def wide(x):
        return jnp.concatenate([x] * (T // LANES), axis=1)

    def count(mask):
        return _dot(mask.astype(BF16), ones)

    lo = jnp.zeros((R, LANES), F32)
    hi = jnp.full((R, LANES), 2.0, F32)
    for _ in range(SEL_BISECT):
        mid = 0.5 * (lo + hi)
        ge = count(a >= wide(mid)) >= cap
        lo = jnp.where(ge, mid, lo)
        hi = jnp.where(ge, hi, mid)

    def cond(st):
        return (st[0] < T) & (jnp.min(st[2]) < cap)

    def body(st):
        it, cur, n, thr = st
        m = jnp.max(jnp.where(a < wide(cur), a, -1.0), axis=1, keepdims=True)
        m = jnp.broadcast_to(m, (R, LANES))
        c = count(a >= wide(m))
        act = n < cap
        return it + 1, jnp.where(act, m, cur), jnp.where(act, c, n), jnp.where(act, m, thr)

    _, _, _, thr = lax.while_loop(cond, body, (jnp.int32(0), hi, count(a >= wide(hi)), lo))
    thr_w = wide(thr)
    gt = a > thr_w
    eq = a == thr_w
    need = wide(cap - count(gt))
    sel = gt | (eq & (_dot(eq.astype(BF16), su) < need))
    slot_ref[...] = jnp.where(sel, _dot(sel.astype(BF16), su), -1.0).astype(jnp.int32)


def _select(at, T):
    cap = EC_CAPACITY_FACTOR * T // N_EXPERTS
    R = at.shape[0]
    return pl.pallas_call(
        functools.partial(_select_kernel, cap),
        grid=(1,),
        in_specs=[pl.BlockSpec((R, T), lambda i: (0, 0))],
        out_specs=pl.BlockSpec((R, T), lambda i: (0, 0)),
        out_shape=jax.ShapeDtypeStruct((R, T), jnp.int32),
        compiler_params=_cp(("arbitrary",)),
    )(at)


def _onehot(cap, slot_ref, g):
    j = lax.broadcasted_iota(jnp.int32, (cap, slot_ref.shape[1]), 0)
    r0 = g * N_EXPERTS
    return jnp.concatenate([(slot_ref[r0 + e:r0 + e + 1, :] == j).astype(BF16) for e in range(N_EXPERTS)], axis=0)


def _dispatch_kernel(T, cap, slot_ref, ha_ref, xd_ref):
    for g in range(ha_ref.shape[0] // T):
        rows = _dot(_onehot(cap, slot_ref, g), ha_ref[g * T:(g + 1) * T, :])
        for e in range(N_EXPERTS):
            xd_ref[e, g * cap:(g + 1) * cap, :] = rows[e * cap:(e + 1) * cap].astype(BF16)
        yield


def _dispatch_part(slot, ha, T, G):
    cap = EC_CAPACITY_FACTOR * T // N_EXPERTS
    B = ha.shape[0] // T
    return dict(
        kernel=functools.partial(_dispatch_kernel, T, cap),
        steps=B // G,
        args=[slot, ha],
        in_specs=[pl.BlockSpec((G * N_EXPERTS, T), lambda b: (b, 0)),
                  pl.BlockSpec((G * T, D_AUG), lambda b: (b, 0))],
        out_specs=[pl.BlockSpec((N_EXPERTS, G * cap, D_AUG), lambda b: (0, b, 0))],
        out_shape=[jax.ShapeDtypeStruct((N_EXPERTS, B * cap, D_AUG), BF16)],
        scratch=[],
    )


def _ffn_kernel(xp_ref, xs_ref, wg_ref, wu_ref, wd_ref, yp_ref, ys_ref):
    e = pl.program_id(0)
    lane = lax.broadcasted_iota(jnp.int32, (FFN_TILE, LANES), 1)
    pick = (lane == e) | (lane == e + N_EXPERTS) | (lane == e + 2 * N_EXPERTS)
    wg = wg_ref[...].astype(BF16)
    wu = wu_ref[...].astype(BF16)
    wd = wd_ref[...].astype(BF16)
    for x_ref, y_ref in ((xp_ref, yp_ref), (xs_ref, ys_ref)):
        x = x_ref[:, 0:D_MODEL]
        gate = jnp.sum(jnp.where(pick, x_ref[:, D_MODEL:D_AUG].astype(F32), 0.0), axis=-1, keepdims=True)
        hg = _dot(x, wg)
        hid = (hg * _sigmoid(hg)) * _dot(x, wu)
        y_ref[...] = (_dot(hid.astype(BF16), wd) * gate).astype(BF16)


def _ffn(xd_p, xd_s, w_gate, w_up, w_down):
    rows = xd_p.shape[1]
    assert xd_s.shape[1] == rows and rows % FFN_TILE == 0
    xspec = pl.BlockSpec((None, FFN_TILE, D_AUG), lambda e, m: (e, m, 0))
    wspec = pl.BlockSpec((None, None, D_MODEL, D_MODEL), lambda e, m: (0, e, 0, 0))
    yspec = pl.BlockSpec((None, FFN_TILE, D_MODEL), lambda e, m: (e, m, 0))
    yshape = jax.ShapeDtypeStruct((N_EXPERTS, rows, D_MODEL), BF16)
    return pl.pallas_call(
        _ffn_kernel,
        grid=(N_EXPERTS, rows // FFN_TILE),
        in_specs=[xspec, xspec, wspec, wspec, wspec],
        out_specs=[yspec, yspec],
        out_shape=[yshape, yshape],
        compiler_params=_cp(("arbitrary", "arbitrary")),
    )(xd_p, xd_s, w_gate, w_up, w_down)


def _combine_kernel(cap, slot_ref, yd_ref, x1_ref, mod_ref, nf_ref, o_ref):
    W = slot_ref.shape[1]
    for g in range(x1_ref.shape[0] // W):
        rows = slice(g * W, (g + 1) * W)
        yd = jnp.concatenate([yd_ref[e, g * cap:(g + 1) * cap, :] for e in range(N_EXPERTS)], axis=0)
        acc = _dot_tn(_onehot(cap, slot_ref, g), yd)
        x2 = x1_ref[rows, :] + mod_ref[0, 5:6, :] * acc
        ms = jnp.mean(x2 * x2, axis=-1, keepdims=True)
        o_ref[rows, :] = x2 * lax.rsqrt(ms + EPS) * nf_ref[...]
        yield


def _combine_part(slot, yd, x1, mods, norm_final, T, tok):
    cap = EC_CAPACITY_FACTOR * T // N_EXPERTS
    n = x1.shape[0]
    G, h = (tok // T, 1) if tok >= T else (1, T // tok)
    assert G == 1 or mods.shape[0] == 1
    mod_map = (lambda i: (i // h, 0, 0)) if mods.shape[0] > 1 else (lambda i: (0, 0, 0))
    return dict(
        kernel=functools.partial(_combine_kernel, cap),
        steps=n // tok,
        args=[slot, yd, x1, mods, norm_final],
        in_specs=[pl.BlockSpec((G * N_EXPERTS, min(T, tok)), lambda i: (i // h, i % h)),
                  pl.BlockSpec((N_EXPERTS, G * cap, D_MODEL), lambda i: (0, i // h, 0)),
                  pl.BlockSpec((tok, D_MODEL), lambda i: (i, 0)),
                  pl.BlockSpec((1, N_MOD, D_MODEL), mod_map),
                  pl.BlockSpec((1, D_MODEL), lambda i: (0, 0))],
        out_specs=[pl.BlockSpec((tok, D_MODEL), lambda i: (i, 0))],
        out_shape=[jax.ShapeDtypeStruct((n, D_MODEL), F32)],
        scratch=[],
    )


def _gate_tiles(wa, wx, ba, bx):
    n_ct = D_LRU // LANES

    def dense(w):
        w2 = w.reshape(n_ct, 2, LRU_BW, LRU_BW)
        z = jnp.zeros((n_ct, LRU_BW, LRU_BW), w.dtype)
        top = jnp.concatenate([w2[:, 0], z], axis=2)
        bot = jnp.concatenate([z, w2[:, 1]], axis=2)
        return jnp.concatenate([top, bot], axis=1)

    tiles = 0.5 * jnp.concatenate([dense(wa[0]), dense(wx[0]), dense(wa[1]), dense(wx[1])], axis=-1)
    t = lambda v: v.reshape(n_ct, 1, LANES)
    bias = 0.5 * jnp.concatenate([t(ba[0]), t(bx[0]), t(ba[1]), t(bx[1])], axis=-1)
    hi = bias.astype(BF16)
    lo = (bias - hi.astype(F32)).astype(BF16)
    pad = jnp.zeros((n_ct, LANES - 2, 4 * LANES), BF16)
    return jnp.concatenate([tiles.astype(BF16), hi, lo, pad], axis=1)


def _router_split(w):
    wp = jnp.pad(w, ((0, 0), (0, LANES - N_EXPERTS)))
    hi = wp.astype(BF16)
    mid = (wp - hi.astype(F32)).astype(BF16)
    return jnp.concatenate([hi, mid], axis=1)


def _proj(x, mods, p, T):
    return _proj_part(x.reshape(-1, D_MODEL), mods, p["norm_mix"], p["w_in"], T)


def _lru(xr4, B, T, row_len, h0f, h0b, p, unroll):
    n_ct = D_LRU // LANES
    return _lru_part(xr4.reshape(n_ct, B, T, LANES), p["conv_w"], p["conv_b"], p["gate_w"], p["lam"],
                     h0f, h0b, T, row_len, unroll)


def _mix(x, mods, p, T, hs4, gr, yh, tile_rows):
    B = x.shape[0]
    return _mix_out_part(hs4.reshape(D_LRU // LANES, B * T, LANES), gr, yh.reshape(B * T, D_HGRN),
                         x.reshape(B * T, D_MODEL), mods,
                         p["norm_lru"], p["w_out"], p["norm_ffn"], p["w_router"], T, tile_rows)


def kernel(x_prompt, x_sample, state_lru, state_hgrn, c, c_ctx, w_ada, b_ada, norm_mix, w_in, conv_w,
           conv_b, lru_wa, lru_ba, lru_wx, lru_bx, lru_lambda, norm_lru, hgrn_gamma, norm_hgrn, w_out,
           norm_ffn, w_router, w_gate, w_up, w_down, norm_final):
    assert w_ada.shape[0] == 1 and hgrn_gamma.shape[1] == 2, "one trunk layer"
    l = 0
    Bp, Tp, _ = x_prompt.shape
    Bs, Ts, _ = x_sample.shape

    cond = jnp.concatenate([c_ctx[None, :], c, jnp.zeros((16 - 1 - Bs, D_MODEL), F32)], axis=0)
    mods = _adaln(cond, w_ada, b_ada).reshape(16, N_MOD, D_MODEL)
    mods_p, mods_s = mods[0:1], mods[1:1 + Bs]

    p = {
        "norm_mix": norm_mix[l][None, :], "w_in": w_in,
        "conv_w": conv_w[l], "conv_b": conv_b[l][None, :],
        "gate_w": _gate_tiles(lru_wa[l], lru_wx[l], lru_ba[l], lru_bx[l]),
        "lam": lru_lambda[l], "norm_lru": norm_lru[l][None, :],
        "gamma": hgrn_gamma, "norm_hgrn": norm_hgrn[l][None, :],
        "w_out": w_out, "norm_ffn": norm_ffn[l][None, :],
        "w_router": _router_split(w_router[l]),
    }
    zeros_p = jnp.zeros((Bp, D_LRU), F32)
    ((xr4_p, gr_p, hg5_p),) = _run([_proj(x_prompt, mods_p, p, Tp)])
    (xr4_s, gr_s, hg5_s), (hs4_p, last_f, last_b) = _run(
        [_proj(x_sample, mods_s, p, Ts), _lru(xr4_p, Bp, Tp, Tp, zeros_p, zeros_p, p, True)])
    yh_p, st_p = _hgrn(hg5_p.reshape(N_HG_IN, Bp, Tp, D_HGRN), p["gamma"], p["norm_hgrn"], None, Tp, True)
    ((x1p, ha_p, at_p),) = _run([_mix(x_prompt, mods_p, p, Tp, hs4_p, gr_p, yh_p, MIX_TILE)])
    slot_p = _select(at_p, Tp)
    ((hs4_s, _, _),) = _run([_lru(xr4_s, Bs, Ts, GRID_W, state_lru[:, l, 0], state_lru[:, l, 1], p, 8)])
    yh_s, _ = _hgrn(hg5_s.reshape(N_HG_IN, Bs, Ts, D_HGRN), p["gamma"], p["norm_hgrn"], state_hgrn, Ts, False)
    fused_tile = MIX_TILE // 2
    (x1s, ha_s, at_s), (xd_p,) = _run(
        [_mix(x_sample, mods_s, p, Ts, hs4_s, gr_s, yh_s, fused_tile),
         _dispatch_part(slot_p, ha_p, Tp, fused_tile // Tp)])
    slot_s = _select(at_s, Ts)
    ((xd_s,),) = _run([_dispatch_part(slot_s, ha_s, Ts, max(ROUTE_TOKENS // Ts, 1))])

    yd_p, yd_s = _ffn(xd_p, xd_s, w_gate, w_up, w_down)
    nf = norm_final[None, :]
    (y_sample,), (y_prompt,) = _run(
        [_combine_part(slot_s, yd_s, x1s, mods_s, nf, Ts, COMBINE_TOKENS),
         _combine_part(slot_p, yd_p, x1p, mods_p, nf, Tp, COMBINE_TOKENS)])
    y_prompt = y_prompt.reshape(Bp, Tp, D_MODEL)
    y_sample = y_sample.reshape(Bs, Ts, D_MODEL)
    new_state_lru = jnp.stack([last_f, last_b], axis=1)[:, None]
    new_state_hgrn = st_p
    return (y_prompt, y_sample, new_state_lru, new_state_hgrn)
```

```python
import functools

import jax
import jax.numpy as jnp
from jax import lax
from jax.experimental import pallas as pl
from jax.experimental.pallas import tpu as pltpu

F32 = jnp.float32
BF16 = jnp.bfloat16

D_MODEL = 1024
D_LRU = 512
D_HGRN = 512
HG_HEADS = 4
HG_DK = 128
LRU_BLOCKS = 8
LRU_BW = 64
LRU_C = 8.0
N_EXPERTS = 16
EC_CAPACITY_FACTOR = 2
N_MOD = 6
D_IN = 7 * 512
N_HG_IN = 5
GRID_W = 64
EPS = 1e-6
LOG2E = 1.4426950408889634
TINY = 1e-37

LANES = 128
TOK_TILE = 256
PROJ_TILE = 512
MIX_TILE = 1024
LRU_GROUP = 8
PITCH_PAD = 8
HG_CHUNK = 64
HG_UNROLL = 16
HG_ROWS = 256
FFN_TILE = 512
ROUTE_TOKENS = 1024
COMBINE_TOKENS = 512
SEL_BISECT = 20
D_AUG = D_MODEL + LANES
VMEM_LIMIT = 56 * 1024 * 1024


def _cp(sem, vmem=VMEM_LIMIT):
    return pltpu.CompilerParams(dimension_semantics=sem, vmem_limit_bytes=vmem)


def _sigmoid(x):
    return 0.5 * (1.0 + jnp.tanh(0.5 * x))


def _dot(a, b):
    return jnp.dot(a, b, preferred_element_type=F32)


def _dot_nt(a, b):
    return lax.dot_general(a, b, (((1,), (1,)), ((), ())), preferred_element_type=F32)


def _dot_tn(a, b):
    return lax.dot_general(a, b, (((0,), (0,)), ((), ())), preferred_element_type=F32)


def _adaln_kernel(c_ref, w_ref, b_ref, o_ref):
    c = c_ref[...]
    s = (c * _sigmoid(c)).astype(BF16)
    o_ref[...] = _dot(s, w_ref[...].astype(BF16)) + b_ref[...]
    yield


def _adaln_part(cond, w, b, col0, ncols, tn):
    n = cond.shape[0]
    j0 = col0 // tn
    return dict(
        kernel=_adaln_kernel,
        steps=ncols // tn,
        args=[cond, w, b],
        in_specs=[pl.BlockSpec((n, D_MODEL), lambda j: (0, 0)),
                  pl.BlockSpec((None, D_MODEL, tn), lambda j: (0, 0, j0 + j)),
                  pl.BlockSpec((1, tn), lambda j: (0, j0 + j))],
        out_specs=[pl.BlockSpec((n, tn), lambda j: (0, j))],
        out_shape=[jax.ShapeDtypeStruct((n, ncols), F32)],
        scratch=[],
    )


def _proj_kernel(x_ref, mod_ref, g_ref, w_ref, xr_ref, gr_ref, hg_ref):
    x = x_ref[...]
    ms = jnp.mean(x * x, axis=-1, keepdims=True)
    y = x * lax.rsqrt(ms + EPS) * g_ref[...]
    h = y * (1.0 + mod_ref[0, 1:2, :]) + mod_ref[0, 0:1, :]
    h = h.astype(BF16)
    W = 2 * LANES
    for j in range(D_IN // W):
        res = _dot(h, w_ref[:, j * W:(j + 1) * W].astype(BF16))
        k, half = divmod(j * W, D_HGRN)
        if k == 0:
            for c in range(W // LANES):
                xr_ref[half // LANES + c] = res[:, c * LANES:(c + 1) * LANES]
        elif k == 1:
            gr_ref[:, half:half + W] = res
        else:
            hg_ref[k - 2, :, half:half + W] = res
        yield


def _run(parts):
    steps = parts[0]["steps"]
    assert all(p["steps"] == steps for p in parts)
    n_in = [len(p["args"]) for p in parts]
    n_out = [len(p["out_shape"]) for p in parts]
    n_sc = [len(p["scratch"]) for p in parts]

    def body(*refs):
        ins, outs, scr = refs[:sum(n_in)], refs[sum(n_in):sum(n_in) + sum(n_out)], refs[sum(n_in) + sum(n_out):]
        i = o = s = 0
        gens = []
        for p, a, b, c in zip(parts, n_in, n_out, n_sc):
            gens.append(p["kernel"](*ins[i:i + a], *outs[o:o + b], *scr[s:s + c]))
            i, o, s = i + a, o + b, s + c
        while gens:
            for g in list(gens):
                if next(g, StopIteration) is StopIteration:
                    gens.remove(g)

    res = pl.pallas_call(
        body,
        grid=(steps,),
        in_specs=[s for p in parts for s in p["in_specs"]],
        out_specs=[s for p in parts for s in p["out_specs"]],
        out_shape=[s for p in parts for s in p["out_shape"]],
        scratch_shapes=[s for p in parts for s in p["scratch"]],
        compiler_params=_cp(("arbitrary",)),
    )(*[a for p in parts for a in p["args"]])
    out, o = [], 0
    for b in n_out:
        out.append(res[o:o + b])
        o += b
    return out


def _proj_part(x, mods, gain, w, T):
    n = x.shape[0]
    per_req = T // PROJ_TILE if T >= PROJ_TILE else None
    if per_req is None:
        reqs = PROJ_TILE // T
        mod_map = lambda i: (0, 0, 0)
        assert mods.shape[0] == 1 and reqs >= 1
    else:
        mod_map = lambda i: (i // per_req, 0, 0)
    n_ct = D_LRU // LANES
    return dict(
        kernel=_proj_kernel,
        steps=n // PROJ_TILE,
        args=[x, mods, gain, w],
        in_specs=[pl.BlockSpec((PROJ_TILE, D_MODEL), lambda i: (i, 0)),
                  pl.BlockSpec((1, mods.shape[1], D_MODEL), mod_map),
                  pl.BlockSpec((1, D_MODEL), lambda i: (0, 0)),
                  pl.BlockSpec((None, D_MODEL, D_IN), lambda i: (0, 0, 0), pipeline_mode=pl.Buffered(1))],
        out_specs=[pl.BlockSpec((n_ct, PROJ_TILE, LANES), lambda i: (0, i, 0)),
                   pl.BlockSpec((PROJ_TILE, D_LRU), lambda i: (i, 0)),
                   pl.BlockSpec((N_HG_IN, PROJ_TILE, D_HGRN), lambda i: (0, i, 0))],
        out_shape=[jax.ShapeDtypeStruct((n_ct, n, LANES), F32),
                   jax.ShapeDtypeStruct((n, D_LRU), F32),
                   jax.ShapeDtypeStruct((N_HG_IN, n, D_HGRN), F32)],
        scratch=[],
    )


def _lru_kernel(T, row_len, unroll, xr_ref, cw_ref, cb_ref, w_ref, lam_ref, h0f_ref, h0b_ref,
                hs_ref, lf_ref, lb_ref, xpad, af, uf, ab, ub, hf_s, hb_s):
    pitch = T + PITCH_PAD
    pos = lax.broadcasted_iota(jnp.int32, (T, LANES), 0) & (row_len - 1)
    cw = cw_ref[...]
    w0 = jnp.where(pos >= 2, cw[0:1], 0.0)
    w1 = jnp.where(pos >= 1, cw[1:2], 0.0)
    w2 = jnp.broadcast_to(cw[2:3], (T, LANES))
    w3 = jnp.where(pos <= row_len - 2, cw[3:4], 0.0)
    cb = cb_ref[...]
    nl = -lam_ref[...]
    sp = jnp.maximum(nl, 0.0) + jnp.log1p(jnp.exp(-jnp.abs(nl)))
    c2 = (-0.5 * LRU_C * LOG2E) * sp
    w = w_ref[0]
    ones = (lax.broadcasted_iota(jnp.int32, (T, LANES), 1) < 2).astype(BF16)
    for s in range(2):
        xpad[s, 0:8, :] = jnp.zeros((8, LANES), F32)
        xpad[s, T + 8:T + 16, :] = jnp.zeros((8, LANES), F32)
    for b in range(LRU_GROUP):
        x = xr_ref[b]
        xp = xpad.at[b % 2]
        xp[8:T + 8, :] = x
        xc = w0 * xp[6:T + 6, :] + w1 * xp[7:T + 7, :] + w2 * x + w3 * xp[9:T + 9, :] + cb
        xh = 0.5 * xc
        z = _dot(jnp.concatenate([xc.astype(BF16), ones], axis=1), w)
        rows = pl.ds(b * pitch, T)
        for d, (a_s, u_s) in enumerate(((af, uf), (ab, ub))):
            tr = jnp.tanh(z[:, (2 * d) * LANES:(2 * d + 1) * LANES])
            ti = jnp.tanh(z[:, (2 * d + 1) * LANES:(2 * d + 2) * LANES])
            a = jnp.exp2(c2[d:d + 1] + c2[d:d + 1] * tr)
            om = 1.0 - a * a
            a_s[rows, :] = a
            u_s[rows, :] = (om * lax.rsqrt(jnp.maximum(om, TINY))) * (xh + ti * xh)
        yield

    def step(t, carry):
        hf, hb = carry
        rf = pl.ds(t, LRU_GROUP, stride=pitch)
        hf = af[rf, :] * hf + uf[rf, :]
        hf_s[rf, :] = hf
        rb = pl.ds(T - 1 - t, LRU_GROUP, stride=pitch)
        hb = ab[rb, :] * hb + ub[rb, :]
        hb_s[rb, :] = hb
        return hf, hb

    carry = (h0f_ref[...], h0b_ref[...])
    if unroll is True:
        for t in range(T):
            carry = step(t, carry)
            if t % (T // 8) == T // 8 - 1 and t != T - 1:
                yield
    else:
        carry = lax.fori_loop(0, T, step, carry, unroll=unroll)
    hf, hb = carry
    lf_ref[...] = hf
    lb_ref[...] = hb
    for b in range(LRU_GROUP):
        rows = pl.ds(b * pitch, T)
        hs_ref[b] = hf_s[rows, :] + hb_s[rows, :]


def _lru_part(xr4, conv_w, conv_b, w_tiles, lam, h0f, h0b, T, row_len, unroll):
    n_ct, B = xr4.shape[:2]
    rows = LRU_GROUP * (T + PITCH_PAD)
    vec = pl.BlockSpec((LRU_GROUP, LANES), lambda i: (i // n_ct, i % n_ct))
    slab = pl.BlockSpec((None, LRU_GROUP, T, LANES), lambda i: (i % n_ct, i // n_ct, 0, 0))
    return dict(
        kernel=functools.partial(_lru_kernel, T, row_len, unroll),
        steps=(B // LRU_GROUP) * n_ct,
        args=[xr4, conv_w, conv_b, w_tiles, lam, h0f, h0b],
        in_specs=[slab,
                  pl.BlockSpec((4, LANES), lambda i: (0, i % n_ct)),
                  pl.BlockSpec((1, LANES), lambda i: (0, i % n_ct)),
                  pl.BlockSpec((1, 2 * LANES, 4 * LANES), lambda i: (i % n_ct, 0, 0)),
                  pl.BlockSpec((2, LANES), lambda i: (0, i % n_ct)),
                  vec, vec],
        out_specs=[slab, vec, vec],
        out_shape=[jax.ShapeDtypeStruct((n_ct, B, T, LANES), F32),
                   jax.ShapeDtypeStruct((B, D_LRU), F32),
                   jax.ShapeDtypeStruct((B, D_LRU), F32)],
        scratch=[pltpu.VMEM((2, T + 16, LANES), F32)] + [pltpu.VMEM((rows, LANES), F32)] * 6,
    )


def _hgrn_kernel(T, has_s0, want_state, *refs):
    q_ref, ff_ref, fb_ref, v_ref, og_ref, gam_ref, gain_ref = refs[:7]
    refs = refs[7:]
    if has_s0:
        s0_ref, refs = refs[0], refs[1:]
    y_ref, refs = refs[0], refs[1:]
    if want_state:
        st_ref, refs = refs[0], refs[1:]
    q_s, v_s, kf_s, kb_s, bf_s, bb_s, of_s, ob_s, st_s = refs
    C = HG_CHUNK
    n = T // C
    RB = HG_ROWS
    heads = [slice(h * LANES, (h + 1) * LANES) for h in range(HG_HEADS)]

    ti = lax.broadcasted_iota(jnp.int32, (RB, RB), 0)
    si = lax.broadcasted_iota(jnp.int32, (RB, RB), 1)
    sh = C.bit_length() - 1
    same = (ti >> sh) == (si >> sh)
    tri = ((same & (si <= ti)).astype(BF16), (same & (si >= ti)).astype(BF16))

    for r in range(T // RB):
        rows = slice(r * RB, (r + 1) * RB)
        qr = q_ref[rows, :]
        qh = (0.5 * HG_DK ** -0.5) * qr
        q_s[rows, :] = qh + qh * jnp.tanh(0.5 * qr)
        v_s[rows, :] = v_ref[rows, :].astype(BF16)
        for d, (f_ref, k_s, b_s) in enumerate(((ff_ref, kf_s, bf_s), (fb_ref, kb_s, bb_s))):
            g0 = gam_ref[d, 0:1, :]
            g1 = gam_ref[d, 1:2, :]
            m = jnp.maximum(g0, g1)
            e0 = jnp.exp(g0 - m)
            lbd = e0 / (e0 + jnp.exp(g1 - m))
            ck = 0.5 * (1.0 - lbd)
            pt = ck * jnp.tanh(0.5 * f_ref[rows, :])
            k_s[rows, :] = ck - pt
            lf = jnp.log((lbd + ck) + pt)
            hi = lf.astype(BF16)
            r1 = lf - hi.astype(F32)
            mid = r1.astype(BF16)
            lo = (r1 - mid.astype(F32)).astype(BF16)
            cs = _dot(tri[d], jnp.concatenate([hi, mid, lo], axis=1))
            b_s[rows, :] = cs[:, 0:D_HGRN] + (cs[:, D_HGRN:2 * D_HGRN] + cs[:, 2 * D_HGRN:3 * D_HGRN])

    for d in range(2):
        for h in range(HG_HEADS):
            st_s[d * HG_HEADS + h] = s0_ref[d, h].T if has_s0 else jnp.zeros((LANES, LANES), F32)

    ri =lax.broadcasted_iota(jnp.int32, (C, C), 0)
    ci = lax.broadcasted_iota(jnp.int32, (C, C), 1)

    def body(i, carry):
        for d, (k_s, b_s, o_s) in enumerate(((kf_s, bf_s, of_s), (kb_s, bb_s, ob_s))):
            c = i if d == 0 else n - 1 - i
            sl = pl.ds(pl.multiple_of(c * C, C), C)
            for h in range(HG_HEADS):
                q = q_s[sl, heads[h]]
                k = k_s[sl, heads[h]]
                b = b_s[sl, heads[h]]
                v = v_s[sl, heads[h]]
                bm = b[C // 2:C // 2 + 1, :]
                g = b[C - 1:C, :] if d == 0 else b[0:1, :]
                qi = q * jnp.exp(b - bm)
                qe = (qi * jnp.exp(bm)).astype(BF16)
                ki = k * jnp.exp(bm - b)
                kd = (ki * jnp.exp(g - bm)).astype(BF16)
                s = _dot_nt(qi.astype(BF16), ki.astype(BF16))
                p = jnp.where((ci <= ri) if d == 0 else (ci >= ri), s, 0.0).astype(BF16)
                st = st_s[d * HG_HEADS + h]
                o_s[sl, heads[h]] = _dot(p, v) + _dot_nt(qe, st.astype(BF16))
                st_s[d * HG_HEADS + h] = st * jnp.exp(g) + _dot_tn(v, kd)
        return carry

    lax.fori_loop(0, n, body, 0, unroll=HG_UNROLL)
    if want_state:
        for d in range(2):
            for h in range(HG_HEADS):
                st_ref[d, h] = st_s[d * HG_HEADS + h].T

    for r in range(T // RB):
        rows = slice(r * RB, (r + 1) * RB)
        for h in range(HG_HEADS):
            o = of_s[rows, heads[h]] + ob_s[rows, heads[h]]
            ms = jnp.mean(o * o, axis=-1, keepdims=True)
            oh = 0.5 * og_ref[rows, heads[h]]
            y = o * lax.rsqrt(ms + EPS) * gain_ref[...] * (oh + oh * jnp.tanh(oh))
            y_ref[rows, heads[h]] = y.astype(BF16)


def _hgrn(hg5, gamma, gain, s0, T, want_state):
    B = hg5.shape[1]
    has_s0 = s0 is not None

    def col(k):
        return pl.BlockSpec((None, None, T, D_HGRN), lambda b, k=k: (k, b, 0, 0))

    st_spec = pl.BlockSpec((None, None, 2, HG_HEADS, HG_DK, HG_DK), lambda b: (b, 0, 0, 0, 0, 0))
    in_specs = [col(0), col(1), col(2), col(3), col(4),
                pl.BlockSpec((2, 2, D_HGRN), lambda b: (0, 0, 0)),
                pl.BlockSpec((1, LANES), lambda b: (0, 0))]
    args = [hg5, hg5, hg5, hg5, hg5, gamma, gain]
    if has_s0:
        in_specs.append(st_spec)
        args.append(s0)
    out_specs = [pl.BlockSpec((None, T, D_HGRN), lambda b: (b, 0, 0))]
    out_shape = [jax.ShapeDtypeStruct((B, T, D_HGRN), BF16)]
    if want_state:
        out_specs.append(st_spec)
        out_shape.append(jax.ShapeDtypeStruct((B, 1, 2, HG_HEADS, HG_DK, HG_DK), F32))
    res = pl.pallas_call(
        functools.partial(_hgrn_kernel, T, has_s0, want_state),
        grid=(B,),
        in_specs=in_specs,
        out_specs=out_specs,
        out_shape=out_shape,
        scratch_shapes=[pltpu.VMEM((T, D_HGRN), F32), pltpu.VMEM((T, D_HGRN), BF16)]
        + [pltpu.VMEM((T, D_HGRN), F32)] * 6
        + [pltpu.VMEM((2 * HG_HEADS, HG_DK, HG_DK), F32)],
        compiler_params=_cp(("arbitrary",)),
    )(*args)
    return res if want_state else (res[0], None)


def _mix_out_kernel(hs_ref, gr_ref, yh_ref, x_ref, mod_ref, nl_ref, wo_ref, nf_ref, wr_ref,
                    x1_ref, ha_ref, at_ref):
    wr = wr_ref[...]
    wo = wo_ref[...].astype(BF16)
    g1 = mod_ref[0, 0:1, :]
    sh2 = mod_ref[0, 1:2, :]
    gain2 = nf_ref[...] * (1.0 + mod_ref[0, 2:3, :])
    lane = lax.broadcasted_iota(jnp.int32, (TOK_TILE, LANES), 1)
    for s in range(x_ref.shape[0] // TOK_TILE):
        rows = slice(s * TOK_TILE, (s + 1) * TOK_TILE)
        hs = jnp.concatenate([hs_ref[c, rows, :] for c in range(D_LRU // LANES)], axis=1)
        ms = jnp.mean(hs * hs, axis=-1, keepdims=True)
        gr = gr_ref[rows, :]
        gelu = 0.5 * gr * (1.0 + jnp.tanh(0.7978845608028654 * (gr + 0.044715 * (gr * gr * gr))))
        y_lru = hs * lax.rsqrt(ms + EPS) * nl_ref[...] * gelu
        ycat = jnp.concatenate([y_lru.astype(BF16), yh_ref[rows, :]], axis=-1)
        x1 = x_ref[rows, :] + g1 * _dot(ycat, wo)
        x1_ref[rows, :] = x1
        yield
        ms = jnp.mean(x1 * x1, axis=-1, keepdims=True)
        h2 = x1 * lax.rsqrt(ms + EPS) * gain2 + sh2
        h2_hi = h2.astype(BF16)
        h2_mid = (h2 - h2_hi.astype(F32)).astype(BF16)
        p_hi = _dot(h2_hi, wr)
        logits = p_hi[:, 0:LANES] + (p_hi[:, LANES:2 * LANES] + _dot(h2_mid, wr)[:, 0:LANES])
        logits = jnp.where(lane < N_EXPERTS, logits, -jnp.inf)
        e = jnp.exp(logits - jnp.max(logits, axis=-1, keepdims=True))
        aff = e / jnp.sum(e, axis=-1, keepdims=True)
        w = at_ref.shape[1]
        r, c = (s * TOK_TILE) // w, (s * TOK_TILE) % w
        at_ref[r * N_EXPERTS:(r + 1) * N_EXPERTS, c:c + TOK_TILE] = aff.T[0:N_EXPERTS, :]
        hi = aff.astype(BF16).astype(F32)
        r1 = aff - hi
        mid = r1.astype(BF16).astype(F32)
        lo = (r1 - mid).astype(BF16).astype(F32)
        split = hi + pltpu.roll(mid, N_EXPERTS, 1) + pltpu.roll(lo, 2 * N_EXPERTS, 1)
        ha_ref[rows, 0:D_MODEL] = h2_hi
        ha_ref[rows, D_MODEL:D_AUG] = split.astype(BF16)
        yield


def _mix_out_part(hs4, gr, yh, x, mods, norm_lru, w_out, norm_ffn, wr_pad, T, tile_rows):
    n = x.shape[0]
    assert T % TOK_TILE == 0 and (tile_rows % T == 0 or T % tile_rows == 0)
    q = max(T // tile_rows, 1)
    at_rows = N_EXPERTS * max(tile_rows // T, 1)
    at_spec = pl.BlockSpec((at_rows, min(T, tile_rows)), lambda i: (i // q, i % q))
    mod_map = (lambda i: (i // q, 0, 0)) if mods.shape[0] > 1 else (lambda i: (0, 0, 0))
    tile = lambda w: pl.BlockSpec((tile_rows, w), lambda i: (i, 0))
    const = lambda r, w: pl.BlockSpec((r, w), lambda i: (0, 0))
    return dict(
        kernel=_mix_out_kernel,
        steps=n // tile_rows,
        args=[hs4, gr, yh, x, mods, norm_lru, w_out, norm_ffn, wr_pad],
        in_specs=[pl.BlockSpec((D_LRU // LANES, tile_rows, LANES), lambda i: (0, i, 0)),
                  tile(D_LRU), tile(D_HGRN), tile(D_MODEL),
                  pl.BlockSpec((1, mods.shape[1], D_MODEL), mod_map),
                  const(1, D_LRU), const(D_MODEL, D_MODEL), const(1, D_MODEL), const(D_MODEL, 2 * LANES)],
        out_specs=[tile(D_MODEL), tile(D_AUG), at_spec],
        out_shape=[jax.ShapeDtypeStruct((n, D_MODEL), F32),
                   jax.ShapeDtypeStruct((n, D_AUG), BF16),
                   jax.ShapeDtypeStruct((n // T * N_EXPERTS, T), F32)],
        scratch=[],
    )


def _select_kernel(cap, at_ref, slot_ref):
    a = at_ref[...]
    R, T = a.shape
    ones = jnp.ones((T, LANES), BF16)
    su = (lax.broadcasted_iota(jnp.int32, (T, T), 0)
          < lax.broadcasted_iota(jnp.int32, (T, T), 1)).astype(BF16)

    def wide(x):
        return jnp.concatenate([x] * (T // LANES), axis=1)

    def count(mask):
        return _dot(mask.astype(BF16), ones)

    lo = jnp.zeros((R, LANES), F32)
    hi = jnp.full((R, LANES), 2.0, F32)
    for _ in range(SEL_BISECT):
        mid = 0.5 * (lo + hi)
        ge = count(a >= wide(mid)) >= cap
        lo = jnp.where(ge, mid, lo)
        hi = jnp.where(ge, hi, mid)

    def cond(st):
        return (st[0] < T) & (jnp.min(st[2]) < cap)

    def body(st):
        it, cur, n, thr = st
        m = jnp.max(jnp.where(a < wide(cur), a, -1.0), axis=1, keepdims=True)
        m = jnp.broadcast_to(m, (R, LANES))
        c = count(a >= wide(m))
        act = n < cap
        return it + 1, jnp.where(act, m, cur), jnp.where(act, c, n), jnp.where(act, m, thr)

    _, _, _, thr = lax.while_loop(cond, body, (jnp.int32(0), hi, count(a >= wide(hi)), lo))
    thr_w = wide(thr)
    gt = a > thr_w
    eq = a == thr_w
    need = wide(cap - count(gt))
    sel = gt | (eq & (_dot(eq.astype(BF16), su) < need))
    slot_ref[...] = jnp.where(sel, _dot(sel.astype(BF16), su), -1.0).astype(jnp.int32)


def _select(at, T):
    cap = EC_CAPACITY_FACTOR * T // N_EXPERTS
    R = at.shape[0]
    return pl.pallas_call(
        functools.partial(_select_kernel, cap),
        grid=(1,),
        in_specs=[pl.BlockSpec((R, T), lambda i: (0, 0))],
        out_specs=pl.BlockSpec((R, T), lambda i: (0, 0)),
        out_shape=jax.ShapeDtypeStruct((R, T), jnp.int32),
        compiler_params=_cp(("arbitrary",)),
    )(at)


def _onehot(cap, slot_ref, g):
    j = lax.broadcasted_iota(jnp.int32, (cap, slot_ref.shape[1]), 0)
    r0 = g * N_EXPERTS
    return jnp.concatenate([(slot_ref[r0 + e:r0 + e + 1, :] == j).astype(BF16) for e in range(N_EXPERTS)], axis=0)


def _dispatch_kernel(T, cap, slot_ref, ha_ref, xd_ref):
    for g in range(ha_ref.shape[0] // T):
        rows = _dot(_onehot(cap, slot_ref, g), ha_ref[g * T:(g + 1) * T, :])
        for e in range(N_EXPERTS):
            xd_ref[e, g * cap:(g + 1) * cap, :] = rows[e * cap:(e + 1) * cap].astype(BF16)
        yield


def _dispatch_part(slot, ha, T, G):
    cap = EC_CAPACITY_FACTOR * T // N_EXPERTS
    B = ha.shape[0] // T
    return dict(
        kernel=functools.partial(_dispatch_kernel, T, cap),
        steps=B // G,
        args=[slot, ha],
        in_specs=[pl.BlockSpec((G * N_EXPERTS, T), lambda b: (b, 0)),
                  pl.BlockSpec((G * T, D_AUG), lambda b: (b, 0))],
        out_specs=[pl.BlockSpec((N_EXPERTS, G * cap, D_AUG), lambda b: (0, b, 0))],
        out_shape=[jax.ShapeDtypeStruct((N_EXPERTS, B * cap, D_AUG), BF16)],
        scratch=[],
    )


def _ffn_kernel(xp_ref, xs_ref, wg_ref, wu_ref, wd_ref, yp_ref, ys_ref):
    e = pl.program_id(0)
    lane = lax.broadcasted_iota(jnp.int32, (FFN_TILE, LANES), 1)
    pick = (lane == e) | (lane == e + N_EXPERTS) | (lane == e + 2 * N_EXPERTS)
    wg = wg_ref[...].astype(BF16)
    wu = wu_ref[...].astype(BF16)
    wd = wd_ref[...].astype(BF16)
    for x_ref, y_ref in ((xp_ref, yp_ref), (xs_ref, ys_ref)):
        x = x_ref[:, 0:D_MODEL]
        gate = jnp.sum(jnp.where(pick, x_ref[:, D_MODEL:D_AUG].astype(F32), 0.0), axis=-1, keepdims=True)
        hg = _dot(x, wg)
        hid = (hg * _sigmoid(hg)) * _dot(x, wu)
        y_ref[...] = (_dot(hid.astype(BF16), wd) * gate).astype(BF16)


def _ffn(xd_p, xd_s, w_gate, w_up, w_down):
    rows = xd_p.shape[1]
    assert xd_s.shape[1] == rows and rows % FFN_TILE == 0
    xspec = pl.BlockSpec((None, FFN_TILE, D_AUG), lambda e, m: (e, m, 0))
    wspec = pl.BlockSpec((None, None, D_MODEL, D_MODEL), lambda e, m: (0, e, 0, 0))
    yspec = pl.BlockSpec((None, FFN_TILE, D_MODEL), lambda e, m: (e, m, 0))
    yshape = jax.ShapeDtypeStruct((N_EXPERTS, rows, D_MODEL), BF16)
    return pl.pallas_call(
        _ffn_kernel,
        grid=(N_EXPERTS, rows // FFN_TILE),
        in_specs=[xspec, xspec, wspec, wspec, wspec],
        out_specs=[yspec, yspec],
        out_shape=[yshape, yshape],
        compiler_params=_cp(("arbitrary", "arbitrary")),
    )(xd_p, xd_s, w_gate, w_up, w_down)


def _combine_kernel(cap, slot_ref, yd_ref, x1_ref, mod_ref, nf_ref, o_ref):
    W = slot_ref.shape[1]
    for g in range(x1_ref.shape[0] // W):
        rows = slice(g * W, (g + 1) * W)
        yd = jnp.concatenate([yd_ref[e, g * cap:(g + 1) * cap, :] for e in range(N_EXPERTS)], axis=0)
        acc = _dot_tn(_onehot(cap, slot_ref, g), yd)
        x2 = x1_ref[rows, :] + mod_ref[0, 3:4, :] * acc
        ms = jnp.mean(x2 * x2, axis=-1, keepdims=True)
        o_ref[rows, :] = x2 * lax.rsqrt(ms + EPS) * nf_ref[...]
        yield


def _combine_part(slot, yd, x1, mods, norm_final, T, tok):
    cap = EC_CAPACITY_FACTOR * T // N_EXPERTS
    n = x1.shape[0]
    G, h = (tok // T, 1) if tok >= T else (1, T // tok)
    assert G == 1 or mods.shape[0] == 1
    mod_map = (lambda i: (i // h, 0, 0)) if mods.shape[0] > 1 else (lambda i: (0, 0, 0))
    return dict(
        kernel=functools.partial(_combine_kernel, cap),
        steps=n // tok,
        args=[slot, yd, x1, mods, norm_final],
        in_specs=[pl.BlockSpec((G * N_EXPERTS, min(T, tok)), lambda i: (i // h, i % h)),
                  pl.BlockSpec((N_EXPERTS, G * cap, D_MODEL), lambda i: (0, i // h, 0)),
                  pl.BlockSpec((tok, D_MODEL), lambda i: (i, 0)),
                  pl.BlockSpec((1, mods.shape[1], D_MODEL), mod_map),
                  pl.BlockSpec((1, D_MODEL), lambda i: (0, 0))],
        out_specs=[pl.BlockSpec((tok, D_MODEL), lambda i: (i, 0))],
        out_shape=[jax.ShapeDtypeStruct((n, D_MODEL), F32)],
        scratch=[],
    )


def _gate_tiles(wa, wx, ba, bx):
    n_ct = D_LRU // LANES

    def dense(w):
        w2 = w.reshape(n_ct, 2, LRU_BW, LRU_BW)
        z = jnp.zeros((n_ct, LRU_BW, LRU_BW), w.dtype)
        top = jnp.concatenate([w2[:, 0], z], axis=2)
        bot = jnp.concatenate([z, w2[:, 1]], axis=2)
        return jnp.concatenate([top, bot], axis=1)

    tiles = 0.5 * jnp.concatenate([dense(wa[0]), dense(wx[0]), dense(wa[1]), dense(wx[1])], axis=-1)
    t = lambda v: v.reshape(n_ct, 1, LANES)
    bias = 0.5 * jnp.concatenate([t(ba[0]), t(bx[0]), t(ba[1]), t(bx[1])], axis=-1)
    hi = bias.astype(BF16)
    lo = (bias - hi.astype(F32)).astype(BF16)
    pad = jnp.zeros((n_ct, LANES - 2, 4 * LANES), BF16)
    return jnp.concatenate([tiles.astype(BF16), hi, lo, pad], axis=1)


def _router_split(w):
    wp = jnp.pad(w, ((0, 0), (0, LANES - N_EXPERTS)))
    hi = wp.astype(BF16)
    mid = (wp - hi.astype(F32)).astype(BF16)
    return jnp.concatenate([hi, mid], axis=1)


def _proj(x, mods, p, T):
    return _proj_part(x.reshape(-1, D_MODEL), mods, p["norm_mix"], p["w_in"], T)


def _lru(xr4, B, T, row_len, h0f, h0b, p, unroll):
    n_ct = D_LRU // LANES
    return _lru_part(xr4.reshape(n_ct, B, T, LANES), p["conv_w"], p["conv_b"], p["gate_w"], p["lam"],
                     h0f, h0b, T, row_len, unroll)


def _mix(x, mods, p, T, hs4, gr, yh, tile_rows):
    B = x.shape[0]
    return _mix_out_part(hs4.reshape(D_LRU // LANES, B * T, LANES), gr, yh.reshape(B * T, D_HGRN),
                         x.reshape(B * T, D_MODEL), mods,
                         p["norm_lru"], p["w_out"], p["norm_ffn"], p["w_router"], T, tile_rows)


def kernel(x_prompt, x_sample, state_lru, state_hgrn, c, c_ctx, w_ada, b_ada, norm_mix, w_in, conv_w,
           conv_b, lru_wa, lru_ba, lru_wx, lru_bx, lru_lambda, norm_lru, hgrn_gamma, norm_hgrn, w_out,
           norm_ffn, w_router, w_gate, w_up, w_down, norm_final):
    assert w_ada.shape[0] == 1 and hgrn_gamma.shape[1] == 2, "one trunk layer"
    l = 0
    Bp, Tp, _ = x_prompt.shape
    Bs, Ts, _ = x_sample.shape

    cond = jnp.concatenate([c_ctx[None, :], c, jnp.zeros((16 - 1 - Bs, D_MODEL), F32)], axis=0)
    n1 = 2 * D_MODEL
    ((m1,),) = _run([_adaln_part(cond, w_ada, b_ada, 0, n1, D_MODEL)])
    m1 = m1.reshape(16, 2, D_MODEL)
    m1_p, m1_s = m1[0:1], m1[1:1 + Bs]

    p = {
        "norm_mix": norm_mix[l][None, :], "w_in": w_in,
        "conv_w": conv_w[l], "conv_b": conv_b[l][None, :],
        "gate_w": _gate_tiles(lru_wa[l], lru_wx[l], lru_ba[l], lru_bx[l]),
        "lam": lru_lambda[l], "norm_lru": norm_lru[l][None, :],
        "gamma": hgrn_gamma, "norm_hgrn": norm_hgrn[l][None, :],
        "w_out": w_out[l], "norm_ffn": norm_ffn[l][None, :],
        "w_router": _router_split(w_router[l]),
    }
    zeros_p = jnp.zeros((Bp, D_LRU), F32)
    proj_p = _proj(x_prompt, m1_p, p, Tp)
    n2 = (N_MOD - 2) * D_MODEL
    (xr4_p, gr_p, hg5_p), (m2,) = _run([proj_p, _adaln_part(cond, w_ada, b_ada, n1, n2, n2 // proj_p["steps"])])
    m2 = m2.reshape(16, N_MOD - 2, D_MODEL)
    mods_p, mods_s = m2[0:1], m2[1:1 + Bs]
    (xr4_s, gr_s, hg5_s), (hs4_p, last_f, last_b) = _run(
        [_proj(x_sample, m1_s, p, Ts), _lru(xr4_p, Bp, Tp, Tp, zeros_p, zeros_p, p, True)])
    yh_p, st_p = _hgrn(hg5_p.reshape(N_HG_IN, Bp, Tp, D_HGRN), p["gamma"], p["norm_hgrn"], None, Tp, True)
    ((x1p, ha_p, at_p),) = _run([_mix(x_prompt, mods_p, p, Tp, hs4_p, gr_p, yh_p, MIX_TILE)])
    slot_p = _select(at_p, Tp)
    ((hs4_s, _, _),) = _run([_lru(xr4_s, Bs, Ts, GRID_W, state_lru[:, l, 0], state_lru[:, l, 1], p, 8)])
    yh_s, _ = _hgrn(hg5_s.reshape(N_HG_IN, Bs, Ts, D_HGRN), p["gamma"], p["norm_hgrn"], state_hgrn, Ts, False)
    fused_tile = MIX_TILE // 2
    (x1s, ha_s, at_s), (xd_p,) = _run(
        [_mix(x_sample, mods_s, p, Ts, hs4_s, gr_s, yh_s, fused_tile),
         _dispatch_part(slot_p, ha_p, Tp, fused_tile // Tp)])
    slot_s = _select(at_s, Ts)
    ((xd_s,),) = _run([_dispatch_part(slot_s, ha_s, Ts, max(ROUTE_TOKENS // Ts, 1))])

    yd_p, yd_s = _ffn(xd_p, xd_s, w_gate, w_up, w_down)
    nf = norm_final[None, :]
    (y_sample,), (y_prompt,) = _run(
        [_combine_part(slot_s, yd_s, x1s, mods_s, nf, Ts, COMBINE_TOKENS),
         _combine_part(slot_p, yd_p, x1p, mods_p, nf, Tp, COMBINE_TOKENS)])
    y_prompt = y_prompt.reshape(Bp, Tp, D_MODEL)
    y_sample = y_sample.reshape(Bs, Ts, D_MODEL)
    new_state_lru = jnp.stack([last_f, last_b], axis=1)[:, None]
    new_state_hgrn = st_p
    return (y_prompt, y_sample, new_state_lru, new_state_hgrn)
```

```python
import functools

import jax
import jax.numpy as jnp
from jax import lax
from jax.experimental import pallas as pl
from jax.experimental.pallas import tpu as pltpu

F32 = jnp.float32
BF16 = jnp.bfloat16

D_MODEL = 1024
D_LRU = 512
D_HGRN = 512
HG_HEADS = 4
HG_DK = 128
LRU_BLOCKS = 8
LRU_BW = 64
LRU_C = 8.0
N_EXPERTS = 16
EC_CAPACITY_FACTOR = 2
N_MOD = 6
D_IN = 7 * 512
N_HG_IN = 5
GRID_W = 64
EPS = 1e-6
LOG2E = 1.4426950408889634
TINY = 1e-37

LANES = 128
TOK_TILE = 256
PROJ_TILE = 512
MIX_TILE = 1024
LRU_GROUP = 8
PITCH_PAD = 8
HG_CHUNK = 64
HG_UNROLL = 16
HG_ROWS = 256
FFN_TILE = 512
ROUTE_TOKENS = 1024
COMBINE_TOKENS = 512
SEL_BISECT = 20
D_AUG = D_MODEL + LANES
VMEM_LIMIT = 56 * 1024 * 1024


def _cp(sem, vmem=VMEM_LIMIT):
    return pltpu.CompilerParams(dimension_semantics=sem, vmem_limit_bytes=vmem)


def _sigmoid(x):
    return 0.5 * (1.0 + jnp.tanh(0.5 * x))


def _dot(a, b):
    return jnp.dot(a, b, preferred_element_type=F32)


def _dot_nt(a, b):
    return lax.dot_general(a, b, (((1,), (1,)), ((), ())), preferred_element_type=F32)


def _dot_tn(a, b):
    return lax.dot_general(a, b, (((0,), (0,)), ((), ())), preferred_element_type=F32)


def _adaln_kernel(c_ref, w_ref, b_ref, o_ref):
    c = c_ref[...]
    s = (c * _sigmoid(c)).astype(BF16)
    o_ref[...] = _dot(s, w_ref[...].astype(BF16)) + b_ref[...]
    yield


def _adaln_part(cond, w, b, col0, ncols, tn):
    n = cond.shape[0]
    j0 = col0 // tn
    return dict(
        kernel=_adaln_kernel,
        steps=ncols // tn,
        args=[cond, w, b],
        in_specs=[pl.BlockSpec((n, D_MODEL), lambda j: (0, 0)),
                  pl.BlockSpec((None, D_MODEL, tn), lambda j: (0, 0, j0 + j)),
                  pl.BlockSpec((1, tn), lambda j: (0, j0 + j))],
        out_specs=[pl.BlockSpec((n, tn), lambda j: (0, j))],
        out_shape=[jax.ShapeDtypeStruct((n, ncols), F32)],
        scratch=[],
    )


def _proj_kernel(x_ref, mod_ref, g_ref, w_ref, xr_ref, gr_ref, hg_ref):
    x = x_ref[...]
    ms = jnp.mean(x * x, axis=-1, keepdims=True)
    y = x * lax.rsqrt(ms + EPS) * g_ref[...]
    h = y * (1.0 + mod_ref[0, 1:2, :]) + mod_ref[0, 0:1, :]
    h = h.astype(BF16)
    W = 2 * LANES
    for j in range(D_IN // W):
        res = _dot(h, w_ref[:, j * W:(j + 1) * W].astype(BF16))
        k, half = divmod(j * W, D_HGRN)
        if k == 0:
            for c in range(W // LANES):
                xr_ref[half // LANES + c] = res[:, c * LANES:(c + 1) * LANES]
        elif k == 1:
            gr_ref[:, half:half + W] = res
        else:
            hg_ref[k - 2, :, half:half + W] = res
        yield


def _run(parts):
    steps = parts[0]["steps"]
    assert all(p["steps"] == steps for p in parts)
    n_in = [len(p["args"]) for p in parts]
    n_out = [len(p["out_shape"]) for p in parts]
    n_sc = [len(p["scratch"]) for p in parts]

    def body(*refs):
        ins, outs, scr = refs[:sum(n_in)], refs[sum(n_in):sum(n_in) + sum(n_out)], refs[sum(n_in) + sum(n_out):]
        i = o = s = 0
        gens = []
        for p, a, b, c in zip(parts, n_in, n_out, n_sc):
            gens.append(p["kernel"](*ins[i:i + a], *outs[o:o + b], *scr[s:s + c]))
            i, o, s = i + a, o + b, s + c
        while gens:
            for g in list(gens):
                if next(g, StopIteration) is StopIteration:
                    gens.remove(g)

    res = pl.pallas_call(
        body,
        grid=(steps,),
        in_specs=[s for p in parts for s in p["in_specs"]],
        out_specs=[s for p in parts for s in p["out_specs"]],
        out_shape=[s for p in parts for s in p["out_shape"]],
        scratch_shapes=[s for p in parts for s in p["scratch"]],
        compiler_params=_cp(("arbitrary",)),
    )(*[a for p in parts for a in p["args"]])
    out, o = [], 0
    for b in n_out:
        out.append(res[o:o + b])
        o += b
    return out


def _mod_map(msel, steps_per_request):
    row0, per_request = msel
    if not per_request:
        return lambda i: (row0, 0, 0)
    assert steps_per_request >= 1
    return lambda i: (row0 + i // steps_per_request, 0, 0)


def _proj_part(x, mods, msel, gain, w, T):
    n = x.shape[0]
    assert T % PROJ_TILE == 0 or (PROJ_TILE % T == 0 and not msel[1])
    mod_map = _mod_map(msel, T // PROJ_TILE)
    n_ct = D_LRU // LANES
    return dict(
        kernel=_proj_kernel,
        steps=n // PROJ_TILE,
        args=[x, mods, gain, w],
        in_specs=[pl.BlockSpec((PROJ_TILE, D_MODEL), lambda i: (i, 0)),
                  pl.BlockSpec((1, mods.shape[1], D_MODEL), mod_map),
                  pl.BlockSpec((1, D_MODEL), lambda i: (0, 0)),
                  pl.BlockSpec((None, D_MODEL, D_IN), lambda i: (0, 0, 0), pipeline_mode=pl.Buffered(1))],
        out_specs=[pl.BlockSpec((n_ct, PROJ_TILE, LANES), lambda i: (0, i, 0)),
                   pl.BlockSpec((PROJ_TILE, D_LRU), lambda i: (i, 0)),
                   pl.BlockSpec((N_HG_IN, PROJ_TILE, D_HGRN), lambda i: (0, i, 0))],
        out_shape=[jax.ShapeDtypeStruct((n_ct, n, LANES), F32),
                   jax.ShapeDtypeStruct((n, D_LRU), F32),
                   jax.ShapeDtypeStruct((N_HG_IN, n, D_HGRN), F32)],
        scratch=[],
    )


def _lru_kernel(T, row_len, unroll, xr_ref, cw_ref, cb_ref, w_ref, lam_ref, h0f_ref, h0b_ref,
                hs_ref, lf_ref, lb_ref, xpad, af, uf, ab, ub, hf_s, hb_s):
    pitch = T + PITCH_PAD
    pos = lax.broadcasted_iota(jnp.int32, (T, LANES), 0) & (row_len - 1)
    cw = cw_ref[...]
    w0 = jnp.where(pos >= 2, cw[0:1], 0.0)
    w1 = jnp.where(pos >= 1, cw[1:2], 0.0)
    w2 = jnp.broadcast_to(cw[2:3], (T, LANES))
    w3 = jnp.where(pos <= row_len - 2, cw[3:4], 0.0)
    cb = cb_ref[...]
    nl = -lam_ref[...]
    sp = jnp.maximum(nl, 0.0) + jnp.log1p(jnp.exp(-jnp.abs(nl)))
    c2 = (-0.5 * LRU_C * LOG2E) * sp
    w = w_ref[0]
    ones = (lax.broadcasted_iota(jnp.int32, (T, LANES), 1) < 2).astype(BF16)
    for s in range(2):
        xpad[s, 0:8, :] = jnp.zeros((8, LANES), F32)
        xpad[s, T + 8:T + 16, :] = jnp.zeros((8, LANES), F32)
    for b in range(LRU_GROUP):
        x = xr_ref[b]
        xp = xpad.at[b % 2]
        xp[8:T + 8, :] = x
        xc = w0 * xp[6:T + 6, :] + w1 * xp[7:T + 7, :] + w2 * x + w3 * xp[9:T + 9, :] + cb
        xh = 0.5 * xc
        z = _dot(jnp.concatenate([xc.astype(BF16), ones], axis=1), w)
        rows = pl.ds(b * pitch, T)
        for d, (a_s, u_s) in enumerate(((af, uf), (ab, ub))):
            tr = jnp.tanh(z[:, (2 * d) * LANES:(2 * d + 1) * LANES])
            ti = jnp.tanh(z[:, (2 * d + 1) * LANES:(2 * d + 2) * LANES])
            a = jnp.exp2(c2[d:d + 1] + c2[d:d + 1] * tr)
            om = 1.0 - a * a
            a_s[rows, :] = a
            u_s[rows, :] = (om * lax.rsqrt(jnp.maximum(om, TINY))) * (xh + ti * xh)
        yield

    def step(t, carry):
        hf, hb = carry
        rf = pl.ds(t, LRU_GROUP, stride=pitch)
        hf = af[rf, :] * hf + uf[rf, :]
        hf_s[rf, :] = hf
        rb = pl.ds(T - 1 - t, LRU_GROUP, stride=pitch)
        hb = ab[rb, :] * hb + ub[rb, :]
        hb_s[rb, :] = hb
        return hf, hb

    carry = (h0f_ref[...], h0b_ref[...])
    if unroll is True:
        for t in range(T):
            carry = step(t, carry)
            if t % (T // 8) == T // 8 - 1 and t != T - 1:
                yield
    else:
        carry = lax.fori_loop(0, T, step, carry, unroll=unroll)
    hf, hb = carry
    lf_ref[...] = hf
    lb_ref[...] = hb
    for b in range(LRU_GROUP):
        rows = pl.ds(b * pitch, T)
        hs_ref[b] = hf_s[rows, :] + hb_s[rows, :]


def _lru_part(xr4, conv_w, conv_b, w_tiles, lam, h0, T, row_len, unroll):
    n_ct, B = xr4.shape[:2]
    rows = LRU_GROUP * (T + PITCH_PAD)
    vec = pl.BlockSpec((LRU_GROUP, LANES), lambda i: (i // n_ct, i % n_ct))
    vec_b = pl.BlockSpec((LRU_GROUP, LANES), lambda i: (i // n_ct, n_ct + i % n_ct))
    slab = pl.BlockSpec((None, LRU_GROUP, T, LANES), lambda i: (i % n_ct, i // n_ct, 0, 0))
    return dict(
        kernel=functools.partial(_lru_kernel, T, row_len, unroll),
        steps=(B // LRU_GROUP) * n_ct,
        args=[xr4, conv_w, conv_b, w_tiles, lam, h0, h0],
        in_specs=[slab,
                  pl.BlockSpec((4, LANES), lambda i: (0, i % n_ct)),
                  pl.BlockSpec((1, LANES), lambda i: (0, i % n_ct)),
                  pl.BlockSpec((1, 2 * LANES, 4 * LANES), lambda i: (i % n_ct, 0, 0)),
                  pl.BlockSpec((2, LANES), lambda i: (0, i % n_ct)),
                  vec, vec_b],
        out_specs=[slab, vec, vec],
        out_shape=[jax.ShapeDtypeStruct((n_ct, B, T, LANES), F32),
                   jax.ShapeDtypeStruct((B, D_LRU), F32),
                   jax.ShapeDtypeStruct((B, D_LRU), F32)],
        scratch=[pltpu.VMEM((2, T + 16, LANES), F32)] + [pltpu.VMEM((rows, LANES), F32)] * 6,
    )


def _hgrn_kernel(T, has_s0, want_state, *refs):
    q_ref, ff_ref, fb_ref, v_ref, og_ref, gam_ref, gain_ref = refs[:7]
    refs = refs[7:]
    if has_s0:
        s0_ref, refs = refs[0], refs[1:]
    y_ref, refs = refs[0], refs[1:]
    if want_state:
        st_ref, refs = refs[0], refs[1:]
    q_s, v_s, kf_s, kb_s, bf_s, bb_s, of_s, ob_s, st_s = refs
    C = HG_CHUNK
    n = T // C
    RB = HG_ROWS
    heads = [slice(h * LANES, (h + 1) * LANES) for h in range(HG_HEADS)]

    ti = lax.broadcasted_iota(jnp.int32, (RB, RB), 0)
    si = lax.broadcasted_iota(jnp.int32, (RB, RB), 1)
    sh = C.bit_length() - 1
    same = (ti >> sh) == (si >> sh)
    tri = ((same & (si <= ti)).astype(BF16), (same & (si >= ti)).astype(BF16))

    for r in range(T // RB):
        rows = slice(r * RB, (r + 1) * RB)
        qr = q_ref[rows, :]
        qh = (0.5 * HG_DK ** -0.5) * qr
        q_s[rows, :] = qh + qh * jnp.tanh(0.5 * qr)
        v_s[rows, :] = v_ref[rows, :].astype(BF16)
        for d, (f_ref, k_s, b_s) in enumerate(((ff_ref, kf_s, bf_s), (fb_ref, kb_s, bb_s))):
            g0 = gam_ref[d, 0:1, :]
            g1 = gam_ref[d, 1:2, :]
            m = jnp.maximum(g0, g1)
            e0 = jnp.exp(g0 - m)
            lbd = e0 / (e0 + jnp.exp(g1 - m))
            ck = 0.5 * (1.0 - lbd)
            pt = ck * jnp.tanh(0.5 * f_ref[rows, :])
            k_s[rows, :] = ck - pt
            lf = jnp.log((lbd + ck) + pt)
            hi = lf.astype(BF16)
            r1 = lf - hi.astype(F32)
            mid = r1.astype(BF16)
            lo = (r1 - mid.astype(F32)).astype(BF16)
            cs = _dot(tri[d], jnp.concatenate([hi, mid, lo], axis=1))
            b_s[rows, :] = cs[:, 0:D_HGRN] + (cs[:, D_HGRN:2 * D_HGRN] + cs[:, 2 * D_HGRN:3 * D_HGRN])

    for d in range(2):
        for h in range(HG_HEADS):
            st_s[d * HG_HEADS + h] = s0_ref[d, h].T if has_s0 else jnp.zeros((LANES, LANES), F32)

    ri =lax.broadcasted_iota(jnp.int32, (C, C), 0)
    ci = lax.broadcasted_iota(jnp.int32, (C, C), 1)

    def body(i, carry):
        for d, (k_s, b_s, o_s) in enumerate(((kf_s, bf_s, of_s), (kb_s, bb_s, ob_s))):
            c = i if d == 0 else n - 1 - i
            sl = pl.ds(pl.multiple_of(c * C, C), C)
            for h in range(HG_HEADS):
                q = q_s[sl, heads[h]]
                k = k_s[sl, heads[h]]
                b = b_s[sl, heads[h]]
                v = v_s[sl, heads[h]]
                bm = b[C // 2:C // 2 + 1, :]
                g = b[C - 1:C, :] if d == 0 else b[0:1, :]
                qi = q * jnp.exp(b - bm)
                qe = (qi * jnp.exp(bm)).astype(BF16)
                ki = k * jnp.exp(bm - b)
                kd = (ki * jnp.exp(g - bm)).astype(BF16)
                s = _dot_nt(qi.astype(BF16), ki.astype(BF16))
                p = jnp.where((ci <= ri) if d == 0 else (ci >= ri), s, 0.0).astype(BF16)
                st = st_s[d * HG_HEADS + h]
                o_s[sl, heads[h]] = _dot(p, v) + _dot_nt(qe, st.astype(BF16))
                st_s[d * HG_HEADS + h] = st * jnp.exp(g) + _dot_tn(v, kd)
        return carry

    lax.fori_loop(0, n, body, 0, unroll=HG_UNROLL)
    if want_state:
        for d in range(2):
            for h in range(HG_HEADS):
                st_ref[d, h] = st_s[d * HG_HEADS + h].T

    for r in range(T // RB):
        rows = slice(r * RB, (r + 1) * RB)
        for h in range(HG_HEADS):
            o = of_s[rows, heads[h]] + ob_s[rows, heads[h]]
            ms = jnp.mean(o * o, axis=-1, keepdims=True)
            oh = 0.5 * og_ref[rows, heads[h]]
            y = o * lax.rsqrt(ms + EPS) * gain_ref[...] * (oh + oh * jnp.tanh(oh))
            y_ref[rows, heads[h]] = y.astype(BF16)


def _hgrn(hg5, gamma, gain, s0, T, want_state):
    B = hg5.shape[1]
    has_s0 = s0 is not None

    def col(k):
        return pl.BlockSpec((None, None, T, D_HGRN), lambda b, k=k: (k, b, 0, 0))

    st_spec = pl.BlockSpec((None, None, 2, HG_HEADS, HG_DK, HG_DK), lambda b: (b, 0, 0, 0, 0, 0))
    in_specs = [col(0), col(1), col(2), col(3), col(4),
                pl.BlockSpec((2, 2, D_HGRN), lambda b: (0, 0, 0)),
                pl.BlockSpec((1, LANES), lambda b: (0, 0))]
    args = [hg5, hg5, hg5, hg5, hg5, gamma, gain]
    if has_s0:
        in_specs.append(st_spec)
        args.append(s0)
    out_specs = [pl.BlockSpec((None, T, D_HGRN), lambda b: (b, 0, 0))]
    out_shape = [jax.ShapeDtypeStruct((B, T, D_HGRN), BF16)]
    if want_state:
        out_specs.append(st_spec)
        out_shape.append(jax.ShapeDtypeStruct((B, 1, 2, HG_HEADS, HG_DK, HG_DK), F32))
    res = pl.pallas_call(
        functools.partial(_hgrn_kernel, T, has_s0, want_state),
        grid=(B,),
        in_specs=in_specs,
        out_specs=out_specs,
        out_shape=out_shape,
        scratch_shapes=[pltpu.VMEM((T, D_HGRN), F32), pltpu.VMEM((T, D_HGRN), BF16)]
        + [pltpu.VMEM((T, D_HGRN), F32)] * 6
        + [pltpu.VMEM((2 * HG_HEADS, HG_DK, HG_DK), F32)],
        compiler_params=_cp(("arbitrary",)),
    )(*args)
    return res if want_state else (res[0], None)


def _mix_out_kernel(hs_ref, gr_ref, yh_ref, x_ref, mod_ref, nl_ref, wo_ref, nf_ref, wr_ref,
                    x1_ref, ha_ref, at_ref):
    wr = wr_ref[...]
    wo = wo_ref[...].astype(BF16)
    g1 = mod_ref[0, 0:1, :]
    sh2 = mod_ref[0, 1:2, :]
    gain2 = nf_ref[...] * (1.0 + mod_ref[0, 2:3, :])
    lane = lax.broadcasted_iota(jnp.int32, (TOK_TILE, LANES), 1)
    for s in range(x_ref.shape[0] // TOK_TILE):
        rows = slice(s * TOK_TILE, (s + 1) * TOK_TILE)
        hs = jnp.concatenate([hs_ref[c, rows, :] for c in range(D_LRU // LANES)], axis=1)
        ms = jnp.mean(hs * hs, axis=-1, keepdims=True)
        gr = gr_ref[rows, :]
        gelu = 0.5 * gr * (1.0 + jnp.tanh(0.7978845608028654 * (gr + 0.044715 * (gr * gr * gr))))
        y_lru = hs * lax.rsqrt(ms + EPS) * nl_ref[...] * gelu
        ycat = jnp.concatenate([y_lru.astype(BF16), yh_ref[rows, :]], axis=-1)
        x1 = x_ref[rows, :] + g1 * _dot(ycat, wo)
        x1_ref[rows, :] = x1
        yield
        ms = jnp.mean(x1 * x1, axis=-1, keepdims=True)
        h2 = x1 * lax.rsqrt(ms + EPS) * gain2 + sh2
        h2_hi = h2.astype(BF16)
        h2_mid = (h2 - h2_hi.astype(F32)).astype(BF16)
        p_hi = _dot(h2_hi, wr)
        logits = p_hi[:, 0:LANES] + (p_hi[:, LANES:2 * LANES] + _dot(h2_mid, wr)[:, 0:LANES])
        logits = jnp.where(lane < N_EXPERTS, logits, -jnp.inf)
        e = jnp.exp(logits - jnp.max(logits, axis=-1, keepdims=True))
        aff = e / jnp.sum(e, axis=-1, keepdims=True)
        w = at_ref.shape[1]
        r, c = (s * TOK_TILE) // w, (s * TOK_TILE) % w
        at_ref[r * N_EXPERTS:(r + 1) * N_EXPERTS, c:c + TOK_TILE] = aff.T[0:N_EXPERTS, :]
        hi = aff.astype(BF16).astype(F32)
        r1 = aff - hi
        mid = r1.astype(BF16).astype(F32)
        lo = (r1 - mid).astype(BF16).astype(F32)
        split = hi + pltpu.roll(mid, N_EXPERTS, 1) + pltpu.roll(lo, 2 * N_EXPERTS, 1)
        ha_ref[rows, 0:D_MODEL] = h2_hi
        ha_ref[rows, D_MODEL:D_AUG] = split.astype(BF16)
        yield


def _mix_out_part(hs4, gr, yh, x, mods, msel, norm_lru, w_out, norm_ffn, wr_pad, T, tile_rows):
    n = x.shape[0]
    assert T % TOK_TILE == 0 and (tile_rows % T == 0 or T % tile_rows == 0)
    q = max(T // tile_rows, 1)
    at_rows = N_EXPERTS * max(tile_rows // T, 1)
    at_spec = pl.BlockSpec((at_rows, min(T, tile_rows)), lambda i: (i // q, i % q))
    mod_map = _mod_map(msel, q)
    tile = lambda w: pl.BlockSpec((tile_rows, w), lambda i: (i, 0))
    const = lambda r, w: pl.BlockSpec((r, w), lambda i: (0, 0))
    return dict(
        kernel=_mix_out_kernel,
        steps=n // tile_rows,
        args=[hs4, gr, yh, x, mods, norm_lru, w_out, norm_ffn, wr_pad],
        in_specs=[pl.BlockSpec((D_LRU // LANES, tile_rows, LANES), lambda i: (0, i, 0)),
                  tile(D_LRU), tile(D_HGRN), tile(D_MODEL),
                  pl.BlockSpec((1, mods.shape[1], D_MODEL), mod_map),
                  const(1, D_LRU), const(D_MODEL, D_MODEL), const(1, D_MODEL), const(D_MODEL, 2 * LANES)],
        out_specs=[tile(D_MODEL), tile(D_AUG), at_spec],
        out_shape=[jax.ShapeDtypeStruct((n, D_MODEL), F32),
                   jax.ShapeDtypeStruct((n, D_AUG), BF16),
                   jax.ShapeDtypeStruct((n // T * N_EXPERTS, T), F32)],
        scratch=[],
    )


def _select_kernel(cap, at_ref, slot_ref):
    a = at_ref[...]
    R, T = a.shape
    ones = jnp.ones((T, LANES), BF16)
    su = (lax.broadcasted_iota(jnp.int32, (T, T), 0)
          < lax.broadcasted_iota(jnp.int32, (T, T), 1)).astype(BF16)

    def wide(x):
        return jnp.concatenate([x] * (T // LANES), axis=1)

    def count(mask):
        return _dot(mask.astype(BF16), ones)

    lo = jnp.zeros((R, LANES), F32)
    hi = jnp.full((R, LANES), 2.0, F32)
    for _ in range(SEL_BISECT):
        mid = 0.5 * (lo + hi)
        ge = count(a >= wide(mid)) >= cap
        lo = jnp.where(ge, mid, lo)
        hi = jnp.where(ge, hi, mid)

    def cond(st):
        return (st[0] < T) & (jnp.min(st[2]) < cap)

    def body(st):
        it, cur, n, thr = st
        m = jnp.max(jnp.where(a < wide(cur), a, -1.0), axis=1, keepdims=True)
        m = jnp.broadcast_to(m, (R, LANES))
        c = count(a >= wide(m))
        act = n < cap
        return it + 1, jnp.where(act, m, cur), jnp.where(act, c, n), jnp.where(act, m, thr)

    _, _, _, thr = lax.while_loop(cond, body, (jnp.int32(0), hi, count(a >= wide(hi)), lo))
    thr_w = wide(thr)
    gt = a > thr_w
    eq = a == thr_w
    need = wide(cap - count(gt))
    sel = gt | (eq & (_dot(eq.astype(BF16), su) < need))
    slot_ref[...] = jnp.where(sel, _dot(sel.astype(BF16), su), -1.0).astype(jnp.int32)


def _select(at, T):
    cap = EC_CAPACITY_FACTOR * T // N_EXPERTS
    R = at.shape[0]
    return pl.pallas_call(
        functools.partial(_select_kernel, cap),
        grid=(1,),
        in_specs=[pl.BlockSpec((R, T), lambda i: (0, 0))],
        out_specs=pl.BlockSpec((R, T), lambda i: (0, 0)),
        out_shape=jax.ShapeDtypeStruct((R, T), jnp.int32),
        compiler_params=_cp(("arbitrary",)),
    )(at)


def _onehot(cap, slot_ref, g):
    j = lax.broadcasted_iota(jnp.int32, (cap, slot_ref.shape[1]), 0)
    r0 = g * N_EXPERTS
    return jnp.concatenate([(slot_ref[r0 + e:r0 + e + 1, :] == j).astype(BF16) for e in range(N_EXPERTS)], axis=0)


def _dispatch_kernel(T, cap, slot_ref, ha_ref, xd_ref):
    for g in range(ha_ref.shape[0] // T):
        rows = _dot(_onehot(cap, slot_ref, g), ha_ref[g * T:(g + 1) * T, :])
        for e in range(N_EXPERTS):
            xd_ref[e, g * cap:(g + 1) * cap, :] = rows[e * cap:(e + 1) * cap].astype(BF16)
        yield


def _dispatch_part(slot, ha, T, G):
    cap = EC_CAPACITY_FACTOR * T // N_EXPERTS
    B = ha.shape[0] // T
    return dict(
        kernel=functools.partial(_dispatch_kernel, T, cap),
        steps=B // G,
        args=[slot, ha],
        in_specs=[pl.BlockSpec((G * N_EXPERTS, T), lambda b: (b, 0)),
                  pl.BlockSpec((G * T, D_AUG), lambda b: (b, 0))],
        out_specs=[pl.BlockSpec((N_EXPERTS, G * cap, D_AUG), lambda b: (0, b, 0))],
        out_shape=[jax.ShapeDtypeStruct((N_EXPERTS, B * cap, D_AUG), BF16)],
        scratch=[],
    )


def _ffn_kernel(xp_ref, xs_ref, wg_ref, wu_ref, wd_ref, yp_ref, ys_ref):
    e = pl.program_id(0)
    lane = lax.broadcasted_iota(jnp.int32, (FFN_TILE, LANES), 1)
    pick = (lane == e) | (lane == e + N_EXPERTS) | (lane == e + 2 * N_EXPERTS)
    wg = wg_ref[...].astype(BF16)
    wu = wu_ref[...].astype(BF16)
    wd = wd_ref[...].astype(BF16)
    for x_ref, y_ref in ((xp_ref, yp_ref), (xs_ref, ys_ref)):
        x = x_ref[:, 0:D_MODEL]
        gate = jnp.sum(jnp.where(pick, x_ref[:, D_MODEL:D_AUG].astype(F32), 0.0), axis=-1, keepdims=True)
        hg = _dot(x, wg)
        hid = (hg * _sigmoid(hg)) * _dot(x, wu)
        y_ref[...] = (_dot(hid.astype(BF16), wd) * gate).astype(BF16)


def _ffn(xd_p, xd_s, w_gate, w_up, w_down):
    rows = xd_p.shape[1]
    assert xd_s.shape[1] == rows and rows % FFN_TILE == 0
    xspec = pl.BlockSpec((None, FFN_TILE, D_AUG), lambda e, m: (e, m, 0))
    wspec = pl.BlockSpec((None, None, D_MODEL, D_MODEL), lambda e, m: (0, e, 0, 0))
    yspec = pl.BlockSpec((None, FFN_TILE, D_MODEL), lambda e, m: (e, m, 0))
    yshape = jax.ShapeDtypeStruct((N_EXPERTS, rows, D_MODEL), BF16)
    return pl.pallas_call(
        _ffn_kernel,
        grid=(N_EXPERTS, rows // FFN_TILE),
        in_specs=[xspec, xspec, wspec, wspec, wspec],
        out_specs=[yspec, yspec],
        out_shape=[yshape, yshape],
        compiler_params=_cp(("arbitrary", "arbitrary")),
    )(xd_p, xd_s, w_gate, w_up, w_down)


def _combine_kernel(cap, slot_ref, yd_ref, x1_ref, mod_ref, nf_ref, o_ref):
    W = slot_ref.shape[1]
    for g in range(x1_ref.shape[0] // W):
        rows = slice(g * W, (g + 1) * W)
        yd = jnp.concatenate([yd_ref[e, g * cap:(g + 1) * cap, :] for e in range(N_EXPERTS)], axis=0)
        acc = _dot_tn(_onehot(cap, slot_ref, g), yd)
        x2 = x1_ref[rows, :] + mod_ref[0, 3:4, :] * acc
        ms = jnp.mean(x2 * x2, axis=-1, keepdims=True)
        o_ref[rows, :] = x2 * lax.rsqrt(ms + EPS) * nf_ref[...]
        yield


def _combine_part(slot, yd, x1, mods, msel, norm_final, T, tok):
    cap = EC_CAPACITY_FACTOR * T // N_EXPERTS
    n = x1.shape[0]
    G, h = (tok // T, 1) if tok >= T else (1, T // tok)
    assert G == 1 or not msel[1]
    mod_map = _mod_map(msel, h)
    return dict(
        kernel=functools.partial(_combine_kernel, cap),
        steps=n // tok,
        args=[slot, yd, x1, mods, norm_final],
        in_specs=[pl.BlockSpec((G * N_EXPERTS, min(T, tok)), lambda i: (i // h, i % h)),
                  pl.BlockSpec((N_EXPERTS, G * cap, D_MODEL), lambda i: (0, i // h, 0)),
                  pl.BlockSpec((tok, D_MODEL), lambda i: (i, 0)),
                  pl.BlockSpec((1, mods.shape[1], D_MODEL), mod_map),
                  pl.BlockSpec((1, D_MODEL), lambda i: (0, 0))],
        out_specs=[pl.BlockSpec((tok, D_MODEL), lambda i: (i, 0))],
        out_shape=[jax.ShapeDtypeStruct((n, D_MODEL), F32)],
        scratch=[],
    )


def _gate_tiles(wa, wx, ba, bx):
    n_ct = D_LRU // LANES

    def dense(w):
        w2 = w.reshape(n_ct, 2, LRU_BW, LRU_BW)
        z = jnp.zeros((n_ct, LRU_BW, LRU_BW), w.dtype)
        top = jnp.concatenate([w2[:, 0], z], axis=2)
        bot = jnp.concatenate([z, w2[:, 1]], axis=2)
        return jnp.concatenate([top, bot], axis=1)

    tiles = 0.5 * jnp.concatenate([dense(wa[0]), dense(wx[0]), dense(wa[1]), dense(wx[1])], axis=-1)
    t = lambda v: v.reshape(n_ct, 1, LANES)
    bias = 0.5 * jnp.concatenate([t(ba[0]), t(bx[0]), t(ba[1]), t(bx[1])], axis=-1)
    hi = bias.astype(BF16)
    lo = (bias - hi.astype(F32)).astype(BF16)
    pad = jnp.zeros((n_ct, LANES - 2, 4 * LANES), BF16)
    return jnp.concatenate([tiles.astype(BF16), hi, lo, pad], axis=1)


def _router_split(w):
    wp = jnp.pad(w, ((0, 0), (0, LANES - N_EXPERTS)))
    hi = wp.astype(BF16)
    mid = (wp - hi.astype(F32)).astype(BF16)
    return jnp.concatenate([hi, mid], axis=1)


def _proj(x, mods, msel, p, T):
    return _proj_part(x.reshape(-1, D_MODEL), mods, msel, p["norm_mix"], p["w_in"], T)


def _lru(xr4, B, T, row_len, h0, p, unroll):
    n_ct = D_LRU // LANES
    return _lru_part(xr4.reshape(n_ct, B, T, LANES), p["conv_w"], p["conv_b"], p["gate_w"], p["lam"],
                     h0, T, row_len, unroll)


def _mix(x, mods, msel, p, T, hs4, gr, yh, tile_rows):
    B = x.shape[0]
    return _mix_out_part(hs4.reshape(D_LRU // LANES, B * T, LANES), gr, yh.reshape(B * T, D_HGRN),
                         x.reshape(B * T, D_MODEL), mods, msel,
                         p["norm_lru"], p["w_out"], p["norm_ffn"], p["w_router"], T, tile_rows)


def kernel(x_prompt, x_sample, state_lru, state_hgrn, c, c_ctx, w_ada, b_ada, norm_mix, w_in, conv_w,
           conv_b, lru_wa, lru_ba, lru_wx, lru_bx, lru_lambda, norm_lru, hgrn_gamma, norm_hgrn, w_out,
           norm_ffn, w_router, w_gate, w_up, w_down, norm_final):
    assert w_ada.shape[0] == 1 and hgrn_gamma.shape[1] == 2, "one trunk layer"
    l = 0
    Bp, Tp, _ = x_prompt.shape
    Bs, Ts, _ = x_sample.shape

    cond = jnp.concatenate([c_ctx[None, :], c, jnp.zeros((16 - 1 - Bs, D_MODEL), F32)], axis=0)
    n1 = 2 * D_MODEL
    ((m1,),) = _run([_adaln_part(cond, w_ada, b_ada, 0, n1, D_MODEL)])
    m1 = m1.reshape(16, 2, D_MODEL)
    sel_p, sel_s = (0, False), (1, True)

    p = {
        "norm_mix": norm_mix[l][None, :], "w_in": w_in,
        "conv_w": conv_w[l], "conv_b": conv_b[l][None, :],
        "gate_w": _gate_tiles(lru_wa[l], lru_wx[l], lru_ba[l], lru_bx[l]),
        "lam": lru_lambda[l], "norm_lru": norm_lru[l][None, :],
        "gamma": hgrn_gamma, "norm_hgrn": norm_hgrn[l][None, :],
        "w_out": w_out[l], "norm_ffn": norm_ffn[l][None, :],
        "w_router": _router_split(w_router[l]),
    }
    zeros_p = jnp.zeros((Bp, 2 * D_LRU), F32)
    proj_p = _proj(x_prompt, m1, sel_p, p, Tp)
    n2 = (N_MOD - 2) * D_MODEL
    (xr4_p, gr_p, hg5_p), (m2,) = _run([proj_p, _adaln_part(cond, w_ada, b_ada, n1, n2, n2 // proj_p["steps"])])
    m2 = m2.reshape(16, N_MOD - 2, D_MODEL)
    (xr4_s, gr_s, hg5_s), (hs4_p, last_f, last_b) = _run(
        [_proj(x_sample, m1, sel_s, p, Ts), _lru(xr4_p, Bp, Tp, Tp, zeros_p, p, True)])
    yh_p, st_p = _hgrn(hg5_p.reshape(N_HG_IN, Bp, Tp, D_HGRN), p["gamma"], p["norm_hgrn"], None, Tp, True)
    ((x1p, ha_p, at_p),) = _run([_mix(x_prompt, m2, sel_p, p, Tp, hs4_p, gr_p, yh_p, MIX_TILE)])
    slot_p = _select(at_p, Tp)
    assert state_lru.shape[1] == 1
    ((hs4_s, _, _),) = _run([_lru(xr4_s, Bs, Ts, GRID_W, state_lru.reshape(Bs, 2 * D_LRU), p, 8)])
    yh_s, _ = _hgrn(hg5_s.reshape(N_HG_IN, Bs, Ts, D_HGRN), p["gamma"], p["norm_hgrn"], state_hgrn, Ts, False)
    fused_tile = MIX_TILE // 2
    (x1s, ha_s, at_s), (xd_p,) = _run(
        [_mix(x_sample, m2, sel_s, p, Ts, hs4_s, gr_s, yh_s, fused_tile),
         _dispatch_part(slot_p, ha_p, Tp, fused_tile // Tp)])
    slot_s = _select(at_s, Ts)
    ((xd_s,),) = _run([_dispatch_part(slot_s, ha_s, Ts, max(ROUTE_TOKENS // Ts, 1))])

    yd_p, yd_s = _ffn(xd_p, xd_s, w_gate, w_up, w_down)
    nf = norm_final[None, :]
    (y_sample,), (y_prompt,) = _run(
        [_combine_part(slot_s, yd_s, x1s, m2, sel_s, nf, Ts, COMBINE_TOKENS),
         _combine_part(slot_p, yd_p, x1p, m2, sel_p, nf, Tp, COMBINE_TOKENS)])
    y_prompt = y_prompt.reshape(Bp, Tp, D_MODEL)
    y_sample = y_sample.reshape(Bs, Ts, D_MODEL)
    new_state_lru = jnp.stack([last_f, last_b], axis=1)[:, None]
    new_state_hgrn = st_p
    return (y_prompt, y_sample, new_state_lru, new_state_hgrn)
```

```python
import functools

import jax
import jax.numpy as jnp
from jax import lax
from jax.experimental import pallas as pl
from jax.experimental.pallas import tpu as pltpu

F32 = jnp.float32
BF16 = jnp.bfloat16

D_MODEL = 1024
D_LRU = 512
D_HGRN = 512
HG_HEADS = 4
HG_DK = 128
LRU_BLOCKS = 8
LRU_BW = 64
LRU_C = 8.0
N_EXPERTS = 16
EC_CAPACITY_FACTOR = 2
N_MOD = 6
D_IN = 7 * 512
N_HG_IN = 5
GRID_W = 64
EPS = 1e-6
LOG2E = 1.4426950408889634
TINY = 1e-37

LANES = 128
TOK_TILE = 256
PROJ_TILE = 512
MIX_TILE = 1024
LRU_GROUP = 8
PITCH_PAD = 8
HG_CHUNK = 64
HG_UNROLL = 16
HG_ROWS = 256
FFN_TILE = 512
ROUTE_TOKENS = 1024
COMBINE_TOKENS = 512
SEL_BISECT = 20
D_AUG = D_MODEL + LANES
VMEM_LIMIT = 56 * 1024 * 1024


def _cp(sem, vmem=VMEM_LIMIT):
    return pltpu.CompilerParams(dimension_semantics=sem, vmem_limit_bytes=vmem)


def _sigmoid(x):
    return 0.5 * (1.0 + jnp.tanh(0.5 * x))


def _dot(a, b):
    return jnp.dot(a, b, preferred_element_type=F32)


def _dot_nt(a, b):
    return lax.dot_general(a, b, (((1,), (1,)), ((), ())), preferred_element_type=F32)


def _dot_tn(a, b):
    return lax.dot_general(a, b, (((0,), (0,)), ((), ())), preferred_element_type=F32)


def _adaln_kernel(c_ref, w_ref, b_ref, o_ref):
    c = c_ref[...]
    s = (c * _sigmoid(c)).astype(BF16)
    o_ref[...] = _dot(s, w_ref[...].astype(BF16)) + b_ref[...]
    yield


def _adaln_part(cond, w, b, col0, ncols, tn):
    n = cond.shape[0]
    j0 = col0 // tn
    return dict(
        kernel=_adaln_kernel,
        steps=ncols // tn,
        args=[cond, w, b],
        in_specs=[pl.BlockSpec((n, D_MODEL), lambda j: (0, 0)),
                  pl.BlockSpec((None, D_MODEL, tn), lambda j: (0, 0, j0 + j)),
                  pl.BlockSpec((1, tn), lambda j: (0, j0 + j))],
        out_specs=[pl.BlockSpec((n, tn), lambda j: (0, j))],
        out_shape=[jax.ShapeDtypeStruct((n, ncols), F32)],
        scratch=[],
    )


def _proj_kernel(x_ref, mod_ref, g_ref, w_ref, xr_ref, gr_ref, hg_ref):
    x = x_ref[...]
    ms = jnp.mean(x * x, axis=-1, keepdims=True)
    y = x * lax.rsqrt(ms + EPS) * g_ref[...]
    h = y * (1.0 + mod_ref[0, 1:2, :]) + mod_ref[0, 0:1, :]
    h = h.astype(BF16)
    W = 2 * LANES
    for j in range(D_IN // W):
        res = _dot(h, w_ref[:, j * W:(j + 1) * W].astype(BF16))
        k, half = divmod(j * W, D_HGRN)
        if k == 0:
            for c in range(W // LANES):
                xr_ref[half // LANES + c] = res[:, c * LANES:(c + 1) * LANES]
        elif k == 1:
            gr_ref[:, half:half + W] = res
        else:
            hg_ref[k - 2, :, half:half + W] = res
        yield


def _run(parts):
    steps = parts[0]["steps"]
    assert all(p["steps"] == steps for p in parts)
    n_in = [len(p["args"]) for p in parts]
    n_out = [len(p["out_shape"]) for p in parts]
    n_sc = [len(p["scratch"]) for p in parts]

    def body(*refs):
        ins, outs, scr = refs[:sum(n_in)], refs[sum(n_in):sum(n_in) + sum(n_out)], refs[sum(n_in) + sum(n_out):]
        i = o = s = 0
        gens = []
        for p, a, b, c in zip(parts, n_in, n_out, n_sc):
            gens.append(p["kernel"](*ins[i:i + a], *outs[o:o + b], *scr[s:s + c]))
            i, o, s = i + a, o + b, s + c
        while gens:
            for g in list(gens):
                if next(g, StopIteration) is StopIteration:
                    gens.remove(g)

    res = pl.pallas_call(
        body,
        grid=(steps,),
        in_specs=[s for p in parts for s in p["in_specs"]],
        out_specs=[s for p in parts for s in p["out_specs"]],
        out_shape=[s for p in parts for s in p["out_shape"]],
        scratch_shapes=[s for p in parts for s in p["scratch"]],
        compiler_params=_cp(("arbitrary",)),
    )(*[a for p in parts for a in p["args"]])
    out, o = [], 0
    for b in n_out:
        out.append(res[o:o + b])
        o += b
    return out


def _mod_map(msel, steps_per_request):
    row0, per_request = msel
    if not per_request:
        return lambda i: (row0, 0, 0)
    assert steps_per_request >= 1
    return lambda i: (row0 + i // steps_per_request, 0, 0)


def _proj_part(x, mods, msel, gain, w, T):
    n = x.shape[0]
    assert T % PROJ_TILE == 0 or (PROJ_TILE % T == 0 and not msel[1])
    mod_map = _mod_map(msel, T // PROJ_TILE)
    n_ct = D_LRU // LANES
    return dict(
        kernel=_proj_kernel,
        steps=n // PROJ_TILE,
        args=[x, mods, gain, w],
        in_specs=[pl.BlockSpec((PROJ_TILE, D_MODEL), lambda i: (i, 0)),
                  pl.BlockSpec((1, mods.shape[1], D_MODEL), mod_map),
                  pl.BlockSpec((1, D_MODEL), lambda i: (0, 0)),
                  pl.BlockSpec((None, D_MODEL, D_IN), lambda i: (0, 0, 0), pipeline_mode=pl.Buffered(1))],
        out_specs=[pl.BlockSpec((n_ct, PROJ_TILE, LANES), lambda i: (0, i, 0)),
                   pl.BlockSpec((PROJ_TILE, D_LRU), lambda i: (i, 0)),
                   pl.BlockSpec((N_HG_IN, PROJ_TILE, D_HGRN), lambda i: (0, i, 0))],
        out_shape=[jax.ShapeDtypeStruct((n_ct, n, LANES), F32),
                   jax.ShapeDtypeStruct((n, D_LRU), F32),
                   jax.ShapeDtypeStruct((N_HG_IN, n, D_HGRN), F32)],
        scratch=[],
    )


def _lru_kernel(T, row_len, unroll, xr_ref, cw_ref, cb_ref, w_ref, lam_ref, h0f_ref, h0b_ref,
                hs_ref, lf_ref, lb_ref, xpad, af, uf, ab, ub, hf_s, hb_s):
    pitch = T + PITCH_PAD
    pos = lax.broadcasted_iota(jnp.int32, (T, LANES), 0) & (row_len - 1)
    cw = cw_ref[...]
    w0 = jnp.where(pos >= 2, cw[0:1], 0.0)
    w1 = jnp.where(pos >= 1, cw[1:2], 0.0)
    w2 = jnp.broadcast_to(cw[2:3], (T, LANES))
    w3 = jnp.where(pos <= row_len - 2, cw[3:4], 0.0)
    cb = cb_ref[...]
    nl = -lam_ref[...]
    sp = jnp.maximum(nl, 0.0) + jnp.log1p(jnp.exp(-jnp.abs(nl)))
    c2 = (-0.5 * LRU_C * LOG2E) * sp
    w = w_ref[0]
    ones = (lax.broadcasted_iota(jnp.int32, (T, LANES), 1) < 2).astype(BF16)
    for s in range(2):
        xpad[s, 0:8, :] = jnp.zeros((8, LANES), F32)
        xpad[s, T + 8:T + 16, :] = jnp.zeros((8, LANES), F32)
    for b in range(LRU_GROUP):
        x = xr_ref[b]
        xp = xpad.at[b % 2]
        xp[8:T + 8, :] = x
        xc = w0 * xp[6:T + 6, :] + w1 * xp[7:T + 7, :] + w2 * x + w3 * xp[9:T + 9, :] + cb
        xh = 0.5 * xc
        z = _dot(jnp.concatenate([xc.astype(BF16), ones], axis=1), w)
        rows = pl.ds(b * pitch, T)
        for d, (a_s, u_s) in enumerate(((af, uf), (ab, ub))):
            tr = jnp.tanh(z[:, (2 * d) * LANES:(2 * d + 1) * LANES])
            ti = jnp.tanh(z[:, (2 * d + 1) * LANES:(2 * d + 2) * LANES])
            a = jnp.exp2(c2[d:d + 1] + c2[d:d + 1] * tr)
            om = 1.0 - a * a
            a_s[rows, :] = a
            u_s[rows, :] = (om * lax.rsqrt(jnp.maximum(om, TINY))) * (xh + ti * xh)
        yield

    def step(t, carry):
        hf, hb = carry
        rf = pl.ds(t, LRU_GROUP, stride=pitch)
        hf = af[rf, :] * hf + uf[rf, :]
        hf_s[rf, :] = hf
        rb = pl.ds(T - 1 - t, LRU_GROUP, stride=pitch)
        hb = ab[rb, :] * hb + ub[rb, :]
        hb_s[rb, :] = hb
        return hf, hb

    carry = (h0f_ref[...], h0b_ref[...])
    if unroll is True:
        for t in range(T):
            carry = step(t, carry)
            if t % (T // 8) == T // 8 - 1 and t != T - 1:
                yield
    else:
        carry = lax.fori_loop(0, T, step, carry, unroll=unroll)
    hf, hb = carry
    lf_ref[...] = hf
    lb_ref[...] = hb
    for b in range(LRU_GROUP):
        rows = pl.ds(b * pitch, T)
        hs_ref[b] = hf_s[rows, :] + hb_s[rows, :]


def _lru_part(xr4, conv_w, conv_b, w_tiles, lam, h0, T, row_len, unroll):
    n_ct, B = xr4.shape[:2]
    rows = LRU_GROUP * (T + PITCH_PAD)
    vec = pl.BlockSpec((LRU_GROUP, LANES), lambda i: (i // n_ct, i % n_ct))
    vec_b = pl.BlockSpec((LRU_GROUP, LANES), lambda i: (i // n_ct, n_ct + i % n_ct))
    slab = pl.BlockSpec((None, LRU_GROUP, T, LANES), lambda i: (i % n_ct, i // n_ct, 0, 0))
    return dict(
        kernel=functools.partial(_lru_kernel, T, row_len, unroll),
        steps=(B // LRU_GROUP) * n_ct,
        args=[xr4, conv_w, conv_b, w_tiles, lam, h0, h0],
        in_specs=[slab,
                  pl.BlockSpec((4, LANES), lambda i: (0, i % n_ct)),
                  pl.BlockSpec((1, LANES), lambda i: (0, i % n_ct)),
                  pl.BlockSpec((1, 2 * LANES, 4 * LANES), lambda i: (i % n_ct, 0, 0)),
                  pl.BlockSpec((2, LANES), lambda i: (0, i % n_ct)),
                  vec, vec_b],
        out_specs=[slab, vec, vec],
        out_shape=[jax.ShapeDtypeStruct((n_ct, B, T, LANES), F32),
                   jax.ShapeDtypeStruct((B, D_LRU), F32),
                   jax.ShapeDtypeStruct((B, D_LRU), F32)],
        scratch=[pltpu.VMEM((2, T + 16, LANES), F32)] + [pltpu.VMEM((rows, LANES), F32)] * 6,
    )


def _hgrn_kernel(T, has_s0, want_state, *refs):
    q_ref, ff_ref, fb_ref, v_ref, og_ref, gam_ref, gain_ref = refs[:7]
    refs = refs[7:]
    if has_s0:
        s0_ref, refs = refs[0], refs[1:]
    y_ref, refs = refs[0], refs[1:]
    if want_state:
        st_ref, refs = refs[0], refs[1:]
    q_s, v_s, kf_s, kb_s, bf_s, bb_s, of_s, ob_s, st_s = refs
    C = HG_CHUNK
    n = T // C
    RB = HG_ROWS
    heads = [slice(h * LANES, (h + 1) * LANES) for h in range(HG_HEADS)]

    ti = lax.broadcasted_iota(jnp.int32, (RB, RB), 0)
    si = lax.broadcasted_iota(jnp.int32, (RB, RB), 1)
    sh = C.bit_length() - 1
    same = (ti >> sh) == (si >> sh)
    tri = ((same & (si <= ti)).astype(BF16), (same & (si >= ti)).astype(BF16))

    for r in range(T // RB):
        rows = slice(r * RB, (r + 1) * RB)
        qr = q_ref[rows, :]
        qh = (0.5 * HG_DK ** -0.5) * qr
        q_s[rows, :] = qh + qh * jnp.tanh(0.5 * qr)
        v_s[rows, :] = v_ref[rows, :].astype(BF16)
        for d, (f_ref, k_s, b_s) in enumerate(((ff_ref, kf_s, bf_s), (fb_ref, kb_s, bb_s))):
            g0 = gam_ref[d, 0:1, :]
            g1 = gam_ref[d, 1:2, :]
            m = jnp.maximum(g0, g1)
            e0 = jnp.exp(g0 - m)
            lbd = e0 / (e0 + jnp.exp(g1 - m))
            ck = 0.5 * (1.0 - lbd)
            pt = ck * jnp.tanh(0.5 * f_ref[rows, :])
            k_s[rows, :] = ck - pt
            lf = jnp.log((lbd + ck) + pt)
            hi = lf.astype(BF16)
            r1 = lf - hi.astype(F32)
            mid = r1.astype(BF16)
            lo = (r1 - mid.astype(F32)).astype(BF16)
            cs = _dot(tri[d], jnp.concatenate([hi, mid, lo], axis=1))
            b_s[rows, :] = cs[:, 0:D_HGRN] + (cs[:, D_HGRN:2 * D_HGRN] + cs[:, 2 * D_HGRN:3 * D_HGRN])

    for d in range(2):
        for h in range(HG_HEADS):
            st_s[d * HG_HEADS + h] = s0_ref[d, h].T if has_s0 else jnp.zeros((LANES, LANES), F32)

    ri =lax.broadcasted_iota(jnp.int32, (C, C), 0)
    ci = lax.broadcasted_iota(jnp.int32, (C, C), 1)

    def body(i, carry):
        for d, (k_s, b_s, o_s) in enumerate(((kf_s, bf_s, of_s), (kb_s, bb_s, ob_s))):
            c = i if d == 0 else n - 1 - i
            sl = pl.ds(pl.multiple_of(c * C, C), C)
            for h in range(HG_HEADS):
                q = q_s[sl, heads[h]]
                k = k_s[sl, heads[h]]
                b = b_s[sl, heads[h]]
                v = v_s[sl, heads[h]]
                bm = b[C // 2:C // 2 + 1, :]
                g = b[C - 1:C, :] if d == 0 else b[0:1, :]
                qi = q * jnp.exp(b - bm)
                qe = (qi * jnp.exp(bm)).astype(BF16)
                ki = k * jnp.exp(bm - b)
                kd = (ki * jnp.exp(g - bm)).astype(BF16)
                s = _dot_nt(qi.astype(BF16), ki.astype(BF16))
                p = jnp.where((ci <= ri) if d == 0 else (ci >= ri), s, 0.0).astype(BF16)
                st = st_s[d * HG_HEADS + h]
                o_s[sl, heads[h]] = _dot(p, v) + _dot(qe, st.T.astype(BF16))
                st_s[d * HG_HEADS + h] = st * jnp.exp(g) + _dot_tn(v, kd)
        return carry

    lax.fori_loop(0, n, body, 0, unroll=HG_UNROLL)
    if want_state:
        for d in range(2):
            for h in range(HG_HEADS):
                st_ref[d, h] = st_s[d * HG_HEADS + h].T

    for r in range(T // RB):
        rows = slice(r * RB, (r + 1) * RB)
        for h in range(HG_HEADS):
            o = of_s[rows, heads[h]] + ob_s[rows, heads[h]]
            ms = jnp.mean(o * o, axis=-1, keepdims=True)
            oh = 0.5 * og_ref[rows, heads[h]]
            y = o * lax.rsqrt(ms + EPS) * gain_ref[...] * (oh + oh * jnp.tanh(oh))
            y_ref[rows, heads[h]] = y.astype(BF16)


def _hgrn(hg5, gamma, gain, s0, T, want_state):
    B = hg5.shape[1]
    has_s0 = s0 is not None

    def col(k):
        return pl.BlockSpec((None, None, T, D_HGRN), lambda b, k=k: (k, b, 0, 0))

    st_spec = pl.BlockSpec((None, None, 2, HG_HEADS, HG_DK, HG_DK), lambda b: (b, 0, 0, 0, 0, 0))
    in_specs = [col(0), col(1), col(2), col(3), col(4),
                pl.BlockSpec((2, 2, D_HGRN), lambda b: (0, 0, 0)),
                pl.BlockSpec((1, LANES), lambda b: (0, 0))]
    args = [hg5, hg5, hg5, hg5, hg5, gamma, gain]
    if has_s0:
        in_specs.append(st_spec)
        args.append(s0)
    out_specs = [pl.BlockSpec((None, T, D_HGRN), lambda b: (b, 0, 0))]
    out_shape = [jax.ShapeDtypeStruct((B, T, D_HGRN), BF16)]
    if want_state:
        out_specs.append(st_spec)
        out_shape.append(jax.ShapeDtypeStruct((B, 1, 2, HG_HEADS, HG_DK, HG_DK), F32))
    res = pl.pallas_call(
        functools.partial(_hgrn_kernel, T, has_s0, want_state),
        grid=(B,),
        in_specs=in_specs,
        out_specs=out_specs,
        out_shape=out_shape,
        scratch_shapes=[pltpu.VMEM((T, D_HGRN), F32), pltpu.VMEM((T, D_HGRN), BF16)]
        + [pltpu.VMEM((T, D_HGRN), F32)] * 6
        + [pltpu.VMEM((2 * HG_HEADS, HG_DK, HG_DK), F32)],
        compiler_params=_cp(("arbitrary",)),
    )(*args)
    return res if want_state else (res[0], None)


def _mix_out_kernel(hs_ref, gr_ref, yh_ref, x_ref, mod_ref, nl_ref, wo_ref, nf_ref, wr_ref,
                    x1_ref, ha_ref, at_ref):
    wr = wr_ref[...]
    wo = wo_ref[...].astype(BF16)
    g1 = mod_ref[0, 0:1, :]
    sh2 = mod_ref[0, 1:2, :]
    gain2 = nf_ref[...] * (1.0 + mod_ref[0, 2:3, :])
    lane = lax.broadcasted_iota(jnp.int32, (TOK_TILE, LANES), 1)
    for s in range(x_ref.shape[0] // TOK_TILE):
        rows = slice(s * TOK_TILE, (s + 1) * TOK_TILE)
        hs = jnp.concatenate([hs_ref[c, rows, :] for c in range(D_LRU // LANES)], axis=1)
        ms = jnp.mean(hs * hs, axis=-1, keepdims=True)
        gr = gr_ref[rows, :]
        gelu = 0.5 * gr * (1.0 + jnp.tanh(0.7978845608028654 * (gr + 0.044715 * (gr * gr * gr))))
        y_lru = hs * lax.rsqrt(ms + EPS) * nl_ref[...] * gelu
        ycat = jnp.concatenate([y_lru.astype(BF16), yh_ref[rows, :]], axis=-1)
        x1 = x_ref[rows, :] + g1 * _dot(ycat, wo)
        x1_ref[rows, :] = x1
        yield
        ms = jnp.mean(x1 * x1, axis=-1, keepdims=True)
        h2 = x1 * lax.rsqrt(ms + EPS) * gain2 + sh2
        h2_hi = h2.astype(BF16)
        h2_mid = (h2 - h2_hi.astype(F32)).astype(BF16)
        p_hi = _dot(h2_hi, wr)
        logits = p_hi[:, 0:LANES] + (p_hi[:, LANES:2 * LANES] + _dot(h2_mid, wr)[:, 0:LANES])
        logits = jnp.where(lane < N_EXPERTS, logits, -jnp.inf)
        e = jnp.exp(logits - jnp.max(logits, axis=-1, keepdims=True))
        aff = e / jnp.sum(e, axis=-1, keepdims=True)
        w = at_ref.shape[1]
        r, c = (s * TOK_TILE) // w, (s * TOK_TILE) % w
        at_ref[r * N_EXPERTS:(r + 1) * N_EXPERTS, c:c + TOK_TILE] = aff.T[0:N_EXPERTS, :]
        hi = aff.astype(BF16).astype(F32)
        r1 = aff - hi
        mid = r1.astype(BF16).astype(F32)
        lo = (r1 - mid).astype(BF16).astype(F32)
        split = hi + pltpu.roll(mid, N_EXPERTS, 1) + pltpu.roll(lo, 2 * N_EXPERTS, 1)
        ha_ref[rows, 0:D_MODEL] = h2_hi
        ha_ref[rows, D_MODEL:D_AUG] = split.astype(BF16)
        yield


def _mix_out_part(hs4, gr, yh, x, mods, msel, norm_lru, w_out, norm_ffn, wr_pad, T, tile_rows):
    n = x.shape[0]
    assert T % TOK_TILE == 0 and (tile_rows % T == 0 or T % tile_rows == 0)
    q = max(T // tile_rows, 1)
    at_rows = N_EXPERTS * max(tile_rows // T, 1)
    at_spec = pl.BlockSpec((at_rows, min(T, tile_rows)), lambda i: (i // q, i % q))
    mod_map = _mod_map(msel, q)
    tile = lambda w: pl.BlockSpec((tile_rows, w), lambda i: (i, 0))
    const = lambda r, w: pl.BlockSpec((r, w), lambda i: (0, 0))
    return dict(
        kernel=_mix_out_kernel,
        steps=n // tile_rows,
        args=[hs4, gr, yh, x, mods, norm_lru, w_out, norm_ffn, wr_pad],
        in_specs=[pl.BlockSpec((D_LRU // LANES, tile_rows, LANES), lambda i: (0, i, 0)),
                  tile(D_LRU), tile(D_HGRN), tile(D_MODEL),
                  pl.BlockSpec((1, mods.shape[1], D_MODEL), mod_map),
                  const(1, D_LRU), const(D_MODEL, D_MODEL), const(1, D_MODEL), const(D_MODEL, 2 * LANES)],
        out_specs=[tile(D_MODEL), tile(D_AUG), at_spec],
        out_shape=[jax.ShapeDtypeStruct((n, D_MODEL), F32),
                   jax.ShapeDtypeStruct((n, D_AUG), BF16),
                   jax.ShapeDtypeStruct((n // T * N_EXPERTS, T), F32)],
        scratch=[],
    )


def _select_kernel(cap, at_ref, slot_ref):
    a = at_ref[...]
    R, T = a.shape
    ones = jnp.ones((T, LANES), BF16)
    su = (lax.broadcasted_iota(jnp.int32, (T, T), 0)
          < lax.broadcasted_iota(jnp.int32, (T, T), 1)).astype(BF16)

    def wide(x):
        return jnp.concatenate([x] * (T // LANES), axis=1)

    def count(mask):
        return _dot(mask.astype(BF16), ones)

    lo = jnp.zeros((R, LANES), F32)
    hi = jnp.full((R, LANES), 2.0, F32)
    for _ in range(SEL_BISECT):
        mid = 0.5 * (lo + hi)
        ge = count(a >= wide(mid)) >= cap
        lo = jnp.where(ge, mid, lo)
        hi = jnp.where(ge, hi, mid)

    def cond(st):
        return (st[0] < T) & (jnp.min(st[2]) < cap)

    def body(st):
        it, cur, n, thr = st
        m = jnp.max(jnp.where(a < wide(cur), a, -1.0), axis=1, keepdims=True)
        m = jnp.broadcast_to(m, (R, LANES))
        c = count(a >= wide(m))
        act = n < cap
        return it + 1, jnp.where(act, m, cur), jnp.where(act, c, n), jnp.where(act, m, thr)

    _, _, _, thr = lax.while_loop(cond, body, (jnp.int32(0), hi, count(a >= wide(hi)), lo))
    thr_w = wide(thr)
    gt = a > thr_w
    eq = a == thr_w
    need = wide(cap - count(gt))
    sel = gt | (eq & (_dot(eq.astype(BF16), su) < need))
    slot_ref[...] = jnp.where(sel, _dot(sel.astype(BF16), su), -1.0).astype(jnp.int32)


def _select(at, T):
    cap = EC_CAPACITY_FACTOR * T // N_EXPERTS
    R = at.shape[0]
    return pl.pallas_call(
        functools.partial(_select_kernel, cap),
        grid=(1,),
        in_specs=[pl.BlockSpec((R, T), lambda i: (0, 0))],
        out_specs=pl.BlockSpec((R, T), lambda i: (0, 0)),
        out_shape=jax.ShapeDtypeStruct((R, T), jnp.int32),
        compiler_params=_cp(("arbitrary",)),
    )(at)


def _onehot(cap, slot_ref, g):
    j = lax.broadcasted_iota(jnp.int32, (cap, slot_ref.shape[1]), 0)
    r0 = g * N_EXPERTS
    return jnp.concatenate([(slot_ref[r0 + e:r0 + e + 1, :] == j).astype(BF16) for e in range(N_EXPERTS)], axis=0)


def _dispatch_kernel(T, cap, slot_ref, ha_ref, xd_ref):
    for g in range(ha_ref.shape[0] // T):
        rows = _dot(_onehot(cap, slot_ref, g), ha_ref[g * T:(g + 1) * T, :])
        for e in range(N_EXPERTS):
            xd_ref[e, g * cap:(g + 1) * cap, :] = rows[e * cap:(e + 1) * cap].astype(BF16)
        yield


def _dispatch_part(slot, ha, T, G):
    cap = EC_CAPACITY_FACTOR * T // N_EXPERTS
    B = ha.shape[0] // T
    return dict(
        kernel=functools.partial(_dispatch_kernel, T, cap),
        steps=B // G,
        args=[slot, ha],
        in_specs=[pl.BlockSpec((G * N_EXPERTS, T), lambda b: (b, 0)),
                  pl.BlockSpec((G * T, D_AUG), lambda b: (b, 0))],
        out_specs=[pl.BlockSpec((N_EXPERTS, G * cap, D_AUG), lambda b: (0, b, 0))],
        out_shape=[jax.ShapeDtypeStruct((N_EXPERTS, B * cap, D_AUG), BF16)],
        scratch=[],
    )


def _ffn_kernel(xp_ref, xs_ref, wg_ref, wu_ref, wd_ref, yp_ref, ys_ref):
    e = pl.program_id(0)
    lane = lax.broadcasted_iota(jnp.int32, (FFN_TILE, LANES), 1)
    pick = (lane == e) | (lane == e + N_EXPERTS) | (lane == e + 2 * N_EXPERTS)
    wg = wg_ref[...].astype(BF16)
    wu = wu_ref[...].astype(BF16)
    wd = wd_ref[...].astype(BF16)
    for x_ref, y_ref in ((xp_ref, yp_ref), (xs_ref, ys_ref)):
        x = x_ref[:, 0:D_MODEL]
        gate = jnp.sum(jnp.where(pick, x_ref[:, D_MODEL:D_AUG].astype(F32), 0.0), axis=-1, keepdims=True)
        hg = _dot(x, wg)
        hid = (hg * _sigmoid(hg)) * _dot(x, wu)
        y_ref[...] = (_dot(hid.astype(BF16), wd) * gate).astype(BF16)


def _ffn(xd_p, xd_s, w_gate, w_up, w_down):
    rows = xd_p.shape[1]
    assert xd_s.shape[1] == rows and rows % FFN_TILE == 0
    xspec = pl.BlockSpec((None, FFN_TILE, D_AUG), lambda e, m: (e, m, 0))
    wspec = pl.BlockSpec((None, None, D_MODEL, D_MODEL), lambda e, m: (0, e, 0, 0))
    yspec = pl.BlockSpec((None, FFN_TILE, D_MODEL), lambda e, m: (e, m, 0))
    yshape = jax.ShapeDtypeStruct((N_EXPERTS, rows, D_MODEL), BF16)
    return pl.pallas_call(
        _ffn_kernel,
        grid=(N_EXPERTS, rows // FFN_TILE),
        in_specs=[xspec, xspec, wspec, wspec, wspec],
        out_specs=[yspec, yspec],
        out_shape=[yshape, yshape],
        compiler_params=_cp(("arbitrary", "arbitrary")),
    )(xd_p, xd_s, w_gate, w_up, w_down)


def _combine_kernel(cap, slot_ref, yd_ref, x1_ref, mod_ref, nf_ref, o_ref):
    W = slot_ref.shape[1]
    for g in range(x1_ref.shape[0] // W):
        rows = slice(g * W, (g + 1) * W)
        yd = jnp.concatenate([yd_ref[e, g * cap:(g + 1) * cap, :] for e in range(N_EXPERTS)], axis=0)
        acc = _dot_tn(_onehot(cap, slot_ref, g), yd)
        x2 = x1_ref[rows, :] + mod_ref[0, 3:4, :] * acc
        ms = jnp.mean(x2 * x2, axis=-1, keepdims=True)
        o_ref[rows, :] = x2 * lax.rsqrt(ms + EPS) * nf_ref[...]
        yield


def _combine_part(slot, yd, x1, mods, msel, norm_final, T, tok):
    cap = EC_CAPACITY_FACTOR * T // N_EXPERTS
    n = x1.shape[0]
    G, h = (tok // T, 1) if tok >= T else (1, T // tok)
    assert G == 1 or not msel[1]
    mod_map = _mod_map(msel, h)
    return dict(
        kernel=functools.partial(_combine_kernel, cap),
        steps=n // tok,
        args=[slot, yd, x1, mods, norm_final],
        in_specs=[pl.BlockSpec((G * N_EXPERTS, min(T, tok)), lambda i: (i // h, i % h)),
                  pl.BlockSpec((N_EXPERTS, G * cap, D_MODEL), lambda i: (0, i // h, 0)),
                  pl.BlockSpec((tok, D_MODEL), lambda i: (i, 0)),
                  pl.BlockSpec((1, mods.shape[1], D_MODEL), mod_map),
                  pl.BlockSpec((1, D_MODEL), lambda i: (0, 0))],
        out_specs=[pl.BlockSpec((tok, D_MODEL), lambda i: (i, 0))],
        out_shape=[jax.ShapeDtypeStruct((n, D_MODEL), F32)],
        scratch=[],
    )


def _gate_tiles(wa, wx, ba, bx):
    n_ct = D_LRU // LANES

    def dense(w):
        w2 = w.reshape(n_ct, 2, LRU_BW, LRU_BW)
        z = jnp.zeros((n_ct, LRU_BW, LRU_BW), w.dtype)
        top = jnp.concatenate([w2[:, 0], z], axis=2)
        bot = jnp.concatenate([z, w2[:, 1]], axis=2)
        return jnp.concatenate([top, bot], axis=1)

    tiles = 0.5 * jnp.concatenate([dense(wa[0]), dense(wx[0]), dense(wa[1]), dense(wx[1])], axis=-1)
    t = lambda v: v.reshape(n_ct, 1, LANES)
    bias = 0.5 * jnp.concatenate([t(ba[0]), t(bx[0]), t(ba[1]), t(bx[1])], axis=-1)
    hi = bias.astype(BF16)
    lo = (bias - hi.astype(F32)).astype(BF16)
    pad = jnp.zeros((n_ct, LANES - 2, 4 * LANES), BF16)
    return jnp.concatenate([tiles.astype(BF16), hi, lo, pad], axis=1)


def _router_split(w):
    wp = jnp.pad(w, ((0, 0), (0, LANES - N_EXPERTS)))
    hi = wp.astype(BF16)
    mid = (wp - hi.astype(F32)).astype(BF16)
    return jnp.concatenate([hi, mid], axis=1)


def _proj(x, mods, msel, p, T):
    return _proj_part(x.reshape(-1, D_MODEL), mods, msel, p["norm_mix"], p["w_in"], T)


def _lru(xr4, B, T, row_len, h0, p, unroll):
    n_ct = D_LRU // LANES
    return _lru_part(xr4.reshape(n_ct, B, T, LANES), p["conv_w"], p["conv_b"], p["gate_w"], p["lam"],
                     h0, T, row_len, unroll)


def _mix(x, mods, msel, p, T, hs4, gr, yh, tile_rows):
    B = x.shape[0]
    return _mix_out_part(hs4.reshape(D_LRU // LANES, B * T, LANES), gr, yh.reshape(B * T, D_HGRN),
                         x.reshape(B * T, D_MODEL), mods, msel,
                         p["norm_lru"], p["w_out"], p["norm_ffn"], p["w_router"], T, tile_rows)


def kernel(x_prompt, x_sample, state_lru, state_hgrn, c, c_ctx, w_ada, b_ada, norm_mix, w_in, conv_w,
           conv_b, lru_wa, lru_ba, lru_wx, lru_bx, lru_lambda, norm_lru, hgrn_gamma, norm_hgrn, w_out,
           norm_ffn, w_router, w_gate, w_up, w_down, norm_final):
    assert w_ada.shape[0] == 1 and hgrn_gamma.shape[1] == 2, "one trunk layer"
    l = 0
    Bp, Tp, _ = x_prompt.shape
    Bs, Ts, _ = x_sample.shape

    cond = jnp.concatenate([c_ctx[None, :], c, jnp.zeros((16 - 1 - Bs, D_MODEL), F32)], axis=0)
    n1 = 2 * D_MODEL
    ((m1,),) = _run([_adaln_part(cond, w_ada, b_ada, 0, n1, D_MODEL)])
    m1 = m1.reshape(16, 2, D_MODEL)
    sel_p, sel_s = (0, False), (1, True)

    p = {
        "norm_mix": norm_mix[l][None, :], "w_in": w_in,
        "conv_w": conv_w[l], "conv_b": conv_b[l][None, :],
        "gate_w": _gate_tiles(lru_wa[l], lru_wx[l], lru_ba[l], lru_bx[l]),
        "lam": lru_lambda[l], "norm_lru": norm_lru[l][None, :],
        "gamma": hgrn_gamma, "norm_hgrn": norm_hgrn[l][None, :],
        "w_out": w_out[l], "norm_ffn": norm_ffn[l][None, :],
        "w_router": _router_split(w_router[l]),
    }
    zeros_p = jnp.zeros((Bp, 2 * D_LRU), F32)
    proj_p = _proj(x_prompt, m1, sel_p, p, Tp)
    n2 = (N_MOD - 2) * D_MODEL
    (xr4_p, gr_p, hg5_p), (m2,) = _run([proj_p, _adaln_part(cond, w_ada, b_ada, n1, n2, n2 // proj_p["steps"])])
    m2 = m2.reshape(16, N_MOD - 2, D_MODEL)
    (xr4_s, gr_s, hg5_s), (hs4_p, last_f, last_b) = _run(
        [_proj(x_sample, m1, sel_s, p, Ts), _lru(xr4_p, Bp, Tp, Tp, zeros_p, p, True)])
    yh_p, st_p = _hgrn(hg5_p.reshape(N_HG_IN, Bp, Tp, D_HGRN), p["gamma"], p["norm_hgrn"], None, Tp, True)
    ((x1p, ha_p, at_p),) = _run([_mix(x_prompt, m2, sel_p, p, Tp, hs4_p, gr_p, yh_p, MIX_TILE)])
    slot_p = _select(at_p, Tp)
    assert state_lru.shape[1] == 1
    ((hs4_s, _, _),) = _run([_lru(xr4_s, Bs, Ts, GRID_W, state_lru.reshape(Bs, 2 * D_LRU), p, 8)])
    yh_s, _ = _hgrn(hg5_s.reshape(N_HG_IN, Bs, Ts, D_HGRN), p["gamma"], p["norm_hgrn"], state_hgrn, Ts, False)
    fused_tile = MIX_TILE // 2
    (x1s, ha_s, at_s), (xd_p,) = _run(
        [_mix(x_sample, m2, sel_s, p, Ts, hs4_s, gr_s, yh_s, fused_tile),
         _dispatch_part(slot_p, ha_p, Tp, fused_tile // Tp)])
    slot_s = _select(at_s, Ts)
    ((xd_s,),) = _run([_dispatch_part(slot_s, ha_s, Ts, max(ROUTE_TOKENS // Ts, 1))])

    yd_p, yd_s = _ffn(xd_p, xd_s, w_gate, w_up, w_down)
    nf = norm_final[None, :]
    (y_sample,), (y_prompt,) = _run(
        [_combine_part(slot_s, yd_s, x1s, m2, sel_s, nf, Ts, COMBINE_TOKENS),
         _combine_part(slot_p, yd_p, x1p, m2, sel_p, nf, Tp, COMBINE_TOKENS)])
    y_prompt = y_prompt.reshape(Bp, Tp, D_MODEL)
    y_sample = y_sample.reshape(Bs, Ts, D_MODEL)
    new_state_lru = jnp.stack([last_f, last_b], axis=1)[:, None]
    new_state_hgrn = st_p
    return (y_prompt, y_sample, new_state_lru, new_state_hgrn)
```

```python
import functools

import jax
import jax.numpy as jnp
from jax import lax
from jax.experimental import pallas as pl
from jax.experimental.pallas import tpu as pltpu

F32 = jnp.float32
BF16 = jnp.bfloat16

D_MODEL = 1024
D_LRU = 512
D_HGRN = 512
HG_HEADS = 4
HG_DK = 128
LRU_BLOCKS = 8
LRU_BW = 64
LRU_C = 8.0
N_EXPERTS = 16
EC_CAPACITY_FACTOR = 2
N_MOD = 6
D_IN = 7 * 512
N_HG_IN = 5
GRID_W = 64
EPS = 1e-6
LOG2E = 1.4426950408889634
TINY = 1e-37

LANES = 128
TOK_TILE = 256
PROJ_TILE = 512
MIX_TILE = 1024
LRU_GROUP = 8
PITCH_PAD = 8
HG_CHUNK = 64
HG_UNROLL = 16
HG_ROWS = 256
FFN_TILE = 512
ROUTE_TOKENS = 1024
COMBINE_TOKENS = 512
SEL_BISECT = 20
D_AUG = D_MODEL + LANES
VMEM_LIMIT = 56 * 1024 * 1024


def _cp(sem, vmem=VMEM_LIMIT):
    return pltpu.CompilerParams(dimension_semantics=sem, vmem_limit_bytes=vmem)


def _sigmoid(x):
    return 0.5 * (1.0 + jnp.tanh(0.5 * x))


def _dot(a, b):
    return jnp.dot(a, b, preferred_element_type=F32)


def _dot_nt(a, b):
    return lax.dot_general(a, b, (((1,), (1,)), ((), ())), preferred_element_type=F32)


def _dot_tn(a, b):
    return lax.dot_general(a, b, (((0,), (0,)), ((), ())), preferred_element_type=F32)


def _adaln_kernel(c_ref, w_ref, b_ref, o_ref):
    c = c_ref[...]
    s = (c * _sigmoid(c)).astype(BF16)
    o_ref[...] = _dot(s, w_ref[...].astype(BF16)) + b_ref[...]
    yield


def _adaln_part(cond, w, b, col0, ncols, tn):
    n = cond.shape[0]
    j0 = col0 // tn
    return dict(
        kernel=_adaln_kernel,
        steps=ncols // tn,
        args=[cond, w, b],
        in_specs=[pl.BlockSpec((n, D_MODEL), lambda j: (0, 0)),
                  pl.BlockSpec((None, D_MODEL, tn), lambda j: (0, 0, j0 + j)),
                  pl.BlockSpec((1, tn), lambda j: (0, j0 + j))],
        out_specs=[pl.BlockSpec((n, tn), lambda j: (0, j))],
        out_shape=[jax.ShapeDtypeStruct((n, ncols), F32)],
        scratch=[],
    )


def _proj_kernel(x_ref, mod_ref, g_ref, w_ref, xr_ref, gr_ref, hg_ref):
    x = x_ref[...]
    ms = jnp.mean(x * x, axis=-1, keepdims=True)
    y = x * lax.rsqrt(ms + EPS) * g_ref[...]
    h = y * (1.0 + mod_ref[0, 1:2, :]) + mod_ref[0, 0:1, :]
    h = h.astype(BF16)
    W = 2 * LANES
    for j in range(D_IN // W):
        res = _dot(h, w_ref[:, j * W:(j + 1) * W].astype(BF16))
        k, half = divmod(j * W, D_HGRN)
        if k == 0:
            for c in range(W // LANES):
                xr_ref[half // LANES + c] = res[:, c * LANES:(c + 1) * LANES]
        elif k == 1:
            gr_ref[:, half:half + W] = res.astype(BF16)
        else:
            hg_ref[k - 2, :, half:half + W] = res
        yield


def _run(parts):
    steps = parts[0]["steps"]
    assert all(p["steps"] == steps for p in parts)
    n_in = [len(p["args"]) for p in parts]
    n_out = [len(p["out_shape"]) for p in parts]
    n_sc = [len(p["scratch"]) for p in parts]

    def body(*refs):
        ins, outs, scr = refs[:sum(n_in)], refs[sum(n_in):sum(n_in) + sum(n_out)], refs[sum(n_in) + sum(n_out):]
        i = o = s = 0
        gens = []
        for p, a, b, c in zip(parts, n_in, n_out, n_sc):
            gens.append(p["kernel"](*ins[i:i + a], *outs[o:o + b], *scr[s:s + c]))
            i, o, s = i + a, o + b, s + c
        while gens:
            for g in list(gens):
                if next(g, StopIteration) is StopIteration:
                    gens.remove(g)

    res = pl.pallas_call(
        body,
        grid=(steps,),
        in_specs=[s for p in parts for s in p["in_specs"]],
        out_specs=[s for p in parts for s in p["out_specs"]],
        out_shape=[s for p in parts for s in p["out_shape"]],
        scratch_shapes=[s for p in parts for s in p["scratch"]],
        compiler_params=_cp(("arbitrary",)),
    )(*[a for p in parts for a in p["args"]])
    out, o = [], 0
    for b in n_out:
        out.append(res[o:o + b])
        o += b
    return out


def _mod_map(msel, steps_per_request):
    row0, per_request = msel
    if not per_request:
        return lambda i: (row0, 0, 0)
    assert steps_per_request >= 1
    return lambda i: (row0 + i // steps_per_request, 0, 0)


def _proj_part(x, mods, msel, gain, w, T):
    n = x.shape[0]
    assert T % PROJ_TILE == 0 or (PROJ_TILE % T == 0 and not msel[1])
    mod_map = _mod_map(msel, T // PROJ_TILE)
    n_ct = D_LRU // LANES
    return dict(
        kernel=_proj_kernel,
        steps=n // PROJ_TILE,
        args=[x, mods, gain, w],
        in_specs=[pl.BlockSpec((PROJ_TILE, D_MODEL), lambda i: (i, 0)),
                  pl.BlockSpec((1, mods.shape[1], D_MODEL), mod_map),
                  pl.BlockSpec((1, D_MODEL), lambda i: (0, 0)),
                  pl.BlockSpec((None, D_MODEL, D_IN), lambda i: (0, 0, 0), pipeline_mode=pl.Buffered(1))],
        out_specs=[pl.BlockSpec((n_ct, PROJ_TILE, LANES), lambda i: (0, i, 0)),
                   pl.BlockSpec((PROJ_TILE, D_LRU), lambda i: (i, 0)),
                   pl.BlockSpec((N_HG_IN, PROJ_TILE, D_HGRN), lambda i: (0, i, 0))],
        out_shape=[jax.ShapeDtypeStruct((n_ct, n, LANES), F32),
                   jax.ShapeDtypeStruct((n, D_LRU), BF16),
                   jax.ShapeDtypeStruct((N_HG_IN, n, D_HGRN), F32)],
        scratch=[],
    )


def _lru_kernel(T, row_len, unroll, xr_ref, cw_ref, cb_ref, w_ref, lam_ref, h0f_ref, h0b_ref,
                hs_ref, lf_ref, lb_ref, xpad, af, uf, ab, ub, hf_s, hb_s):
    pitch = T + PITCH_PAD
    pos = lax.broadcasted_iota(jnp.int32, (T, LANES), 0) & (row_len - 1)
    cw = cw_ref[...]
    w0 = jnp.where(pos >= 2, cw[0:1], 0.0)
    w1 = jnp.where(pos >= 1, cw[1:2], 0.0)
    w2 = jnp.broadcast_to(cw[2:3], (T, LANES))
    w3 = jnp.where(pos <= row_len - 2, cw[3:4], 0.0)
    cb = cb_ref[...]
    nl = -lam_ref[...]
    sp = jnp.maximum(nl, 0.0) + jnp.log1p(jnp.exp(-jnp.abs(nl)))
    c2 = (-0.5 * LRU_C * LOG2E) * sp
    w = w_ref[0]
    ones = (lax.broadcasted_iota(jnp.int32, (T, LANES), 1) < 2).astype(BF16)
    for s in range(2):
        xpad[s, 0:8, :] = jnp.zeros((8, LANES), F32)
        xpad[s, T + 8:T + 16, :] = jnp.zeros((8, LANES), F32)
    for b in range(LRU_GROUP):
        x = xr_ref[b]
        xp = xpad.at[b % 2]
        xp[8:T + 8, :] = x
        xc = w0 * xp[6:T + 6, :] + w1 * xp[7:T + 7, :] + w2 * x + w3 * xp[9:T + 9, :] + cb
        xh = 0.5 * xc
        z = _dot(jnp.concatenate([xc.astype(BF16), ones], axis=1), w)
        rows = pl.ds(b * pitch, T)
        for d, (a_s, u_s) in enumerate(((af, uf), (ab, ub))):
            tr = jnp.tanh(z[:, (2 * d) * LANES:(2 * d + 1) * LANES])
            ti = jnp.tanh(z[:, (2 * d + 1) * LANES:(2 * d + 2) * LANES])
            a = jnp.exp2(c2[d:d + 1] + c2[d:d + 1] * tr)
            om = 1.0 - a * a
            a_s[rows, :] = a
            u_s[rows, :] = (om * lax.rsqrt(jnp.maximum(om, TINY))) * (xh + ti * xh)
        yield

    def step(t, carry):
        hf, hb = carry
        rf = pl.ds(t, LRU_GROUP, stride=pitch)
        hf = af[rf, :] * hf + uf[rf, :]
        hf_s[rf, :] = hf
        rb = pl.ds(T - 1 - t, LRU_GROUP, stride=pitch)
        hb = ab[rb, :] * hb + ub[rb, :]
        hb_s[rb, :] = hb
        return hf, hb

    carry = (h0f_ref[...], h0b_ref[...])
    if unroll is True:
        for t in range(T):
            carry = step(t, carry)
            if t % (T // 8) == T // 8 - 1 and t != T - 1:
                yield
    else:
        carry = lax.fori_loop(0, T, step, carry, unroll=unroll)
    hf, hb = carry
    lf_ref[...] = hf
    lb_ref[...] = hb
    for b in range(LRU_GROUP):
        rows = pl.ds(b * pitch, T)
        hs_ref[b] = (hf_s[rows, :] + hb_s[rows, :]).astype(BF16)


def _lru_part(xr4, conv_w, conv_b, w_tiles, lam, h0, T, row_len, unroll):
    n_ct, B = xr4.shape[:2]
    rows = LRU_GROUP * (T + PITCH_PAD)
    vec = pl.BlockSpec((LRU_GROUP, LANES), lambda i: (i // n_ct, i % n_ct))
    vec_b = pl.BlockSpec((LRU_GROUP, LANES), lambda i: (i // n_ct, n_ct + i % n_ct))
    slab = pl.BlockSpec((None, LRU_GROUP, T, LANES), lambda i: (i % n_ct, i // n_ct, 0, 0))
    return dict(
        kernel=functools.partial(_lru_kernel, T, row_len, unroll),
        steps=(B // LRU_GROUP) * n_ct,
        args=[xr4, conv_w, conv_b, w_tiles, lam, h0, h0],
        in_specs=[slab,
                  pl.BlockSpec((4, LANES), lambda i: (0, i % n_ct)),
                  pl.BlockSpec((1, LANES), lambda i: (0, i % n_ct)),
                  pl.BlockSpec((1, 2 * LANES, 4 * LANES), lambda i: (i % n_ct, 0, 0)),
                  pl.BlockSpec((2, LANES), lambda i: (0, i % n_ct)),
                  vec, vec_b],
        out_specs=[slab, vec, vec],
        out_shape=[jax.ShapeDtypeStruct((n_ct, B, T, LANES), BF16),
                   jax.ShapeDtypeStruct((B, D_LRU), F32),
                   jax.ShapeDtypeStruct((B, D_LRU), F32)],
        scratch=[pltpu.VMEM((2, T + 16, LANES), F32)] + [pltpu.VMEM((rows, LANES), F32)] * 6,
    )


def _hgrn_kernel(T, has_s0, want_state, *refs):
    q_ref, ff_ref, fb_ref, v_ref, og_ref, gam_ref, gain_ref = refs[:7]
    refs = refs[7:]
    if has_s0:
        s0_ref, refs = refs[0], refs[1:]
    y_ref, refs = refs[0], refs[1:]
    if want_state:
        st_ref, refs = refs[0], refs[1:]
    q_s, v_s, kf_s, kb_s, bf_s, bb_s, of_s, ob_s, st_s = refs
    C = HG_CHUNK
    n = T // C
    RB = HG_ROWS
    heads = [slice(h * LANES, (h + 1) * LANES) for h in range(HG_HEADS)]

    ti = lax.broadcasted_iota(jnp.int32, (RB, RB), 0)
    si = lax.broadcasted_iota(jnp.int32, (RB, RB), 1)
    sh = C.bit_length() - 1
    same = (ti >> sh) == (si >> sh)
    tri = ((same & (si <= ti)).astype(BF16), (same & (si >= ti)).astype(BF16))

    for r in range(T // RB):
        rows = slice(r * RB, (r + 1) * RB)
        qr = q_ref[rows, :]
        qh = (0.5 * HG_DK ** -0.5) * qr
        q_s[rows, :] = qh + qh * jnp.tanh(0.5 * qr)
        v_s[rows, :] = v_ref[rows, :].astype(BF16)
        for d, (f_ref, k_s, b_s) in enumerate(((ff_ref, kf_s, bf_s), (fb_ref, kb_s, bb_s))):
            g0 = gam_ref[d, 0:1, :]
            g1 = gam_ref[d, 1:2, :]
            m = jnp.maximum(g0, g1)
            e0 = jnp.exp(g0 - m)
            lbd = e0 / (e0 + jnp.exp(g1 - m))
            ck = 0.5 * (1.0 - lbd)
            pt = ck * jnp.tanh(0.5 * f_ref[rows, :])
            k_s[rows, :] = ck - pt
            lf = jnp.log((lbd + ck) + pt)
            hi = lf.astype(BF16)
            r1 = lf - hi.astype(F32)
            mid = r1.astype(BF16)
            lo = (r1 - mid.astype(F32)).astype(BF16)
            cs = _dot(tri[d], jnp.concatenate([hi, mid, lo], axis=1))
            b_s[rows, :] = cs[:, 0:D_HGRN] + (cs[:, D_HGRN:2 * D_HGRN] + cs[:, 2 * D_HGRN:3 * D_HGRN])

    for d in range(2):
        for h in range(HG_HEADS):
            st_s[d * HG_HEADS + h] = s0_ref[d, h].T if has_s0 else jnp.zeros((LANES, LANES), F32)

    ri =lax.broadcasted_iota(jnp.int32, (C, C), 0)
    ci = lax.broadcasted_iota(jnp.int32, (C, C), 1)

    def body(i, carry):
        for d, (k_s, b_s, o_s) in enumerate(((kf_s, bf_s, of_s), (kb_s, bb_s, ob_s))):
            c = i if d == 0 else n - 1 - i
            sl = pl.ds(pl.multiple_of(c * C, C), C)
            for h in range(HG_HEADS):
                q = q_s[sl, heads[h]]
                k = k_s[sl, heads[h]]
                b = b_s[sl, heads[h]]
                v = v_s[sl, heads[h]]
                bm = b[C // 2:C // 2 + 1, :]
                g = b[C - 1:C, :] if d == 0 else b[0:1, :]
                qi = q * jnp.exp(b - bm)
                qe = (qi * jnp.exp(bm)).astype(BF16)
                ki = k * jnp.exp(bm - b)
                kd = (ki * jnp.exp(g - bm)).astype(BF16)
                s = _dot_nt(qi.astype(BF16), ki.astype(BF16))
                p = jnp.where((ci <= ri) if d == 0 else (ci >= ri), s, 0.0).astype(BF16)
                st = st_s[d * HG_HEADS + h]
                o_s[sl, heads[h]] = _dot(p, v) + _dot(qe, st.T.astype(BF16))
                st_s[d * HG_HEADS + h] = st * jnp.exp(g) + _dot_tn(v, kd)
        return carry

    lax.fori_loop(0, n, body, 0, unroll=HG_UNROLL)
    if want_state:
        for d in range(2):
            for h in range(HG_HEADS):
                st_ref[d, h] = st_s[d * HG_HEADS + h].T

    for r in range(T // RB):
        rows = slice(r * RB, (r + 1) * RB)
        for h in range(HG_HEADS):
            o = of_s[rows, heads[h]] + ob_s[rows, heads[h]]
            ms = jnp.mean(o * o, axis=-1, keepdims=True)
            oh = 0.5 * og_ref[rows, heads[h]]
            y = o * lax.rsqrt(ms + EPS) * gain_ref[...] * (oh + oh * jnp.tanh(oh))
            y_ref[rows, heads[h]] = y.astype(BF16)


def _hgrn(hg5, gamma, gain, s0, T, want_state):
    B = hg5.shape[1]
    has_s0 = s0 is not None

    def col(k):
        return pl.BlockSpec((None, None, T, D_HGRN), lambda b, k=k: (k, b, 0, 0))

    st_spec = pl.BlockSpec((None, None, 2, HG_HEADS, HG_DK, HG_DK), lambda b: (b, 0, 0, 0, 0, 0))
    in_specs = [col(0), col(1), col(2), col(3), col(4),
                pl.BlockSpec((2, 2, D_HGRN), lambda b: (0, 0, 0)),
                pl.BlockSpec((1, LANES), lambda b: (0, 0))]
    args = [hg5, hg5, hg5, hg5, hg5, gamma, gain]
    if has_s0:
        in_specs.append(st_spec)
        args.append(s0)
    out_specs = [pl.BlockSpec((None, T, D_HGRN), lambda b: (b, 0, 0))]
    out_shape = [jax.ShapeDtypeStruct((B, T, D_HGRN), BF16)]
    if want_state:
        out_specs.append(st_spec)
        out_shape.append(jax.ShapeDtypeStruct((B, 1, 2, HG_HEADS, HG_DK, HG_DK), F32))
    res = pl.pallas_call(
        functools.partial(_hgrn_kernel, T, has_s0, want_state),
        grid=(B,),
        in_specs=in_specs,
        out_specs=out_specs,
        out_shape=out_shape,
        scratch_shapes=[pltpu.VMEM((T, D_HGRN), F32), pltpu.VMEM((T, D_HGRN), BF16)]
        + [pltpu.VMEM((T, D_HGRN), F32)] * 6
        + [pltpu.VMEM((2 * HG_HEADS, HG_DK, HG_DK), F32)],
        compiler_params=_cp(("arbitrary",)),
    )(*args)
    return res if want_state else (res[0], None)


def _mix_out_kernel(hs_ref, gr_ref, yh_ref, x_ref, mod_ref, nl_ref, wo_ref, nf_ref, wr_ref,
                    x1_ref, ha_ref, at_ref):
    wr = wr_ref[...]
    wo = wo_ref[...].astype(BF16)
    g1 = mod_ref[0, 0:1, :]
    sh2 = mod_ref[0, 1:2, :]
    gain2 = nf_ref[...] * (1.0 + mod_ref[0, 2:3, :])
    lane = lax.broadcasted_iota(jnp.int32, (TOK_TILE, LANES), 1)
    for s in range(x_ref.shape[0] // TOK_TILE):
        rows = slice(s * TOK_TILE, (s + 1) * TOK_TILE)
        hs = jnp.concatenate([hs_ref[c, rows, :] for c in range(D_LRU // LANES)], axis=1).astype(F32)
        ms = jnp.mean(hs * hs, axis=-1, keepdims=True)
        gr = gr_ref[rows, :].astype(F32)
        gelu = 0.5 * gr * (1.0 + jnp.tanh(0.7978845608028654 * (gr + 0.044715 * (gr * gr * gr))))
        y_lru = hs * lax.rsqrt(ms + EPS) * nl_ref[...] * gelu
        ycat = jnp.concatenate([y_lru.astype(BF16), yh_ref[rows, :]], axis=-1)
        x1 = x_ref[rows, :] + g1 * _dot(ycat, wo)
        x1_ref[rows, :] = x1
        yield
        ms = jnp.mean(x1 * x1, axis=-1, keepdims=True)
        h2 = x1 * lax.rsqrt(ms + EPS) * gain2 + sh2
        h2_hi = h2.astype(BF16)
        h2_mid = (h2 - h2_hi.astype(F32)).astype(BF16)
        p_hi = _dot(h2_hi, wr)
        logits = p_hi[:, 0:LANES] + (p_hi[:, LANES:2 * LANES] + _dot(h2_mid, wr)[:, 0:LANES])
        logits = jnp.where(lane < N_EXPERTS, logits, -jnp.inf)
        e = jnp.exp(logits - jnp.max(logits, axis=-1, keepdims=True))
        aff = e / jnp.sum(e, axis=-1, keepdims=True)
        w = at_ref.shape[1]
        r, c = (s * TOK_TILE) // w, (s * TOK_TILE) % w
        at_ref[r * N_EXPERTS:(r + 1) * N_EXPERTS, c:c + TOK_TILE] = aff.T[0:N_EXPERTS, :]
        hi = aff.astype(BF16).astype(F32)
        r1 = aff - hi
        mid = r1.astype(BF16).astype(F32)
        lo = (r1 - mid).astype(BF16).astype(F32)
        split = hi + pltpu.roll(mid, N_EXPERTS, 1) + pltpu.roll(lo, 2 * N_EXPERTS, 1)
        ha_ref[rows, 0:D_MODEL] = h2_hi
        ha_ref[rows, D_MODEL:D_AUG] = split.astype(BF16)
        yield


def _mix_out_part(hs4, gr, yh, x, mods, msel, norm_lru, w_out, norm_ffn, wr_pad, T, tile_rows):
    n = x.shape[0]
    assert T % TOK_TILE == 0 and (tile_rows % T == 0 or T % tile_rows == 0)
    q = max(T // tile_rows, 1)
    at_rows = N_EXPERTS * max(tile_rows // T, 1)
    at_spec = pl.BlockSpec((at_rows, min(T, tile_rows)), lambda i: (i // q, i % q))
    mod_map = _mod_map(msel, q)
    tile = lambda w: pl.BlockSpec((tile_rows, w), lambda i: (i, 0))
    const = lambda r, w: pl.BlockSpec((r, w), lambda i: (0, 0))
    return dict(
        kernel=_mix_out_kernel,
        steps=n // tile_rows,
        args=[hs4, gr, yh, x, mods, norm_lru, w_out, norm_ffn, wr_pad],
        in_specs=[pl.BlockSpec((D_LRU // LANES, tile_rows, LANES), lambda i: (0, i, 0)),
                  tile(D_LRU), tile(D_HGRN), tile(D_MODEL),
                  pl.BlockSpec((1, mods.shape[1], D_MODEL), mod_map),
                  const(1, D_LRU), const(D_MODEL, D_MODEL), const(1, D_MODEL), const(D_MODEL, 2 * LANES)],
        out_specs=[tile(D_MODEL), tile(D_AUG), at_spec],
        out_shape=[jax.ShapeDtypeStruct((n, D_MODEL), F32),
                   jax.ShapeDtypeStruct((n, D_AUG), BF16),
                   jax.ShapeDtypeStruct((n // T * N_EXPERTS, T), F32)],
        scratch=[],
    )


def _select_kernel(cap, at_ref, slot_ref):
    a = at_ref[...]
    R, T = a.shape
    ones = jnp.ones((T, LANES), BF16)
    su = (lax.broadcasted_iota(jnp.int32, (T, T), 0)
          < lax.broadcasted_iota(jnp.int32, (T, T), 1)).astype(BF16)

    def wide(x):
        return jnp.concatenate([x] * (T // LANES), axis=1)

    def count(mask):
        return _dot(mask.astype(BF16), ones)

    lo = jnp.zeros((R, LANES), F32)
    hi = jnp.full((R, LANES), 2.0, F32)
    for _ in range(SEL_BISECT):
        mid = 0.5 * (lo + hi)
        ge = count(a >= wide(mid)) >= cap
        lo = jnp.where(ge, mid, lo)
        hi = jnp.where(ge, hi, mid)

    def cond(st):
        return (st[0] < T) & (jnp.min(st[2]) < cap)

    def body(st):
        it, cur, n, thr = st
        m = jnp.max(jnp.where(a < wide(cur), a, -1.0), axis=1, keepdims=True)
        m = jnp.broadcast_to(m, (R, LANES))
        c = count(a >= wide(m))
        act = n < cap
        return it + 1, jnp.where(act, m, cur), jnp.where(act, c, n), jnp.where(act, m, thr)

    _, _, _, thr = lax.while_loop(cond, body, (jnp.int32(0), hi, count(a >= wide(hi)), lo))
    thr_w = wide(thr)
    gt = a > thr_w
    eq = a == thr_w
    need = wide(cap - count(gt))
    sel = gt | (eq & (_dot(eq.astype(BF16), su) < need))
    slot_ref[...] = jnp.where(sel, _dot(sel.astype(BF16), su), -1.0).astype(jnp.int32)


def _select(at, T):
    cap = EC_CAPACITY_FACTOR * T // N_EXPERTS
    R = at.shape[0]
    return pl.pallas_call(
        functools.partial(_select_kernel, cap),
        grid=(1,),
        in_specs=[pl.BlockSpec((R, T), lambda i: (0, 0))],
        out_specs=pl.BlockSpec((R, T), lambda i: (0, 0)),
        out_shape=jax.ShapeDtypeStruct((R, T), jnp.int32),
        compiler_params=_cp(("arbitrary",)),
    )(at)


def _onehot(cap, slot_ref, g):
    j = lax.broadcasted_iota(jnp.int32, (cap, slot_ref.shape[1]), 0)
    r0 = g * N_EXPERTS
    return jnp.concatenate([(slot_ref[r0 + e:r0 + e + 1, :] == j).astype(BF16) for e in range(N_EXPERTS)], axis=0)


def _dispatch_kernel(T, cap, slot_ref, ha_ref, xd_ref):
    for g in range(ha_ref.shape[0] // T):
        rows = _dot(_onehot(cap, slot_ref, g), ha_ref[g * T:(g + 1) * T, :])
        for e in range(N_EXPERTS):
            xd_ref[e, g * cap:(g + 1) * cap, :] = rows[e * cap:(e + 1) * cap].astype(BF16)
        yield


def _dispatch_part(slot, ha, T, G):
    cap = EC_CAPACITY_FACTOR * T // N_EXPERTS
    B = ha.shape[0] // T
    return dict(
        kernel=functools.partial(_dispatch_kernel, T, cap),
        steps=B // G,
        args=[slot, ha],
        in_specs=[pl.BlockSpec((G * N_EXPERTS, T), lambda b: (b, 0)),
                  pl.BlockSpec((G * T, D_AUG), lambda b: (b, 0))],
        out_specs=[pl.BlockSpec((N_EXPERTS, G * cap, D_AUG), lambda b: (0, b, 0))],
        out_shape=[jax.ShapeDtypeStruct((N_EXPERTS, B * cap, D_AUG), BF16)],
        scratch=[],
    )


def _ffn_kernel(xp_ref, xs_ref, wg_ref, wu_ref, wd_ref, yp_ref, ys_ref):
    e = pl.program_id(0)
    lane = lax.broadcasted_iota(jnp.int32, (FFN_TILE, LANES), 1)
    pick = (lane == e) | (lane == e + N_EXPERTS) | (lane == e + 2 * N_EXPERTS)
    wg = wg_ref[...].astype(BF16)
    wu = wu_ref[...].astype(BF16)
    wd = wd_ref[...].astype(BF16)
    for x_ref, y_ref in ((xp_ref, yp_ref), (xs_ref, ys_ref)):
        x = x_ref[:, 0:D_MODEL]
        gate = jnp.sum(jnp.where(pick, x_ref[:, D_MODEL:D_AUG].astype(F32), 0.0), axis=-1, keepdims=True)
        hg = _dot(x, wg)
        hid = (hg * _sigmoid(hg)) * _dot(x, wu)
        y_ref[...] = (_dot(hid.astype(BF16), wd) * gate).astype(BF16)


def _ffn(xd_p, xd_s, w_gate, w_up, w_down):
    rows = xd_p.shape[1]
    assert xd_s.shape[1] == rows and rows % FFN_TILE == 0
    xspec = pl.BlockSpec((None, FFN_TILE, D_AUG), lambda e, m: (e, m, 0))
    wspec = pl.BlockSpec((None, None, D_MODEL, D_MODEL), lambda e, m: (0, e, 0, 0))
    yspec = pl.BlockSpec((None, FFN_TILE, D_MODEL), lambda e, m: (e, m, 0))
    yshape = jax.ShapeDtypeStruct((N_EXPERTS, rows, D_MODEL), BF16)
    return pl.pallas_call(
        _ffn_kernel,
        grid=(N_EXPERTS, rows // FFN_TILE),
        in_specs=[xspec, xspec, wspec, wspec, wspec],
        out_specs=[yspec, yspec],
        out_shape=[yshape, yshape],
        compiler_params=_cp(("arbitrary", "arbitrary")),
    )(xd_p, xd_s, w_gate, w_up, w_down)


def _combine_kernel(cap, slot_ref, yd_ref, x1_ref, mod_ref, nf_ref, o_ref):
    W = slot_ref.shape[1]
    for g in range(x1_ref.shape[0] // W):
        rows = slice(g * W, (g + 1) * W)
        yd = jnp.concatenate([yd_ref[e, g * cap:(g + 1) * cap, :] for e in range(N_EXPERTS)], axis=0)
        acc = _dot_tn(_onehot(cap, slot_ref, g), yd)
        x2 = x1_ref[rows, :] + mod_ref[0, 3:4, :] * acc
        ms = jnp.mean(x2 * x2, axis=-1, keepdims=True)
        o_ref[rows, :] = x2 * lax.rsqrt(ms + EPS) * nf_ref[...]
        yield


def _combine_part(slot, yd, x1, mods, msel, norm_final, T, tok):
    cap = EC_CAPACITY_FACTOR * T // N_EXPERTS
    n = x1.shape[0]
    G, h = (tok // T, 1) if tok >= T else (1, T // tok)
    assert G == 1 or not msel[1]
    mod_map = _mod_map(msel, h)
    return dict(
        kernel=functools.partial(_combine_kernel, cap),
        steps=n // tok,
        args=[slot, yd, x1, mods, norm_final],
        in_specs=[pl.BlockSpec((G * N_EXPERTS, min(T, tok)), lambda i: (i // h, i % h)),
                  pl.BlockSpec((N_EXPERTS, G * cap, D_MODEL), lambda i: (0, i // h, 0)),
                  pl.BlockSpec((tok, D_MODEL), lambda i: (i, 0)),
                  pl.BlockSpec((1, mods.shape[1], D_MODEL), mod_map),
                  pl.BlockSpec((1, D_MODEL), lambda i: (0, 0))],
        out_specs=[pl.BlockSpec((tok, D_MODEL), lambda i: (i, 0))],
        out_shape=[jax.ShapeDtypeStruct((n, D_MODEL), F32)],
        scratch=[],
    )


def _gate_tiles(wa, wx, ba, bx):
    n_ct = D_LRU // LANES

    def dense(w):
        w2 = w.reshape(n_ct, 2, LRU_BW, LRU_BW)
        z = jnp.zeros((n_ct, LRU_BW, LRU_BW), w.dtype)
        top = jnp.concatenate([w2[:, 0], z], axis=2)
        bot = jnp.concatenate([z, w2[:, 1]], axis=2)
        return jnp.concatenate([top, bot], axis=1)

    tiles = 0.5 * jnp.concatenate([dense(wa[0]), dense(wx[0]), dense(wa[1]), dense(wx[1])], axis=-1)
    t = lambda v: v.reshape(n_ct, 1, LANES)
    bias = 0.5 * jnp.concatenate([t(ba[0]), t(bx[0]), t(ba[1]), t(bx[1])], axis=-1)
    hi = bias.astype(BF16)
    lo = (bias - hi.astype(F32)).astype(BF16)
    pad = jnp.zeros((n_ct, LANES - 2, 4 * LANES), BF16)
    return jnp.concatenate([tiles.astype(BF16), hi, lo, pad], axis=1)


def _router_split(w):
    wp = jnp.pad(w, ((0, 0), (0, LANES - N_EXPERTS)))
    hi = wp.astype(BF16)
    mid = (wp - hi.astype(F32)).astype(BF16)
    return jnp.concatenate([hi, mid], axis=1)


def _proj(x, mods, msel, p, T):
    return _proj_part(x.reshape(-1, D_MODEL), mods, msel, p["norm_mix"], p["w_in"], T)


def _lru(xr4, B, T, row_len, h0, p, unroll):
    n_ct = D_LRU // LANES
    return _lru_part(xr4.reshape(n_ct, B, T, LANES), p["conv_w"], p["conv_b"], p["gate_w"], p["lam"],
                     h0, T, row_len, unroll)


def _mix(x, mods, msel, p, T, hs4, gr, yh, tile_rows):
    B = x.shape[0]
    return _mix_out_part(hs4.reshape(D_LRU // LANES, B * T, LANES), gr, yh.reshape(B * T, D_HGRN),
                         x.reshape(B * T, D_MODEL), mods, msel,
                         p["norm_lru"], p["w_out"], p["norm_ffn"], p["w_router"], T, tile_rows)


def kernel(x_prompt, x_sample, state_lru, state_hgrn, c, c_ctx, w_ada, b_ada, norm_mix, w_in, conv_w,
           conv_b, lru_wa, lru_ba, lru_wx, lru_bx, lru_lambda, norm_lru, hgrn_gamma, norm_hgrn, w_out,
           norm_ffn, w_router, w_gate, w_up, w_down, norm_final):
    assert w_ada.shape[0] == 1 and hgrn_gamma.shape[1] == 2, "one trunk layer"
    l = 0
    Bp, Tp, _ = x_prompt.shape
    Bs, Ts, _ = x_sample.shape

    cond = jnp.concatenate([c_ctx[None, :], c, jnp.zeros((16 - 1 - Bs, D_MODEL), F32)], axis=0)
    n1 = 2 * D_MODEL
    ((m1,),) = _run([_adaln_part(cond, w_ada, b_ada, 0, n1, D_MODEL)])
    m1 = m1.reshape(16, 2, D_MODEL)
    sel_p, sel_s = (0, False), (1, True)

    p = {
        "norm_mix": norm_mix[l][None, :], "w_in": w_in,
        "conv_w": conv_w[l], "conv_b": conv_b[l][None, :],
        "gate_w": _gate_tiles(lru_wa[l], lru_wx[l], lru_ba[l], lru_bx[l]),
        "lam": lru_lambda[l], "norm_lru": norm_lru[l][None, :],
        "gamma": hgrn_gamma, "norm_hgrn": norm_hgrn[l][None, :],
        "w_out": w_out[l], "norm_ffn": norm_ffn[l][None, :],
        "w_router": _router_split(w_router[l]),
    }
    zeros_p = jnp.zeros((Bp, 2 * D_LRU), F32)
    proj_p = _proj(x_prompt, m1, sel_p, p, Tp)
    n2 = (N_MOD - 2) * D_MODEL
    (xr4_p, gr_p, hg5_p), (m2,) = _run([proj_p, _adaln_part(cond, w_ada, b_ada, n1, n2, n2 // proj_p["steps"])])
    m2 = m2.reshape(16, N_MOD - 2, D_MODEL)
    (xr4_s, gr_s, hg5_s), (hs4_p, last_f, last_b) = _run(
        [_proj(x_sample, m1, sel_s, p, Ts), _lru(xr4_p, Bp, Tp, Tp, zeros_p, p, True)])
    yh_p, st_p = _hgrn(hg5_p.reshape(N_HG_IN, Bp, Tp, D_HGRN), p["gamma"], p["norm_hgrn"], None, Tp, True)
    ((x1p, ha_p, at_p),) = _run([_mix(x_prompt, m2, sel_p, p, Tp, hs4_p, gr_p, yh_p, MIX_TILE)])
    slot_p = _select(at_p, Tp)
    assert state_lru.shape[1] == 1
    ((hs4_s, _, _),) = _run([_lru(xr4_s, Bs, Ts, GRID_W, state_lru.reshape(Bs, 2 * D_LRU), p, 8)])
    yh_s, _ = _hgrn(hg5_s.reshape(N_HG_IN, Bs, Ts, D_HGRN), p["gamma"], p["norm_hgrn"], state_hgrn, Ts, False)
    fused_tile = MIX_TILE // 2
    (x1s, ha_s, at_s), (xd_p,) = _run(
        [_mix(x_sample, m2, sel_s, p, Ts, hs4_s, gr_s, yh_s, fused_tile),
         _dispatch_part(slot_p, ha_p, Tp, fused_tile // Tp)])
    slot_s = _select(at_s, Ts)
    ((xd_s,),) = _run([_dispatch_part(slot_s, ha_s, Ts, max(ROUTE_TOKENS // Ts, 1))])

    yd_p, yd_s = _ffn(xd_p, xd_s, w_gate, w_up, w_down)
    nf = norm_final[None, :]
    (y_sample,), (y_prompt,) = _run(
        [_combine_part(slot_s, yd_s, x1s, m2, sel_s, nf, Ts, COMBINE_TOKENS),
         _combine_part(slot_p, yd_p, x1p, m2, sel_p, nf, Tp, COMBINE_TOKENS)])
    y_prompt = y_prompt.reshape(Bp, Tp, D_MODEL)
    y_sample = y_sample.reshape(Bs, Ts, D_MODEL)
    new_state_lru = jnp.stack([last_f, last_b], axis=1)[:, None]
    new_state_hgrn = st_p
    return (y_prompt, y_sample, new_state_lru, new_state_hgrn)
```

```python
import functools

import jax
import jax.numpy as jnp
from jax import lax
from jax.experimental import pallas as pl
from jax.experimental.pallas import tpu as pltpu

F32 = jnp.float32
BF16 = jnp.bfloat16

D_MODEL = 1024
D_LRU = 512
D_HGRN = 512
HG_HEADS = 4
HG_DK = 128
LRU_BLOCKS = 8
LRU_BW = 64
LRU_C = 8.0
N_EXPERTS = 16
EC_CAPACITY_FACTOR = 2
N_MOD = 6
D_IN = 7 * 512
N_HG_IN = 5
GRID_W = 64
EPS = 1e-6
LOG2E = 1.4426950408889634
TINY = 1e-37

LANES = 128
TOK_TILE = 256
PROJ_TILE = 512
MIX_TILE = 1024
LRU_GROUP = 8
PITCH_PAD = 8
HG_CHUNK = 64
HG_UNROLL = 16
HG_ROWS = 256
FFN_TILE = 512
ROUTE_TOKENS = 1024
COMBINE_TOKENS = 1024
SEL_BISECT = 20
D_AUG = D_MODEL + LANES
VMEM_LIMIT = 60 * 1024 * 1024


def _cp(sem, vmem=VMEM_LIMIT):
    return pltpu.CompilerParams(dimension_semantics=sem, vmem_limit_bytes=vmem)


def _sigmoid(x):
    return 0.5 * (1.0 + jnp.tanh(0.5 * x))


def _dot(a, b):
    return jnp.dot(a, b, preferred_element_type=F32)


def _dot_nt(a, b):
    return lax.dot_general(a, b, (((1,), (1,)), ((), ())), preferred_element_type=F32)


def _dot_tn(a, b):
    return lax.dot_general(a, b, (((0,), (0,)), ((), ())), preferred_element_type=F32)


def _adaln_kernel(c_ref, w_ref, b_ref, o_ref):
    c = c_ref[...]
    s = (c * _sigmoid(c)).astype(BF16)
    o_ref[...] = _dot(s, w_ref[...].astype(BF16)) + b_ref[...]
    yield


def _adaln_part(cond, w, b, col0, ncols, tn):
    n = cond.shape[0]
    j0 = col0 // tn
    return dict(
        kernel=_adaln_kernel,
        steps=ncols // tn,
        args=[cond, w, b],
        in_specs=[pl.BlockSpec((n, D_MODEL), lambda j: (0, 0)),
                  pl.BlockSpec((None, D_MODEL, tn), lambda j: (0, 0, j0 + j)),
                  pl.BlockSpec((1, tn), lambda j: (0, j0 + j))],
        out_specs=[pl.BlockSpec((n, tn), lambda j: (0, j))],
        out_shape=[jax.ShapeDtypeStruct((n, ncols), F32)],
        scratch=[],
    )


def _proj_kernel(x_ref, mod_ref, g_ref, w_ref, xr_ref, gr_ref, hg_ref):
    x = x_ref[...]
    ms = jnp.mean(x * x, axis=-1, keepdims=True)
    y = x * lax.rsqrt(ms + EPS) * g_ref[...]
    h = y * (1.0 + mod_ref[0, 1:2, :]) + mod_ref[0, 0:1, :]
    h = h.astype(BF16)
    W = 2 * LANES
    for j in range(D_IN // W):
        res = _dot(h, w_ref[:, j * W:(j + 1) * W].astype(BF16))
        k, half = divmod(j * W, D_HGRN)
        if k == 0:
            for c in range(W // LANES):
                xr_ref[half // LANES + c] = res[:, c * LANES:(c + 1) * LANES]
        elif k == 1:
            gr_ref[:, half:half + W] = res.astype(BF16)
        else:
            hg_ref[k - 2, :, half:half + W] = res
        yield


def _run(parts):
    steps = parts[0]["steps"]
    assert all(p["steps"] == steps for p in parts)
    n_in = [len(p["args"]) for p in parts]
    n_out = [len(p["out_shape"]) for p in parts]
    n_sc = [len(p["scratch"]) for p in parts]

    def body(*refs):
        ins, outs, scr = refs[:sum(n_in)], refs[sum(n_in):sum(n_in) + sum(n_out)], refs[sum(n_in) + sum(n_out):]
        i = o = s = 0
        gens = []
        for p, a, b, c in zip(parts, n_in, n_out, n_sc):
            gens.append(p["kernel"](*ins[i:i + a], *outs[o:o + b], *scr[s:s + c]))
            i, o, s = i + a, o + b, s + c
        while gens:
            for g in list(gens):
                if next(g, StopIteration) is StopIteration:
                    gens.remove(g)

    res = pl.pallas_call(
        body,
        grid=(steps,),
        in_specs=[s for p in parts for s in p["in_specs"]],
        out_specs=[s for p in parts for s in p["out_specs"]],
        out_shape=[s for p in parts for s in p["out_shape"]],
        scratch_shapes=[s for p in parts for s in p["scratch"]],
        compiler_params=_cp(("arbitrary",)),
    )(*[a for p in parts for a in p["args"]])
    out, o = [], 0
    for b in n_out:
        out.append(res[o:o + b])
        o += b
    return out


def _mod_map(msel, steps_per_request):
    row0, per_request = msel
    if not per_request:
        return lambda i: (row0, 0, 0)
    assert steps_per_request >= 1
    return lambda i: (row0 + i // steps_per_request, 0, 0)


def _proj_part(x, mods, msel, gain, w, T):
    n = x.shape[0]
    assert T % PROJ_TILE == 0 or (PROJ_TILE % T == 0 and not msel[1])
    mod_map = _mod_map(msel, T // PROJ_TILE)
    n_ct = D_LRU // LANES
    return dict(
        kernel=_proj_kernel,
        steps=n // PROJ_TILE,
        args=[x, mods, gain, w],
        in_specs=[pl.BlockSpec((PROJ_TILE, D_MODEL), lambda i: (i, 0)),
                  pl.BlockSpec((1, mods.shape[1], D_MODEL), mod_map),
                  pl.BlockSpec((1, D_MODEL), lambda i: (0, 0)),
                  pl.BlockSpec((None, D_MODEL, D_IN), lambda i: (0, 0, 0), pipeline_mode=pl.Buffered(1))],
        out_specs=[pl.BlockSpec((n_ct, PROJ_TILE, LANES), lambda i: (0, i, 0)),
                   pl.BlockSpec((PROJ_TILE, D_LRU), lambda i: (i, 0)),
                   pl.BlockSpec((N_HG_IN, PROJ_TILE, D_HGRN), lambda i: (0, i, 0))],
        out_shape=[jax.ShapeDtypeStruct((n_ct, n, LANES), F32),
                   jax.ShapeDtypeStruct((n, D_LRU), BF16),
                   jax.ShapeDtypeStruct((N_HG_IN, n, D_HGRN), F32)],
        scratch=[],
    )


def _lru_kernel(T, row_len, unroll, xr_ref, cw_ref, cb_ref, w_ref, lam_ref, h0f_ref, h0b_ref,
                hs_ref, lf_ref, lb_ref, xpad, af, uf, ab, ub, hf_s, hb_s):
    pitch = T + PITCH_PAD
    pos = lax.broadcasted_iota(jnp.int32, (T, LANES), 0) & (row_len - 1)
    cw = cw_ref[...]
    w0 = jnp.where(pos >= 2, cw[0:1], 0.0)
    w1 = jnp.where(pos >= 1, cw[1:2], 0.0)
    w2 = jnp.broadcast_to(cw[2:3], (T, LANES))
    w3 = jnp.where(pos <= row_len - 2, cw[3:4], 0.0)
    cb = cb_ref[...]
    nl = -lam_ref[...]
    sp = jnp.maximum(nl, 0.0) + jnp.log1p(jnp.exp(-jnp.abs(nl)))
    c2 = (-0.5 * LRU_C * LOG2E) * sp
    w = w_ref[0]
    ones = (lax.broadcasted_iota(jnp.int32, (T, LANES), 1) < 2).astype(BF16)
    for s in range(2):
        xpad[s, 0:8, :] = jnp.zeros((8, LANES), F32)
        xpad[s, T + 8:T + 16, :] = jnp.zeros((8, LANES), F32)
    for b in range(LRU_GROUP):
        x = xr_ref[b]
        xp = xpad.at[b % 2]
        xp[8:T + 8, :] = x
        xc = w0 * xp[6:T + 6, :] + w1 * xp[7:T + 7, :] + w2 * x + w3 * xp[9:T + 9, :] + cb
        xh = 0.5 * xc
        z = _dot(jnp.concatenate([xc.astype(BF16), ones], axis=1), w)
        rows = pl.ds(b * pitch, T)
        for d, (a_s, u_s) in enumerate(((af, uf), (ab, ub))):
            tr = jnp.tanh(z[:, (2 * d) * LANES:(2 * d + 1) * LANES])
            ti = jnp.tanh(z[:, (2 * d + 1) * LANES:(2 * d + 2) * LANES])
            a = jnp.exp2(c2[d:d + 1] + c2[d:d + 1] * tr)
            om = 1.0 - a * a
            a_s[rows, :] = a
            u_s[rows, :] = (om * lax.rsqrt(jnp.maximum(om, TINY))) * (xh + ti * xh)
        yield

    def step(t, carry):
        hf, hb = carry
        rf = pl.ds(t, LRU_GROUP, stride=pitch)
        hf = af[rf, :] * hf + uf[rf, :]
        hf_s[rf, :] = hf
        rb = pl.ds(T - 1 - t, LRU_GROUP, stride=pitch)
        hb = ab[rb, :] * hb + ub[rb, :]
        hb_s[rb, :] = hb
        return hf, hb

    carry = (h0f_ref[...], h0b_ref[...])
    if unroll is True:
        for t in range(T):
            carry = step(t, carry)
            if t % (T // 8) == T // 8 - 1 and t != T - 1:
                yield
    else:
        carry = lax.fori_loop(0, T, step, carry, unroll=unroll)
    hf, hb = carry
    lf_ref[...] = hf
    lb_ref[...] = hb
    for b in range(LRU_GROUP):
        rows = pl.ds(b * pitch, T)
        hs_ref[b] = (hf_s[rows, :] + hb_s[rows, :]).astype(BF16)


def _lru_part(xr4, conv_w, conv_b, w_tiles, lam, h0, T, row_len, unroll):
    n_ct, B = xr4.shape[:2]
    rows = LRU_GROUP * (T + PITCH_PAD)
    vec = pl.BlockSpec((LRU_GROUP, LANES), lambda i: (i // n_ct, i % n_ct))
    vec_b = pl.BlockSpec((LRU_GROUP, LANES), lambda i: (i // n_ct, n_ct + i % n_ct))
    slab = pl.BlockSpec((None, LRU_GROUP, T, LANES), lambda i: (i % n_ct, i // n_ct, 0, 0))
    return dict(
        kernel=functools.partial(_lru_kernel, T, row_len, unroll),
        steps=(B // LRU_GROUP) * n_ct,
        args=[xr4, conv_w, conv_b, w_tiles, lam, h0, h0],
        in_specs=[slab,
                  pl.BlockSpec((4, LANES), lambda i: (0, i % n_ct)),
                  pl.BlockSpec((1, LANES), lambda i: (0, i % n_ct)),
                  pl.BlockSpec((1, 2 * LANES, 4 * LANES), lambda i: (i % n_ct, 0, 0)),
                  pl.BlockSpec((2, LANES), lambda i: (0, i % n_ct)),
                  vec, vec_b],
        out_specs=[slab, vec, vec],
        out_shape=[jax.ShapeDtypeStruct((n_ct, B, T, LANES), BF16),
                   jax.ShapeDtypeStruct((B, D_LRU), F32),
                   jax.ShapeDtypeStruct((B, D_LRU), F32)],
        scratch=[pltpu.VMEM((2, T + 16, LANES), F32)] + [pltpu.VMEM((rows, LANES), F32)] * 6,
    )


def _hgrn_kernel(T, has_s0, want_state, *refs):
    q_ref, ff_ref, fb_ref, v_ref, og_ref, gam_ref, gain_ref = refs[:7]
    refs = refs[7:]
    if has_s0:
        s0_ref, refs = refs[0], refs[1:]
    y_ref, refs = refs[0], refs[1:]
    if want_state:
        st_ref, refs = refs[0], refs[1:]
    q_s, v_s, kf_s, kb_s, bf_s, bb_s, of_s, ob_s, st_s = refs
    C = HG_CHUNK
    n = T // C
    RB = HG_ROWS
    heads = [slice(h * LANES, (h + 1) * LANES) for h in range(HG_HEADS)]

    ti = lax.broadcasted_iota(jnp.int32, (RB, RB), 0)
    si = lax.broadcasted_iota(jnp.int32, (RB, RB), 1)
    sh = C.bit_length() - 1
    same = (ti >> sh) == (si >> sh)
    tri = ((same & (si <= ti)).astype(BF16), (same & (si >= ti)).astype(BF16))

    for r in range(T // RB):
        rows = slice(r * RB, (r + 1) * RB)
        qr = q_ref[rows, :]
        qh = (0.5 * HG_DK ** -0.5) * qr
        q_s[rows, :] = qh + qh * jnp.tanh(0.5 * qr)
        v_s[rows, :] = v_ref[rows, :].astype(BF16)
        for d, (f_ref, k_s, b_s) in enumerate(((ff_ref, kf_s, bf_s), (fb_ref, kb_s, bb_s))):
            g0 = gam_ref[d, 0:1, :]
            g1 = gam_ref[d, 1:2, :]
            m = jnp.maximum(g0, g1)
            e0 = jnp.exp(g0 - m)
            lbd = e0 / (e0 + jnp.exp(g1 - m))
            ck = 0.5 * (1.0 - lbd)
            pt = ck * jnp.tanh(0.5 * f_ref[rows, :])
            k_s[rows, :] = ck - pt
            lf = jnp.log((lbd + ck) + pt)
            hi = lf.astype(BF16)
            r1 = lf - hi.astype(F32)
            mid = r1.astype(BF16)
            lo = (r1 - mid.astype(F32)).astype(BF16)
            cs = _dot(tri[d], jnp.concatenate([hi, mid, lo], axis=1))
            b_s[rows, :] = cs[:, 0:D_HGRN] + (cs[:, D_HGRN:2 * D_HGRN] + cs[:, 2 * D_HGRN:3 * D_HGRN])

    for d in range(2):
        for h in range(HG_HEADS):
            st_s[d * HG_HEADS + h] = s0_ref[d, h].T if has_s0 else jnp.zeros((LANES, LANES), F32)

    ri =lax.broadcasted_iota(jnp.int32, (C, C), 0)
    ci = lax.broadcasted_iota(jnp.int32, (C, C), 1)

    def body(i, carry):
        for d, (k_s, b_s, o_s) in enumerate(((kf_s, bf_s, of_s), (kb_s, bb_s, ob_s))):
            c = i if d == 0 else n - 1 - i
            sl = pl.ds(pl.multiple_of(c * C, C), C)
            for h in range(HG_HEADS):
                q = q_s[sl, heads[h]]
                k = k_s[sl, heads[h]]
                b = b_s[sl, heads[h]]
                v = v_s[sl, heads[h]]
                bm = b[C // 2:C // 2 + 1, :]
                g = b[C - 1:C, :] if d == 0 else b[0:1, :]
                qi = q * jnp.exp(b - bm)
                qe = (qi * jnp.exp(bm)).astype(BF16)
                ki = k * jnp.exp(bm - b)
                kd = (ki * jnp.exp(g - bm)).astype(BF16)
                s = _dot_nt(qi.astype(BF16), ki.astype(BF16))
                p = jnp.where((ci <= ri) if d == 0 else (ci >= ri), s, 0.0).astype(BF16)
                st = st_s[d * HG_HEADS + h]
                o_s[sl, heads[h]] = _dot(p, v) + _dot(qe, st.T.astype(BF16))
                st_s[d * HG_HEADS + h] = st * jnp.exp(g) + _dot_tn(v, kd)
        return carry

    lax.fori_loop(0, n, body, 0, unroll=HG_UNROLL)
    if want_state:
        for d in range(2):
            for h in range(HG_HEADS):
                st_ref[d, h] = st_s[d * HG_HEADS + h].T

    for r in range(T // RB):
        rows = slice(r * RB, (r + 1) * RB)
        for h in range(HG_HEADS):
            o = of_s[rows, heads[h]] + ob_s[rows, heads[h]]
            ms = jnp.mean(o * o, axis=-1, keepdims=True)
            oh = 0.5 * og_ref[rows, heads[h]]
            y = o * lax.rsqrt(ms + EPS) * gain_ref[...] * (oh + oh * jnp.tanh(oh))
            y_ref[rows, heads[h]] = y.astype(BF16)


def _hgrn(hg5, gamma, gain, s0, T, want_state):
    B = hg5.shape[1]
    has_s0 = s0 is not None

    def col(k):
        return pl.BlockSpec((None, None, T, D_HGRN), lambda b, k=k: (k, b, 0, 0))

    st_spec = pl.BlockSpec((None, None, 2, HG_HEADS, HG_DK, HG_DK), lambda b: (b, 0, 0, 0, 0, 0))
    in_specs = [col(0), col(1), col(2), col(3), col(4),
                pl.BlockSpec((2, 2, D_HGRN), lambda b: (0, 0, 0)),
                pl.BlockSpec((1, LANES), lambda b: (0, 0))]
    args = [hg5, hg5, hg5, hg5, hg5, gamma, gain]
    if has_s0:
        in_specs.append(st_spec)
        args.append(s0)
    out_specs = [pl.BlockSpec((None, T, D_HGRN), lambda b: (b, 0, 0))]
    out_shape = [jax.ShapeDtypeStruct((B, T, D_HGRN), BF16)]
    if want_state:
        out_specs.append(st_spec)
        out_shape.append(jax.ShapeDtypeStruct((B, 1, 2, HG_HEADS, HG_DK, HG_DK), F32))
    res = pl.pallas_call(
        functools.partial(_hgrn_kernel, T, has_s0, want_state),
        grid=(B,),
        in_specs=in_specs,
        out_specs=out_specs,
        out_shape=out_shape,
        scratch_shapes=[pltpu.VMEM((T, D_HGRN), F32), pltpu.VMEM((T, D_HGRN), BF16)]
        + [pltpu.VMEM((T, D_HGRN), F32)] * 6
        + [pltpu.VMEM((2 * HG_HEADS, HG_DK, HG_DK), F32)],
        compiler_params=_cp(("arbitrary",)),
    )(*args)
    return res if want_state else (res[0], None)


def _mix_out_kernel(hs_ref, gr_ref, yh_ref, x_ref, mod_ref, nl_ref, wo_ref, nf_ref, wr_ref,
                    x1_ref, ha_ref, at_ref):
    wr = wr_ref[...]
    wo = wo_ref[...].astype(BF16)
    g1 = mod_ref[0, 0:1, :]
    sh2 = mod_ref[0, 1:2, :]
    gain2 = nf_ref[...] * (1.0 + mod_ref[0, 2:3, :])
    lane = lax.broadcasted_iota(jnp.int32, (TOK_TILE, LANES), 1)
    for s in range(x_ref.shape[0] // TOK_TILE):
        rows = slice(s * TOK_TILE, (s + 1) * TOK_TILE)
        hs = jnp.concatenate([hs_ref[c, rows, :] for c in range(D_LRU // LANES)], axis=1).astype(F32)
        ms = jnp.mean(hs * hs, axis=-1, keepdims=True)
        gr = gr_ref[rows, :].astype(F32)
        gelu = 0.5 * gr * (1.0 + jnp.tanh(0.7978845608028654 * (gr + 0.044715 * (gr * gr * gr))))
        y_lru = hs * lax.rsqrt(ms + EPS) * nl_ref[...] * gelu
        ycat = jnp.concatenate([y_lru.astype(BF16), yh_ref[rows, :]], axis=-1)
        x1 = x_ref[rows, :] + g1 * _dot(ycat, wo)
        x1_ref[rows, :] = x1
        yield
        ms = jnp.mean(x1 * x1, axis=-1, keepdims=True)
        h2 = x1 * lax.rsqrt(ms + EPS) * gain2 + sh2
        h2_hi = h2.astype(BF16)
        h2_mid = (h2 - h2_hi.astype(F32)).astype(BF16)
        p_hi = _dot(h2_hi, wr)
        logits = p_hi[:, 0:LANES] + (p_hi[:, LANES:2 * LANES] + _dot(h2_mid, wr)[:, 0:LANES])
        logits = jnp.where(lane < N_EXPERTS, logits, -jnp.inf)
        e = jnp.exp(logits - jnp.max(logits, axis=-1, keepdims=True))
        aff = e / jnp.sum(e, axis=-1, keepdims=True)
        w = at_ref.shape[1]
        r, c = (s * TOK_TILE) // w, (s * TOK_TILE) % w
        at_ref[r * N_EXPERTS:(r + 1) * N_EXPERTS, c:c + TOK_TILE] = aff.T[0:N_EXPERTS, :]
        hi = aff.astype(BF16).astype(F32)
        r1 = aff - hi
        mid = r1.astype(BF16).astype(F32)
        lo = (r1 - mid).astype(BF16).astype(F32)
        split = hi + pltpu.roll(mid, N_EXPERTS, 1) + pltpu.roll(lo, 2 * N_EXPERTS, 1)
        ha_ref[rows, 0:D_MODEL] = h2_hi
        ha_ref[rows, D_MODEL:D_AUG] = split.astype(BF16)
        yield


def _mix_out_part(hs4, gr, yh, x, mods, msel, norm_lru, w_out, norm_ffn, wr_pad, T, tile_rows):
    n = x.shape[0]
    assert T % TOK_TILE == 0 and (tile_rows % T == 0 or T % tile_rows == 0)
    q = max(T // tile_rows, 1)
    at_rows = N_EXPERTS * max(tile_rows // T, 1)
    at_spec = pl.BlockSpec((at_rows, min(T, tile_rows)), lambda i: (i // q, i % q))
    mod_map = _mod_map(msel, q)
    tile = lambda w: pl.BlockSpec((tile_rows, w), lambda i: (i, 0))
    const = lambda r, w: pl.BlockSpec((r, w), lambda i: (0, 0))
    return dict(
        kernel=_mix_out_kernel,
        steps=n // tile_rows,
        args=[hs4, gr, yh, x, mods, norm_lru, w_out, norm_ffn, wr_pad],
        in_specs=[pl.BlockSpec((D_LRU // LANES, tile_rows, LANES), lambda i: (0, i, 0)),
                  tile(D_LRU), tile(D_HGRN), tile(D_MODEL),
                  pl.BlockSpec((1, mods.shape[1], D_MODEL), mod_map),
                  const(1, D_LRU), const(D_MODEL, D_MODEL), const(1, D_MODEL), const(D_MODEL, 2 * LANES)],
        out_specs=[tile(D_MODEL), tile(D_AUG), at_spec],
        out_shape=[jax.ShapeDtypeStruct((n, D_MODEL), F32),
                   jax.ShapeDtypeStruct((n, D_AUG), BF16),
                   jax.ShapeDtypeStruct((n // T * N_EXPERTS, T), F32)],
        scratch=[],
    )


def _select_kernel(cap, at_ref, slot_ref):
    a = at_ref[...]
    R, T = a.shape
    ones = jnp.ones((T, LANES), BF16)
    su = (lax.broadcasted_iota(jnp.int32, (T, T), 0)
          < lax.broadcasted_iota(jnp.int32, (T, T), 1)).astype(BF16)

    def wide(x):
        return jnp.concatenate([x] * (T // LANES), axis=1)

    def count(mask):
        return _dot(mask.astype(BF16), ones)

    lo = jnp.zeros((R, LANES), F32)
    hi = jnp.full((R, LANES), 2.0, F32)
    for _ in range(SEL_BISECT):
        mid = 0.5 * (lo + hi)
        ge = count(a >= wide(mid)) >= cap
        lo = jnp.where(ge, mid, lo)
        hi = jnp.where(ge, hi, mid)

    def cond(st):
        return (st[0] < T) & (jnp.min(st[2]) < cap)

    def body(st):
        it, cur, n, thr = st
        m = jnp.max(jnp.where(a < wide(cur), a, -1.0), axis=1, keepdims=True)
        m = jnp.broadcast_to(m, (R, LANES))
        c = count(a >= wide(m))
        act = n < cap
        return it + 1, jnp.where(act, m, cur), jnp.where(act, c, n), jnp.where(act, m, thr)

    _, _, _, thr = lax.while_loop(cond, body, (jnp.int32(0), hi, count(a >= wide(hi)), lo))
    thr_w = wide(thr)
    gt = a > thr_w
    eq = a == thr_w
    need = wide(cap - count(gt))
    sel = gt | (eq & (_dot(eq.astype(BF16), su) < need))
    slot_ref[...] = jnp.where(sel, _dot(sel.astype(BF16), su), -1.0).astype(jnp.int32)


def _select(at, T):
    cap = EC_CAPACITY_FACTOR * T // N_EXPERTS
    R = at.shape[0]
    return pl.pallas_call(
        functools.partial(_select_kernel, cap),
        grid=(1,),
        in_specs=[pl.BlockSpec((R, T), lambda i: (0, 0))],
        out_specs=pl.BlockSpec((R, T), lambda i: (0, 0)),
        out_shape=jax.ShapeDtypeStruct((R, T), jnp.int32),
        compiler_params=_cp(("arbitrary",)),
    )(at)


def _onehot(cap, slot_ref, g):
    j = lax.broadcasted_iota(jnp.int32, (cap, slot_ref.shape[1]), 0)
    r0 = g * N_EXPERTS
    return jnp.concatenate([(slot_ref[r0 + e:r0 + e + 1, :] == j).astype(BF16) for e in range(N_EXPERTS)], axis=0)


def _dispatch_kernel(T, cap, slot_ref, ha_ref, xd_ref):
    for g in range(ha_ref.shape[0] // T):
        rows = _dot(_onehot(cap, slot_ref, g), ha_ref[g * T:(g + 1) * T, :])
        for e in range(N_EXPERTS):
            xd_ref[e, g * cap:(g + 1) * cap, :] = rows[e * cap:(e + 1) * cap].astype(BF16)
        yield


def _dispatch_part(slot, ha, T, G):
    cap = EC_CAPACITY_FACTOR * T // N_EXPERTS
    B = ha.shape[0] // T
    return dict(
        kernel=functools.partial(_dispatch_kernel, T, cap),
        steps=B // G,
        args=[slot, ha],
        in_specs=[pl.BlockSpec((G * N_EXPERTS, T), lambda b: (b, 0)),
                  pl.BlockSpec((G * T, D_AUG), lambda b: (b, 0))],
        out_specs=[pl.BlockSpec((N_EXPERTS, G * cap, D_AUG), lambda b: (0, b, 0))],
        out_shape=[jax.ShapeDtypeStruct((N_EXPERTS, B * cap, D_AUG), BF16)],
        scratch=[],
    )


def _ffn_kernel(xp_ref, xs_ref, wg_ref, wu_ref, wd_ref, yp_ref, ys_ref):
    e = pl.program_id(0)
    lane = lax.broadcasted_iota(jnp.int32, (FFN_TILE, LANES), 1)
    pick = (lane == e) | (lane == e + N_EXPERTS) | (lane == e + 2 * N_EXPERTS)
    wg = wg_ref[...].astype(BF16)
    wu = wu_ref[...].astype(BF16)
    wd = wd_ref[...].astype(BF16)
    for x_ref, y_ref in ((xp_ref, yp_ref), (xs_ref, ys_ref)):
        x = x_ref[:, 0:D_MODEL]
        gate = jnp.sum(jnp.where(pick, x_ref[:, D_MODEL:D_AUG].astype(F32), 0.0), axis=-1, keepdims=True)
        hg = _dot(x, wg)
        hid = (hg * _sigmoid(hg)) * _dot(x, wu)
        y_ref[...] = (_dot(hid.astype(BF16), wd) * gate).astype(BF16)


def _ffn(xd_p, xd_s, w_gate, w_up, w_down):
    rows = xd_p.shape[1]
    assert xd_s.shape[1] == rows and rows % FFN_TILE == 0
    xspec = pl.BlockSpec((None, FFN_TILE, D_AUG), lambda e, m: (e, m, 0))
    wspec = pl.BlockSpec((None, None, D_MODEL, D_MODEL), lambda e, m: (0, e, 0, 0))
    yspec = pl.BlockSpec((None, FFN_TILE, D_MODEL), lambda e, m: (e, m, 0))
    yshape = jax.ShapeDtypeStruct((N_EXPERTS, rows, D_MODEL), BF16)
    return pl.pallas_call(
        _ffn_kernel,
        grid=(N_EXPERTS, rows // FFN_TILE),
        in_specs=[xspec, xspec, wspec, wspec, wspec],
        out_specs=[yspec, yspec],
        out_shape=[yshape, yshape],
        compiler_params=_cp(("arbitrary", "arbitrary")),
    )(xd_p, xd_s, w_gate, w_up, w_down)


def _combine_kernel(cap, slot_ref, yd_ref, x1_ref, mod_ref, nf_ref, o_ref):
    W = slot_ref.shape[1]
    for g in range(x1_ref.shape[0] // W):
        rows = slice(g * W, (g + 1) * W)
        yd = jnp.concatenate([yd_ref[e, g * cap:(g + 1) * cap, :] for e in range(N_EXPERTS)], axis=0)
        acc = _dot_tn(_onehot(cap, slot_ref, g), yd)
        x2 = x1_ref[rows, :] + mod_ref[0, 3:4, :] * acc
        ms = jnp.mean(x2 * x2, axis=-1, keepdims=True)
        o_ref[rows, :] = x2 * lax.rsqrt(ms + EPS) * nf_ref[...]
        yield


def _combine_part(slot, yd, x1, mods, msel, norm_final, T, tok):
    cap = EC_CAPACITY_FACTOR * T // N_EXPERTS
    n = x1.shape[0]
    G, h = (tok // T, 1) if tok >= T else (1, T // tok)
    assert G == 1 or not msel[1]
    mod_map = _mod_map(msel, h)
    return dict(
        kernel=functools.partial(_combine_kernel, cap),
        steps=n // tok,
        args=[slot, yd, x1, mods, norm_final],
        in_specs=[pl.BlockSpec((G * N_EXPERTS, min(T, tok)), lambda i: (i // h, i % h)),
                  pl.BlockSpec((N_EXPERTS, G * cap, D_MODEL), lambda i: (0, i // h, 0)),
                  pl.BlockSpec((tok, D_MODEL), lambda i: (i, 0)),
                  pl.BlockSpec((1, mods.shape[1], D_MODEL), mod_map),
                  pl.BlockSpec((1, D_MODEL), lambda i: (0, 0))],
        out_specs=[pl.BlockSpec((tok, D_MODEL), lambda i: (i, 0))],
        out_shape=[jax.ShapeDtypeStruct((n, D_MODEL), F32)],
        scratch=[],
    )


def _gate_tiles(wa, wx, ba, bx):
    n_ct = D_LRU // LANES

    def dense(w):
        w2 = w.reshape(n_ct, 2, LRU_BW, LRU_BW)
        z = jnp.zeros((n_ct, LRU_BW, LRU_BW), w.dtype)
        top = jnp.concatenate([w2[:, 0], z], axis=2)
        bot = jnp.concatenate([z, w2[:, 1]], axis=2)
        return jnp.concatenate([top, bot], axis=1)

    tiles = 0.5 * jnp.concatenate([dense(wa[0]), dense(wx[0]), dense(wa[1]), dense(wx[1])], axis=-1)
    t = lambda v: v.reshape(n_ct, 1, LANES)
    bias = 0.5 * jnp.concatenate([t(ba[0]), t(bx[0]), t(ba[1]), t(bx[1])], axis=-1)
    hi = bias.astype(BF16)
    lo = (bias - hi.astype(F32)).astype(BF16)
    pad = jnp.zeros((n_ct, LANES - 2, 4 * LANES), BF16)
    return jnp.concatenate([tiles.astype(BF16), hi, lo, pad], axis=1)


def _router_split(w):
    wp = jnp.pad(w, ((0, 0), (0, LANES - N_EXPERTS)))
    hi = wp.astype(BF16)
    mid = (wp - hi.astype(F32)).astype(BF16)
    return jnp.concatenate([hi, mid], axis=1)


def _proj(x, mods, msel, p, T):
    return _proj_part(x.reshape(-1, D_MODEL), mods, msel, p["norm_mix"], p["w_in"], T)


def _lru(xr4, B, T, row_len, h0, p, unroll):
    n_ct = D_LRU // LANES
    return _lru_part(xr4.reshape(n_ct, B, T, LANES), p["conv_w"], p["conv_b"], p["gate_w"], p["lam"],
                     h0, T, row_len, unroll)


def _mix(x, mods, msel, p, T, hs4, gr, yh, tile_rows):
    B = x.shape[0]
    return _mix_out_part(hs4.reshape(D_LRU // LANES, B * T, LANES), gr, yh.reshape(B * T, D_HGRN),
                         x.reshape(B * T, D_MODEL), mods, msel,
                         p["norm_lru"], p["w_out"], p["norm_ffn"], p["w_router"], T, tile_rows)


def kernel(x_prompt, x_sample, state_lru, state_hgrn, c, c_ctx, w_ada, b_ada, norm_mix, w_in, conv_w,
           conv_b, lru_wa, lru_ba, lru_wx, lru_bx, lru_lambda, norm_lru, hgrn_gamma, norm_hgrn, w_out,
           norm_ffn, w_router, w_gate, w_up, w_down, norm_final):
    assert w_ada.shape[0] == 1 and hgrn_gamma.shape[1] == 2, "one trunk layer"
    l = 0
    Bp, Tp, _ = x_prompt.shape
    Bs, Ts, _ = x_sample.shape

    cond = jnp.concatenate([c_ctx[None, :], c, jnp.zeros((16 - 1 - Bs, D_MODEL), F32)], axis=0)
    n1 = 2 * D_MODEL
    ((m1,),) = _run([_adaln_part(cond, w_ada, b_ada, 0, n1, D_MODEL)])
    m1 = m1.reshape(16, 2, D_MODEL)
    sel_p, sel_s = (0, False), (1, True)

    p = {
        "norm_mix": norm_mix[l][None, :], "w_in": w_in,
        "conv_w": conv_w[l], "conv_b": conv_b[l][None, :],
        "gate_w": _gate_tiles(lru_wa[l], lru_wx[l], lru_ba[l], lru_bx[l]),
        "lam": lru_lambda[l], "norm_lru": norm_lru[l][None, :],
        "gamma": hgrn_gamma, "norm_hgrn": norm_hgrn[l][None, :],
        "w_out": w_out[l], "norm_ffn": norm_ffn[l][None, :],
        "w_router": _router_split(w_router[l]),
    }
    zeros_p = jnp.zeros((Bp, 2 * D_LRU), F32)
    proj_p = _proj(x_prompt, m1, sel_p, p, Tp)
    n2 = (N_MOD - 2) * D_MODEL
    (xr4_p, gr_p, hg5_p), (m2,) = _run([proj_p, _adaln_part(cond, w_ada, b_ada, n1, n2, n2 // proj_p["steps"])])
    m2 = m2.reshape(16, N_MOD - 2, D_MODEL)
    (xr4_s, gr_s, hg5_s), (hs4_p, last_f, last_b) = _run(
        [_proj(x_sample, m1, sel_s, p, Ts), _lru(xr4_p, Bp, Tp, Tp, zeros_p, p, True)])
    yh_p, st_p = _hgrn(hg5_p.reshape(N_HG_IN, Bp, Tp, D_HGRN), p["gamma"], p["norm_hgrn"], None, Tp, True)
    ((x1p, ha_p, at_p),) = _run([_mix(x_prompt, m2, sel_p, p, Tp, hs4_p, gr_p, yh_p, MIX_TILE)])
    slot_p = _select(at_p, Tp)
    assert state_lru.shape[1] == 1
    ((hs4_s, _, _),) = _run([_lru(xr4_s, Bs, Ts, GRID_W, state_lru.reshape(Bs, 2 * D_LRU), p, 8)])
    yh_s, _ = _hgrn(hg5_s.reshape(N_HG_IN, Bs, Ts, D_HGRN), p["gamma"], p["norm_hgrn"], state_hgrn, Ts, False)
    fused_tile = MIX_TILE
    (x1s, ha_s, at_s), (xd_p,) = _run(
        [_mix(x_sample, m2, sel_s, p, Ts, hs4_s, gr_s, yh_s, fused_tile),
         _dispatch_part(slot_p, ha_p, Tp, fused_tile // Tp)])
    slot_s = _select(at_s, Ts)
    ((xd_s,),) = _run([_dispatch_part(slot_s, ha_s, Ts, max(ROUTE_TOKENS // Ts, 1))])

    yd_p, yd_s = _ffn(xd_p, xd_s, w_gate, w_up, w_down)
    nf = norm_final[None, :]
    (y_sample,), (y_prompt,) = _run(
        [_combine_part(slot_s, yd_s, x1s, m2, sel_s, nf, Ts, COMBINE_TOKENS),
         _combine_part(slot_p, yd_p, x1p, m2, sel_p, nf, Tp, COMBINE_TOKENS)])
    y_prompt = y_prompt.reshape(Bp, Tp, D_MODEL)
    y_sample = y_sample.reshape(Bs, Ts, D_MODEL)
    new_state_lru = jnp.stack([last_f, last_b], axis=1)[:, None]
    new_state_hgrn = st_p
    return (y_prompt, y_sample, new_state_lru, new_state_hgrn)
```

```python
import functools

import jax
import jax.numpy as jnp
from jax import lax
from jax.experimental import pallas as pl
from jax.experimental.pallas import tpu as pltpu

F32 = jnp.float32
BF16 = jnp.bfloat16

D_MODEL = 1024
D_LRU = 512
D_HGRN = 512
HG_HEADS = 4
HG_DK = 128
LRU_BLOCKS = 8
LRU_BW = 64
LRU_C = 8.0
N_EXPERTS = 16
EC_CAPACITY_FACTOR = 2
N_MOD = 6
D_IN = 7 * 512
N_HG_IN = 5
GRID_W = 64
EPS = 1e-6
LOG2E = 1.4426950408889634
TINY = 1e-37

LANES = 128
TOK_TILE = 256
PROJ_TILE = 512
MIX_TILE = 1024
LRU_GROUP = 8
PITCH_PAD = 8
HG_CHUNK = 64
HG_UNROLL = 16
HG_ROWS = 256
FFN_TILE = 1024
ROUTE_TOKENS = 1024
COMBINE_TOKENS = 1024
SEL_BISECT = 20
D_AUG = D_MODEL + LANES
VMEM_LIMIT = 60 * 1024 * 1024


def _cp(sem, vmem=VMEM_LIMIT):
    return pltpu.CompilerParams(dimension_semantics=sem, vmem_limit_bytes=vmem)


def _sigmoid(x):
    return 0.5 * (1.0 + jnp.tanh(0.5 * x))


def _dot(a, b):
    return jnp.dot(a, b, preferred_element_type=F32)


def _dot_nt(a, b):
    return lax.dot_general(a, b, (((1,), (1,)), ((), ())), preferred_element_type=F32)


def _dot_tn(a, b):
    return lax.dot_general(a, b, (((0,), (0,)), ((), ())), preferred_element_type=F32)


def _adaln_kernel(c_ref, w_ref, b_ref, o_ref):
    c = c_ref[...]
    s = (c * _sigmoid(c)).astype(BF16)
    o_ref[...] = _dot(s, w_ref[...].astype(BF16)) + b_ref[...]
    yield


def _adaln_part(cond, w, b, col0, ncols, tn):
    n = cond.shape[0]
    j0 = col0 // tn
    return dict(
        kernel=_adaln_kernel,
        steps=ncols // tn,
        args=[cond, w, b],
        in_specs=[pl.BlockSpec((n, D_MODEL), lambda j: (0, 0)),
                  pl.BlockSpec((None, D_MODEL, tn), lambda j: (0, 0, j0 + j)),
                  pl.BlockSpec((1, tn), lambda j: (0, j0 + j))],
        out_specs=[pl.BlockSpec((n, tn), lambda j: (0, j))],
        out_shape=[jax.ShapeDtypeStruct((n, ncols), F32)],
        scratch=[],
    )


def _proj_kernel(x_ref, mod_ref, g_ref, w_ref, xr_ref, gr_ref, hg_ref):
    x = x_ref[...]
    ms = jnp.mean(x * x, axis=-1, keepdims=True)
    y = x * lax.rsqrt(ms + EPS) * g_ref[...]
    h = y * (1.0 + mod_ref[0, 1:2, :]) + mod_ref[0, 0:1, :]
    h = h.astype(BF16)
    W = 2 * LANES
    for j in range(D_IN // W):
        res = _dot(h, w_ref[:, j * W:(j + 1) * W].astype(BF16))
        k, half = divmod(j * W, D_HGRN)
        if k == 0:
            for c in range(W // LANES):
                xr_ref[half // LANES + c] = res[:, c * LANES:(c + 1) * LANES]
        elif k == 1:
            gr_ref[:, half:half + W] = res.astype(BF16)
        else:
            hg_ref[k - 2, :, half:half + W] = res
        yield


def _run(parts):
    steps = parts[0]["steps"]
    assert all(p["steps"] == steps for p in parts)
    n_in = [len(p["args"]) for p in parts]
    n_out = [len(p["out_shape"]) for p in parts]
    n_sc = [len(p["scratch"]) for p in parts]

    def body(*refs):
        ins, outs, scr = refs[:sum(n_in)], refs[sum(n_in):sum(n_in) + sum(n_out)], refs[sum(n_in) + sum(n_out):]
        i = o = s = 0
        gens = []
        for p, a, b, c in zip(parts, n_in, n_out, n_sc):
            gens.append(p["kernel"](*ins[i:i + a], *outs[o:o + b], *scr[s:s + c]))
            i, o, s = i + a, o + b, s + c
        while gens:
            for g in list(gens):
                if next(g, StopIteration) is StopIteration:
                    gens.remove(g)

    res = pl.pallas_call(
        body,
        grid=(steps,),
        in_specs=[s for p in parts for s in p["in_specs"]],
        out_specs=[s for p in parts for s in p["out_specs"]],
        out_shape=[s for p in parts for s in p["out_shape"]],
        scratch_shapes=[s for p in parts for s in p["scratch"]],
        compiler_params=_cp(("arbitrary",)),
    )(*[a for p in parts for a in p["args"]])
    out, o = [], 0
    for b in n_out:
        out.append(res[o:o + b])
        o += b
    return out


def _mod_map(msel, steps_per_request):
    row0, per_request = msel
    if not per_request:
        return lambda i: (row0, 0, 0)
    assert steps_per_request >= 1
    return lambda i: (row0 + i // steps_per_request, 0, 0)


def _proj_part(x, mods, msel, gain, w, T, tile):
    n = x.shape[0]
    assert T % tile == 0 or (tile % T == 0 and not msel[1])
    mod_map = _mod_map(msel, T // tile)
    n_ct = D_LRU // LANES
    return dict(
        kernel=_proj_kernel,
        steps=n // tile,
        args=[x, mods, gain, w],
        in_specs=[pl.BlockSpec((tile, D_MODEL), lambda i: (i, 0)),
                  pl.BlockSpec((1, mods.shape[1], D_MODEL), mod_map),
                  pl.BlockSpec((1, D_MODEL), lambda i: (0, 0)),
                  pl.BlockSpec((None, D_MODEL, D_IN), lambda i: (0, 0, 0), pipeline_mode=pl.Buffered(1))],
        out_specs=[pl.BlockSpec((n_ct, tile, LANES), lambda i: (0, i, 0)),
                   pl.BlockSpec((tile, D_LRU), lambda i: (i, 0)),
                   pl.BlockSpec((N_HG_IN, tile, D_HGRN), lambda i: (0, i, 0))],
        out_shape=[jax.ShapeDtypeStruct((n_ct, n, LANES), F32),
                   jax.ShapeDtypeStruct((n, D_LRU), BF16),
                   jax.ShapeDtypeStruct((N_HG_IN, n, D_HGRN), F32)],
        scratch=[],
    )


def _lru_kernel(T, row_len, unroll, xr_ref, cw_ref, cb_ref, w_ref, lam_ref, h0f_ref, h0b_ref,
                hs_ref, lf_ref, lb_ref, xpad, af, uf, ab, ub, hf_s, hb_s):
    pitch = T + PITCH_PAD
    pos = lax.broadcasted_iota(jnp.int32, (T, LANES), 0) & (row_len - 1)
    cw = cw_ref[...]
    w0 = jnp.where(pos >= 2, cw[0:1], 0.0)
    w1 = jnp.where(pos >= 1, cw[1:2], 0.0)
    w2 = jnp.broadcast_to(cw[2:3], (T, LANES))
    w3 = jnp.where(pos <= row_len - 2, cw[3:4], 0.0)
    cb = cb_ref[...]
    nl = -lam_ref[...]
    sp = jnp.maximum(nl, 0.0) + jnp.log1p(jnp.exp(-jnp.abs(nl)))
    c2 = (-0.5 * LRU_C * LOG2E) * sp
    w = w_ref[0]
    ones = (lax.broadcasted_iota(jnp.int32, (T, LANES), 1) < 2).astype(BF16)
    for s in range(2):
        xpad[s, 0:8, :] = jnp.zeros((8, LANES), F32)
        xpad[s, T + 8:T + 16, :] = jnp.zeros((8, LANES), F32)
    for b in range(LRU_GROUP):
        x = xr_ref[b]
        xp = xpad.at[b % 2]
        xp[8:T + 8, :] = x
        xc = w0 * xp[6:T + 6, :] + w1 * xp[7:T + 7, :] + w2 * x + w3 * xp[9:T + 9, :] + cb
        xh = 0.5 * xc
        z = _dot(jnp.concatenate([xc.astype(BF16), ones], axis=1), w)
        rows = pl.ds(b * pitch, T)
        for d, (a_s, u_s) in enumerate(((af, uf), (ab, ub))):
            tr = jnp.tanh(z[:, (2 * d) * LANES:(2 * d + 1) * LANES])
            ti = jnp.tanh(z[:, (2 * d + 1) * LANES:(2 * d + 2) * LANES])
            a = jnp.exp2(c2[d:d + 1] + c2[d:d + 1] * tr)
            om = 1.0 - a * a
            a_s[rows, :] = a
            u_s[rows, :] = (om * lax.rsqrt(jnp.maximum(om, TINY))) * (xh + ti * xh)
        yield

    def step(t, carry):
        hf, hb = carry
        rf = pl.ds(t, LRU_GROUP, stride=pitch)
        hf = af[rf, :] * hf + uf[rf, :]
        hf_s[rf, :] = hf
        rb = pl.ds(T - 1 - t, LRU_GROUP, stride=pitch)
        hb = ab[rb, :] * hb + ub[rb, :]
        hb_s[rb, :] = hb
        return hf, hb

    carry = (h0f_ref[...], h0b_ref[...])
    if unroll is True:
        for t in range(T):
            carry = step(t, carry)
            if t % (T // 8) == T // 8 - 1 and t != T - 1:
                yield
    else:
        carry = lax.fori_loop(0, T, step, carry, unroll=unroll)
    hf, hb = carry
    lf_ref[...] = hf
    lb_ref[...] = hb
    for b in range(LRU_GROUP):
        rows = pl.ds(b * pitch, T)
        hs_ref[b] = (hf_s[rows, :] + hb_s[rows, :]).astype(BF16)


def _lru_part(xr4, conv_w, conv_b, w_tiles, lam, h0, T, row_len, unroll):
    n_ct, B = xr4.shape[:2]
    rows = LRU_GROUP * (T + PITCH_PAD)
    vec = pl.BlockSpec((LRU_GROUP, LANES), lambda i: (i // n_ct, i % n_ct))
    vec_b = pl.BlockSpec((LRU_GROUP, LANES), lambda i: (i // n_ct, n_ct + i % n_ct))
    slab = pl.BlockSpec((None, LRU_GROUP, T, LANES), lambda i: (i % n_ct, i // n_ct, 0, 0))
    return dict(
        kernel=functools.partial(_lru_kernel, T, row_len, unroll),
        steps=(B // LRU_GROUP) * n_ct,
        args=[xr4, conv_w, conv_b, w_tiles, lam, h0, h0],
        in_specs=[slab,
                  pl.BlockSpec((4, LANES), lambda i: (0, i % n_ct)),
                  pl.BlockSpec((1, LANES), lambda i: (0, i % n_ct)),
                  pl.BlockSpec((1, 2 * LANES, 4 * LANES), lambda i: (i % n_ct, 0, 0)),
                  pl.BlockSpec((2, LANES), lambda i: (0, i % n_ct)),
                  vec, vec_b],
        out_specs=[slab, vec, vec],
        out_shape=[jax.ShapeDtypeStruct((n_ct, B, T, LANES), BF16),
                   jax.ShapeDtypeStruct((B, D_LRU), F32),
                   jax.ShapeDtypeStruct((B, D_LRU), F32)],
        scratch=[pltpu.VMEM((2, T + 16, LANES), F32)] + [pltpu.VMEM((rows, LANES), F32)] * 6,
    )


def _hgrn_kernel(T, has_s0, want_state, *refs):
    q_ref, ff_ref, fb_ref, v_ref, og_ref, gam_ref, gain_ref = refs[:7]
    refs = refs[7:]
    if has_s0:
        s0_ref, refs = refs[0], refs[1:]
    y_ref, refs = refs[0], refs[1:]
    if want_state:
        st_ref, refs = refs[0], refs[1:]
    q_s, v_s, kf_s, kb_s, bf_s, bb_s, of_s, ob_s, st_s = refs
    C = HG_CHUNK
    n = T // C
    RB = HG_ROWS
    heads = [slice(h * LANES, (h + 1) * LANES) for h in range(HG_HEADS)]

    ti = lax.broadcasted_iota(jnp.int32, (RB, RB), 0)
    si = lax.broadcasted_iota(jnp.int32, (RB, RB), 1)
    sh = C.bit_length() - 1
    same = (ti >> sh) == (si >> sh)
    tri = ((same & (si <= ti)).astype(BF16), (same & (si >= ti)).astype(BF16))

    for r in range(T // RB):
        rows = slice(r * RB, (r + 1) * RB)
        qr = q_ref[rows, :]
        qh = (0.5 * HG_DK ** -0.5) * qr
        q_s[rows, :] = qh + qh * jnp.tanh(0.5 * qr)
        v_s[rows, :] = v_ref[rows, :].astype(BF16)
        for d, (f_ref, k_s, b_s) in enumerate(((ff_ref, kf_s, bf_s), (fb_ref, kb_s, bb_s))):
            g0 = gam_ref[d, 0:1, :]
            g1 = gam_ref[d, 1:2, :]
            m = jnp.maximum(g0, g1)
            e0 = jnp.exp(g0 - m)
            lbd = e0 / (e0 + jnp.exp(g1 - m))
            ck = 0.5 * (1.0 - lbd)
            pt = ck * jnp.tanh(0.5 * f_ref[rows, :])
            k_s[rows, :] = ck - pt
            lf = jnp.log((lbd + ck) + pt)
            hi = lf.astype(BF16)
            r1 = lf - hi.astype(F32)
            mid = r1.astype(BF16)
            lo = (r1 - mid.astype(F32)).astype(BF16)
            cs = _dot(tri[d], jnp.concatenate([hi, mid, lo], axis=1))
            b_s[rows, :] = cs[:, 0:D_HGRN] + (cs[:, D_HGRN:2 * D_HGRN] + cs[:, 2 * D_HGRN:3 * D_HGRN])

    for d in range(2):
        for h in range(HG_HEADS):
            st_s[d * HG_HEADS + h] = s0_ref[d, h].T if has_s0 else jnp.zeros((LANES, LANES), F32)

    ri =lax.broadcasted_iota(jnp.int32, (C, C), 0)
    ci = lax.broadcasted_iota(jnp.int32, (C, C), 1)

    def body(i, carry):
        for d, (k_s, b_s, o_s) in enumerate(((kf_s, bf_s, of_s), (kb_s, bb_s, ob_s))):
            c = i if d == 0 else n - 1 - i
            sl = pl.ds(pl.multiple_of(c * C, C), C)
            for h in range(HG_HEADS):
                q = q_s[sl, heads[h]]
                k = k_s[sl, heads[h]]
                b = b_s[sl, heads[h]]
                v = v_s[sl, heads[h]]
                bm = b[C // 2:C // 2 + 1, :]
                g = b[C - 1:C, :] if d == 0 else b[0:1, :]
                qi = q * jnp.exp(b - bm)
                qe = (qi * jnp.exp(bm)).astype(BF16)
                ki = k * jnp.exp(bm - b)
                kd = (ki * jnp.exp(g - bm)).astype(BF16)
                s = _dot_nt(qi.astype(BF16), ki.astype(BF16))
                p = jnp.where((ci <= ri) if d == 0 else (ci >= ri), s, 0.0).astype(BF16)
                st = st_s[d * HG_HEADS + h]
                o_s[sl, heads[h]] = _dot(p, v) + _dot(qe, st.T.astype(BF16))
                st_s[d * HG_HEADS + h] = st * jnp.exp(g) + _dot_tn(v, kd)
        return carry

    lax.fori_loop(0, n, body, 0, unroll=HG_UNROLL)
    if want_state:
        for d in range(2):
            for h in range(HG_HEADS):
                st_ref[d, h] = st_s[d * HG_HEADS + h].T

    for r in range(T // RB):
        rows = slice(r * RB, (r + 1) * RB)
        for h in range(HG_HEADS):
            o = of_s[rows, heads[h]] + ob_s[rows, heads[h]]
            ms = jnp.mean(o * o, axis=-1, keepdims=True)
            oh = 0.5 * og_ref[rows, heads[h]]
            y = o * lax.rsqrt(ms + EPS) * gain_ref[...] * (oh + oh * jnp.tanh(oh))
            y_ref[rows, heads[h]] = y.astype(BF16)


def _hgrn(hg5, gamma, gain, s0, T, want_state):
    B = hg5.shape[1]
    has_s0 = s0 is not None

    def col(k):
        return pl.BlockSpec((None, None, T, D_HGRN), lambda b, k=k: (k, b, 0, 0))

    st_spec = pl.BlockSpec((None, None, 2, HG_HEADS, HG_DK, HG_DK), lambda b: (b, 0, 0, 0, 0, 0))
    in_specs = [col(0), col(1), col(2), col(3), col(4),
                pl.BlockSpec((2, 2, D_HGRN), lambda b: (0, 0, 0)),
                pl.BlockSpec((1, LANES), lambda b: (0, 0))]
    args = [hg5, hg5, hg5, hg5, hg5, gamma, gain]
    if has_s0:
        in_specs.append(st_spec)
        args.append(s0)
    out_specs = [pl.BlockSpec((None, T, D_HGRN), lambda b: (b, 0, 0))]
    out_shape = [jax.ShapeDtypeStruct((B, T, D_HGRN), BF16)]
    if want_state:
        out_specs.append(st_spec)
        out_shape.append(jax.ShapeDtypeStruct((B, 1, 2, HG_HEADS, HG_DK, HG_DK), F32))
    res = pl.pallas_call(
        functools.partial(_hgrn_kernel, T, has_s0, want_state),
        grid=(B,),
        in_specs=in_specs,
        out_specs=out_specs,
        out_shape=out_shape,
        scratch_shapes=[pltpu.VMEM((T, D_HGRN), F32), pltpu.VMEM((T, D_HGRN), BF16)]
        + [pltpu.VMEM((T, D_HGRN), F32)] * 6
        + [pltpu.VMEM((2 * HG_HEADS, HG_DK, HG_DK), F32)],
        compiler_params=_cp(("arbitrary",)),
    )(*args)
    return res if want_state else (res[0], None)


def _mix_out_kernel(hs_ref, gr_ref, yh_ref, x_ref, mod_ref, nl_ref, wo_ref, nf_ref, wr_ref,
                    x1_ref, ha_ref, at_ref):
    wr = wr_ref[...]
    wo = wo_ref[...].astype(BF16)
    g1 = mod_ref[0, 0:1, :]
    sh2 = mod_ref[0, 1:2, :]
    gain2 = nf_ref[...] * (1.0 + mod_ref[0, 2:3, :])
    lane = lax.broadcasted_iota(jnp.int32, (TOK_TILE, LANES), 1)
    for s in range(x_ref.shape[0] // TOK_TILE):
        rows = slice(s * TOK_TILE, (s + 1) * TOK_TILE)
        hs = jnp.concatenate([hs_ref[c, rows, :] for c in range(D_LRU // LANES)], axis=1).astype(F32)
        ms = jnp.mean(hs * hs, axis=-1, keepdims=True)
        gr = gr_ref[rows, :].astype(F32)
        gelu = 0.5 * gr * (1.0 + jnp.tanh(0.7978845608028654 * (gr + 0.044715 * (gr * gr * gr))))
        y_lru = hs * lax.rsqrt(ms + EPS) * nl_ref[...] * gelu
        ycat = jnp.concatenate([y_lru.astype(BF16), yh_ref[rows, :]], axis=-1)
        x1 = x_ref[rows, :] + g1 * _dot(ycat, wo)
        x1_ref[rows, :] = x1
        yield
        ms = jnp.mean(x1 * x1, axis=-1, keepdims=True)
        h2 = x1 * lax.rsqrt(ms + EPS) * gain2 + sh2
        h2_hi = h2.astype(BF16)
        h2_mid = (h2 - h2_hi.astype(F32)).astype(BF16)
        p_hi = _dot(h2_hi, wr)
        logits = p_hi[:, 0:LANES] + (p_hi[:, LANES:2 * LANES] + _dot(h2_mid, wr)[:, 0:LANES])
        logits = jnp.where(lane < N_EXPERTS, logits, -jnp.inf)
        e = jnp.exp(logits - jnp.max(logits, axis=-1, keepdims=True))
        aff = e / jnp.sum(e, axis=-1, keepdims=True)
        w = at_ref.shape[1]
        r, c = (s * TOK_TILE) // w, (s * TOK_TILE) % w
        at_ref[r * N_EXPERTS:(r + 1) * N_EXPERTS, c:c + TOK_TILE] = aff.T[0:N_EXPERTS, :]
        hi = aff.astype(BF16).astype(F32)
        r1 = aff - hi
        mid = r1.astype(BF16).astype(F32)
        lo = (r1 - mid).astype(BF16).astype(F32)
        split = hi + pltpu.roll(mid, N_EXPERTS, 1) + pltpu.roll(lo, 2 * N_EXPERTS, 1)
        ha_ref[rows, 0:D_MODEL] = h2_hi
        ha_ref[rows, D_MODEL:D_AUG] = split.astype(BF16)
        yield


def _mix_out_part(hs4, gr, yh, x, mods, msel, norm_lru, w_out, norm_ffn, wr_pad, T, tile_rows):
    n = x.shape[0]
    assert T % TOK_TILE == 0 and (tile_rows % T == 0 or T % tile_rows == 0)
    q = max(T // tile_rows, 1)
    at_rows = N_EXPERTS * max(tile_rows // T, 1)
    at_spec = pl.BlockSpec((at_rows, min(T, tile_rows)), lambda i: (i // q, i % q))
    mod_map = _mod_map(msel, q)
    tile = lambda w: pl.BlockSpec((tile_rows, w), lambda i: (i, 0))
    const = lambda r, w: pl.BlockSpec((r, w), lambda i: (0, 0))
    return dict(
        kernel=_mix_out_kernel,
        steps=n // tile_rows,
        args=[hs4, gr, yh, x, mods, norm_lru, w_out, norm_ffn, wr_pad],
        in_specs=[pl.BlockSpec((D_LRU // LANES, tile_rows, LANES), lambda i: (0, i, 0)),
                  tile(D_LRU), tile(D_HGRN), tile(D_MODEL),
                  pl.BlockSpec((1, mods.shape[1], D_MODEL), mod_map),
                  const(1, D_LRU), const(D_MODEL, D_MODEL), const(1, D_MODEL), const(D_MODEL, 2 * LANES)],
        out_specs=[tile(D_MODEL), tile(D_AUG), at_spec],
        out_shape=[jax.ShapeDtypeStruct((n, D_MODEL), F32),
                   jax.ShapeDtypeStruct((n, D_AUG), BF16),
                   jax.ShapeDtypeStruct((n // T * N_EXPERTS, T), F32)],
        scratch=[],
    )


def _select_kernel(cap, at_ref, slot_ref):
    a = at_ref[...]
    R, T = a.shape
    ones = jnp.ones((T, LANES), BF16)
    su = (lax.broadcasted_iota(jnp.int32, (T, T), 0)
          < lax.broadcasted_iota(jnp.int32, (T, T), 1)).astype(BF16)

    def wide(x):
        return jnp.concatenate([x] * (T // LANES), axis=1)

    def count(mask):
        return _dot(mask.astype(BF16), ones)

    lo = jnp.zeros((R, LANES), F32)
    hi = jnp.full((R, LANES), 2.0, F32)
    for _ in range(SEL_BISECT):
        mid = 0.5 * (lo + hi)
        ge = count(a >= wide(mid)) >= cap
        lo = jnp.where(ge, mid, lo)
        hi = jnp.where(ge, hi, mid)

    def cond(st):
        return (st[0] < T) & (jnp.min(st[2]) < cap)

    def body(st):
        it, cur, n, thr = st
        m = jnp.max(jnp.where(a < wide(cur), a, -1.0), axis=1, keepdims=True)
        m = jnp.broadcast_to(m, (R, LANES))
        c = count(a >= wide(m))
        act = n < cap
        return it + 1, jnp.where(act, m, cur), jnp.where(act, c, n), jnp.where(act, m, thr)

    _, _, _, thr = lax.while_loop(cond, body, (jnp.int32(0), hi, count(a >= wide(hi)), lo))
    thr_w = wide(thr)
    gt = a > thr_w
    eq = a == thr_w
    need = wide(cap - count(gt))
    sel = gt | (eq & (_dot(eq.astype(BF16), su) < need))
    slot_ref[...] = jnp.where(sel, _dot(sel.astype(BF16), su), -1.0).astype(jnp.int32)


def _select(at, T):
    cap = EC_CAPACITY_FACTOR * T // N_EXPERTS
    R = at.shape[0]
    return pl.pallas_call(
        functools.partial(_select_kernel, cap),
        grid=(1,),
        in_specs=[pl.BlockSpec((R, T), lambda i: (0, 0))],
        out_specs=pl.BlockSpec((R, T), lambda i: (0, 0)),
        out_shape=jax.ShapeDtypeStruct((R, T), jnp.int32),
        compiler_params=_cp(("arbitrary",)),
    )(at)


def _onehot(cap, slot_ref, g):
    j = lax.broadcasted_iota(jnp.int32, (cap, slot_ref.shape[1]), 0)
    r0 = g * N_EXPERTS
    return jnp.concatenate([(slot_ref[r0 + e:r0 + e + 1, :] == j).astype(BF16) for e in range(N_EXPERTS)], axis=0)


def _dispatch_kernel(T, cap, slot_ref, ha_ref, xd_ref):
    for g in range(ha_ref.shape[0] // T):
        rows = _dot(_onehot(cap, slot_ref, g), ha_ref[g * T:(g + 1) * T, :])
        for e in range(N_EXPERTS):
            xd_ref[e, g * cap:(g + 1) * cap, :] = rows[e * cap:(e + 1) * cap].astype(BF16)
        yield


def _dispatch_part(slot, ha, T, G):
    cap = EC_CAPACITY_FACTOR * T // N_EXPERTS
    B = ha.shape[0] // T
    return dict(
        kernel=functools.partial(_dispatch_kernel, T, cap),
        steps=B // G,
        args=[slot, ha],
        in_specs=[pl.BlockSpec((G * N_EXPERTS, T), lambda b: (b, 0)),
                  pl.BlockSpec((G * T, D_AUG), lambda b: (b, 0))],
        out_specs=[pl.BlockSpec((N_EXPERTS, G * cap, D_AUG), lambda b: (0, b, 0))],
        out_shape=[jax.ShapeDtypeStruct((N_EXPERTS, B * cap, D_AUG), BF16)],
        scratch=[],
    )


def _ffn_kernel(xp_ref, xs_ref, wg_ref, wu_ref, wd_ref, yp_ref, ys_ref):
    e = pl.program_id(0)
    lane = lax.broadcasted_iota(jnp.int32, (FFN_TILE, LANES), 1)
    pick = (lane == e) | (lane == e + N_EXPERTS) | (lane == e + 2 * N_EXPERTS)
    wg = wg_ref[...].astype(BF16)
    wu = wu_ref[...].astype(BF16)
    wd = wd_ref[...].astype(BF16)
    for x_ref, y_ref in ((xp_ref, yp_ref), (xs_ref, ys_ref)):
        x = x_ref[:, 0:D_MODEL]
        gate = jnp.sum(jnp.where(pick, x_ref[:, D_MODEL:D_AUG].astype(F32), 0.0), axis=-1, keepdims=True)
        hg = _dot(x, wg)
        hid = (hg * _sigmoid(hg)) * _dot(x, wu)
        y_ref[...] = (_dot(hid.astype(BF16), wd) * gate).astype(BF16)


def _ffn(xd_p, xd_s, w_gate, w_up, w_down):
    rows = xd_p.shape[1]
    assert xd_s.shape[1] == rows and rows % FFN_TILE == 0
    xspec = pl.BlockSpec((None, FFN_TILE, D_AUG), lambda e, m: (e, m, 0))
    wspec = pl.BlockSpec((None, None, D_MODEL, D_MODEL), lambda e, m: (0, e, 0, 0))
    yspec = pl.BlockSpec((None, FFN_TILE, D_MODEL), lambda e, m: (e, m, 0))
    yshape = jax.ShapeDtypeStruct((N_EXPERTS, rows, D_MODEL), BF16)
    return pl.pallas_call(
        _ffn_kernel,
        grid=(N_EXPERTS, rows // FFN_TILE),
        in_specs=[xspec, xspec, wspec, wspec, wspec],
        out_specs=[yspec, yspec],
        out_shape=[yshape, yshape],
        compiler_params=_cp(("arbitrary", "arbitrary")),
    )(xd_p, xd_s, w_gate, w_up, w_down)


def _combine_kernel(cap, slot_ref, yd_ref, x1_ref, mod_ref, nf_ref, o_ref):
    W = slot_ref.shape[1]
    for g in range(x1_ref.shape[0] // W):
        rows = slice(g * W, (g + 1) * W)
        yd = jnp.concatenate([yd_ref[e, g * cap:(g + 1) * cap, :] for e in range(N_EXPERTS)], axis=0)
        acc = _dot_tn(_onehot(cap, slot_ref, g), yd)
        x2 = x1_ref[rows, :] + mod_ref[0, 3:4, :] * acc
        ms = jnp.mean(x2 * x2, axis=-1, keepdims=True)
        o_ref[rows, :] = x2 * lax.rsqrt(ms + EPS) * nf_ref[...]
        yield


def _combine_part(slot, yd, x1, mods, msel, norm_final, T, tok):
    cap = EC_CAPACITY_FACTOR * T // N_EXPERTS
    n = x1.shape[0]
    G, h = (tok // T, 1) if tok >= T else (1, T // tok)
    assert G == 1 or not msel[1]
    mod_map = _mod_map(msel, h)
    return dict(
        kernel=functools.partial(_combine_kernel, cap),
        steps=n // tok,
        args=[slot, yd, x1, mods, norm_final],
        in_specs=[pl.BlockSpec((G * N_EXPERTS, min(T, tok)), lambda i: (i // h, i % h)),
                  pl.BlockSpec((N_EXPERTS, G * cap, D_MODEL), lambda i: (0, i // h, 0)),
                  pl.BlockSpec((tok, D_MODEL), lambda i: (i, 0)),
                  pl.BlockSpec((1, mods.shape[1], D_MODEL), mod_map),
                  pl.BlockSpec((1, D_MODEL), lambda i: (0, 0))],
        out_specs=[pl.BlockSpec((tok, D_MODEL), lambda i: (i, 0))],
        out_shape=[jax.ShapeDtypeStruct((n, D_MODEL), F32)],
        scratch=[],
    )


def _gate_tiles(wa, wx, ba, bx):
    n_ct = D_LRU // LANES

    def dense(w):
        w2 = w.reshape(n_ct, 2, LRU_BW, LRU_BW)
        z = jnp.zeros((n_ct, LRU_BW, LRU_BW), w.dtype)
        top = jnp.concatenate([w2[:, 0], z], axis=2)
        bot = jnp.concatenate([z, w2[:, 1]], axis=2)
        return jnp.concatenate([top, bot], axis=1)

    tiles = 0.5 * jnp.concatenate([dense(wa[0]), dense(wx[0]), dense(wa[1]), dense(wx[1])], axis=-1)
    t = lambda v: v.reshape(n_ct, 1, LANES)
    bias = 0.5 * jnp.concatenate([t(ba[0]), t(bx[0]), t(ba[1]), t(bx[1])], axis=-1)
    hi = bias.astype(BF16)
    lo = (bias - hi.astype(F32)).astype(BF16)
    pad = jnp.zeros((n_ct, LANES - 2, 4 * LANES), BF16)
    return jnp.concatenate([tiles.astype(BF16), hi, lo, pad], axis=1)


def _router_split(w):
    wp = jnp.pad(w, ((0, 0), (0, LANES - N_EXPERTS)))
    hi = wp.astype(BF16)
    mid = (wp - hi.astype(F32)).astype(BF16)
    return jnp.concatenate([hi, mid], axis=1)


def _proj(x, mods, msel, p, T, tile):
    return _proj_part(x.reshape(-1, D_MODEL), mods, msel, p["norm_mix"], p["w_in"], T, tile)


def _lru(xr4, B, T, row_len, h0, p, unroll):
    n_ct = D_LRU // LANES
    return _lru_part(xr4.reshape(n_ct, B, T, LANES), p["conv_w"], p["conv_b"], p["gate_w"], p["lam"],
                     h0, T, row_len, unroll)


def _mix(x, mods, msel, p, T, hs4, gr, yh, tile_rows):
    B = x.shape[0]
    return _mix_out_part(hs4.reshape(D_LRU // LANES, B * T, LANES), gr, yh.reshape(B * T, D_HGRN),
                         x.reshape(B * T, D_MODEL), mods, msel,
                         p["norm_lru"], p["w_out"], p["norm_ffn"], p["w_router"], T, tile_rows)


def kernel(x_prompt, x_sample, state_lru, state_hgrn, c, c_ctx, w_ada, b_ada, norm_mix, w_in, conv_w,
           conv_b, lru_wa, lru_ba, lru_wx, lru_bx, lru_lambda, norm_lru, hgrn_gamma, norm_hgrn, w_out,
           norm_ffn, w_router, w_gate, w_up, w_down, norm_final):
    assert w_ada.shape[0] == 1 and hgrn_gamma.shape[1] == 2, "one trunk layer"
    l = 0
    Bp, Tp, _ = x_prompt.shape
    Bs, Ts, _ = x_sample.shape

    cond = jnp.concatenate([c_ctx[None, :], c, jnp.zeros((16 - 1 - Bs, D_MODEL), F32)], axis=0)
    n1 = 2 * D_MODEL
    ((m1,),) = _run([_adaln_part(cond, w_ada, b_ada, 0, n1, D_MODEL)])
    m1 = m1.reshape(16, 2, D_MODEL)
    sel_p, sel_s = (0, False), (1, True)

    p = {
        "norm_mix": norm_mix[l][None, :], "w_in": w_in,
        "conv_w": conv_w[l], "conv_b": conv_b[l][None, :],
        "gate_w": _gate_tiles(lru_wa[l], lru_wx[l], lru_ba[l], lru_bx[l]),
        "lam": lru_lambda[l], "norm_lru": norm_lru[l][None, :],
        "gamma": hgrn_gamma, "norm_hgrn": norm_hgrn[l][None, :],
        "w_out": w_out[l], "norm_ffn": norm_ffn[l][None, :],
        "w_router": _router_split(w_router[l]),
    }
    zeros_p = jnp.zeros((Bp, 2 * D_LRU), F32)
    proj_p = _proj(x_prompt, m1, sel_p, p, Tp, 2 * PROJ_TILE)
    n2 = (N_MOD - 2) * D_MODEL
    (xr4_p, gr_p, hg5_p), (m2,) = _run([proj_p, _adaln_part(cond, w_ada, b_ada, n1, n2, n2 // proj_p["steps"])])
    m2 = m2.reshape(16, N_MOD - 2, D_MODEL)
    (xr4_s, gr_s, hg5_s), (hs4_p, last_f, last_b) = _run(
        [_proj(x_sample, m1, sel_s, p, Ts, PROJ_TILE), _lru(xr4_p, Bp, Tp, Tp, zeros_p, p, True)])
    yh_p, st_p = _hgrn(hg5_p.reshape(N_HG_IN, Bp, Tp, D_HGRN), p["gamma"], p["norm_hgrn"], None, Tp, True)
    ((x1p, ha_p, at_p),) = _run([_mix(x_prompt, m2, sel_p, p, Tp, hs4_p, gr_p, yh_p, MIX_TILE)])
    slot_p = _select(at_p, Tp)
    assert state_lru.shape[1] == 1
    ((hs4_s, _, _),) = _run([_lru(xr4_s, Bs, Ts, GRID_W, state_lru.reshape(Bs, 2 * D_LRU), p, 8)])
    yh_s, _ = _hgrn(hg5_s.reshape(N_HG_IN, Bs, Ts, D_HGRN), p["gamma"], p["norm_hgrn"], state_hgrn, Ts, False)
    fused_tile = MIX_TILE
    (x1s, ha_s, at_s), (xd_p,) = _run(
        [_mix(x_sample, m2, sel_s, p, Ts, hs4_s, gr_s, yh_s, fused_tile),
         _dispatch_part(slot_p, ha_p, Tp, fused_tile // Tp)])
    slot_s = _select(at_s, Ts)
    ((xd_s,),) = _run([_dispatch_part(slot_s, ha_s, Ts, max(ROUTE_TOKENS // Ts, 1))])

    yd_p, yd_s = _ffn(xd_p, xd_s, w_gate, w_up, w_down)
    nf = norm_final[None, :]
    (y_sample,), (y_prompt,) = _run(
        [_combine_part(slot_s, yd_s, x1s, m2, sel_s, nf, Ts, COMBINE_TOKENS),
         _combine_part(slot_p, yd_p, x1p, m2, sel_p, nf, Tp, COMBINE_TOKENS)])
    y_prompt = y_prompt.reshape(Bp, Tp, D_MODEL)
    y_sample = y_sample.reshape(Bs, Ts, D_MODEL)
    new_state_lru = jnp.stack([last_f, last_b], axis=1)[:, None]
    new_state_hgrn = st_p
    return (y_prompt, y_sample, new_state_lru, new_state_hgrn)
```

```python
import functools

import jax
import jax.numpy as jnp
from jax import lax
from jax.experimental import pallas as pl
from jax.experimental.pallas import tpu as pltpu

F32 = jnp.float32
BF16 = jnp.bfloat16

D_MODEL = 1024
D_LRU = 512
D_HGRN = 512
HG_HEADS = 4
HG_DK = 128
LRU_BLOCKS = 8
LRU_BW = 64
LRU_C = 8.0
N_EXPERTS = 16
EC_CAPACITY_FACTOR = 2
N_MOD = 6
D_IN = 7 * 512
N_HG_IN = 5
GRID_W = 64
EPS = 1e-6
LOG2E = 1.4426950408889634
TINY = 1e-37

LANES = 128
MOD_ROWS = 8
TOK_TILE = 256
PROJ_TILE = 512
MIX_TILE = 1024
LRU_GROUP = 8
PITCH_PAD = 8
HG_CHUNK = 64
HG_UNROLL = 16
HG_ROWS = 256
FFN_TILE = 1024
ROUTE_TOKENS = 1024
COMBINE_TOKENS = 1024
SEL_BISECT = 20
D_AUG = D_MODEL + LANES
VMEM_LIMIT = 60 * 1024 * 1024


def _cp(sem, vmem=VMEM_LIMIT):
    return pltpu.CompilerParams(dimension_semantics=sem, vmem_limit_bytes=vmem)


def _sigmoid(x):
    return 0.5 * (1.0 + jnp.tanh(0.5 * x))


def _dot(a, b):
    return jnp.dot(a, b, preferred_element_type=F32)


def _dot_nt(a, b):
    return lax.dot_general(a, b, (((1,), (1,)), ((), ())), preferred_element_type=F32)


def _dot_tn(a, b):
    return lax.dot_general(a, b, (((0,), (0,)), ((), ())), preferred_element_type=F32)


def _adaln_kernel(c_ref, w_ref, b_ref, o_ref):
    c = c_ref[...]
    s = (c * _sigmoid(c)).astype(BF16)
    o_ref[...] = _dot(s, w_ref[...].astype(BF16)) + b_ref[...]
    yield


def _adaln_part(cond, w, b, col0, ncols, tn):
    n = cond.shape[0]
    j0 = col0 // tn
    return dict(
        kernel=_adaln_kernel,
        steps=ncols // tn,
        args=[cond, w, b],
        in_specs=[pl.BlockSpec((n, D_MODEL), lambda j: (0, 0)),
                  pl.BlockSpec((None, D_MODEL, tn), lambda j: (0, 0, j0 + j)),
                  pl.BlockSpec((1, tn), lambda j: (0, j0 + j))],
        out_specs=[pl.BlockSpec((n, tn), lambda j: (0, j))],
        out_shape=[jax.ShapeDtypeStruct((n, ncols), F32)],
        scratch=[],
    )


def _mod_get(mod_ref, spr, j):
    cols = slice(j * D_MODEL, (j + 1) * D_MODEL)
    if spr is None:
        return mod_ref[0:1, cols]
    return mod_ref[pl.ds(pl.program_id(0) // spr, 1), cols]


def _proj_kernel(spr, x_ref, mod_ref, g_ref, w_ref, xr_ref, gr_ref, hg_ref):
    x = x_ref[...]
    ms = jnp.mean(x * x, axis=-1, keepdims=True)
    y = x * lax.rsqrt(ms + EPS) * g_ref[...]
    h = y * (1.0 + _mod_get(mod_ref, spr, 1)) + _mod_get(mod_ref, spr, 0)
    h = h.astype(BF16)
    W = 2 * LANES
    for j in range(D_IN // W):
        res = _dot(h, w_ref[:, j * W:(j + 1) * W].astype(BF16))
        k, half = divmod(j * W, D_HGRN)
        if k == 0:
            for c in range(W // LANES):
                xr_ref[half // LANES + c] = res[:, c * LANES:(c + 1) * LANES]
        elif k == 1:
            gr_ref[:, half:half + W] = res.astype(BF16)
        else:
            hg_ref[k - 2, :, half:half + W] = res
        yield


def _run(parts):
    steps = parts[0]["steps"]
    assert all(p["steps"] == steps for p in parts)
    n_in = [len(p["args"]) for p in parts]
    n_out = [len(p["out_shape"]) for p in parts]
    n_sc = [len(p["scratch"]) for p in parts]

    def body(*refs):
        ins, outs, scr = refs[:sum(n_in)], refs[sum(n_in):sum(n_in) + sum(n_out)], refs[sum(n_in) + sum(n_out):]
        i = o = s = 0
        gens = []
        for p, a, b, c in zip(parts, n_in, n_out, n_sc):
            gens.append(p["kernel"](*ins[i:i + a], *outs[o:o + b], *scr[s:s + c]))
            i, o, s = i + a, o + b, s + c
        while gens:
            for g in list(gens):
                if next(g, StopIteration) is StopIteration:
                    gens.remove(g)

    res = pl.pallas_call(
        body,
        grid=(steps,),
        in_specs=[s for p in parts for s in p["in_specs"]],
        out_specs=[s for p in parts for s in p["out_specs"]],
        out_shape=[s for p in parts for s in p["out_shape"]],
        scratch_shapes=[s for p in parts for s in p["scratch"]],
        compiler_params=_cp(("arbitrary",)),
    )(*[a for p in parts for a in p["args"]])
    out, o = [], 0
    for b in n_out:
        out.append(res[o:o + b])
        o += b
    return out


def _mod_spec(mods, msel):
    blk = msel[0]
    return pl.BlockSpec((MOD_ROWS, mods.shape[1]), lambda i: (blk, 0))


def _spr(msel, steps_per_request):
    if not msel[1]:
        return None
    assert steps_per_request >= 1
    return steps_per_request


def _proj_part(x, mods, msel, gain, w, T, tile):
    n = x.shape[0]
    assert T % tile == 0 or (tile % T == 0 and not msel[1])
    n_ct = D_LRU // LANES
    return dict(
        kernel=functools.partial(_proj_kernel, _spr(msel, T // tile)),
        steps=n // tile,
        args=[x, mods, gain, w],
        in_specs=[pl.BlockSpec((tile, D_MODEL), lambda i: (i, 0)),
                  _mod_spec(mods, msel),
                  pl.BlockSpec((1, D_MODEL), lambda i: (0, 0)),
                  pl.BlockSpec((None, D_MODEL, D_IN), lambda i: (0, 0, 0), pipeline_mode=pl.Buffered(1))],
        out_specs=[pl.BlockSpec((n_ct, tile, LANES), lambda i: (0, i, 0)),
                   pl.BlockSpec((tile, D_LRU), lambda i: (i, 0)),
                   pl.BlockSpec((N_HG_IN, tile, D_HGRN), lambda i: (0, i, 0))],
        out_shape=[jax.ShapeDtypeStruct((n_ct, n, LANES), F32),
                   jax.ShapeDtypeStruct((n, D_LRU), BF16),
                   jax.ShapeDtypeStruct((N_HG_IN, n, D_HGRN), F32)],
        scratch=[],
    )


def _lru_kernel(T, row_len, unroll, xr_ref, cw_ref, cb_ref, w_ref, lam_ref, h0f_ref, h0b_ref,
                hs_ref, lf_ref, lb_ref, xpad, af, uf, ab, ub, hf_s, hb_s):
    pitch = T + PITCH_PAD
    pos = lax.broadcasted_iota(jnp.int32, (T, LANES), 0) & (row_len - 1)
    cw = cw_ref[...]
    w0 = jnp.where(pos >= 2, cw[0:1], 0.0)
    w1 = jnp.where(pos >= 1, cw[1:2], 0.0)
    w2 = jnp.broadcast_to(cw[2:3], (T, LANES))
    w3 = jnp.where(pos <= row_len - 2, cw[3:4], 0.0)
    cb = cb_ref[...]
    nl = -lam_ref[...]
    sp = jnp.maximum(nl, 0.0) + jnp.log1p(jnp.exp(-jnp.abs(nl)))
    c2 = (-0.5 * LRU_C * LOG2E) * sp
    w = w_ref[0]
    ones = (lax.broadcasted_iota(jnp.int32, (T, LANES), 1) < 2).astype(BF16)
    for s in range(2):
        xpad[s, 0:8, :] = jnp.zeros((8, LANES), F32)
        xpad[s, T + 8:T + 16, :] = jnp.zeros((8, LANES), F32)
    for b in range(LRU_GROUP):
        x = xr_ref[b]
        xp = xpad.at[b % 2]
        xp[8:T + 8, :] = x
        xc = w0 * xp[6:T + 6, :] + w1 * xp[7:T + 7, :] + w2 * x + w3 * xp[9:T + 9, :] + cb
        xh = 0.5 * xc
        z = _dot(jnp.concatenate([xc.astype(BF16), ones], axis=1), w)
        rows = pl.ds(b * pitch, T)
        for d, (a_s, u_s) in enumerate(((af, uf), (ab, ub))):
            tr = jnp.tanh(z[:, (2 * d) * LANES:(2 * d + 1) * LANES])
            ti = jnp.tanh(z[:, (2 * d + 1) * LANES:(2 * d + 2) * LANES])
            a = jnp.exp2(c2[d:d + 1] + c2[d:d + 1] * tr)
            om = 1.0 - a * a
            a_s[rows, :] = a
            u_s[rows, :] = (om * lax.rsqrt(jnp.maximum(om, TINY))) * (xh + ti * xh)
        yield

    def step(t, carry):
        hf, hb = carry
        rf = pl.ds(t, LRU_GROUP, stride=pitch)
        hf = af[rf, :] * hf + uf[rf, :]
        hf_s[rf, :] = hf
        rb = pl.ds(T - 1 - t, LRU_GROUP, stride=pitch)
        hb = ab[rb, :] * hb + ub[rb, :]
        hb_s[rb, :] = hb
        return hf, hb

    carry = (h0f_ref[...], h0b_ref[...])
    if unroll is True:
        for t in range(T):
            carry = step(t, carry)
            if t % (T // 8) == T // 8 - 1 and t != T - 1:
                yield
    else:
        carry = lax.fori_loop(0, T, step, carry, unroll=unroll)
    hf, hb = carry
    lf_ref[...] = hf
    lb_ref[...] = hb
    for b in range(LRU_GROUP):
        rows = pl.ds(b * pitch, T)
        hs_ref[b] = (hf_s[rows, :] + hb_s[rows, :]).astype(BF16)


def _lru_part(xr4, conv_w, conv_b, w_tiles, lam, h0, T, row_len, unroll):
    n_ct, B = xr4.shape[:2]
    rows = LRU_GROUP * (T + PITCH_PAD)
    vec = pl.BlockSpec((LRU_GROUP, LANES), lambda i: (i // n_ct, i % n_ct))
    vec_b = pl.BlockSpec((LRU_GROUP, LANES), lambda i: (i // n_ct, n_ct + i % n_ct))
    slab = pl.BlockSpec((None, LRU_GROUP, T, LANES), lambda i: (i % n_ct, i // n_ct, 0, 0))
    return dict(
        kernel=functools.partial(_lru_kernel, T, row_len, unroll),
        steps=(B // LRU_GROUP) * n_ct,
        args=[xr4, conv_w, conv_b, w_tiles, lam, h0, h0],
        in_specs=[slab,
                  pl.BlockSpec((4, LANES), lambda i: (0, i % n_ct)),
                  pl.BlockSpec((1, LANES), lambda i: (0, i % n_ct)),
                  pl.BlockSpec((1, 2 * LANES, 4 * LANES), lambda i: (i % n_ct, 0, 0)),
                  pl.BlockSpec((2, LANES), lambda i: (0, i % n_ct)),
                  vec, vec_b],
        out_specs=[slab, vec, vec],
        out_shape=[jax.ShapeDtypeStruct((n_ct, B, T, LANES), BF16),
                   jax.ShapeDtypeStruct((B, D_LRU), F32),
                   jax.ShapeDtypeStruct((B, D_LRU), F32)],
        scratch=[pltpu.VMEM((2, T + 16, LANES), F32)] + [pltpu.VMEM((rows, LANES), F32)] * 6,
    )


def _hgrn_kernel(T, has_s0, want_state, *refs):
    q_ref, ff_ref, fb_ref, v_ref, og_ref, gam_ref, gain_ref = refs[:7]
    refs = refs[7:]
    if has_s0:
        s0_ref, refs = refs[0], refs[1:]
    y_ref, refs = refs[0], refs[1:]
    if want_state:
        st_ref, refs = refs[0], refs[1:]
    q_s, v_s, kf_s, kb_s, bf_s, bb_s, of_s, ob_s, st_s = refs
    C = HG_CHUNK
    n = T // C
    RB = HG_ROWS
    heads = [slice(h * LANES, (h + 1) * LANES) for h in range(HG_HEADS)]

    ti = lax.broadcasted_iota(jnp.int32, (RB, RB), 0)
    si = lax.broadcasted_iota(jnp.int32, (RB, RB), 1)
    sh = C.bit_length() - 1
    same = (ti >> sh) == (si >> sh)
    tri = ((same & (si <= ti)).astype(BF16), (same & (si >= ti)).astype(BF16))

    for r in range(T // RB):
        rows = slice(r * RB, (r + 1) * RB)
        qr = q_ref[rows, :]
        qh = (0.5 * HG_DK ** -0.5) * qr
        q_s[rows, :] = qh + qh * jnp.tanh(0.5 * qr)
        v_s[rows, :] = v_ref[rows, :].astype(BF16)
        for d, (f_ref, k_s, b_s) in enumerate(((ff_ref, kf_s, bf_s), (fb_ref, kb_s, bb_s))):
            g0 = gam_ref[d, 0:1, :]
            g1 = gam_ref[d, 1:2, :]
            m = jnp.maximum(g0, g1)
            e0 = jnp.exp(g0 - m)
            lbd = e0 / (e0 + jnp.exp(g1 - m))
            ck = 0.5 * (1.0 - lbd)
            pt = ck * jnp.tanh(0.5 * f_ref[rows, :])
            k_s[rows, :] = ck - pt
            lf = jnp.log((lbd + ck) + pt)
            hi = lf.astype(BF16)
            r1 = lf - hi.astype(F32)
            mid = r1.astype(BF16)
            lo = (r1 - mid.astype(F32)).astype(BF16)
            cs = _dot(tri[d], jnp.concatenate([hi, mid, lo], axis=1))
            b_s[rows, :] = cs[:, 0:D_HGRN] + (cs[:, D_HGRN:2 * D_HGRN] + cs[:, 2 * D_HGRN:3 * D_HGRN])

    for d in range(2):
        for h in range(HG_HEADS):
            st_s[d * HG_HEADS + h] = s0_ref[d, h].T if has_s0 else jnp.zeros((LANES, LANES), F32)

    ri =lax.broadcasted_iota(jnp.int32, (C, C), 0)
    ci = lax.broadcasted_iota(jnp.int32, (C, C), 1)

    def body(i, carry):
        for d, (k_s, b_s, o_s) in enumerate(((kf_s, bf_s, of_s), (kb_s, bb_s, ob_s))):
            c = i if d == 0 else n - 1 - i
            sl = pl.ds(pl.multiple_of(c * C, C), C)
            for h in range(HG_HEADS):
                q = q_s[sl, heads[h]]
                k = k_s[sl, heads[h]]
                b = b_s[sl, heads[h]]
                v = v_s[sl, heads[h]]
                bm = b[C // 2:C // 2 + 1, :]
                g = b[C - 1:C, :] if d == 0 else b[0:1, :]
                qi = q * jnp.exp(b - bm)
                qe = (qi * jnp.exp(bm)).astype(BF16)
                ki = k * jnp.exp(bm - b)
                kd = (ki * jnp.exp(g - bm)).astype(BF16)
                s = _dot_nt(qi.astype(BF16), ki.astype(BF16))
                p = jnp.where((ci <= ri) if d == 0 else (ci >= ri), s, 0.0).astype(BF16)
                st = st_s[d * HG_HEADS + h]
                o_s[sl, heads[h]] = _dot(p, v) + _dot(qe, st.T.astype(BF16))
                st_s[d * HG_HEADS + h] = st * jnp.exp(g) + _dot_tn(v, kd)
        return carry

    lax.fori_loop(0, n, body, 0, unroll=HG_UNROLL)
    if want_state:
        for d in range(2):
            for h in range(HG_HEADS):
                st_ref[d, h] = st_s[d * HG_HEADS + h].T

    for r in range(T // RB):
        rows = slice(r * RB, (r + 1) * RB)
        for h in range(HG_HEADS):
            o = of_s[rows, heads[h]] + ob_s[rows, heads[h]]
            ms = jnp.mean(o * o, axis=-1, keepdims=True)
            oh = 0.5 * og_ref[rows, heads[h]]
            y = o * lax.rsqrt(ms + EPS) * gain_ref[...] * (oh + oh * jnp.tanh(oh))
            y_ref[rows, heads[h]] = y.astype(BF16)


def _hgrn(hg5, gamma, gain, s0, T, want_state):
    B = hg5.shape[1]
    has_s0 = s0 is not None

    def col(k):
        return pl.BlockSpec((None, None, T, D_HGRN), lambda b, k=k: (k, b, 0, 0))

    st_spec = pl.BlockSpec((None, None, 2, HG_HEADS, HG_DK, HG_DK), lambda b: (b, 0, 0, 0, 0, 0))
    in_specs = [col(0), col(1), col(2), col(3), col(4),
                pl.BlockSpec((2, 2, D_HGRN), lambda b: (0, 0, 0)),
                pl.BlockSpec((1, LANES), lambda b: (0, 0))]
    args = [hg5, hg5, hg5, hg5, hg5, gamma, gain]
    if has_s0:
        in_specs.append(st_spec)
        args.append(s0)
    out_specs = [pl.BlockSpec((None, T, D_HGRN), lambda b: (b, 0, 0))]
    out_shape = [jax.ShapeDtypeStruct((B, T, D_HGRN), BF16)]
    if want_state:
        out_specs.append(st_spec)
        out_shape.append(jax.ShapeDtypeStruct((B, 1, 2, HG_HEADS, HG_DK, HG_DK), F32))
    res = pl.pallas_call(
        functools.partial(_hgrn_kernel, T, has_s0, want_state),
        grid=(B,),
        in_specs=in_specs,
        out_specs=out_specs,
        out_shape=out_shape,
        scratch_shapes=[pltpu.VMEM((T, D_HGRN), F32), pltpu.VMEM((T, D_HGRN), BF16)]
        + [pltpu.VMEM((T, D_HGRN), F32)] * 6
        + [pltpu.VMEM((2 * HG_HEADS, HG_DK, HG_DK), F32)],
        compiler_params=_cp(("arbitrary",)),
    )(*args)
    return res if want_state else (res[0], None)


def _mix_out_kernel(spr, hs_ref, gr_ref, yh_ref, x_ref, mod_ref, nl_ref, wo_ref, nf_ref, wr_ref,
                    x1_ref, ha_ref, at_ref):
    wr = wr_ref[...]
    wo = wo_ref[...].astype(BF16)
    g1 = _mod_get(mod_ref, spr, 0)
    sh2 = _mod_get(mod_ref, spr, 1)
    gain2 = nf_ref[...] * (1.0 + _mod_get(mod_ref, spr, 2))
    lane = lax.broadcasted_iota(jnp.int32, (TOK_TILE, LANES), 1)
    for s in range(x_ref.shape[0] // TOK_TILE):
        rows = slice(s * TOK_TILE, (s + 1) * TOK_TILE)
        hs = jnp.concatenate([hs_ref[c, rows, :] for c in range(D_LRU // LANES)], axis=1).astype(F32)
        ms = jnp.mean(hs * hs, axis=-1, keepdims=True)
        gr = gr_ref[rows, :].astype(F32)
        gelu = 0.5 * gr * (1.0 + jnp.tanh(0.7978845608028654 * (gr + 0.044715 * (gr * gr * gr))))
        y_lru = hs * lax.rsqrt(ms + EPS) * nl_ref[...] * gelu
        ycat = jnp.concatenate([y_lru.astype(BF16), yh_ref[rows, :]], axis=-1)
        x1 = x_ref[rows, :] + g1 * _dot(ycat, wo)
        x1_ref[rows, :] = x1
        yield
        ms = jnp.mean(x1 * x1, axis=-1, keepdims=True)
        h2 = x1 * lax.rsqrt(ms + EPS) * gain2 + sh2
        h2_hi = h2.astype(BF16)
        h2_mid = (h2 - h2_hi.astype(F32)).astype(BF16)
        p_hi = _dot(h2_hi, wr)
        logits = p_hi[:, 0:LANES] + (p_hi[:, LANES:2 * LANES] + _dot(h2_mid, wr)[:, 0:LANES])
        logits = jnp.where(lane < N_EXPERTS, logits, -jnp.inf)
        e = jnp.exp(logits - jnp.max(logits, axis=-1, keepdims=True))
        aff = e / jnp.sum(e, axis=-1, keepdims=True)
        w = at_ref.shape[1]
        r, c = (s * TOK_TILE) // w, (s * TOK_TILE) % w
        at_ref[r * N_EXPERTS:(r + 1) * N_EXPERTS, c:c + TOK_TILE] = aff.T[0:N_EXPERTS, :]
        hi = aff.astype(BF16).astype(F32)
        r1 = aff - hi
        mid = r1.astype(BF16).astype(F32)
        lo = (r1 - mid).astype(BF16).astype(F32)
        split = hi + pltpu.roll(mid, N_EXPERTS, 1) + pltpu.roll(lo, 2 * N_EXPERTS, 1)
        ha_ref[rows, 0:D_MODEL] = h2_hi
        ha_ref[rows, D_MODEL:D_AUG] = split.astype(BF16)
        yield


def _mix_out_part(hs4, gr, yh, x, mods, msel, norm_lru, w_out, norm_ffn, wr_pad, T, tile_rows):
    n = x.shape[0]
    assert T % TOK_TILE == 0 and (tile_rows % T == 0 or T % tile_rows == 0)
    q = max(T // tile_rows, 1)
    at_rows = N_EXPERTS * max(tile_rows // T, 1)
    at_spec = pl.BlockSpec((at_rows, min(T, tile_rows)), lambda i: (i // q, i % q))
    tile = lambda w: pl.BlockSpec((tile_rows, w), lambda i: (i, 0))
    const = lambda r, w: pl.BlockSpec((r, w), lambda i: (0, 0))
    return dict(
        kernel=functools.partial(_mix_out_kernel, _spr(msel, q)),
        steps=n // tile_rows,
        args=[hs4, gr, yh, x, mods, norm_lru, w_out, norm_ffn, wr_pad],
        in_specs=[pl.BlockSpec((D_LRU // LANES, tile_rows, LANES), lambda i: (0, i, 0)),
                  tile(D_LRU), tile(D_HGRN), tile(D_MODEL),
                  _mod_spec(mods, msel),
                  const(1, D_LRU), const(D_MODEL, D_MODEL), const(1, D_MODEL), const(D_MODEL, 2 * LANES)],
        out_specs=[tile(D_MODEL), tile(D_AUG), at_spec],
        out_shape=[jax.ShapeDtypeStruct((n, D_MODEL), F32),
                   jax.ShapeDtypeStruct((n, D_AUG), BF16),
                   jax.ShapeDtypeStruct((n // T * N_EXPERTS, T), F32)],
        scratch=[],
    )


def _select_kernel(cap, at_ref, slot_ref):
    a = at_ref[...]
    R, T = a.shape
    ones = jnp.ones((T, LANES), BF16)
    su = (lax.broadcasted_iota(jnp.int32, (T, T), 0)
          < lax.broadcasted_iota(jnp.int32, (T, T), 1)).astype(BF16)

    def wide(x):
        return jnp.concatenate([x] * (T // LANES), axis=1)

    def count(mask):
        return _dot(mask.astype(BF16), ones)

    lo = jnp.zeros((R, LANES), F32)
    hi = jnp.full((R, LANES), 2.0, F32)
    for _ in range(SEL_BISECT):
        mid = 0.5 * (lo + hi)
        ge = count(a >= wide(mid)) >= cap
        lo = jnp.where(ge, mid, lo)
        hi = jnp.where(ge, hi, mid)

    def cond(st):
        return (st[0] < T) & (jnp.min(st[2]) < cap)

    def body(st):
        it, cur, n, thr = st
        m = jnp.max(jnp.where(a < wide(cur), a, -1.0), axis=1, keepdims=True)
        m = jnp.broadcast_to(m, (R, LANES))
        c = count(a >= wide(m))
        act = n < cap
        return it + 1, jnp.where(act, m, cur), jnp.where(act, c, n), jnp.where(act, m, thr)

    _, _, _, thr = lax.while_loop(cond, body, (jnp.int32(0), hi, count(a >= wide(hi)), lo))
    thr_w = wide(thr)
    gt = a > thr_w
    eq = a == thr_w
    need = wide(cap - count(gt))
    sel = gt | (eq & (_dot(eq.astype(BF16), su) < need))
    slot_ref[...] = jnp.where(sel, _dot(sel.astype(BF16), su), -1.0).astype(jnp.int32)


def _select(at, T):
    cap = EC_CAPACITY_FACTOR * T // N_EXPERTS
    R = at.shape[0]
    return pl.pallas_call(
        functools.partial(_select_kernel, cap),
        grid=(1,),
        in_specs=[pl.BlockSpec((R, T), lambda i: (0, 0))],
        out_specs=pl.BlockSpec((R, T), lambda i: (0, 0)),
        out_shape=jax.ShapeDtypeStruct((R, T), jnp.int32),
        compiler_params=_cp(("arbitrary",)),
    )(at)


def _onehot(cap, slot_ref, g):
    j = lax.broadcasted_iota(jnp.int32, (cap, slot_ref.shape[1]), 0)
    r0 = g * N_EXPERTS
    return jnp.concatenate([(slot_ref[r0 + e:r0 + e + 1, :] == j).astype(BF16) for e in range(N_EXPERTS)], axis=0)


def _dispatch_kernel(T, cap, slot_ref, ha_ref, xd_ref):
    for g in range(ha_ref.shape[0] // T):
        rows = _dot(_onehot(cap, slot_ref, g), ha_ref[g * T:(g + 1) * T, :])
        for e in range(N_EXPERTS):
            xd_ref[e, g * cap:(g + 1) * cap, :] = rows[e * cap:(e + 1) * cap].astype(BF16)
        yield


def _dispatch_part(slot, ha, T, G):
    cap = EC_CAPACITY_FACTOR * T // N_EXPERTS
    B = ha.shape[0] // T
    return dict(
        kernel=functools.partial(_dispatch_kernel, T, cap),
        steps=B // G,
        args=[slot, ha],
        in_specs=[pl.BlockSpec((G * N_EXPERTS, T), lambda b: (b, 0)),
                  pl.BlockSpec((G * T, D_AUG), lambda b: (b, 0))],
        out_specs=[pl.BlockSpec((N_EXPERTS, G * cap, D_AUG), lambda b: (0, b, 0))],
        out_shape=[jax.ShapeDtypeStruct((N_EXPERTS, B * cap, D_AUG), BF16)],
        scratch=[],
    )


def _ffn_kernel(xp_ref, xs_ref, wg_ref, wu_ref, wd_ref, yp_ref, ys_ref):
    e = pl.program_id(0)
    lane = lax.broadcasted_iota(jnp.int32, (FFN_TILE, LANES), 1)
    pick = (lane == e) | (lane == e + N_EXPERTS) | (lane == e + 2 * N_EXPERTS)
    wg = wg_ref[...].astype(BF16)
    wu = wu_ref[...].astype(BF16)
    wd = wd_ref[...].astype(BF16)
    for x_ref, y_ref in ((xp_ref, yp_ref), (xs_ref, ys_ref)):
        x = x_ref[:, 0:D_MODEL]
        gate = jnp.sum(jnp.where(pick, x_ref[:, D_MODEL:D_AUG].astype(F32), 0.0), axis=-1, keepdims=True)
        hg = _dot(x, wg)
        hid = (hg * _sigmoid(hg)) * _dot(x, wu)
        y_ref[...] = (_dot(hid.astype(BF16), wd) * gate).astype(BF16)


def _ffn(xd_p, xd_s, w_gate, w_up, w_down):
    rows = xd_p.shape[1]
    assert xd_s.shape[1] == rows and rows % FFN_TILE == 0
    xspec = pl.BlockSpec((None, FFN_TILE, D_AUG), lambda e, m: (e, m, 0))
    wspec = pl.BlockSpec((None, None, D_MODEL, D_MODEL), lambda e, m: (0, e, 0, 0))
    yspec = pl.BlockSpec((None, FFN_TILE, D_MODEL), lambda e, m: (e, m, 0))
    yshape = jax.ShapeDtypeStruct((N_EXPERTS, rows, D_MODEL), BF16)
    return pl.pallas_call(
        _ffn_kernel,
        grid=(N_EXPERTS, rows // FFN_TILE),
        in_specs=[xspec, xspec, wspec, wspec, wspec],
        out_specs=[yspec, yspec],
        out_shape=[yshape, yshape],
        compiler_params=_cp(("arbitrary", "arbitrary")),
    )(xd_p, xd_s, w_gate, w_up, w_down)


def _combine_kernel(cap, spr, slot_ref, yd_ref, x1_ref, mod_ref, nf_ref, o_ref):
    W = slot_ref.shape[1]
    g2 = _mod_get(mod_ref, spr, 3)
    for g in range(x1_ref.shape[0] // W):
        rows = slice(g * W, (g + 1) * W)
        yd = jnp.concatenate([yd_ref[e, g * cap:(g + 1) * cap, :] for e in range(N_EXPERTS)], axis=0)
        acc = _dot_tn(_onehot(cap, slot_ref, g), yd)
        x2 = x1_ref[rows, :] + g2 * acc
        ms = jnp.mean(x2 * x2, axis=-1, keepdims=True)
        o_ref[rows, :] = x2 * lax.rsqrt(ms + EPS) * nf_ref[...]
        yield


def _combine_part(slot, yd, x1, mods, msel, norm_final, T, tok):
    cap = EC_CAPACITY_FACTOR * T // N_EXPERTS
    n = x1.shape[0]
    G, h = (tok // T, 1) if tok >= T else (1, T // tok)
    assert G == 1 or not msel[1]
    return dict(
        kernel=functools.partial(_combine_kernel, cap, _spr(msel, h)),
        steps=n // tok,
        args=[slot, yd, x1, mods, norm_final],
        in_specs=[pl.BlockSpec((G * N_EXPERTS, min(T, tok)), lambda i: (i // h, i % h)),
                  pl.BlockSpec((N_EXPERTS, G * cap, D_MODEL), lambda i: (0, i // h, 0)),
                  pl.BlockSpec((tok, D_MODEL), lambda i: (i, 0)),
                  _mod_spec(mods, msel),
                  pl.BlockSpec((1, D_MODEL), lambda i: (0, 0))],
        out_specs=[pl.BlockSpec((tok, D_MODEL), lambda i: (i, 0))],
        out_shape=[jax.ShapeDtypeStruct((n, D_MODEL), F32)],
        scratch=[],
    )


def _gate_tiles(wa, wx, ba, bx):
    n_ct = D_LRU // LANES

    def dense(w):
        w2 = w.reshape(n_ct, 2, LRU_BW, LRU_BW)
        z = jnp.zeros((n_ct, LRU_BW, LRU_BW), w.dtype)
        top = jnp.concatenate([w2[:, 0], z], axis=2)
        bot = jnp.concatenate([z, w2[:, 1]], axis=2)
        return jnp.concatenate([top, bot], axis=1)

    tiles = 0.5 * jnp.concatenate([dense(wa[0]), dense(wx[0]), dense(wa[1]), dense(wx[1])], axis=-1)
    t = lambda v: v.reshape(n_ct, 1, LANES)
    bias = 0.5 * jnp.concatenate([t(ba[0]), t(bx[0]), t(ba[1]), t(bx[1])], axis=-1)
    hi = bias.astype(BF16)
    lo = (bias - hi.astype(F32)).astype(BF16)
    pad = jnp.zeros((n_ct, LANES - 2, 4 * LANES), BF16)
    return jnp.concatenate([tiles.astype(BF16), hi, lo, pad], axis=1)


def _router_split(w):
    wp = jnp.pad(w, ((0, 0), (0, LANES - N_EXPERTS)))
    hi = wp.astype(BF16)
    mid = (wp - hi.astype(F32)).astype(BF16)
    return jnp.concatenate([hi, mid], axis=1)


def _proj(x, mods, msel, p, T, tile):
    return _proj_part(x.reshape(-1, D_MODEL), mods, msel, p["norm_mix"], p["w_in"], T, tile)


def _lru(xr4, B, T, row_len, h0, p, unroll):
    n_ct = D_LRU // LANES
    return _lru_part(xr4.reshape(n_ct, B, T, LANES), p["conv_w"], p["conv_b"], p["gate_w"], p["lam"],
                     h0, T, row_len, unroll)


def _mix(x, mods, msel, p, T, hs4, gr, yh, tile_rows):
    B = x.shape[0]
    return _mix_out_part(hs4.reshape(D_LRU // LANES, B * T, LANES), gr, yh.reshape(B * T, D_HGRN),
                         x.reshape(B * T, D_MODEL), mods, msel,
                         p["norm_lru"], p["w_out"], p["norm_ffn"], p["w_router"], T, tile_rows)


def kernel(x_prompt, x_sample, state_lru, state_hgrn, c, c_ctx, w_ada, b_ada, norm_mix, w_in, conv_w,
           conv_b, lru_wa, lru_ba, lru_wx, lru_bx, lru_lambda, norm_lru, hgrn_gamma, norm_hgrn, w_out,
           norm_ffn, w_router, w_gate, w_up, w_down, norm_final):
    assert w_ada.shape[0] == 1 and hgrn_gamma.shape[1] == 2, "one trunk layer"
    l = 0
    Bp, Tp, _ = x_prompt.shape
    Bs, Ts, _ = x_sample.shape

    assert Bs == MOD_ROWS
    cond = jnp.concatenate([c, c_ctx[None, :], jnp.zeros((MOD_ROWS - 1, D_MODEL), F32)], axis=0)
    n1 = 2 * D_MODEL
    ((m1,),) = _run([_adaln_part(cond, w_ada, b_ada, 0, n1, D_MODEL)])
    sel_p, sel_s = (1, False), (0, True)

    p = {
        "norm_mix": norm_mix[l][None, :], "w_in": w_in,
        "conv_w": conv_w[l], "conv_b": conv_b[l][None, :],
        "gate_w": _gate_tiles(lru_wa[l], lru_wx[l], lru_ba[l], lru_bx[l]),
        "lam": lru_lambda[l], "norm_lru": norm_lru[l][None, :],
        "gamma": hgrn_gamma, "norm_hgrn": norm_hgrn[l][None, :],
        "w_out": w_out[l], "norm_ffn": norm_ffn[l][None, :],
        "w_router": _router_split(w_router[l]),
    }
    zeros_p = jnp.zeros((Bp, 2 * D_LRU), F32)
    proj_p = _proj(x_prompt, m1, sel_p, p, Tp, PROJ_TILE)
    n2 = (N_MOD - 2) * D_MODEL
    (xr4_p, gr_p, hg5_p), (m2,) = _run([proj_p, _adaln_part(cond, w_ada, b_ada, n1, n2, n2 // proj_p["steps"])])
    (xr4_s, gr_s, hg5_s), (hs4_p, last_f, last_b) = _run(
        [_proj(x_sample, m1, sel_s, p, Ts, PROJ_TILE), _lru(xr4_p, Bp, Tp, Tp, zeros_p, p, True)])
    yh_p, st_p = _hgrn(hg5_p.reshape(N_HG_IN, Bp, Tp, D_HGRN), p["gamma"], p["norm_hgrn"], None, Tp, True)
    ((x1p, ha_p, at_p),) = _run([_mix(x_prompt, m2, sel_p, p, Tp, hs4_p, gr_p, yh_p, MIX_TILE)])
    slot_p = _select(at_p, Tp)
    assert state_lru.shape[1] == 1
    ((hs4_s, _, _),) = _run([_lru(xr4_s, Bs, Ts, GRID_W, state_lru.reshape(Bs, 2 * D_LRU), p, 8)])
    yh_s, _ = _hgrn(hg5_s.reshape(N_HG_IN, Bs, Ts, D_HGRN), p["gamma"], p["norm_hgrn"], state_hgrn, Ts, False)
    fused_tile = MIX_TILE
    (x1s, ha_s, at_s), (xd_p,) = _run(
        [_mix(x_sample, m2, sel_s, p, Ts, hs4_s, gr_s, yh_s, fused_tile),
         _dispatch_part(slot_p, ha_p, Tp, fused_tile // Tp)])
    slot_s = _select(at_s, Ts)
    ((xd_s,),) = _run([_dispatch_part(slot_s, ha_s, Ts, max(ROUTE_TOKENS // Ts, 1))])

    yd_p, yd_s = _ffn(xd_p, xd_s, w_gate, w_up, w_down)
    nf = norm_final[None, :]
    (y_sample,), (y_prompt,) = _run(
        [_combine_part(slot_s, yd_s, x1s, m2, sel_s, nf, Ts, COMBINE_TOKENS),
         _combine_part(slot_p, yd_p, x1p, m2, sel_p, nf, Tp, COMBINE_TOKENS)])
    y_prompt = y_prompt.reshape(Bp, Tp, D_MODEL)
    y_sample = y_sample.reshape(Bs, Ts, D_MODEL)
    new_state_lru = jnp.stack([last_f, last_b], axis=1)[:, None]
    new_state_hgrn = st_p
    return (y_prompt, y_sample, new_state_lru, new_state_hgrn)
```

```python
import functools

import jax
import jax.numpy as jnp
from jax import lax
from jax.experimental import pallas as pl
from jax.experimental.pallas import tpu as pltpu

F32 = jnp.float32
BF16 = jnp.bfloat16

D_MODEL = 1024
D_LRU = 512
D_HGRN = 512
HG_HEADS = 4
HG_DK = 128
LRU_BLOCKS = 8
LRU_BW = 64
LRU_C = 8.0
N_EXPERTS = 16
EC_CAPACITY_FACTOR = 2
N_MOD = 6
D_IN = 7 * 512
N_HG_IN = 5
GRID_W = 64
EPS = 1e-6
LOG2E = 1.4426950408889634
TINY = 1e-37

LANES = 128
MOD_ROWS = 8
TOK_TILE = 256
PROJ_TILE = 512
MIX_TILE = 1024
LRU_GROUP = 8
PITCH_PAD = 8
HG_CHUNK = 64
HG_UNROLL = 16
HG_ROWS = 256
FFN_TILE = 1024
ROUTE_TOKENS = 1024
COMBINE_TOKENS = 1024
SEL_BISECT = 20
D_AUG = D_MODEL + LANES
VMEM_LIMIT = 60 * 1024 * 1024


def _cp(sem, vmem=VMEM_LIMIT):
    return pltpu.CompilerParams(dimension_semantics=sem, vmem_limit_bytes=vmem)


def _sigmoid(x):
    return 0.5 * (1.0 + jnp.tanh(0.5 * x))


def _dot(a, b):
    return jnp.dot(a, b, preferred_element_type=F32)


def _dot_nt(a, b):
    return lax.dot_general(a, b, (((1,), (1,)), ((), ())), preferred_element_type=F32)


def _dot_tn(a, b):
    return lax.dot_general(a, b, (((0,), (0,)), ((), ())), preferred_element_type=F32)


def _adaln_kernel(c_ref, w_ref, b_ref, o_ref):
    c = c_ref[...]
    s = (c * _sigmoid(c)).astype(BF16)
    o_ref[...] = _dot(s, w_ref[...].astype(BF16)) + b_ref[...]
    yield


def _adaln_part(cond, w, b, col0, ncols, tn):
    n = cond.shape[0]
    j0 = col0 // tn
    return dict(
        kernel=_adaln_kernel,
        steps=ncols // tn,
        args=[cond, w, b],
        in_specs=[pl.BlockSpec((n, D_MODEL), lambda j: (0, 0)),
                  pl.BlockSpec((None, D_MODEL, tn), lambda j: (0, 0, j0 + j)),
                  pl.BlockSpec((1, tn), lambda j: (0, j0 + j))],
        out_specs=[pl.BlockSpec((n, tn), lambda j: (0, j))],
        out_shape=[jax.ShapeDtypeStruct((n, ncols), F32)],
        scratch=[],
    )


def _mod_get(mod_ref, spr, j):
    cols = slice(j * D_MODEL, (j + 1) * D_MODEL)
    if spr is None:
        return mod_ref[0:1, cols]
    return mod_ref[pl.ds(pl.program_id(0) // spr, 1), cols]


def _proj_kernel(spr, x_ref, mod_ref, g_ref, w_ref, xr_ref, gr_ref, hg_ref):
    x = x_ref[...]
    ms = jnp.mean(x * x, axis=-1, keepdims=True)
    y = x * lax.rsqrt(ms + EPS) * g_ref[...]
    h = y * (1.0 + _mod_get(mod_ref, spr, 1)) + _mod_get(mod_ref, spr, 0)
    h = h.astype(BF16)
    W = 2 * LANES
    for j in range(D_IN // W):
        res = _dot(h, w_ref[:, j * W:(j + 1) * W].astype(BF16))
        k, half = divmod(j * W, D_HGRN)
        if k == 0:
            for c in range(W // LANES):
                xr_ref[half // LANES + c] = res[:, c * LANES:(c + 1) * LANES]
        elif k == 1:
            gr_ref[:, half:half + W] = res.astype(BF16)
        else:
            hg_ref[k - 2, :, half:half + W] = res
        yield


def _run(parts):
    steps = parts[0]["steps"]
    assert all(p["steps"] == steps for p in parts)
    n_in = [len(p["args"]) for p in parts]
    n_out = [len(p["out_shape"]) for p in parts]
    n_sc = [len(p["scratch"]) for p in parts]

    def body(*refs):
        ins, outs, scr = refs[:sum(n_in)], refs[sum(n_in):sum(n_in) + sum(n_out)], refs[sum(n_in) + sum(n_out):]
        i = o = s = 0
        gens = []
        for p, a, b, c in zip(parts, n_in, n_out, n_sc):
            gens.append(p["kernel"](*ins[i:i + a], *outs[o:o + b], *scr[s:s + c]))
            i, o, s = i + a, o + b, s + c
        while gens:
            for g in list(gens):
                if next(g, StopIteration) is StopIteration:
                    gens.remove(g)

    res = pl.pallas_call(
        body,
        grid=(steps,),
        in_specs=[s for p in parts for s in p["in_specs"]],
        out_specs=[s for p in parts for s in p["out_specs"]],
        out_shape=[s for p in parts for s in p["out_shape"]],
        scratch_shapes=[s for p in parts for s in p["scratch"]],
        compiler_params=_cp(("arbitrary",)),
    )(*[a for p in parts for a in p["args"]])
    out, o = [], 0
    for b in n_out:
        out.append(res[o:o + b])
        o += b
    return out


def _mod_spec(mods, msel):
    blk = msel[0]
    return pl.BlockSpec((MOD_ROWS, mods.shape[1]), lambda i: (blk, 0))


def _spr(msel, steps_per_request):
    if not msel[1]:
        return None
    assert steps_per_request >= 1
    return steps_per_request


def _proj_part(x, mods, msel, gain, w, T, tile):
    n = x.shape[0]
    assert T % tile == 0 or (tile % T == 0 and not msel[1])
    n_ct = D_LRU // LANES
    return dict(
        kernel=functools.partial(_proj_kernel, _spr(msel, T // tile)),
        steps=n // tile,
        args=[x, mods, gain, w],
        in_specs=[pl.BlockSpec((tile, D_MODEL), lambda i: (i, 0)),
                  _mod_spec(mods, msel),
                  pl.BlockSpec((1, D_MODEL), lambda i: (0, 0)),
                  pl.BlockSpec((None, D_MODEL, D_IN), lambda i: (0, 0, 0), pipeline_mode=pl.Buffered(1))],
        out_specs=[pl.BlockSpec((n_ct, tile, LANES), lambda i: (0, i, 0)),
                   pl.BlockSpec((tile, D_LRU), lambda i: (i, 0)),
                   pl.BlockSpec((N_HG_IN, tile, D_HGRN), lambda i: (0, i, 0))],
        out_shape=[jax.ShapeDtypeStruct((n_ct, n, LANES), F32),
                   jax.ShapeDtypeStruct((n, D_LRU), BF16),
                   jax.ShapeDtypeStruct((N_HG_IN, n, D_HGRN), F32)],
        scratch=[],
    )


def _lru_kernel(T, row_len, unroll, xr_ref, cw_ref, cb_ref, w_ref, lam_ref, h0f_ref, h0b_ref,
                hs_ref, lf_ref, lb_ref, xpad, af, uf, ab, ub, hf_s, hb_s):
    pitch = T + PITCH_PAD
    pos = lax.broadcasted_iota(jnp.int32, (T, LANES), 0) & (row_len - 1)
    cw = cw_ref[...]
    w0 = jnp.where(pos >= 2, cw[0:1], 0.0)
    w1 = jnp.where(pos >= 1, cw[1:2], 0.0)
    w2 = jnp.broadcast_to(cw[2:3], (T, LANES))
    w3 = jnp.where(pos <= row_len - 2, cw[3:4], 0.0)
    cb = cb_ref[...]
    nl = -lam_ref[...]
    sp = jnp.maximum(nl, 0.0) + jnp.log1p(jnp.exp(-jnp.abs(nl)))
    c2 = (-0.5 * LRU_C * LOG2E) * sp
    w = w_ref[0]
    ones = (lax.broadcasted_iota(jnp.int32, (T, LANES), 1) < 2).astype(BF16)
    for s in range(2):
        xpad[s, 0:8, :] = jnp.zeros((8, LANES), F32)
        xpad[s, T + 8:T + 16, :] = jnp.zeros((8, LANES), F32)
    for b in range(LRU_GROUP):
        x = xr_ref[b]
        xp = xpad.at[b % 2]
        xp[8:T + 8, :] = x
        xc = w0 * xp[6:T + 6, :] + w1 * xp[7:T + 7, :] + w2 * x + w3 * xp[9:T + 9, :] + cb
        xh = 0.5 * xc
        z = _dot(jnp.concatenate([xc.astype(BF16), ones], axis=1), w)
        rows = pl.ds(b * pitch, T)
        for d, (a_s, u_s) in enumerate(((af, uf), (ab, ub))):
            tr = jnp.tanh(z[:, (2 * d) * LANES:(2 * d + 1) * LANES])
            ti = jnp.tanh(z[:, (2 * d + 1) * LANES:(2 * d + 2) * LANES])
            a = jnp.exp2(c2[d:d + 1] + c2[d:d + 1] * tr)
            om = 1.0 - a * a
            a_s[rows, :] = a
            u_s[rows, :] = (om * lax.rsqrt(jnp.maximum(om, TINY))) * (xh + ti * xh)
        yield

    def step(t, carry):
        hf, hb = carry
        rf = pl.ds(t, LRU_GROUP, stride=pitch)
        hf = af[rf, :] * hf + uf[rf, :]
        hf_s[rf, :] = hf
        rb = pl.ds(T - 1 - t, LRU_GROUP, stride=pitch)
        hb = ab[rb, :] * hb + ub[rb, :]
        hb_s[rb, :] = hb
        return hf, hb

    carry = (h0f_ref[...], h0b_ref[...])
    if unroll is True:
        for t in range(T):
            carry = step(t, carry)
            if t % (T // 8) == T // 8 - 1 and t != T - 1:
                yield
    else:
        carry = lax.fori_loop(0, T, step, carry, unroll=unroll)
    hf, hb = carry
    lf_ref[...] = hf
    lb_ref[...] = hb
    for b in range(LRU_GROUP):
        rows = pl.ds(b * pitch, T)
        hs_ref[b] = (hf_s[rows, :] + hb_s[rows, :]).astype(BF16)


def _lru_part(xr4, conv_w, conv_b, w_tiles, lam, h0, T, row_len, unroll):
    n_ct, B = xr4.shape[:2]
    rows = LRU_GROUP * (T + PITCH_PAD)
    vec = pl.BlockSpec((LRU_GROUP, LANES), lambda i: (i // n_ct, i % n_ct))
    vec_b = pl.BlockSpec((LRU_GROUP, LANES), lambda i: (i // n_ct, n_ct + i % n_ct))
    slab = pl.BlockSpec((None, LRU_GROUP, T, LANES), lambda i: (i % n_ct, i // n_ct, 0, 0))
    return dict(
        kernel=functools.partial(_lru_kernel, T, row_len, unroll),
        steps=(B // LRU_GROUP) * n_ct,
        args=[xr4, conv_w, conv_b, w_tiles, lam, h0, h0],
        in_specs=[slab,
                  pl.BlockSpec((4, LANES), lambda i: (0, i % n_ct)),
                  pl.BlockSpec((1, LANES), lambda i: (0, i % n_ct)),
                  pl.BlockSpec((1, 2 * LANES, 4 * LANES), lambda i: (i % n_ct, 0, 0)),
                  pl.BlockSpec((2, LANES), lambda i: (0, i % n_ct)),
                  vec, vec_b],
        out_specs=[slab, vec, vec],
        out_shape=[jax.ShapeDtypeStruct((n_ct, B, T, LANES), BF16),
                   jax.ShapeDtypeStruct((B, D_LRU), F32),
                   jax.ShapeDtypeStruct((B, D_LRU), F32)],
        scratch=[pltpu.VMEM((2, T + 16, LANES), F32)] + [pltpu.VMEM((rows, LANES), F32)] * 6,
    )


def _hgrn_kernel(T, has_s0, want_state, *refs):
    q_ref, ff_ref, fb_ref, v_ref, og_ref, gam_ref, gain_ref = refs[:7]
    refs = refs[7:]
    if has_s0:
        s0_ref, refs = refs[0], refs[1:]
    y_ref, refs = refs[0], refs[1:]
    if want_state:
        st_ref, refs = refs[0], refs[1:]
    q_s, v_s, kf_s, kb_s, bf_s, bb_s, of_s, ob_s, st_s = refs
    C = HG_CHUNK
    n = T // C
    RB = HG_ROWS
    heads = [slice(h * LANES, (h + 1) * LANES) for h in range(HG_HEADS)]

    ti = lax.broadcasted_iota(jnp.int32, (RB, RB), 0)
    si = lax.broadcasted_iota(jnp.int32, (RB, RB), 1)
    sh = C.bit_length() - 1
    same = (ti >> sh) == (si >> sh)
    tri = ((same & (si <= ti)).astype(BF16), (same & (si >= ti)).astype(BF16))

    for r in range(T // RB):
        rows = slice(r * RB, (r + 1) * RB)
        qr = q_ref[rows, :]
        qh = (0.5 * HG_DK ** -0.5) * qr
        q_s[rows, :] = qh + qh * jnp.tanh(0.5 * qr)
        v_s[rows, :] = v_ref[rows, :].astype(BF16)
        for d, (f_ref, k_s, b_s) in enumerate(((ff_ref, kf_s, bf_s), (fb_ref, kb_s, bb_s))):
            g0 = gam_ref[d, 0:1, :]
            g1 = gam_ref[d, 1:2, :]
            m = jnp.maximum(g0, g1)
            e0 = jnp.exp(g0 - m)
            lbd = e0 / (e0 + jnp.exp(g1 - m))
            ck = 0.5 * (1.0 - lbd)
            pt = ck * jnp.tanh(0.5 * f_ref[rows, :])
            k_s[rows, :] = ck - pt
            lf = jnp.log((lbd + ck) + pt)
            hi = lf.astype(BF16)
            lo = (lf - hi.astype(F32)).astype(BF16)
            cs = _dot(tri[d], jnp.concatenate([hi, lo], axis=1))
            b_s[rows, :] = cs[:, 0:D_HGRN] + cs[:, D_HGRN:2 * D_HGRN]

    for d in range(2):
        for h in range(HG_HEADS):
            st_s[d * HG_HEADS + h] = s0_ref[d, h].T if has_s0 else jnp.zeros((LANES, LANES), F32)

    ri =lax.broadcasted_iota(jnp.int32, (C, C), 0)
    ci = lax.broadcasted_iota(jnp.int32, (C, C), 1)

    def body(i, carry):
        for d, (k_s, b_s, o_s) in enumerate(((kf_s, bf_s, of_s), (kb_s, bb_s, ob_s))):
            c = i if d == 0 else n - 1 - i
            sl = pl.ds(pl.multiple_of(c * C, C), C)
            for h in range(HG_HEADS):
                q = q_s[sl, heads[h]]
                k = k_s[sl, heads[h]]
                b = b_s[sl, heads[h]]
                v = v_s[sl, heads[h]]
                bm = b[C // 2:C // 2 + 1, :]
                g = b[C - 1:C, :] if d == 0 else b[0:1, :]
                qi = q * jnp.exp(b - bm)
                qe = (qi * jnp.exp(bm)).astype(BF16)
                ki = k * jnp.exp(bm - b)
                kd = (ki * jnp.exp(g - bm)).astype(BF16)
                s = _dot_nt(qi.astype(BF16), ki.astype(BF16))
                p = jnp.where((ci <= ri) if d == 0 else (ci >= ri), s, 0.0).astype(BF16)
                st = st_s[d * HG_HEADS + h]
                o_s[sl, heads[h]] = _dot(p, v) + _dot(qe, st.T.astype(BF16))
                st_s[d * HG_HEADS + h] = st * jnp.exp(g) + _dot_tn(v, kd)
        return carry

    lax.fori_loop(0, n, body, 0, unroll=HG_UNROLL)
    if want_state:
        for d in range(2):
            for h in range(HG_HEADS):
                st_ref[d, h] = st_s[d * HG_HEADS + h].T

    for r in range(T // RB):
        rows = slice(r * RB, (r + 1) * RB)
        for h in range(HG_HEADS):
            o = of_s[rows, heads[h]] + ob_s[rows, heads[h]]
            ms = jnp.mean(o * o, axis=-1, keepdims=True)
            oh = 0.5 * og_ref[rows, heads[h]]
            y = o * lax.rsqrt(ms + EPS) * gain_ref[...] * (oh + oh * jnp.tanh(oh))
            y_ref[rows, heads[h]] = y.astype(BF16)


def _hgrn(hg5, gamma, gain, s0, T, want_state):
    B = hg5.shape[1]
    has_s0 = s0 is not None

    def col(k):
        return pl.BlockSpec((None, None, T, D_HGRN), lambda b, k=k: (k, b, 0, 0))

    st_spec = pl.BlockSpec((None, None, 2, HG_HEADS, HG_DK, HG_DK), lambda b: (b, 0, 0, 0, 0, 0))
    in_specs = [col(0), col(1), col(2), col(3), col(4),
                pl.BlockSpec((2, 2, D_HGRN), lambda b: (0, 0, 0)),
                pl.BlockSpec((1, LANES), lambda b: (0, 0))]
    args = [hg5, hg5, hg5, hg5, hg5, gamma, gain]
    if has_s0:
        in_specs.append(st_spec)
        args.append(s0)
    out_specs = [pl.BlockSpec((None, T, D_HGRN), lambda b: (b, 0, 0))]
    out_shape = [jax.ShapeDtypeStruct((B, T, D_HGRN), BF16)]
    if want_state:
        out_specs.append(st_spec)
        out_shape.append(jax.ShapeDtypeStruct((B, 1, 2, HG_HEADS, HG_DK, HG_DK), F32))
    res = pl.pallas_call(
        functools.partial(_hgrn_kernel, T, has_s0, want_state),
        grid=(B,),
        in_specs=in_specs,
        out_specs=out_specs,
        out_shape=out_shape,
        scratch_shapes=[pltpu.VMEM((T, D_HGRN), F32), pltpu.VMEM((T, D_HGRN), BF16)]
        + [pltpu.VMEM((T, D_HGRN), F32)] * 6
        + [pltpu.VMEM((2 * HG_HEADS, HG_DK, HG_DK), F32)],
        compiler_params=_cp(("arbitrary",)),
    )(*args)
    return res if want_state else (res[0], None)


def _mix_out_kernel(spr, hs_ref, gr_ref, yh_ref, x_ref, mod_ref, nl_ref, wo_ref, nf_ref, wr_ref,
                    x1_ref, ha_ref, at_ref):
    wr = wr_ref[...]
    wo = wo_ref[...].astype(BF16)
    g1 = _mod_get(mod_ref, spr, 0)
    sh2 = _mod_get(mod_ref, spr, 1)
    gain2 = nf_ref[...] * (1.0 + _mod_get(mod_ref, spr, 2))
    lane = lax.broadcasted_iota(jnp.int32, (TOK_TILE, LANES), 1)
    for s in range(x_ref.shape[0] // TOK_TILE):
        rows = slice(s * TOK_TILE, (s + 1) * TOK_TILE)
        hs = jnp.concatenate([hs_ref[c, rows, :] for c in range(D_LRU // LANES)], axis=1).astype(F32)
        ms = jnp.mean(hs * hs, axis=-1, keepdims=True)
        gr = gr_ref[rows, :].astype(F32)
        gelu = 0.5 * gr * (1.0 + jnp.tanh(0.7978845608028654 * (gr + 0.044715 * (gr * gr * gr))))
        y_lru = hs * lax.rsqrt(ms + EPS) * nl_ref[...] * gelu
        ycat = jnp.concatenate([y_lru.astype(BF16), yh_ref[rows, :]], axis=-1)
        x1 = x_ref[rows, :] + g1 * _dot(ycat, wo)
        x1_ref[rows, :] = x1
        yield
        ms = jnp.mean(x1 * x1, axis=-1, keepdims=True)
        h2 = x1 * lax.rsqrt(ms + EPS) * gain2 + sh2
        h2_hi = h2.astype(BF16)
        h2_mid = (h2 - h2_hi.astype(F32)).astype(BF16)
        p_hi = _dot(h2_hi, wr)
        logits = p_hi[:, 0:LANES] + (p_hi[:, LANES:2 * LANES] + _dot(h2_mid, wr)[:, 0:LANES])
        logits = jnp.where(lane < N_EXPERTS, logits, -jnp.inf)
        e = jnp.exp(logits - jnp.max(logits, axis=-1, keepdims=True))
        aff = e / jnp.sum(e, axis=-1, keepdims=True)
        w = at_ref.shape[1]
        r, c = (s * TOK_TILE) // w, (s * TOK_TILE) % w
        at_ref[r * N_EXPERTS:(r + 1) * N_EXPERTS, c:c + TOK_TILE] = aff.T[0:N_EXPERTS, :]
        hi = aff.astype(BF16).astype(F32)
        r1 = aff - hi
        mid = r1.astype(BF16).astype(F32)
        lo = (r1 - mid).astype(BF16).astype(F32)
        split = hi + pltpu.roll(mid, N_EXPERTS, 1) + pltpu.roll(lo, 2 * N_EXPERTS, 1)
        ha_ref[rows, 0:D_MODEL] = h2_hi
        ha_ref[rows, D_MODEL:D_AUG] = split.astype(BF16)
        yield


def _mix_out_part(hs4, gr, yh, x, mods, msel, norm_lru, w_out, norm_ffn, wr_pad, T, tile_rows):
    n = x.shape[0]
    assert T % TOK_TILE == 0 and (tile_rows % T == 0 or T % tile_rows == 0)
    q = max(T // tile_rows, 1)
    at_rows = N_EXPERTS * max(tile_rows // T, 1)
    at_spec = pl.BlockSpec((at_rows, min(T, tile_rows)), lambda i: (i // q, i % q))
    tile = lambda w: pl.BlockSpec((tile_rows, w), lambda i: (i, 0))
    const = lambda r, w: pl.BlockSpec((r, w), lambda i: (0, 0))
    return dict(
        kernel=functools.partial(_mix_out_kernel, _spr(msel, q)),
        steps=n // tile_rows,
        args=[hs4, gr, yh, x, mods, norm_lru, w_out, norm_ffn, wr_pad],
        in_specs=[pl.BlockSpec((D_LRU // LANES, tile_rows, LANES), lambda i: (0, i, 0)),
                  tile(D_LRU), tile(D_HGRN), tile(D_MODEL),
                  _mod_spec(mods, msel),
                  const(1, D_LRU), const(D_MODEL, D_MODEL), const(1, D_MODEL), const(D_MODEL, 2 * LANES)],
        out_specs=[tile(D_MODEL), tile(D_AUG), at_spec],
        out_shape=[jax.ShapeDtypeStruct((n, D_MODEL), F32),
                   jax.ShapeDtypeStruct((n, D_AUG), BF16),
                   jax.ShapeDtypeStruct((n // T * N_EXPERTS, T), F32)],
        scratch=[],
    )


def _select_kernel(cap, at_ref, slot_ref):
    a = at_ref[...]
    R, T = a.shape
    ones = jnp.ones((T, LANES), BF16)
    su = (lax.broadcasted_iota(jnp.int32, (T, T), 0)
          < lax.broadcasted_iota(jnp.int32, (T, T), 1)).astype(BF16)

    def wide(x):
        return jnp.concatenate([x] * (T // LANES), axis=1)

    def count(mask):
        return _dot(mask.astype(BF16), ones)

    lo = jnp.zeros((R, LANES), F32)
    hi = jnp.full((R, LANES), 2.0, F32)
    for _ in range(SEL_BISECT):
        mid = 0.5 * (lo + hi)
        ge = count(a >= wide(mid)) >= cap
        lo = jnp.where(ge, mid, lo)
        hi = jnp.where(ge, hi, mid)

    def cond(st):
        return (st[0] < T) & (jnp.min(st[2]) < cap)

    def body(st):
        it, cur, n, thr = st
        m = jnp.max(jnp.where(a < wide(cur), a, -1.0), axis=1, keepdims=True)
        m = jnp.broadcast_to(m, (R, LANES))
        c = count(a >= wide(m))
        act = n < cap
        return it + 1, jnp.where(act, m, cur), jnp.where(act, c, n), jnp.where(act, m, thr)

    _, _, _, thr = lax.while_loop(cond, body, (jnp.int32(0), hi, count(a >= wide(hi)), lo))
    thr_w = wide(thr)
    gt = a > thr_w
    eq = a == thr_w
    need = wide(cap - count(gt))
    sel = gt | (eq & (_dot(eq.astype(BF16), su) < need))
    slot_ref[...] = jnp.where(sel, _dot(sel.astype(BF16), su), -1.0).astype(jnp.int32)


def _select(at, T):
    cap = EC_CAPACITY_FACTOR * T // N_EXPERTS
    R = at.shape[0]
    return pl.pallas_call(
        functools.partial(_select_kernel, cap),
        grid=(1,),
        in_specs=[pl.BlockSpec((R, T), lambda i: (0, 0))],
        out_specs=pl.BlockSpec((R, T), lambda i: (0, 0)),
        out_shape=jax.ShapeDtypeStruct((R, T), jnp.int32),
        compiler_params=_cp(("arbitrary",)),
    )(at)


def _onehot(cap, slot_ref, g):
    j = lax.broadcasted_iota(jnp.int32, (cap, slot_ref.shape[1]), 0)
    r0 = g * N_EXPERTS
    return jnp.concatenate([(slot_ref[r0 + e:r0 + e + 1, :] == j).astype(BF16) for e in range(N_EXPERTS)], axis=0)


def _dispatch_kernel(T, cap, slot_ref, ha_ref, xd_ref):
    for g in range(ha_ref.shape[0] // T):
        rows = _dot(_onehot(cap, slot_ref, g), ha_ref[g * T:(g + 1) * T, :])
        for e in range(N_EXPERTS):
            xd_ref[e, g * cap:(g + 1) * cap, :] = rows[e * cap:(e + 1) * cap].astype(BF16)
        yield


def _dispatch_part(slot, ha, T, G):
    cap = EC_CAPACITY_FACTOR * T // N_EXPERTS
    B = ha.shape[0] // T
    return dict(
        kernel=functools.partial(_dispatch_kernel, T, cap),
        steps=B // G,
        args=[slot, ha],
        in_specs=[pl.BlockSpec((G * N_EXPERTS, T), lambda b: (b, 0)),
                  pl.BlockSpec((G * T, D_AUG), lambda b: (b, 0))],
        out_specs=[pl.BlockSpec((N_EXPERTS, G * cap, D_AUG), lambda b: (0, b, 0))],
        out_shape=[jax.ShapeDtypeStruct((N_EXPERTS, B * cap, D_AUG), BF16)],
        scratch=[],
    )


def _ffn_kernel(xp_ref, xs_ref, wg_ref, wu_ref, wd_ref, yp_ref, ys_ref):
    e = pl.program_id(0)
    lane = lax.broadcasted_iota(jnp.int32, (FFN_TILE, LANES), 1)
    pick = (lane == e) | (lane == e + N_EXPERTS) | (lane == e + 2 * N_EXPERTS)
    wg = wg_ref[...].astype(BF16)
    wu = wu_ref[...].astype(BF16)
    wd = wd_ref[...].astype(BF16)
    for x_ref, y_ref in ((xp_ref, yp_ref), (xs_ref, ys_ref)):
        x = x_ref[:, 0:D_MODEL]
        gate = jnp.sum(jnp.where(pick, x_ref[:, D_MODEL:D_AUG].astype(F32), 0.0), axis=-1, keepdims=True)
        hg = _dot(x, wg)
        hid = (hg * _sigmoid(hg)) * _dot(x, wu)
        y_ref[...] = (_dot(hid.astype(BF16), wd) * gate).astype(BF16)


def _ffn(xd_p, xd_s, w_gate, w_up, w_down):
    rows = xd_p.shape[1]
    assert xd_s.shape[1] == rows and rows % FFN_TILE == 0
    xspec = pl.BlockSpec((None, FFN_TILE, D_AUG), lambda e, m: (e, m, 0))
    wspec = pl.BlockSpec((None, None, D_MODEL, D_MODEL), lambda e, m: (0, e, 0, 0))
    yspec = pl.BlockSpec((None, FFN_TILE, D_MODEL), lambda e, m: (e, m, 0))
    yshape = jax.ShapeDtypeStruct((N_EXPERTS, rows, D_MODEL), BF16)
    return pl.pallas_call(
        _ffn_kernel,
        grid=(N_EXPERTS, rows // FFN_TILE),
        in_specs=[xspec, xspec, wspec, wspec, wspec],
        out_specs=[yspec, yspec],
        out_shape=[yshape, yshape],
        compiler_params=_cp(("arbitrary", "arbitrary")),
    )(xd_p, xd_s, w_gate, w_up, w_down)


def _combine_kernel(cap, spr, slot_ref, yd_ref, x1_ref, mod_ref, nf_ref, o_ref):
    W = slot_ref.shape[1]
    g2 = _mod_get(mod_ref, spr, 3)
    for g in range(x1_ref.shape[0] // W):
        rows = slice(g * W, (g + 1) * W)
        yd = jnp.concatenate([yd_ref[e, g * cap:(g + 1) * cap, :] for e in range(N_EXPERTS)], axis=0)
        acc = _dot_tn(_onehot(cap, slot_ref, g), yd)
        x2 = x1_ref[rows, :] + g2 * acc
        ms = jnp.mean(x2 * x2, axis=-1, keepdims=True)
        o_ref[rows, :] = x2 * lax.rsqrt(ms + EPS) * nf_ref[...]
        yield


def _combine_part(slot, yd, x1, mods, msel, norm_final, T, tok):
    cap = EC_CAPACITY_FACTOR * T // N_EXPERTS
    n = x1.shape[0]
    G, h = (tok // T, 1) if tok >= T else (1, T // tok)
    assert G == 1 or not msel[1]
    return dict(
        kernel=functools.partial(_combine_kernel, cap, _spr(msel, h)),
        steps=n // tok,
        args=[slot, yd, x1, mods, norm_final],
        in_specs=[pl.BlockSpec((G * N_EXPERTS, min(T, tok)), lambda i: (i // h, i % h)),
                  pl.BlockSpec((N_EXPERTS, G * cap, D_MODEL), lambda i: (0, i // h, 0)),
                  pl.BlockSpec((tok, D_MODEL), lambda i: (i, 0)),
                  _mod_spec(mods, msel),
                  pl.BlockSpec((1, D_MODEL), lambda i: (0, 0))],
        out_specs=[pl.BlockSpec((tok, D_MODEL), lambda i: (i, 0))],
        out_shape=[jax.ShapeDtypeStruct((n, D_MODEL), F32)],
        scratch=[],
    )


def _gate_tiles(wa, wx, ba, bx):
    n_ct = D_LRU // LANES

    def dense(w):
        w2 = w.reshape(n_ct, 2, LRU_BW, LRU_BW)
        z = jnp.zeros((n_ct, LRU_BW, LRU_BW), w.dtype)
        top = jnp.concatenate([w2[:, 0], z], axis=2)
        bot = jnp.concatenate([z, w2[:, 1]], axis=2)
        return jnp.concatenate([top, bot], axis=1)

    tiles = 0.5 * jnp.concatenate([dense(wa[0]), dense(wx[0]), dense(wa[1]), dense(wx[1])], axis=-1)
    t = lambda v: v.reshape(n_ct, 1, LANES)
    bias = 0.5 * jnp.concatenate([t(ba[0]), t(bx[0]), t(ba[1]), t(bx[1])], axis=-1)
    hi = bias.astype(BF16)
    lo = (bias - hi.astype(F32)).astype(BF16)
    pad = jnp.zeros((n_ct, LANES - 2, 4 * LANES), BF16)
    return jnp.concatenate([tiles.astype(BF16), hi, lo, pad], axis=1)


def _router_split(w):
    wp = jnp.pad(w, ((0, 0), (0, LANES - N_EXPERTS)))
    hi = wp.astype(BF16)
    mid = (wp - hi.astype(F32)).astype(BF16)
    return jnp.concatenate([hi, mid], axis=1)


def _proj(x, mods, msel, p, T, tile):
    return _proj_part(x.reshape(-1, D_MODEL), mods, msel, p["norm_mix"], p["w_in"], T, tile)


def _lru(xr4, B, T, row_len, h0, p, unroll):
    n_ct = D_LRU // LANES
    return _lru_part(xr4.reshape(n_ct, B, T, LANES), p["conv_w"], p["conv_b"], p["gate_w"], p["lam"],
                     h0, T, row_len, unroll)


def _mix(x, mods, msel, p, T, hs4, gr, yh, tile_rows):
    B = x.shape[0]
    return _mix_out_part(hs4.reshape(D_LRU // LANES, B * T, LANES), gr, yh.reshape(B * T, D_HGRN),
                         x.reshape(B * T, D_MODEL), mods, msel,
                         p["norm_lru"], p["w_out"], p["norm_ffn"], p["w_router"], T, tile_rows)


def kernel(x_prompt, x_sample, state_lru, state_hgrn, c, c_ctx, w_ada, b_ada, norm_mix, w_in, conv_w,
           conv_b, lru_wa, lru_ba, lru_wx, lru_bx, lru_lambda, norm_lru, hgrn_gamma, norm_hgrn, w_out,
           norm_ffn, w_router, w_gate, w_up, w_down, norm_final):
    assert w_ada.shape[0] == 1 and hgrn_gamma.shape[1] == 2, "one trunk layer"
    l = 0
    Bp, Tp, _ = x_prompt.shape
    Bs, Ts, _ = x_sample.shape

    assert Bs == MOD_ROWS
    cond = jnp.concatenate([c, c_ctx[None, :], jnp.zeros((MOD_ROWS - 1, D_MODEL), F32)], axis=0)
    n1 = 2 * D_MODEL
    ((m1,),) = _run([_adaln_part(cond, w_ada, b_ada, 0, n1, D_MODEL)])
    sel_p, sel_s = (1, False), (0, True)

    p = {
        "norm_mix": norm_mix[l][None, :], "w_in": w_in,
        "conv_w": conv_w[l], "conv_b": conv_b[l][None, :],
        "gate_w": _gate_tiles(lru_wa[l], lru_wx[l], lru_ba[l], lru_bx[l]),
        "lam": lru_lambda[l], "norm_lru": norm_lru[l][None, :],
        "gamma": hgrn_gamma, "norm_hgrn": norm_hgrn[l][None, :],
        "w_out": w_out[l], "norm_ffn": norm_ffn[l][None, :],
        "w_router": _router_split(w_router[l]),
    }
    zeros_p = jnp.zeros((Bp, 2 * D_LRU), F32)
    proj_p = _proj(x_prompt, m1, sel_p, p, Tp, PROJ_TILE)
    n2 = (N_MOD - 2) * D_MODEL
    (xr4_p, gr_p, hg5_p), (m2,) = _run([proj_p, _adaln_part(cond, w_ada, b_ada, n1, n2, n2 // proj_p["steps"])])
    (xr4_s, gr_s, hg5_s), (hs4_p, last_f, last_b) = _run(
        [_proj(x_sample, m1, sel_s, p, Ts, PROJ_TILE), _lru(xr4_p, Bp, Tp, Tp, zeros_p, p, True)])
    yh_p, st_p = _hgrn(hg5_p.reshape(N_HG_IN, Bp, Tp, D_HGRN), p["gamma"], p["norm_hgrn"], None, Tp, True)
    ((x1p, ha_p, at_p),) = _run([_mix(x_prompt, m2, sel_p, p, Tp, hs4_p, gr_p, yh_p, MIX_TILE)])
    slot_p = _select(at_p, Tp)
    assert state_lru.shape[1] == 1
    ((hs4_s, _, _),) = _run([_lru(xr4_s, Bs, Ts, GRID_W, state_lru.reshape(Bs, 2 * D_LRU), p, 8)])
    yh_s, _ = _hgrn(hg5_s.reshape(N_HG_IN, Bs, Ts, D_HGRN), p["gamma"], p["norm_hgrn"], state_hgrn, Ts, False)
    fused_tile = MIX_TILE
    (x1s, ha_s, at_s), (xd_p,) = _run(
        [_mix(x_sample, m2, sel_s, p, Ts, hs4_s, gr_s, yh_s, fused_tile),
         _dispatch_part(slot_p, ha_p, Tp, fused_tile // Tp)])
    slot_s = _select(at_s, Ts)
    ((xd_s,),) = _run([_dispatch_part(slot_s, ha_s, Ts, max(ROUTE_TOKENS // Ts, 1))])

    yd_p, yd_s = _ffn(xd_p, xd_s, w_gate, w_up, w_down)
    nf = norm_final[None, :]
    (y_sample,), (y_prompt,) = _run(
        [_combine_part(slot_s, yd_s, x1s, m2, sel_s, nf, Ts, COMBINE_TOKENS),
         _combine_part(slot_p, yd_p, x1p, m2, sel_p, nf, Tp, COMBINE_TOKENS)])
    y_prompt = y_prompt.reshape(Bp, Tp, D_MODEL)
    y_sample = y_sample.reshape(Bs, Ts, D_MODEL)
    new_state_lru = jnp.stack([last_f, last_b], axis=1)[:, None]
    new_state_hgrn = st_p
    return (y_prompt, y_sample, new_state_lru, new_state_hgrn)
```

```python
import functools

import jax
import jax.numpy as jnp
from jax import lax
from jax.experimental import pallas as pl
from jax.experimental.pallas import tpu as pltpu

F32 = jnp.float32
BF16 = jnp.bfloat16

D_MODEL = 1024
D_LRU = 512
D_HGRN = 512
HG_HEADS = 4
HG_DK = 128
LRU_BLOCKS = 8
LRU_BW = 64
LRU_C = 8.0
N_EXPERTS = 16
EC_CAPACITY_FACTOR = 2
N_MOD = 6
D_IN = 7 * 512
N_HG_IN = 5
GRID_W = 64
EPS = 1e-6
LOG2E = 1.4426950408889634
TINY = 1e-37

LANES = 128
MOD_ROWS = 8
TOK_TILE = 256
PROJ_TILE = 512
MIX_TILE = 1024
LRU_GROUP = 8
PITCH_PAD = 8
HG_CHUNK = 64
HG_UNROLL = 16
HG_ROWS = 256
FFN_TILE = 1024
ROUTE_TOKENS = 1024
COMBINE_TOKENS = 1024
SEL_BISECT = 20
VMEM_LIMIT = 60 * 1024 * 1024


def _cp(sem, vmem=VMEM_LIMIT):
    return pltpu.CompilerParams(dimension_semantics=sem, vmem_limit_bytes=vmem)


def _sigmoid(x):
    return 0.5 * (1.0 + jnp.tanh(0.5 * x))


def _dot(a, b):
    return jnp.dot(a, b, preferred_element_type=F32)


def _dot_nt(a, b):
    return lax.dot_general(a, b, (((1,), (1,)), ((), ())), preferred_element_type=F32)


def _dot_tn(a, b):
    return lax.dot_general(a, b, (((0,), (0,)), ((), ())), preferred_element_type=F32)


def _adaln_kernel(c_ref, w_ref, b_ref, o_ref):
    c = c_ref[...]
    s = (c * _sigmoid(c)).astype(BF16)
    o_ref[...] = _dot(s, w_ref[...].astype(BF16)) + b_ref[...]
    yield


def _adaln_part(cond, w, b, col0, ncols, tn):
    n = cond.shape[0]
    j0 = col0 // tn
    return dict(
        kernel=_adaln_kernel,
        steps=ncols // tn,
        args=[cond, w, b],
        in_specs=[pl.BlockSpec((n, D_MODEL), lambda j: (0, 0)),
                  pl.BlockSpec((None, D_MODEL, tn), lambda j: (0, 0, j0 + j)),
                  pl.BlockSpec((1, tn), lambda j: (0, j0 + j))],
        out_specs=[pl.BlockSpec((n, tn), lambda j: (0, j))],
        out_shape=[jax.ShapeDtypeStruct((n, ncols), F32)],
        scratch=[],
    )


def _mod_get(mod_ref, spr, j):
    cols = slice(j * D_MODEL, (j + 1) * D_MODEL)
    if spr is None:
        return mod_ref[0:1, cols]
    return mod_ref[pl.ds(pl.program_id(0) // spr, 1), cols]


def _proj_kernel(spr, x_ref, mod_ref, g_ref, w_ref, xr_ref, gr_ref, hg_ref):
    x = x_ref[...]
    ms = jnp.mean(x * x, axis=-1, keepdims=True)
    y = x * lax.rsqrt(ms + EPS) * g_ref[...]
    h = y * (1.0 + _mod_get(mod_ref, spr, 1)) + _mod_get(mod_ref, spr, 0)
    h = h.astype(BF16)
    W = 2 * LANES
    for j in range(D_IN // W):
        res = _dot(h, w_ref[:, j * W:(j + 1) * W].astype(BF16))
        k, half = divmod(j * W, D_HGRN)
        if k == 0:
            for c in range(W // LANES):
                xr_ref[half // LANES + c] = res[:, c * LANES:(c + 1) * LANES]
        elif k == 1:
            gr_ref[:, half:half + W] = res.astype(BF16)
        else:
            hg_ref[k - 2, :, half:half + W] = res
        yield


def _run(parts):
    steps = parts[0]["steps"]
    assert all(p["steps"] == steps for p in parts)
    n_in = [len(p["args"]) for p in parts]
    n_out = [len(p["out_shape"]) for p in parts]
    n_sc = [len(p["scratch"]) for p in parts]

    def body(*refs):
        ins, outs, scr = refs[:sum(n_in)], refs[sum(n_in):sum(n_in) + sum(n_out)], refs[sum(n_in) + sum(n_out):]
        i = o = s = 0
        gens = []
        for p, a, b, c in zip(parts, n_in, n_out, n_sc):
            gens.append(p["kernel"](*ins[i:i + a], *outs[o:o + b], *scr[s:s + c]))
            i, o, s = i + a, o + b, s + c
        while gens:
            for g in list(gens):
                if next(g, StopIteration) is StopIteration:
                    gens.remove(g)

    res = pl.pallas_call(
        body,
        grid=(steps,),
        in_specs=[s for p in parts for s in p["in_specs"]],
        out_specs=[s for p in parts for s in p["out_specs"]],
        out_shape=[s for p in parts for s in p["out_shape"]],
        scratch_shapes=[s for p in parts for s in p["scratch"]],
        compiler_params=_cp(("arbitrary",)),
    )(*[a for p in parts for a in p["args"]])
    out, o = [], 0
    for b in n_out:
        out.append(res[o:o + b])
        o += b
    return out


def _mod_spec(mods, msel):
    blk = msel[0]
    return pl.BlockSpec((MOD_ROWS, mods.shape[1]), lambda i: (blk, 0))


def _spr(msel, steps_per_request):
    if not msel[1]:
        return None
    assert steps_per_request >= 1
    return steps_per_request


def _proj_part(x, mods, msel, gain, w, T, tile):
    n = x.shape[0]
    assert T % tile == 0 or (tile % T == 0 and not msel[1])
    n_ct = D_LRU // LANES
    return dict(
        kernel=functools.partial(_proj_kernel, _spr(msel, T // tile)),
        steps=n // tile,
        args=[x, mods, gain, w],
        in_specs=[pl.BlockSpec((tile, D_MODEL), lambda i: (i, 0)),
                  _mod_spec(mods, msel),
                  pl.BlockSpec((1, D_MODEL), lambda i: (0, 0)),
                  pl.BlockSpec((None, D_MODEL, D_IN), lambda i: (0, 0, 0), pipeline_mode=pl.Buffered(1))],
        out_specs=[pl.BlockSpec((n_ct, tile, LANES), lambda i: (0, i, 0)),
                   pl.BlockSpec((tile, D_LRU), lambda i: (i, 0)),
                   pl.BlockSpec((N_HG_IN, tile, D_HGRN), lambda i: (0, i, 0))],
        out_shape=[jax.ShapeDtypeStruct((n_ct, n, LANES), F32),
                   jax.ShapeDtypeStruct((n, D_LRU), BF16),
                   jax.ShapeDtypeStruct((N_HG_IN, n, D_HGRN), F32)],
        scratch=[],
    )


def _lru_kernel(T, row_len, unroll, xr_ref, cw_ref, cb_ref, w_ref, lam_ref, h0f_ref, h0b_ref,
                hs_ref, lf_ref, lb_ref, xpad, af, uf, ab, ub, hf_s, hb_s):
    pitch = T + PITCH_PAD
    pos = lax.broadcasted_iota(jnp.int32, (T, LANES), 0) & (row_len - 1)
    cw = cw_ref[...]
    w0 = jnp.where(pos >= 2, cw[0:1], 0.0)
    w1 = jnp.where(pos >= 1, cw[1:2], 0.0)
    w2 = jnp.broadcast_to(cw[2:3], (T, LANES))
    w3 = jnp.where(pos <= row_len - 2, cw[3:4], 0.0)
    cb = cb_ref[...]
    nl = -lam_ref[...]
    sp = jnp.maximum(nl, 0.0) + jnp.log1p(jnp.exp(-jnp.abs(nl)))
    c2 = (-0.5 * LRU_C * LOG2E) * sp
    w = w_ref[0]
    ones = (lax.broadcasted_iota(jnp.int32, (T, LANES), 1) < 2).astype(BF16)
    for s in range(2):
        xpad[s, 0:8, :] = jnp.zeros((8, LANES), F32)
        xpad[s, T + 8:T + 16, :] = jnp.zeros((8, LANES), F32)
    for b in range(LRU_GROUP):
        x = xr_ref[b]
        xp = xpad.at[b % 2]
        xp[8:T + 8, :] = x
        xc = w0 * xp[6:T + 6, :] + w1 * xp[7:T + 7, :] + w2 * x + w3 * xp[9:T + 9, :] + cb
        xh = 0.5 * xc
        z = _dot(jnp.concatenate([xc.astype(BF16), ones], axis=1), w)
        rows = pl.ds(b * pitch, T)
        for d, (a_s, u_s) in enumerate(((af, uf), (ab, ub))):
            tr = jnp.tanh(z[:, (2 * d) * LANES:(2 * d + 1) * LANES])
            ti = jnp.tanh(z[:, (2 * d + 1) * LANES:(2 * d + 2) * LANES])
            a = jnp.exp2(c2[d:d + 1] + c2[d:d + 1] * tr)
            om = 1.0 - a * a
            a_s[rows, :] = a
            u_s[rows, :] = (om * lax.rsqrt(jnp.maximum(om, TINY))) * (xh + ti * xh)
        yield

    def step(t, carry):
        hf, hb = carry
        rf = pl.ds(t, LRU_GROUP, stride=pitch)
        hf = af[rf, :] * hf + uf[rf, :]
        hf_s[rf, :] = hf
        rb = pl.ds(T - 1 - t, LRU_GROUP, stride=pitch)
        hb = ab[rb, :] * hb + ub[rb, :]
        hb_s[rb, :] = hb
        return hf, hb

    carry = (h0f_ref[...], h0b_ref[...])
    if unroll is True:
        for t in range(T):
            carry = step(t, carry)
            if t % (T // 8) == T // 8 - 1 and t != T - 1:
                yield
    else:
        carry = lax.fori_loop(0, T, step, carry, unroll=unroll)
    hf, hb = carry
    lf_ref[...] = hf
    lb_ref[...] = hb
    for b in range(LRU_GROUP):
        rows = pl.ds(b * pitch, T)
        hs_ref[b] = (hf_s[rows, :] + hb_s[rows, :]).astype(BF16)


def _lru_part(xr4, conv_w, conv_b, w_tiles, lam, h0, T, row_len, unroll):
    n_ct, B = xr4.shape[:2]
    rows = LRU_GROUP * (T + PITCH_PAD)
    vec = pl.BlockSpec((LRU_GROUP, LANES), lambda i: (i // n_ct, i % n_ct))
    vec_b = pl.BlockSpec((LRU_GROUP, LANES), lambda i: (i // n_ct, n_ct + i % n_ct))
    slab = pl.BlockSpec((None, LRU_GROUP, T, LANES), lambda i: (i % n_ct, i // n_ct, 0, 0))
    return dict(
        kernel=functools.partial(_lru_kernel, T, row_len, unroll),
        steps=(B // LRU_GROUP) * n_ct,
        args=[xr4, conv_w, conv_b, w_tiles, lam, h0, h0],
        in_specs=[slab,
                  pl.BlockSpec((4, LANES), lambda i: (0, i % n_ct)),
                  pl.BlockSpec((1, LANES), lambda i: (0, i % n_ct)),
                  pl.BlockSpec((1, 2 * LANES, 4 * LANES), lambda i: (i % n_ct, 0, 0)),
                  pl.BlockSpec((2, LANES), lambda i: (0, i % n_ct)),
                  vec, vec_b],
        out_specs=[slab, vec, vec],
        out_shape=[jax.ShapeDtypeStruct((n_ct, B, T, LANES), BF16),
                   jax.ShapeDtypeStruct((B, D_LRU), F32),
                   jax.ShapeDtypeStruct((B, D_LRU), F32)],
        scratch=[pltpu.VMEM((2, T + 16, LANES), F32)] + [pltpu.VMEM((rows, LANES), F32)] * 6,
    )


def _hgrn_kernel(T, has_s0, want_state, *refs):
    q_ref, ff_ref, fb_ref, v_ref, og_ref, gam_ref, gain_ref = refs[:7]
    refs = refs[7:]
    if has_s0:
        s0_ref, refs = refs[0], refs[1:]
    y_ref, refs = refs[0], refs[1:]
    if want_state:
        st_ref, refs = refs[0], refs[1:]
    q_s, v_s, kf_s, kb_s, bf_s, bb_s, of_s, ob_s, st_s = refs
    C = HG_CHUNK
    n = T // C
    RB = HG_ROWS
    heads = [slice(h * LANES, (h + 1) * LANES) for h in range(HG_HEADS)]

    ti = lax.broadcasted_iota(jnp.int32, (RB, RB), 0)
    si = lax.broadcasted_iota(jnp.int32, (RB, RB), 1)
    sh = C.bit_length() - 1
    same = (ti >> sh) == (si >> sh)
    tri = ((same & (si <= ti)).astype(BF16), (same & (si >= ti)).astype(BF16))

    for r in range(T // RB):
        rows = slice(r * RB, (r + 1) * RB)
        qr = q_ref[rows, :]
        qh = (0.5 * HG_DK ** -0.5) * qr
        q_s[rows, :] = qh + qh * jnp.tanh(0.5 * qr)
        v_s[rows, :] = v_ref[rows, :].astype(BF16)
        for d, (f_ref, k_s, b_s) in enumerate(((ff_ref, kf_s, bf_s), (fb_ref, kb_s, bb_s))):
            g0 = gam_ref[d, 0:1, :]
            g1 = gam_ref[d, 1:2, :]
            m = jnp.maximum(g0, g1)
            e0 = jnp.exp(g0 - m)
            lbd = e0 / (e0 + jnp.exp(g1 - m))
            ck = 0.5 * (1.0 - lbd)
            pt = ck * jnp.tanh(0.5 * f_ref[rows, :])
            k_s[rows, :] = ck - pt
            lf = jnp.log((lbd + ck) + pt)
            hi = lf.astype(BF16)
            lo = (lf - hi.astype(F32)).astype(BF16)
            cs = _dot(tri[d], jnp.concatenate([hi, lo], axis=1))
            b_s[rows, :] = cs[:, 0:D_HGRN] + cs[:, D_HGRN:2 * D_HGRN]

    for d in range(2):
        for h in range(HG_HEADS):
            st_s[d * HG_HEADS + h] = s0_ref[d, h].T if has_s0 else jnp.zeros((LANES, LANES), F32)

    ri =lax.broadcasted_iota(jnp.int32, (C, C), 0)
    ci = lax.broadcasted_iota(jnp.int32, (C, C), 1)

    def body(i, carry):
        for d, (k_s, b_s, o_s) in enumerate(((kf_s, bf_s, of_s), (kb_s, bb_s, ob_s))):
            c = i if d == 0 else n - 1 - i
            sl = pl.ds(pl.multiple_of(c * C, C), C)
            for h in range(HG_HEADS):
                q = q_s[sl, heads[h]]
                k = k_s[sl, heads[h]]
                b = b_s[sl, heads[h]]
                v = v_s[sl, heads[h]]
                bm = b[C // 2:C // 2 + 1, :]
                g = b[C - 1:C, :] if d == 0 else b[0:1, :]
                qi = q * jnp.exp(b - bm)
                qe = (qi * jnp.exp(bm)).astype(BF16)
                ki = k * jnp.exp(bm - b)
                kd = (ki * jnp.exp(g - bm)).astype(BF16)
                s = _dot_nt(qi.astype(BF16), ki.astype(BF16))
                p = jnp.where((ci <= ri) if d == 0 else (ci >= ri), s, 0.0).astype(BF16)
                st = st_s[d * HG_HEADS + h]
                o_s[sl, heads[h]] = _dot(p, v) + _dot(qe, st.T.astype(BF16))
                st_s[d * HG_HEADS + h] = st * jnp.exp(g) + _dot_tn(v, kd)
        return carry

    lax.fori_loop(0, n, body, 0, unroll=HG_UNROLL)
    if want_state:
        for d in range(2):
            for h in range(HG_HEADS):
                st_ref[d, h] = st_s[d * HG_HEADS + h].T

    for r in range(T // RB):
        rows = slice(r * RB, (r + 1) * RB)
        for h in range(HG_HEADS):
            o = of_s[rows, heads[h]] + ob_s[rows, heads[h]]
            ms = jnp.mean(o * o, axis=-1, keepdims=True)
            oh = 0.5 * og_ref[rows, heads[h]]
            y = o * lax.rsqrt(ms + EPS) * gain_ref[...] * (oh + oh * jnp.tanh(oh))
            y_ref[rows, heads[h]] = y.astype(BF16)


def _hgrn(hg5, gamma, gain, s0, T, want_state):
    B = hg5.shape[1]
    has_s0 = s0 is not None

    def col(k):
        return pl.BlockSpec((None, None, T, D_HGRN), lambda b, k=k: (k, b, 0, 0))

    st_spec = pl.BlockSpec((None, None, 2, HG_HEADS, HG_DK, HG_DK), lambda b: (b, 0, 0, 0, 0, 0))
    in_specs = [col(0), col(1), col(2), col(3), col(4),
                pl.BlockSpec((2, 2, D_HGRN), lambda b: (0, 0, 0)),
                pl.BlockSpec((1, LANES), lambda b: (0, 0))]
    args = [hg5, hg5, hg5, hg5, hg5, gamma, gain]
    if has_s0:
        in_specs.append(st_spec)
        args.append(s0)
    out_specs = [pl.BlockSpec((None, T, D_HGRN), lambda b: (b, 0, 0))]
    out_shape = [jax.ShapeDtypeStruct((B, T, D_HGRN), BF16)]
    if want_state:
        out_specs.append(st_spec)
        out_shape.append(jax.ShapeDtypeStruct((B, 1, 2, HG_HEADS, HG_DK, HG_DK), F32))
    res = pl.pallas_call(
        functools.partial(_hgrn_kernel, T, has_s0, want_state),
        grid=(B,),
        in_specs=in_specs,
        out_specs=out_specs,
        out_shape=out_shape,
        scratch_shapes=[pltpu.VMEM((T, D_HGRN), F32), pltpu.VMEM((T, D_HGRN), BF16)]
        + [pltpu.VMEM((T, D_HGRN), F32)] * 6
        + [pltpu.VMEM((2 * HG_HEADS, HG_DK, HG_DK), F32)],
        compiler_params=_cp(("arbitrary",)),
    )(*args)
    return res if want_state else (res[0], None)


def _mix_out_kernel(spr, hs_ref, gr_ref, yh_ref, x_ref, mod_ref, nl_ref, wo_ref, nf_ref, wr_ref,
                    x1_ref, ha_ref, at_ref):
    wr = wr_ref[...]
    wo = wo_ref[...].astype(BF16)
    g1 = _mod_get(mod_ref, spr, 0)
    sh2 = _mod_get(mod_ref, spr, 1)
    gain2 = nf_ref[...] * (1.0 + _mod_get(mod_ref, spr, 2))
    lane = lax.broadcasted_iota(jnp.int32, (TOK_TILE, LANES), 1)
    for s in range(x_ref.shape[0] // TOK_TILE):
        rows = slice(s * TOK_TILE, (s + 1) * TOK_TILE)
        hs = jnp.concatenate([hs_ref[c, rows, :] for c in range(D_LRU // LANES)], axis=1).astype(F32)
        ms = jnp.mean(hs * hs, axis=-1, keepdims=True)
        gr = gr_ref[rows, :].astype(F32)
        gelu = 0.5 * gr * (1.0 + jnp.tanh(0.7978845608028654 * (gr + 0.044715 * (gr * gr * gr))))
        y_lru = hs * lax.rsqrt(ms + EPS) * nl_ref[...] * gelu
        ycat = jnp.concatenate([y_lru.astype(BF16), yh_ref[rows, :]], axis=-1)
        x1 = x_ref[rows, :] + g1 * _dot(ycat, wo)
        x1_ref[rows, :] = x1
        yield
        ms = jnp.mean(x1 * x1, axis=-1, keepdims=True)
        h2 = x1 * lax.rsqrt(ms + EPS) * gain2 + sh2
        h2_hi = h2.astype(BF16)
        h2_mid = (h2 - h2_hi.astype(F32)).astype(BF16)
        p_hi = _dot(h2_hi, wr)
        logits = p_hi[:, 0:LANES] + (p_hi[:, LANES:2 * LANES] + _dot(h2_mid, wr)[:, 0:LANES])
        logits = jnp.where(lane < N_EXPERTS, logits, -jnp.inf)
        e = jnp.exp(logits - jnp.max(logits, axis=-1, keepdims=True))
        aff = e / jnp.sum(e, axis=-1, keepdims=True)
        w = at_ref.shape[1]
        r, c = (s * TOK_TILE) // w, (s * TOK_TILE) % w
        at_ref[r * N_EXPERTS:(r + 1) * N_EXPERTS, c:c + TOK_TILE] = aff.T[0:N_EXPERTS, :]
        ha_ref[rows, :] = h2_hi
        yield


def _mix_out_part(hs4, gr, yh, x, mods, msel, norm_lru, w_out, norm_ffn, wr_pad, T, tile_rows):
    n = x.shape[0]
    assert T % TOK_TILE == 0 and (tile_rows % T == 0 or T % tile_rows == 0)
    q = max(T // tile_rows, 1)
    at_rows = N_EXPERTS * max(tile_rows // T, 1)
    at_spec = pl.BlockSpec((at_rows, min(T, tile_rows)), lambda i: (i // q, i % q))
    tile = lambda w: pl.BlockSpec((tile_rows, w), lambda i: (i, 0))
    const = lambda r, w: pl.BlockSpec((r, w), lambda i: (0, 0))
    return dict(
        kernel=functools.partial(_mix_out_kernel, _spr(msel, q)),
        steps=n // tile_rows,
        args=[hs4, gr, yh, x, mods, norm_lru, w_out, norm_ffn, wr_pad],
        in_specs=[pl.BlockSpec((D_LRU // LANES, tile_rows, LANES), lambda i: (0, i, 0)),
                  tile(D_LRU), tile(D_HGRN), tile(D_MODEL),
                  _mod_spec(mods, msel),
                  const(1, D_LRU), const(D_MODEL, D_MODEL), const(1, D_MODEL), const(D_MODEL, 2 * LANES)],
        out_specs=[tile(D_MODEL), tile(D_MODEL), at_spec],
        out_shape=[jax.ShapeDtypeStruct((n, D_MODEL), F32),
                   jax.ShapeDtypeStruct((n, D_MODEL), BF16),
                   jax.ShapeDtypeStruct((n // T * N_EXPERTS, T), F32)],
        scratch=[],
    )


def _select_kernel(cap, at_ref, slot_ref):
    a = at_ref[...]
    R, T = a.shape
    ones = jnp.ones((T, LANES), BF16)
    su = (lax.broadcasted_iota(jnp.int32, (T, T), 0)
          < lax.broadcasted_iota(jnp.int32, (T, T), 1)).astype(BF16)

    def wide(x):
        return jnp.concatenate([x] * (T // LANES), axis=1)

    def count(mask):
        return _dot(mask.astype(BF16), ones)

    lo = jnp.zeros((R, LANES), F32)
    hi = jnp.full((R, LANES), 2.0, F32)
    for _ in range(SEL_BISECT):
        mid = 0.5 * (lo + hi)
        ge = count(a >= wide(mid)) >= cap
        lo = jnp.where(ge, mid, lo)
        hi = jnp.where(ge, hi, mid)

    def cond(st):
        return (st[0] < T) & (jnp.min(st[2]) < cap)

    def body(st):
        it, cur, n, thr = st
        m = jnp.max(jnp.where(a < wide(cur), a, -1.0), axis=1, keepdims=True)
        m = jnp.broadcast_to(m, (R, LANES))
        c = count(a >= wide(m))
        act = n < cap
        return it + 1, jnp.where(act, m, cur), jnp.where(act, c, n), jnp.where(act, m, thr)

    _, _, _, thr = lax.while_loop(cond, body, (jnp.int32(0), hi, count(a >= wide(hi)), lo))
    thr_w = wide(thr)
    gt = a > thr_w
    eq = a == thr_w
    need = wide(cap - count(gt))
    sel = gt | (eq & (_dot(eq.astype(BF16), su) < need))
    slot_ref[...] = jnp.where(sel, _dot(sel.astype(BF16), su), -1.0).astype(jnp.int32)


def _select(at, T):
    cap = EC_CAPACITY_FACTOR * T // N_EXPERTS
    R = at.shape[0]
    return pl.pallas_call(
        functools.partial(_select_kernel, cap),
        grid=(1,),
        in_specs=[pl.BlockSpec((R, T), lambda i: (0, 0))],
        out_specs=pl.BlockSpec((R, T), lambda i: (0, 0)),
        out_shape=jax.ShapeDtypeStruct((R, T), jnp.int32),
        compiler_params=_cp(("arbitrary",)),
    )(at)


def _onehot(cap, slot_ref, g):
    j = lax.broadcasted_iota(jnp.int32, (cap, slot_ref.shape[1]), 0)
    r0 = g * N_EXPERTS
    return jnp.concatenate([(slot_ref[r0 + e:r0 + e + 1, :] == j).astype(BF16) for e in range(N_EXPERTS)], axis=0)


def _dispatch_kernel(T, cap, slot_ref, ha_ref, xd_ref):
    for g in range(ha_ref.shape[0] // T):
        rows = _dot(_onehot(cap, slot_ref, g), ha_ref[g * T:(g + 1) * T, :])
        for e in range(N_EXPERTS):
            xd_ref[e, g * cap:(g + 1) * cap, :] = rows[e * cap:(e + 1) * cap].astype(BF16)
        yield


def _dispatch_part(slot, ha, T, G):
    cap = EC_CAPACITY_FACTOR * T // N_EXPERTS
    B = ha.shape[0] // T
    return dict(
        kernel=functools.partial(_dispatch_kernel, T, cap),
        steps=B // G,
        args=[slot, ha],
        in_specs=[pl.BlockSpec((G * N_EXPERTS, T), lambda b: (b, 0)),
                  pl.BlockSpec((G * T, D_MODEL), lambda b: (b, 0))],
        out_specs=[pl.BlockSpec((N_EXPERTS, G * cap, D_MODEL), lambda b: (0, b, 0))],
        out_shape=[jax.ShapeDtypeStruct((N_EXPERTS, B * cap, D_MODEL), BF16)],
        scratch=[],
    )


def _slot_gates(slot_ref, at_ref, cap):
    j = lax.broadcasted_iota(jnp.int32, (cap, slot_ref.shape[1]), 0)
    return jnp.concatenate(
        [jnp.sum(jnp.where(slot_ref[g:g + 1, :] == j, at_ref[g:g + 1, :], 0.0), axis=-1, keepdims=True)
         for g in range(slot_ref.shape[0])], axis=0)


def _ffn_kernel(xp_ref, xs_ref, slot_p_ref, at_p_ref, slot_s_ref, at_s_ref, wg_ref, wu_ref, wd_ref, yp_ref, ys_ref):
    wg = wg_ref[...].astype(BF16)
    wu = wu_ref[...].astype(BF16)
    wd = wd_ref[...].astype(BF16)
    for x_ref, slot_ref, at_ref, y_ref in ((xp_ref, slot_p_ref, at_p_ref, yp_ref),
                                           (xs_ref, slot_s_ref, at_s_ref, ys_ref)):
        x = x_ref[...]
        gate = _slot_gates(slot_ref, at_ref, x.shape[0] // slot_ref.shape[0])
        hg = _dot(x, wg)
        hid = (hg * _sigmoid(hg)) * _dot(x, wu)
        y_ref[...] = (_dot(hid.astype(BF16), wd) * gate).astype(BF16)


def _ffn(xd_p, xd_s, slot_p, at_p, slot_s, at_s, w_gate, w_up, w_down):
    rows = xd_p.shape[1]
    assert xd_s.shape[1] == rows == FFN_TILE
    xspec = pl.BlockSpec((None, rows, D_MODEL), lambda e: (e, 0, 0))
    wspec = pl.BlockSpec((None, None, D_MODEL, D_MODEL), lambda e: (0, e, 0, 0))
    yshape = jax.ShapeDtypeStruct((N_EXPERTS, rows, D_MODEL), BF16)

    def by_expert(a):
        a2 = a.reshape(a.shape[0] // N_EXPERTS, N_EXPERTS * a.shape[1])
        return a2, pl.BlockSpec((a2.shape[0], a.shape[1]), lambda e: (0, e))

    routing, rspecs = zip(*(by_expert(a) for a in (slot_p, at_p, slot_s, at_s)))
    return pl.pallas_call(
        _ffn_kernel,
        grid=(N_EXPERTS,),
        in_specs=[xspec, xspec, *rspecs, wspec, wspec, wspec],
        out_specs=[xspec, xspec],
        out_shape=[yshape, yshape],
        compiler_params=_cp(("arbitrary",)),
    )(xd_p, xd_s, *routing, w_gate, w_up, w_down)


def _combine_kernel(cap, spr, slot_ref, yd_ref, x1_ref, mod_ref, nf_ref, o_ref):
    W = slot_ref.shape[1]
    g2 = _mod_get(mod_ref, spr, 3)
    for g in range(x1_ref.shape[0] // W):
        rows = slice(g * W, (g + 1) * W)
        yd = jnp.concatenate([yd_ref[e, g * cap:(g + 1) * cap, :] for e in range(N_EXPERTS)], axis=0)
        acc = _dot_tn(_onehot(cap, slot_ref, g), yd)
        x2 = x1_ref[rows, :] + g2 * acc
        ms = jnp.mean(x2 * x2, axis=-1, keepdims=True)
        o_ref[rows, :] = x2 * lax.rsqrt(ms + EPS) * nf_ref[...]
        yield


def _combine_part(slot, yd, x1, mods, msel, norm_final, T, tok):
    cap = EC_CAPACITY_FACTOR * T // N_EXPERTS
    n = x1.shape[0]
    G, h = (tok // T, 1) if tok >= T else (1, T // tok)
    assert G == 1 or not msel[1]
    return dict(
        kernel=functools.partial(_combine_kernel, cap, _spr(msel, h)),
        steps=n // tok,
        args=[slot, yd, x1, mods, norm_final],
        in_specs=[pl.BlockSpec((G * N_EXPERTS, min(T, tok)), lambda i: (i // h, i % h)),
                  pl.BlockSpec((N_EXPERTS, G * cap, D_MODEL), lambda i: (0, i // h, 0)),
                  pl.BlockSpec((tok, D_MODEL), lambda i: (i, 0)),
                  _mod_spec(mods, msel),
                  pl.BlockSpec((1, D_MODEL), lambda i: (0, 0))],
        out_specs=[pl.BlockSpec((tok, D_MODEL), lambda i: (i, 0))],
        out_shape=[jax.ShapeDtypeStruct((n, D_MODEL), F32)],
        scratch=[],
    )


def _gate_tiles(wa, wx, ba, bx):
    n_ct = D_LRU // LANES

    def dense(w):
        w2 = w.reshape(n_ct, 2, LRU_BW, LRU_BW)
        z = jnp.zeros((n_ct, LRU_BW, LRU_BW), w.dtype)
        top = jnp.concatenate([w2[:, 0], z], axis=2)
        bot = jnp.concatenate([z, w2[:, 1]], axis=2)
        return jnp.concatenate([top, bot], axis=1)

    tiles = 0.5 * jnp.concatenate([dense(wa[0]), dense(wx[0]), dense(wa[1]), dense(wx[1])], axis=-1)
    t = lambda v: v.reshape(n_ct, 1, LANES)
    bias = 0.5 * jnp.concatenate([t(ba[0]), t(bx[0]), t(ba[1]), t(bx[1])], axis=-1)
    hi = bias.astype(BF16)
    lo = (bias - hi.astype(F32)).astype(BF16)
    pad = jnp.zeros((n_ct, LANES - 2, 4 * LANES), BF16)
    return jnp.concatenate([tiles.astype(BF16), hi, lo, pad], axis=1)


def _router_split(w):
    wp = jnp.pad(w, ((0, 0), (0, LANES - N_EXPERTS)))
    hi = wp.astype(BF16)
    mid = (wp - hi.astype(F32)).astype(BF16)
    return jnp.concatenate([hi, mid], axis=1)


def _proj(x, mods, msel, p, T, tile):
    return _proj_part(x.reshape(-1, D_MODEL), mods, msel, p["norm_mix"], p["w_in"], T, tile)


def _lru(xr4, B, T, row_len, h0, p, unroll):
    n_ct = D_LRU // LANES
    return _lru_part(xr4.reshape(n_ct, B, T, LANES), p["conv_w"], p["conv_b"], p["gate_w"], p["lam"],
                     h0, T, row_len, unroll)


def _mix(x, mods, msel, p, T, hs4, gr, yh, tile_rows):
    B = x.shape[0]
    return _mix_out_part(hs4.reshape(D_LRU // LANES, B * T, LANES), gr, yh.reshape(B * T, D_HGRN),
                         x.reshape(B * T, D_MODEL), mods, msel,
                         p["norm_lru"], p["w_out"], p["norm_ffn"], p["w_router"], T, tile_rows)


def kernel(x_prompt, x_sample, state_lru, state_hgrn, c, c_ctx, w_ada, b_ada, norm_mix, w_in, conv_w,
           conv_b, lru_wa, lru_ba, lru_wx, lru_bx, lru_lambda, norm_lru, hgrn_gamma, norm_hgrn, w_out,
           norm_ffn, w_router, w_gate, w_up, w_down, norm_final):
    assert w_ada.shape[0] == 1 and hgrn_gamma.shape[1] == 2, "one trunk layer"
    l = 0
    Bp, Tp, _ = x_prompt.shape
    Bs, Ts, _ = x_sample.shape

    assert Bs == MOD_ROWS
    cond = jnp.concatenate([c, c_ctx[None, :], jnp.zeros((MOD_ROWS - 1, D_MODEL), F32)], axis=0)
    n1 = 2 * D_MODEL
    ((m1,),) = _run([_adaln_part(cond, w_ada, b_ada, 0, n1, D_MODEL)])
    sel_p, sel_s = (1, False), (0, True)

    p = {
        "norm_mix": norm_mix[l][None, :], "w_in": w_in,
        "conv_w": conv_w[l], "conv_b": conv_b[l][None, :],
        "gate_w": _gate_tiles(lru_wa[l], lru_wx[l], lru_ba[l], lru_bx[l]),
        "lam": lru_lambda[l], "norm_lru": norm_lru[l][None, :],
        "gamma": hgrn_gamma, "norm_hgrn": norm_hgrn[l][None, :],
        "w_out": w_out[l], "norm_ffn": norm_ffn[l][None, :],
        "w_router": _router_split(w_router[l]),
    }
    zeros_p = jnp.zeros((Bp, 2 * D_LRU), F32)
    proj_p = _proj(x_prompt, m1, sel_p, p, Tp, PROJ_TILE)
    n2 = (N_MOD - 2) * D_MODEL
    (xr4_p, gr_p, hg5_p), (m2,) = _run([proj_p, _adaln_part(cond, w_ada, b_ada, n1, n2, n2 // proj_p["steps"])])
    (xr4_s, gr_s, hg5_s), (hs4_p, last_f, last_b) = _run(
        [_proj(x_sample, m1, sel_s, p, Ts, PROJ_TILE), _lru(xr4_p, Bp, Tp, Tp, zeros_p, p, True)])
    yh_p, st_p = _hgrn(hg5_p.reshape(N_HG_IN, Bp, Tp, D_HGRN), p["gamma"], p["norm_hgrn"], None, Tp, True)
    ((x1p, ha_p, at_p),) = _run([_mix(x_prompt, m2, sel_p, p, Tp, hs4_p, gr_p, yh_p, MIX_TILE)])
    slot_p = _select(at_p, Tp)
    assert state_lru.shape[1] == 1
    ((hs4_s, _, _),) = _run([_lru(xr4_s, Bs, Ts, GRID_W, state_lru.reshape(Bs, 2 * D_LRU), p, 8)])
    yh_s, _ = _hgrn(hg5_s.reshape(N_HG_IN, Bs, Ts, D_HGRN), p["gamma"], p["norm_hgrn"], state_hgrn, Ts, False)
    fused_tile = MIX_TILE
    (x1s, ha_s, at_s), (xd_p,) = _run(
        [_mix(x_sample, m2, sel_s, p, Ts, hs4_s, gr_s, yh_s, fused_tile),
         _dispatch_part(slot_p, ha_p, Tp, fused_tile // Tp)])
    slot_s = _select(at_s, Ts)
    ((xd_s,),) = _run([_dispatch_part(slot_s, ha_s, Ts, max(ROUTE_TOKENS // Ts, 1))])

    yd_p, yd_s = _ffn(xd_p, xd_s, slot_p, at_p, slot_s, at_s, w_gate, w_up, w_down)
    nf = norm_final[None, :]
    (y_sample,), (y_prompt,) = _run(
        [_combine_part(slot_s, yd_s, x1s, m2, sel_s, nf, Ts, COMBINE_TOKENS),
         _combine_part(slot_p, yd_p, x1p, m2, sel_p, nf, Tp, COMBINE_TOKENS)])
    y_prompt = y_prompt.reshape(Bp, Tp, D_MODEL)
    y_sample = y_sample.reshape(Bs, Ts, D_MODEL)
    new_state_lru = jnp.stack([last_f, last_b], axis=1)[:, None]
    new_state_hgrn = st_p
    return (y_prompt, y_sample, new_state_lru, new_state_hgrn)
```

```python
import functools

import jax
import jax.numpy as jnp
from jax import lax
from jax.experimental import pallas as pl
from jax.experimental.pallas import tpu as pltpu

F32 = jnp.float32
BF16 = jnp.bfloat16

D_MODEL = 1024
D_LRU = 512
D_HGRN = 512
HG_HEADS = 4
HG_DK = 128
LRU_BLOCKS = 8
LRU_BW = 64
LRU_C = 8.0
N_EXPERTS = 16
EC_CAPACITY_FACTOR = 2
N_MOD = 6
D_IN = 7 * 512
N_HG_IN = 5
GRID_W = 64
EPS = 1e-6
LOG2E = 1.4426950408889634
TINY = 1e-37

LANES = 128
MOD_ROWS = 8
TOK_TILE = 256
PROJ_TILE = 512
MIX_TILE = 1024
LRU_GROUP = 8
PITCH_PAD = 8
HG_CHUNK = 64
HG_UNROLL = 16
HG_ROWS = 256
FFN_TILE = 1024
ROUTE_TOKENS = 1024
COMBINE_TOKENS = 1024
SEL_BISECT = 20
VMEM_LIMIT = 60 * 1024 * 1024


def _cp(sem, vmem=VMEM_LIMIT):
    return pltpu.CompilerParams(dimension_semantics=sem, vmem_limit_bytes=vmem)


def _sigmoid(x):
    return 0.5 * (1.0 + jnp.tanh(0.5 * x))


def _dot(a, b):
    return jnp.dot(a, b, preferred_element_type=F32)


def _dot_nt(a, b):
    return lax.dot_general(a, b, (((1,), (1,)), ((), ())), preferred_element_type=F32)


def _dot_tn(a, b):
    return lax.dot_general(a, b, (((0,), (0,)), ((), ())), preferred_element_type=F32)


def _adaln_kernel(c_ref, w_ref, b_ref, o_ref):
    c = c_ref[...]
    s = (c * _sigmoid(c)).astype(BF16)
    o_ref[...] = _dot(s, w_ref[...].astype(BF16)) + b_ref[...]
    yield


def _adaln_part(cond, w, b, col0, ncols, tn):
    n = cond.shape[0]
    j0 = col0 // tn
    return dict(
        kernel=_adaln_kernel,
        steps=ncols // tn,
        args=[cond, w, b],
        in_specs=[pl.BlockSpec((n, D_MODEL), lambda j: (0, 0)),
                  pl.BlockSpec((None, D_MODEL, tn), lambda j: (0, 0, j0 + j)),
                  pl.BlockSpec((1, tn), lambda j: (0, j0 + j))],
        out_specs=[pl.BlockSpec((n, tn), lambda j: (0, j))],
        out_shape=[jax.ShapeDtypeStruct((n, ncols), F32)],
        scratch=[],
    )


def _mod_get(mod_ref, spr, j):
    cols = slice(j * D_MODEL, (j + 1) * D_MODEL)
    if spr is None:
        return mod_ref[0:1, cols]
    return mod_ref[pl.ds(pl.program_id(0) // spr, 1), cols]


def _proj_kernel(spr, x_ref, mod_ref, g_ref, w_ref, xr_ref, gr_ref, hg_ref):
    x = x_ref[...]
    ms = jnp.mean(x * x, axis=-1, keepdims=True)
    y = x * lax.rsqrt(ms + EPS) * g_ref[...]
    h = y * (1.0 + _mod_get(mod_ref, spr, 1)) + _mod_get(mod_ref, spr, 0)
    h = h.astype(BF16)
    W = 2 * LANES
    for j in range(D_IN // W):
        res = _dot(h, w_ref[:, j * W:(j + 1) * W].astype(BF16))
        k, half = divmod(j * W, D_HGRN)
        if k == 0:
            for c in range(W // LANES):
                xr_ref[half // LANES + c] = res[:, c * LANES:(c + 1) * LANES]
        elif k == 1:
            gr_ref[:, half:half + W] = res.astype(BF16)
        else:
            hg_ref[k - 2, :, half:half + W] = res
        yield


def _run(parts):
    steps = parts[0]["steps"]
    assert all(p["steps"] == steps for p in parts)
    n_in = [len(p["args"]) for p in parts]
    n_out = [len(p["out_shape"]) for p in parts]
    n_sc = [len(p["scratch"]) for p in parts]

    def body(*refs):
        ins, outs, scr = refs[:sum(n_in)], refs[sum(n_in):sum(n_in) + sum(n_out)], refs[sum(n_in) + sum(n_out):]
        i = o = s = 0
        gens = []
        for p, a, b, c in zip(parts, n_in, n_out, n_sc):
            gens.append(p["kernel"](*ins[i:i + a], *outs[o:o + b], *scr[s:s + c]))
            i, o, s = i + a, o + b, s + c
        while gens:
            for g in list(gens):
                if next(g, StopIteration) is StopIteration:
                    gens.remove(g)

    res = pl.pallas_call(
        body,
        grid=(steps,),
        in_specs=[s for p in parts for s in p["in_specs"]],
        out_specs=[s for p in parts for s in p["out_specs"]],
        out_shape=[s for p in parts for s in p["out_shape"]],
        scratch_shapes=[s for p in parts for s in p["scratch"]],
        compiler_params=_cp(("arbitrary",)),
    )(*[a for p in parts for a in p["args"]])
    out, o = [], 0
    for b in n_out:
        out.append(res[o:o + b])
        o += b
    return out


def _mod_spec(mods, msel):
    blk = msel[0]
    return pl.BlockSpec((MOD_ROWS, mods.shape[1]), lambda i: (blk, 0))


def _spr(msel, steps_per_request):
    if not msel[1]:
        return None
    assert steps_per_request >= 1
    return steps_per_request


def _proj_part(x, mods, msel, gain, w, T, tile):
    n = x.shape[0]
    assert T % tile == 0 or (tile % T == 0 and not msel[1])
    n_ct = D_LRU // LANES
    return dict(
        kernel=functools.partial(_proj_kernel, _spr(msel, T // tile)),
        steps=n // tile,
        args=[x, mods, gain, w],
        in_specs=[pl.BlockSpec((tile, D_MODEL), lambda i: (i, 0)),
                  _mod_spec(mods, msel),
                  pl.BlockSpec((1, D_MODEL), lambda i: (0, 0)),
                  pl.BlockSpec((None, D_MODEL, D_IN), lambda i: (0, 0, 0), pipeline_mode=pl.Buffered(1))],
        out_specs=[pl.BlockSpec((n_ct, tile, LANES), lambda i: (0, i, 0)),
                   pl.BlockSpec((tile, D_LRU), lambda i: (i, 0)),
                   pl.BlockSpec((N_HG_IN, tile, D_HGRN), lambda i: (0, i, 0))],
        out_shape=[jax.ShapeDtypeStruct((n_ct, n, LANES), F32),
                   jax.ShapeDtypeStruct((n, D_LRU), BF16),
                   jax.ShapeDtypeStruct((N_HG_IN, n, D_HGRN), F32)],
        scratch=[],
    )


def _lru_kernel(T, row_len, unroll, xr_ref, cw_ref, cb_ref, w_ref, lam_ref, h0f_ref, h0b_ref,
                hs_ref, lf_ref, lb_ref, xpad, af, uf, ab, ub, hf_s, hb_s):
    pitch = T + PITCH_PAD
    pos = lax.broadcasted_iota(jnp.int32, (T, LANES), 0) & (row_len - 1)
    cw = cw_ref[...]
    w0 = jnp.where(pos >= 2, cw[0:1], 0.0)
    w1 = jnp.where(pos >= 1, cw[1:2], 0.0)
    w2 = jnp.broadcast_to(cw[2:3], (T, LANES))
    w3 = jnp.where(pos <= row_len - 2, cw[3:4], 0.0)
    cb = cb_ref[...]
    nl = -lam_ref[...]
    sp = jnp.maximum(nl, 0.0) + jnp.log1p(jnp.exp(-jnp.abs(nl)))
    c2 = (-0.5 * LRU_C * LOG2E) * sp
    w = w_ref[0]
    ones = (lax.broadcasted_iota(jnp.int32, (T, LANES), 1) < 2).astype(BF16)
    for s in range(2):
        xpad[s, 0:8, :] = jnp.zeros((8, LANES), F32)
        xpad[s, T + 8:T + 16, :] = jnp.zeros((8, LANES), F32)
    for b in range(LRU_GROUP):
        x = xr_ref[b]
        xp = xpad.at[b % 2]
        xp[8:T + 8, :] = x
        xc = w0 * xp[6:T + 6, :] + w1 * xp[7:T + 7, :] + w2 * x + w3 * xp[9:T + 9, :] + cb
        xh = 0.5 * xc
        z = _dot(jnp.concatenate([xc.astype(BF16), ones], axis=1), w)
        rows = pl.ds(b * pitch, T)
        for d, (a_s, u_s) in enumerate(((af, uf), (ab, ub))):
            tr = jnp.tanh(z[:, (2 * d) * LANES:(2 * d + 1) * LANES])
            ti = jnp.tanh(z[:, (2 * d + 1) * LANES:(2 * d + 2) * LANES])
            a = jnp.exp2(c2[d:d + 1] + c2[d:d + 1] * tr)
            om = 1.0 - a * a
            a_s[rows, :] = a
            u_s[rows, :] = (om * lax.rsqrt(jnp.maximum(om, TINY))) * (xh + ti * xh)
        yield

    def step(t, carry):
        hf, hb = carry
        rf = pl.ds(t, LRU_GROUP, stride=pitch)
        hf = af[rf, :] * hf + uf[rf, :]
        hf_s[rf, :] = hf
        rb = pl.ds(T - 1 - t, LRU_GROUP, stride=pitch)
        hb = ab[rb, :] * hb + ub[rb, :]
        hb_s[rb, :] = hb
        return hf, hb

    carry = (h0f_ref[...], h0b_ref[...])
    if unroll is True:
        for t in range(T):
            carry = step(t, carry)
            if t % (T // 8) == T // 8 - 1 and t != T - 1:
                yield
    else:
        carry = lax.fori_loop(0, T, step, carry, unroll=unroll)
    hf, hb = carry
    lf_ref[...] = hf
    lb_ref[...] = hb
    for b in range(LRU_GROUP):
        rows = pl.ds(b * pitch, T)
        hs_ref[b] = (hf_s[rows, :] + hb_s[rows, :]).astype(BF16)


def _lru_part(xr4, conv_w, conv_b, w_tiles, lam, h0, T, row_len, unroll):
    n_ct, B = xr4.shape[:2]
    rows = LRU_GROUP * (T + PITCH_PAD)
    vec = pl.BlockSpec((LRU_GROUP, LANES), lambda i: (i // n_ct, i % n_ct))
    vec_b = pl.BlockSpec((LRU_GROUP, LANES), lambda i: (i // n_ct, n_ct + i % n_ct))
    slab = pl.BlockSpec((None, LRU_GROUP, T, LANES), lambda i: (i % n_ct, i // n_ct, 0, 0))
    return dict(
        kernel=functools.partial(_lru_kernel, T, row_len, unroll),
        steps=(B // LRU_GROUP) * n_ct,
        args=[xr4, conv_w, conv_b, w_tiles, lam, h0, h0],
        in_specs=[slab,
                  pl.BlockSpec((4, LANES), lambda i: (0, i % n_ct)),
                  pl.BlockSpec((1, LANES), lambda i: (0, i % n_ct)),
                  pl.BlockSpec((1, 2 * LANES, 4 * LANES), lambda i: (i % n_ct, 0, 0)),
                  pl.BlockSpec((2, LANES), lambda i: (0, i % n_ct)),
                  vec, vec_b],
        out_specs=[slab, vec, vec],
        out_shape=[jax.ShapeDtypeStruct((n_ct, B, T, LANES), BF16),
                   jax.ShapeDtypeStruct((B, D_LRU), F32),
                   jax.ShapeDtypeStruct((B, D_LRU), F32)],
        scratch=[pltpu.VMEM((2, T + 16, LANES), F32)] + [pltpu.VMEM((rows, LANES), F32)] * 6,
    )


def _hgrn_kernel(T, has_s0, want_state, *refs):
    q_ref, ff_ref, fb_ref, v_ref, og_ref, gam_ref, gain_ref = refs[:7]
    refs = refs[7:]
    if has_s0:
        s0_ref, refs = refs[0], refs[1:]
    y_ref, refs = refs[0], refs[1:]
    if want_state:
        st_ref, refs = refs[0], refs[1:]
    q_s, v_s, kf_s, kb_s, bf_s, bb_s, of_s, ob_s, st_s = refs
    C = HG_CHUNK
    n = T // C
    RB = HG_ROWS
    heads = [slice(h * LANES, (h + 1) * LANES) for h in range(HG_HEADS)]

    ti = lax.broadcasted_iota(jnp.int32, (RB, RB), 0)
    si = lax.broadcasted_iota(jnp.int32, (RB, RB), 1)
    sh = C.bit_length() - 1
    same = (ti >> sh) == (si >> sh)
    tri = ((same & (si <= ti)).astype(BF16), (same & (si >= ti)).astype(BF16))

    for r in range(T // RB):
        rows = slice(r * RB, (r + 1) * RB)
        qr = q_ref[rows, :]
        qh = (0.5 * HG_DK ** -0.5) * qr
        q_s[rows, :] = qh + qh * jnp.tanh(0.5 * qr)
        v_s[rows, :] = v_ref[rows, :].astype(BF16)
        for d, (f_ref, k_s, b_s) in enumerate(((ff_ref, kf_s, bf_s), (fb_ref, kb_s, bb_s))):
            g0 = gam_ref[d, 0:1, :]
            g1 = gam_ref[d, 1:2, :]
            m = jnp.maximum(g0, g1)
            e0 = jnp.exp(g0 - m)
            lbd = e0 / (e0 + jnp.exp(g1 - m))
            ck = 0.5 * (1.0 - lbd)
            pt = ck * jnp.tanh(0.5 * f_ref[rows, :])
            k_s[rows, :] = ck - pt
            lf = jnp.log((lbd + ck) + pt)
            hi = lf.astype(BF16)
            lo = (lf - hi.astype(F32)).astype(BF16)
            cs = _dot(tri[d], jnp.concatenate([hi, lo], axis=1))
            b_s[rows, :] = cs[:, 0:D_HGRN] + cs[:, D_HGRN:2 * D_HGRN]

    for d in range(2):
        for h in range(HG_HEADS):
            st_s[d * HG_HEADS + h] = s0_ref[d, h].T if has_s0 else jnp.zeros((LANES, LANES), F32)

    ri =lax.broadcasted_iota(jnp.int32, (C, C), 0)
    ci = lax.broadcasted_iota(jnp.int32, (C, C), 1)

    def body(i, carry):
        for d, (k_s, b_s, o_s) in enumerate(((kf_s, bf_s, of_s), (kb_s, bb_s, ob_s))):
            c = i if d == 0 else n - 1 - i
            sl = pl.ds(pl.multiple_of(c * C, C), C)
            for h in range(HG_HEADS):
                q = q_s[sl, heads[h]]
                k = k_s[sl, heads[h]]
                b = b_s[sl, heads[h]]
                v = v_s[sl, heads[h]]
                bm = b[C // 2:C // 2 + 1, :]
                g = b[C - 1:C, :] if d == 0 else b[0:1, :]
                qi = q * jnp.exp(b - bm)
                qe = (qi * jnp.exp(bm)).astype(BF16)
                ki = k * jnp.exp(bm - b)
                kd = (ki * jnp.exp(g - bm)).astype(BF16)
                s = _dot_nt(qi.astype(BF16), ki.astype(BF16))
                p = jnp.where((ci <= ri) if d == 0 else (ci >= ri), s, 0.0).astype(BF16)
                st = st_s[d * HG_HEADS + h]
                o_s[sl, heads[h]] = _dot(p, v) + _dot(qe, st.T.astype(BF16))
                st_s[d * HG_HEADS + h] = st * jnp.exp(g) + _dot_tn(v, kd)
        return carry

    lax.fori_loop(0, n, body, 0, unroll=HG_UNROLL)
    if want_state:
        for d in range(2):
            for h in range(HG_HEADS):
                st_ref[d, h] = st_s[d * HG_HEADS + h].T

    for r in range(T // RB):
        rows = slice(r * RB, (r + 1) * RB)
        for h in range(HG_HEADS):
            o = of_s[rows, heads[h]] + ob_s[rows, heads[h]]
            ms = jnp.mean(o * o, axis=-1, keepdims=True)
            oh = 0.5 * og_ref[rows, heads[h]]
            y = o * lax.rsqrt(ms + EPS) * gain_ref[...] * (oh + oh * jnp.tanh(oh))
            y_ref[rows, heads[h]] = y.astype(BF16)


def _hgrn(hg5, gamma, gain, s0, T, want_state):
    B = hg5.shape[1]
    has_s0 = s0 is not None

    def col(k):
        return pl.BlockSpec((None, None, T, D_HGRN), lambda b, k=k: (k, b, 0, 0))

    st_spec = pl.BlockSpec((None, None, 2, HG_HEADS, HG_DK, HG_DK), lambda b: (b, 0, 0, 0, 0, 0))
    in_specs = [col(0), col(1), col(2), col(3), col(4),
                pl.BlockSpec((2, 2, D_HGRN), lambda b: (0, 0, 0)),
                pl.BlockSpec((1, LANES), lambda b: (0, 0))]
    args = [hg5, hg5, hg5, hg5, hg5, gamma, gain]
    if has_s0:
        in_specs.append(st_spec)
        args.append(s0)
    out_specs = [pl.BlockSpec((None, T, D_HGRN), lambda b: (b, 0, 0))]
    out_shape = [jax.ShapeDtypeStruct((B, T, D_HGRN), BF16)]
    if want_state:
        out_specs.append(st_spec)
        out_shape.append(jax.ShapeDtypeStruct((B, 1, 2, HG_HEADS, HG_DK, HG_DK), F32))
    res = pl.pallas_call(
        functools.partial(_hgrn_kernel, T, has_s0, want_state),
        grid=(B,),
        in_specs=in_specs,
        out_specs=out_specs,
        out_shape=out_shape,
        scratch_shapes=[pltpu.VMEM((T, D_HGRN), F32), pltpu.VMEM((T, D_HGRN), BF16)]
        + [pltpu.VMEM((T, D_HGRN), F32)] * 6
        + [pltpu.VMEM((2 * HG_HEADS, HG_DK, HG_DK), F32)],
        compiler_params=_cp(("arbitrary",)),
    )(*args)
    return res if want_state else (res[0], None)


def _mix_out_kernel(spr, hs_ref, gr_ref, yh_ref, x_ref, mod_ref, nl_ref, wo_ref, nf_ref, wr_ref,
                    x1_ref, ha_ref, at_ref):
    wr = wr_ref[...]
    wo = wo_ref[...].astype(BF16)
    g1 = _mod_get(mod_ref, spr, 0)
    sh2 = _mod_get(mod_ref, spr, 1)
    gain2 = nf_ref[...] * (1.0 + _mod_get(mod_ref, spr, 2))
    lane = lax.broadcasted_iota(jnp.int32, (TOK_TILE, LANES), 1)
    for s in range(x_ref.shape[0] // TOK_TILE):
        rows = slice(s * TOK_TILE, (s + 1) * TOK_TILE)
        hs = jnp.concatenate([hs_ref[c, rows, :] for c in range(D_LRU // LANES)], axis=1).astype(F32)
        ms = jnp.mean(hs * hs, axis=-1, keepdims=True)
        gr = gr_ref[rows, :].astype(F32)
        gelu = 0.5 * gr * (1.0 + jnp.tanh(0.7978845608028654 * (gr + 0.044715 * (gr * gr * gr))))
        y_lru = hs * lax.rsqrt(ms + EPS) * nl_ref[...] * gelu
        ycat = jnp.concatenate([y_lru.astype(BF16), yh_ref[rows, :]], axis=-1)
        x1 = x_ref[rows, :] + g1 * _dot(ycat, wo)
        x1_ref[rows, :] = x1
        yield
        ms = jnp.mean(x1 * x1, axis=-1, keepdims=True)
        h2 = x1 * lax.rsqrt(ms + EPS) * gain2 + sh2
        h2_hi = h2.astype(BF16)
        h2_mid = (h2 - h2_hi.astype(F32)).astype(BF16)
        p_hi = _dot(h2_hi, wr)
        logits = p_hi[:, 0:LANES] + (p_hi[:, LANES:2 * LANES] + _dot(h2_mid, wr)[:, 0:LANES])
        logits = jnp.where(lane < N_EXPERTS, logits, -jnp.inf)
        e = jnp.exp(logits - jnp.max(logits, axis=-1, keepdims=True))
        aff = e / jnp.sum(e, axis=-1, keepdims=True)
        w = at_ref.shape[1]
        r, c = (s * TOK_TILE) // w, (s * TOK_TILE) % w
        at_ref[r * N_EXPERTS:(r + 1) * N_EXPERTS, c:c + TOK_TILE] = aff.T[0:N_EXPERTS, :]
        ha_ref[rows, :] = h2_hi
        yield


def _mix_out_part(hs4, gr, yh, x, mods, msel, norm_lru, w_out, norm_ffn, wr_pad, T, tile_rows):
    n = x.shape[0]
    assert T % TOK_TILE == 0 and (tile_rows % T == 0 or T % tile_rows == 0)
    q = max(T // tile_rows, 1)
    at_rows = N_EXPERTS * max(tile_rows // T, 1)
    at_spec = pl.BlockSpec((at_rows, min(T, tile_rows)), lambda i: (i // q, i % q))
    tile = lambda w: pl.BlockSpec((tile_rows, w), lambda i: (i, 0))
    const = lambda r, w: pl.BlockSpec((r, w), lambda i: (0, 0))
    return dict(
        kernel=functools.partial(_mix_out_kernel, _spr(msel, q)),
        steps=n // tile_rows,
        args=[hs4, gr, yh, x, mods, norm_lru, w_out, norm_ffn, wr_pad],
        in_specs=[pl.BlockSpec((D_LRU // LANES, tile_rows, LANES), lambda i: (0, i, 0)),
                  tile(D_LRU), tile(D_HGRN), tile(D_MODEL),
                  _mod_spec(mods, msel),
                  const(1, D_LRU), const(D_MODEL, D_MODEL), const(1, D_MODEL), const(D_MODEL, 2 * LANES)],
        out_specs=[tile(D_MODEL), tile(D_MODEL), at_spec],
        out_shape=[jax.ShapeDtypeStruct((n, D_MODEL), F32),
                   jax.ShapeDtypeStruct((n, D_MODEL), BF16),
                   jax.ShapeDtypeStruct((n // T * N_EXPERTS, T), F32)],
        scratch=[],
    )


def _select_kernel(cap, at_ref, slot_ref):
    a = at_ref[...]
    R, T = a.shape
    ones = jnp.ones((T, LANES), BF16)
    su = (lax.broadcasted_iota(jnp.int32, (T, T), 0)
          < lax.broadcasted_iota(jnp.int32, (T, T), 1)).astype(BF16)

    def wide(x):
        return jnp.concatenate([x] * (T // LANES), axis=1)

    def count(mask):
        return _dot(mask.astype(BF16), ones)

    lo = jnp.zeros((R, LANES), F32)
    hi = jnp.full((R, LANES), 2.0, F32)
    for _ in range(SEL_BISECT):
        mid = 0.5 * (lo + hi)
        ge = count(a >= wide(mid)) >= cap
        lo = jnp.where(ge, mid, lo)
        hi = jnp.where(ge, hi, mid)

    def cond(st):
        return (st[0] < T) & (jnp.min(st[2]) < cap)

    def body(st):
        it, cur, n, thr = st
        m = jnp.max(jnp.where(a < wide(cur), a, -1.0), axis=1, keepdims=True)
        m = jnp.broadcast_to(m, (R, LANES))
        c = count(a >= wide(m))
        act = n < cap
        return it + 1, jnp.where(act, m, cur), jnp.where(act, c, n), jnp.where(act, m, thr)

    _, _, _, thr = lax.while_loop(cond, body, (jnp.int32(0), hi, count(a >= wide(hi)), lo))
    thr_w = wide(thr)
    gt = a > thr_w
    eq = a == thr_w
    need = wide(cap - count(gt))
    sel = gt | (eq & (_dot(eq.astype(BF16), su) < need))
    slot_ref[...] = jnp.where(sel, _dot(sel.astype(BF16), su), -1.0).astype(jnp.int32)


def _select(at, T):
    cap = EC_CAPACITY_FACTOR * T // N_EXPERTS
    R = at.shape[0]
    return pl.pallas_call(
        functools.partial(_select_kernel, cap),
        grid=(1,),
        in_specs=[pl.BlockSpec((R, T), lambda i: (0, 0))],
        out_specs=pl.BlockSpec((R, T), lambda i: (0, 0)),
        out_shape=jax.ShapeDtypeStruct((R, T), jnp.int32),
        compiler_params=_cp(("arbitrary",)),
    )(at)


def _onehot(cap, slot_ref, g):
    j = lax.broadcasted_iota(jnp.int32, (cap, slot_ref.shape[1]), 0)
    r0 = g * N_EXPERTS
    return jnp.concatenate([(slot_ref[r0 + e:r0 + e + 1, :] == j).astype(BF16) for e in range(N_EXPERTS)], axis=0)


def _dispatch_kernel(T, cap, slot_ref, ha_ref, xd_ref):
    for g in range(ha_ref.shape[0] // T):
        rows = _dot(_onehot(cap, slot_ref, g), ha_ref[g * T:(g + 1) * T, :])
        for e in range(N_EXPERTS):
            xd_ref[e, g * cap:(g + 1) * cap, :] = rows[e * cap:(e + 1) * cap].astype(BF16)
        yield


def _dispatch_part(slot, ha, T, G):
    cap = EC_CAPACITY_FACTOR * T // N_EXPERTS
    B = ha.shape[0] // T
    return dict(
        kernel=functools.partial(_dispatch_kernel, T, cap),
        steps=B // G,
        args=[slot, ha],
        in_specs=[pl.BlockSpec((G * N_EXPERTS, T), lambda b: (b, 0)),
                  pl.BlockSpec((G * T, D_MODEL), lambda b: (b, 0))],
        out_specs=[pl.BlockSpec((N_EXPERTS, G * cap, D_MODEL), lambda b: (0, b, 0))],
        out_shape=[jax.ShapeDtypeStruct((N_EXPERTS, B * cap, D_MODEL), BF16)],
        scratch=[],
    )


def _slot_gates(slot_ref, at_ref, cap):
    e = pl.program_id(0)
    j = lax.broadcasted_iota(jnp.int32, (cap, slot_ref.shape[1]), 0)

    def request(g):
        row = pl.ds(g * N_EXPERTS + e, 1)
        return jnp.sum(jnp.where(slot_ref[row, :] == j, at_ref[row, :], 0.0), axis=-1, keepdims=True)

    return jnp.concatenate([request(g) for g in range(slot_ref.shape[0] // N_EXPERTS)], axis=0)


def _ffn_kernel(xp_ref, xs_ref, slot_p_ref, at_p_ref, slot_s_ref, at_s_ref, wg_ref, wu_ref, wd_ref, yp_ref, ys_ref):
    wg = wg_ref[...].astype(BF16)
    wu = wu_ref[...].astype(BF16)
    wd = wd_ref[...].astype(BF16)
    for x_ref, slot_ref, at_ref, y_ref in ((xp_ref, slot_p_ref, at_p_ref, yp_ref),
                                           (xs_ref, slot_s_ref, at_s_ref, ys_ref)):
        x = x_ref[...]
        gate = _slot_gates(slot_ref, at_ref, x.shape[0] * N_EXPERTS // slot_ref.shape[0])
        hg = _dot(x, wg)
        hid = (hg * _sigmoid(hg)) * _dot(x, wu)
        y_ref[...] = (_dot(hid.astype(BF16), wd) * gate).astype(BF16)


def _ffn(xd_p, xd_s, slot_p, at_p, slot_s, at_s, w_gate, w_up, w_down):
    rows = xd_p.shape[1]
    assert xd_s.shape[1] == rows == FFN_TILE
    xspec = pl.BlockSpec((None, rows, D_MODEL), lambda e: (e, 0, 0))
    wspec = pl.BlockSpec((None, None, D_MODEL, D_MODEL), lambda e: (0, e, 0, 0))
    yshape = jax.ShapeDtypeStruct((N_EXPERTS, rows, D_MODEL), BF16)

    routing = (slot_p, at_p, slot_s, at_s)
    rspecs = [pl.BlockSpec(a.shape, lambda e: (0, 0)) for a in routing]
    return pl.pallas_call(
        _ffn_kernel,
        grid=(N_EXPERTS,),
        in_specs=[xspec, xspec, *rspecs, wspec, wspec, wspec],
        out_specs=[xspec, xspec],
        out_shape=[yshape, yshape],
        compiler_params=_cp(("arbitrary",)),
    )(xd_p, xd_s, *routing, w_gate, w_up, w_down)


def _combine_kernel(cap, spr, slot_ref, yd_ref, x1_ref, mod_ref, nf_ref, o_ref):
    W = slot_ref.shape[1]
    g2 = _mod_get(mod_ref, spr, 3)
    for g in range(x1_ref.shape[0] // W):
        rows = slice(g * W, (g + 1) * W)
        yd = jnp.concatenate([yd_ref[e, g * cap:(g + 1) * cap, :] for e in range(N_EXPERTS)], axis=0)
        acc = _dot_tn(_onehot(cap, slot_ref, g), yd)
        x2 = x1_ref[rows, :] + g2 * acc
        ms = jnp.mean(x2 * x2, axis=-1, keepdims=True)
        o_ref[rows, :] = x2 * lax.rsqrt(ms + EPS) * nf_ref[...]
        yield


def _combine_part(slot, yd, x1, mods, msel, norm_final, T, tok):
    cap = EC_CAPACITY_FACTOR * T // N_EXPERTS
    n = x1.shape[0]
    G, h = (tok // T, 1) if tok >= T else (1, T // tok)
    assert G == 1 or not msel[1]
    return dict(
        kernel=functools.partial(_combine_kernel, cap, _spr(msel, h)),
        steps=n // tok,
        args=[slot, yd, x1, mods, norm_final],
        in_specs=[pl.BlockSpec((G * N_EXPERTS, min(T, tok)), lambda i: (i // h, i % h)),
                  pl.BlockSpec((N_EXPERTS, G * cap, D_MODEL), lambda i: (0, i // h, 0)),
                  pl.BlockSpec((tok, D_MODEL), lambda i: (i, 0)),
                  _mod_spec(mods, msel),
                  pl.BlockSpec((1, D_MODEL), lambda i: (0, 0))],
        out_specs=[pl.BlockSpec((tok, D_MODEL), lambda i: (i, 0))],
        out_shape=[jax.ShapeDtypeStruct((n, D_MODEL), F32)],
        scratch=[],
    )


def _gate_tiles(wa, wx, ba, bx):
    n_ct = D_LRU // LANES

    def dense(w):
        w2 = w.reshape(n_ct, 2, LRU_BW, LRU_BW)
        z = jnp.zeros((n_ct, LRU_BW, LRU_BW), w.dtype)
        top = jnp.concatenate([w2[:, 0], z], axis=2)
        bot = jnp.concatenate([z, w2[:, 1]], axis=2)
        return jnp.concatenate([top, bot], axis=1)

    tiles = 0.5 * jnp.concatenate([dense(wa[0]), dense(wx[0]), dense(wa[1]), dense(wx[1])], axis=-1)
    t = lambda v: v.reshape(n_ct, 1, LANES)
    bias = 0.5 * jnp.concatenate([t(ba[0]), t(bx[0]), t(ba[1]), t(bx[1])], axis=-1)
    hi = bias.astype(BF16)
    lo = (bias - hi.astype(F32)).astype(BF16)
    pad = jnp.zeros((n_ct, LANES - 2, 4 * LANES), BF16)
    return jnp.concatenate([tiles.astype(BF16), hi, lo, pad], axis=1)


def _router_split(w):
    wp = jnp.pad(w, ((0, 0), (0, LANES - N_EXPERTS)))
    hi = wp.astype(BF16)
    mid = (wp - hi.astype(F32)).astype(BF16)
    return jnp.concatenate([hi, mid], axis=1)


def _proj(x, mods, msel, p, T, tile):
    return _proj_part(x.reshape(-1, D_MODEL), mods, msel, p["norm_mix"], p["w_in"], T, tile)


def _lru(xr4, B, T, row_len, h0, p, unroll):
    n_ct = D_LRU // LANES
    return _lru_part(xr4.reshape(n_ct, B, T, LANES), p["conv_w"], p["conv_b"], p["gate_w"], p["lam"],
                     h0, T, row_len, unroll)


def _mix(x, mods, msel, p, T, hs4, gr, yh, tile_rows):
    B = x.shape[0]
    return _mix_out_part(hs4.reshape(D_LRU // LANES, B * T, LANES), gr, yh.reshape(B * T, D_HGRN),
                         x.reshape(B * T, D_MODEL), mods, msel,
                         p["norm_lru"], p["w_out"], p["norm_ffn"], p["w_router"], T, tile_rows)


def kernel(x_prompt, x_sample, state_lru, state_hgrn, c, c_ctx, w_ada, b_ada, norm_mix, w_in, conv_w,
           conv_b, lru_wa, lru_ba, lru_wx, lru_bx, lru_lambda, norm_lru, hgrn_gamma, norm_hgrn, w_out,
           norm_ffn, w_router, w_gate, w_up, w_down, norm_final):
    assert w_ada.shape[0] == 1 and hgrn_gamma.shape[1] == 2, "one trunk layer"
    l = 0
    Bp, Tp, _ = x_prompt.shape
    Bs, Ts, _ = x_sample.shape

    assert Bs == MOD_ROWS
    cond = jnp.concatenate([c, c_ctx[None, :], jnp.zeros((MOD_ROWS - 1, D_MODEL), F32)], axis=0)
    n1 = 2 * D_MODEL
    ((m1,),) = _run([_adaln_part(cond, w_ada, b_ada, 0, n1, D_MODEL)])
    sel_p, sel_s = (1, False), (0, True)

    p = {
        "norm_mix": norm_mix[l][None, :], "w_in": w_in,
        "conv_w": conv_w[l], "conv_b": conv_b[l][None, :],
        "gate_w": _gate_tiles(lru_wa[l], lru_wx[l], lru_ba[l], lru_bx[l]),
        "lam": lru_lambda[l], "norm_lru": norm_lru[l][None, :],
        "gamma": hgrn_gamma, "norm_hgrn": norm_hgrn[l][None, :],
        "w_out": w_out[l], "norm_ffn": norm_ffn[l][None, :],
        "w_router": _router_split(w_router[l]),
    }
    zeros_p = jnp.zeros((Bp, 2 * D_LRU), F32)
    proj_p = _proj(x_prompt, m1, sel_p, p, Tp, PROJ_TILE)
    n2 = (N_MOD - 2) * D_MODEL
    (xr4_p, gr_p, hg5_p), (m2,) = _run([proj_p, _adaln_part(cond, w_ada, b_ada, n1, n2, n2 // proj_p["steps"])])
    (xr4_s, gr_s, hg5_s), (hs4_p, last_f, last_b) = _run(
        [_proj(x_sample, m1, sel_s, p, Ts, PROJ_TILE), _lru(xr4_p, Bp, Tp, Tp, zeros_p, p, True)])
    yh_p, st_p = _hgrn(hg5_p.reshape(N_HG_IN, Bp, Tp, D_HGRN), p["gamma"], p["norm_hgrn"], None, Tp, True)
    ((x1p, ha_p, at_p),) = _run([_mix(x_prompt, m2, sel_p, p, Tp, hs4_p, gr_p, yh_p, MIX_TILE)])
    slot_p = _select(at_p, Tp)
    assert state_lru.shape[1] == 1
    ((hs4_s, _, _),) = _run([_lru(xr4_s, Bs, Ts, GRID_W, state_lru.reshape(Bs, 2 * D_LRU), p, 8)])
    yh_s, _ = _hgrn(hg5_s.reshape(N_HG_IN, Bs, Ts, D_HGRN), p["gamma"], p["norm_hgrn"], state_hgrn, Ts, False)
    fused_tile = MIX_TILE
    (x1s, ha_s, at_s), (xd_p,) = _run(
        [_mix(x_sample, m2, sel_s, p, Ts, hs4_s, gr_s, yh_s, fused_tile),
         _dispatch_part(slot_p, ha_p, Tp, fused_tile // Tp)])
    slot_s = _select(at_s, Ts)
    ((xd_s,),) = _run([_dispatch_part(slot_s, ha_s, Ts, max(ROUTE_TOKENS // Ts, 1))])

    yd_p, yd_s = _ffn(xd_p, xd_s, slot_p, at_p, slot_s, at_s, w_gate, w_up, w_down)
    nf = norm_final[None, :]
    (y_sample,), (y_prompt,) = _run(
        [_combine_part(slot_s, yd_s, x1s, m2, sel_s, nf, Ts, COMBINE_TOKENS),
         _combine_part(slot_p, yd_p, x1p, m2, sel_p, nf, Tp, COMBINE_TOKENS)])
    y_prompt = y_prompt.reshape(Bp, Tp, D_MODEL)
    y_sample = y_sample.reshape(Bs, Ts, D_MODEL)
    new_state_lru = jnp.stack([last_f, last_b], axis=1)[:, None]
    new_state_hgrn = st_p
    return (y_prompt, y_sample, new_state_lru, new_state_hgrn)
```
